```python
import math
import jax, jax.numpy as jnp
from jax import lax
import numpy as np

D_MODEL = 1024
BATCH = 1
SEQ = 16384
DEPTH = 1
DEC_BATCH = 32
DEC_SEQ = 32
PAST_LEN = 1024

CHUNK = 64
Q_BLOCK = 128
SSM_WIDTH = D_MODEL // 2
SSM_GROUP = 16
SSM_GROUPS = SSM_WIDTH // SSM_GROUP
SSM_STATE = 64
DT_MIN = 1e-3
DT_MAX = 1e-1
MLA_HEADS = 4
QK_NOPE = 128
QK_ROPE = 64
V_HEAD = 128
MLA_WIDTH = MLA_HEADS * V_HEAD
Q_LORA = 384
KV_LORA = 256
ROPE_THETA = 10000.0
MLA_SCALE = (QK_NOPE + QK_ROPE) ** -0.5
IN_WIDTH = SSM_WIDTH + Q_LORA + KV_LORA + QK_ROPE
MIX_WIDTH = SSM_WIDTH + MLA_WIDTH
N_MEM = 256
X_HEADS = 4
X_HEAD_DIM = D_MODEL // X_HEADS
D_FF = 4 * D_MODEL
ALPHA = (2 * DEPTH) ** 0.25
BETA = (8 * DEPTH) ** -0.25
EPS = 1e-5
NEG_INF = -1e30

kernel_name = "hymba_s5_mla_deepnorm_stream_step"


def _layer_norm(x, g, b):
    xf = x.astype(jnp.float32)
    mu = jnp.mean(xf, -1, keepdims=True)
    var = jnp.mean(jnp.square(xf - mu), -1, keepdims=True)
    return ((xf - mu) * lax.rsqrt(var + EPS) * g.astype(jnp.float32) + b.astype(jnp.float32)).astype(x.dtype)


def _rms_norm(x, g):
    xf = x.astype(jnp.float32)
    return (xf * lax.rsqrt(jnp.mean(xf * xf, -1, keepdims=True) + EPS) * g.astype(jnp.float32)).astype(x.dtype)


def _rope(x, pos):
    half = QK_ROPE // 2
    inv = ROPE_THETA ** (-jnp.arange(half, dtype=jnp.float32) / half)
    ang = pos.astype(jnp.float32)[:, None] * inv[None, :]
    cos = jnp.cos(ang)[None, :, None, :]
    sin = jnp.sin(ang)[None, :, None, :]
    xf = x.astype(jnp.float32)
    x1, x2 = xf[..., :half], xf[..., half:]
    return jnp.concatenate([x1 * cos - x2 * sin, x2 * cos + x1 * sin], -1).astype(x.dtype)


def _chunk_attend(q, k, v, q_pos, k_pos):
    s = jnp.einsum('bqhd,bkhd->bhqk', q, k).astype(jnp.float32) * MLA_SCALE
    mask = (k_pos // CHUNK)[None, :] <= (q_pos // CHUNK)[:, None]
    s = jnp.where(mask[None, None], s, NEG_INF)
    p = jax.nn.softmax(s, axis=-1).astype(v.dtype)
    return jnp.einsum('bhqk,bkhd->bqhd', p, v)


def _blocked_attend(q, k, v, q_pos, k_pos):
    b, s, hh, dk = q.shape
    nblk = s // Q_BLOCK
    qb = jnp.moveaxis(q.reshape(b, nblk, Q_BLOCK, hh, dk), 1, 0)
    pb = q_pos.reshape(nblk, Q_BLOCK)
    ob = lax.map(lambda qp: _chunk_attend(qp[0], k, v, qp[1], k_pos), (qb, pb))
    return jnp.moveaxis(ob, 0, 1).reshape(b, s, hh, v.shape[-1])


def _s5_scan(u, h0_re, h0_im, a_re, a_im, b_re, b_im, c_re, c_im, d_skip, log_dt):
    nb, s = u.shape[:2]
    f32 = jnp.float32
    ug = u.astype(f32).reshape(nb, s, SSM_GROUPS, SSM_GROUP)
    ar, ai = a_re.astype(f32), a_im.astype(f32)
    dt = jnp.exp(log_dt.astype(f32))[:, None]
    mag = jnp.exp(ar * dt)
    lb_re, lb_im = mag * jnp.cos(ai * dt), mag * jnp.sin(ai * dt)
    nr, ni = lb_re - 1.0, lb_im
    den = ar * ar + ai * ai
    f_re, f_im = (nr * ar + ni * ai) / den, (ni * ar - nr * ai) / den
    br, bi = b_re.astype(f32), b_im.astype(f32)
    bb_re = f_re[..., None] * br - f_im[..., None] * bi
    bb_im = f_re[..., None] * bi + f_im[..., None] * br
    bu_re = jnp.einsum('gph,bsgh->bsgp', bb_re, ug)
    bu_im = jnp.einsum('gph,bsgh->bsgp', bb_im, ug)
    h0r, h0i = h0_re.astype(f32), h0_im.astype(f32)
    bu_re = bu_re.at[:, 0].add(lb_re * h0r - lb_im * h0i)
    bu_im = bu_im.at[:, 0].add(lb_re * h0i + lb_im * h0r)
    a_seq_re = jnp.broadcast_to(lb_re, bu_re.shape)
    a_seq_im = jnp.broadcast_to(lb_im, bu_im.shape)

    def combine(e1, e2):
        a1r, a1i, b1r, b1i = e1
        a2r, a2i, b2r, b2i = e2
        return (a1r * a2r - a1i * a2i, a1r * a2i + a1i * a2r,
                a2r * b1r - a2i * b1i + b2r, a2r * b1i + a2i * b1r + b2i)

    _, _, xr, xi = lax.associative_scan(combine, (a_seq_re, a_seq_im, bu_re, bu_im), axis=1)
    y = (jnp.einsum('ghp,bsgp->bsgh', c_re.astype(f32), xr)
         - jnp.einsum('ghp,bsgp->bsgh', c_im.astype(f32), xi)
         + d_skip.astype(f32) * ug)
    return y.reshape(nb, s, SSM_WIDTH).astype(u.dtype), xr[:, -1], xi[:, -1]


def _token_mixer(h, pos, past_ckv, past_kpe, ssm0_re, ssm0_im, block_sweep,
                 w_in, g_q, w_q_up, g_kv, w_kv_up, a_re, a_im, b_re, b_im, c_re, c_im,
                 d_skip, log_dt, w_glu, g_out_ssm, g_out_mla, w_o):
    nb, s = h.shape[:2]
    proj = jnp.einsum('bsd,de->bse', h, w_in)
    u, c_q, c_kv, k_pe = jnp.split(
        proj, [SSM_WIDTH, SSM_WIDTH + Q_LORA, SSM_WIDTH + Q_LORA + KV_LORA], axis=-1)
    y_ssm, ssm_re, ssm_im = _s5_scan(u, ssm0_re, ssm0_im, a_re, a_im, b_re, b_im,
                                     c_re, c_im, d_skip, log_dt)
    g = jnp.einsum('bsc,ce->bse', jax.nn.gelu(y_ssm), w_glu)
    o_ssm = g[..., :SSM_WIDTH] * jax.nn.sigmoid(g[..., SSM_WIDTH:])
    q = jnp.einsum('bsr,rhe->bshe', _rms_norm(c_q, g_q), w_q_up)
    q = jnp.concatenate([q[..., :QK_NOPE], _rope(q[..., QK_NOPE:], pos)], -1)
    ckv_new = _rms_norm(c_kv, g_kv)
    kpe_new = _rope(k_pe[:, :, None, :], pos)[:, :, 0, :]
    if past_ckv is None:
        ckv_all, kpe_all, k_pos = ckv_new, kpe_new, pos
    else:
        past_len = past_ckv.shape[1]
        ckv_all = jnp.concatenate([past_ckv.astype(ckv_new.dtype), ckv_new], 1)
        kpe_all = jnp.concatenate([past_kpe.astype(kpe_new.dtype), kpe_new], 1)
        k_pos = jnp.concatenate([jnp.arange(past_len, dtype=jnp.int32), pos])
    kv = jnp.einsum('bsr,rhe->bshe', ckv_all, w_kv_up)
    k = jnp.concatenate(
        [kv[..., :QK_NOPE], jnp.broadcast_to(kpe_all[:, :, None, :], kv.shape[:3] + (QK_ROPE,))], -1)
    v = kv[..., QK_NOPE:]
    if block_sweep:
        attn = _blocked_attend(q, k, v, pos, k_pos)
    else:
        attn = _chunk_attend(q, k, v, pos, k_pos)
    o_mla = attn.reshape(nb, s, MLA_WIDTH)
    mixed = jnp.concatenate([_rms_norm(o_ssm, g_out_ssm), _rms_norm(o_mla, g_out_mla)], -1)
    return jnp.einsum('bsc,cd->bsd', mixed, w_o), ckv_new, kpe_new, ssm_re, ssm_im


def _mem_kv(mem, w_xk, w_xv):
    return (jnp.einsum('bmd,dhe->bmhe', mem, w_xk), jnp.einsum('bmd,dhe->bmhe', mem, w_xv))


def _mem_attend(h, mem_k, mem_v, w_xq, w_xo):
    q = jnp.einsum('bsd,dhe->bshe', h, w_xq)
    s = jnp.einsum('bshe,bmhe->bhsm', q, mem_k.astype(q.dtype)).astype(jnp.float32) * (X_HEAD_DIM ** -0.5)
    p = jax.nn.softmax(s, axis=-1).astype(q.dtype)
    o = jnp.einsum('bhsm,bmhe->bshe', p, mem_v.astype(q.dtype))
    return jnp.einsum('bshe,hed->bsd', o, w_xo)


def _sq_relu_mlp(h, w_ff1, w_ff2):
    z = jax.nn.relu(jnp.einsum('bsd,df->bsf', h, w_ff1))
    return jnp.einsum('bsf,fd->bsd', z * z, w_ff2)


def _layer(h, pos, past_ckv, past_kpe, ssm0_re, ssm0_im, mem_k, mem_v, block_sweep,
           w_in, g_q, w_q_up, g_kv, w_kv_up, a_re, a_im, b_re, b_im, c_re, c_im, d_skip,
           log_dt, w_glu, g_out_ssm, g_out_mla, w_o, w_xq, w_xo, w_ff1, w_ff2, ln_g, ln_b):
    a, ckv, kpe, sr, si = _token_mixer(
        h, pos, past_ckv, past_kpe, ssm0_re, ssm0_im, block_sweep,
        w_in, g_q, w_q_up, g_kv, w_kv_up, a_re, a_im, b_re, b_im, c_re, c_im,
        d_skip, log_dt, w_glu, g_out_ssm, g_out_mla, w_o)
    h = _layer_norm(ALPHA * h + a, ln_g[0], ln_b[0])
    h = _layer_norm(ALPHA * h + _mem_attend(h, mem_k, mem_v, w_xq, w_xo), ln_g[1], ln_b[1])
    h = _layer_norm(ALPHA * h + _sq_relu_mlp(h, w_ff1, w_ff2), ln_g[2], ln_b[2])
    return h, ckv, kpe, sr, si


def setup_inputs(seed: int = 0) -> dict:
    key = jax.random.key(seed)
    ks = iter(jax.random.split(key, 40))
    f32 = jnp.float32

    def nrm(shape, scale):
        return jax.random.normal(next(ks), shape, f32) * scale

    L, G, P, H = DEPTH, SSM_GROUPS, SSM_STATE, SSM_GROUP
    n = jnp.arange(P, dtype=f32)
    log_dt = math.log(DT_MIN) + jax.random.uniform(next(ks), (L, G), f32) * (math.log(DT_MAX) - math.log(DT_MIN))
    return {
        "x_prompt": nrm((BATCH, SEQ, D_MODEL), 1.0),
        "x_sample": nrm((DEC_BATCH, DEC_SEQ, D_MODEL), 1.0),
        "mem_prompt": nrm((BATCH, N_MEM, D_MODEL), 1.0),
        "cache_mla_ckv": nrm((L, DEC_BATCH, PAST_LEN, KV_LORA), 1.0),
        "cache_mla_kpe": nrm((L, DEC_BATCH, PAST_LEN, QK_ROPE), 1.0),
        "state_ssm_re": nrm((L, DEC_BATCH, G, P), 0.1),
        "state_ssm_im": nrm((L, DEC_BATCH, G, P), 0.1),
        "cache_mem_k": nrm((L, DEC_BATCH, N_MEM, X_HEADS, X_HEAD_DIM), 1.0),
        "cache_mem_v": nrm((L, DEC_BATCH, N_MEM, X_HEADS, X_HEAD_DIM), 1.0),
        "w_in": nrm((L, D_MODEL, IN_WIDTH), D_MODEL ** -0.5),
        "g_q": 1.0 + nrm((L, Q_LORA), 0.01),
        "w_q_up": nrm((L, Q_LORA, MLA_HEADS, QK_NOPE + QK_ROPE), Q_LORA ** -0.5),
        "g_kv": 1.0 + nrm((L, KV_LORA), 0.01),
        "w_kv_up": nrm((L, KV_LORA, MLA_HEADS, QK_NOPE + V_HEAD), KV_LORA ** -0.5),
        "a_re": -0.5 + nrm((L, G, P), 0.01),
        "a_im": math.pi * n + nrm((L, G, P), 0.01),
        "b_re": nrm((L, G, P, H), (2 * H) ** -0.5),
        "b_im": nrm((L, G, P, H), (2 * H) ** -0.5),
        "c_re": nrm((L, G, H, P), (2 * P) ** -0.5),
        "c_im": nrm((L, G, H, P), (2 * P) ** -0.5),
        "d_skip": nrm((L, G, H), 1.0),
        "log_dt": log_dt,
        "w_glu": nrm((L, SSM_WIDTH, 2 * SSM_WIDTH), SSM_WIDTH ** -0.5),
        "g_out_ssm": 1.0 + nrm((L, SSM_WIDTH), 0.01),
        "g_out_mla": 1.0 + nrm((L, MLA_WIDTH), 0.01),
        "w_o": nrm((L, MIX_WIDTH, D_MODEL), MIX_WIDTH ** -0.5 * BETA),
        "w_xq": nrm((L, D_MODEL, X_HEADS, X_HEAD_DIM), D_MODEL ** -0.5),
        "w_xk": nrm((L, D_MODEL, X_HEADS, X_HEAD_DIM), D_MODEL ** -0.5),
        "w_xv": nrm((L, D_MODEL, X_HEADS, X_HEAD_DIM), D_MODEL ** -0.5),
        "w_xo": nrm((L, X_HEADS, X_HEAD_DIM, D_MODEL), D_MODEL ** -0.5 * BETA),
        "w_ff1": nrm((L, D_MODEL, D_FF), D_MODEL ** -0.5),
        "w_ff2": nrm((L, D_FF, D_MODEL), D_FF ** -0.5 * BETA),
        "ln_g": 1.0 + nrm((L, 3, D_MODEL), 0.01),
        "ln_b": nrm((L, 3, D_MODEL), 0.01),
    }


def reference(x_prompt, x_sample, mem_prompt, cache_mla_ckv, cache_mla_kpe, state_ssm_re,
              state_ssm_im, cache_mem_k, cache_mem_v, w_in, g_q, w_q_up, g_kv, w_kv_up,
              a_re, a_im, b_re, b_im, c_re, c_im, d_skip, log_dt, w_glu, g_out_ssm,
              g_out_mla, w_o, w_xq, w_xk, w_xv, w_xo, w_ff1, w_ff2, ln_g, ln_b):
    nbp, sp = x_prompt.shape[:2]
    sd = x_sample.shape[1]
    past_len = cache_mla_ckv.shape[2]
    pos_p = jnp.arange(sp, dtype=jnp.int32)
    pos_s = past_len + jnp.arange(sd, dtype=jnp.int32)
    zero_state = jnp.zeros((nbp, SSM_GROUPS, SSM_STATE), jnp.float32)

    hp, hs = x_prompt, x_sample
    ckv_p, kpe_p, sre_p, sim_p, mk_p, mv_p = [], [], [], [], [], []
    ckv_s, kpe_s, sre_s, sim_s = [], [], [], []
    for l in range(DEPTH):
        lw = (w_in[l], g_q[l], w_q_up[l], g_kv[l], w_kv_up[l], a_re[l], a_im[l], b_re[l],
              b_im[l], c_re[l], c_im[l], d_skip[l], log_dt[l], w_glu[l], g_out_ssm[l],
              g_out_mla[l], w_o[l], w_xq[l], w_xo[l], w_ff1[l], w_ff2[l], ln_g[l], ln_b[l])
        mk, mv = _mem_kv(mem_prompt, w_xk[l], w_xv[l])
        hp, c1, k1, r1, i1 = _layer(hp, pos_p, None, None, zero_state, zero_state,
                                    mk, mv, True, *lw)
        ckv_p.append(c1); kpe_p.append(k1); sre_p.append(r1); sim_p.append(i1)
        mk_p.append(mk); mv_p.append(mv)
        hs, c2, k2, r2, i2 = _layer(hs, pos_s, cache_mla_ckv[l], cache_mla_kpe[l],
                                    state_ssm_re[l], state_ssm_im[l],
                                    cache_mem_k[l], cache_mem_v[l], False, *lw)
        ckv_s.append(c2); kpe_s.append(k2); sre_s.append(r2); sim_s.append(i2)

    return (hp, hs,
            jnp.stack(ckv_p), jnp.stack(kpe_p), jnp.stack(sre_p), jnp.stack(sim_p),
            jnp.stack(mk_p), jnp.stack(mv_p),
            jnp.stack(ckv_s), jnp.stack(kpe_s), jnp.stack(sre_s), jnp.stack(sim_s))
```

```python
import functools
import math

import jax
import jax.numpy as jnp
from jax import lax
from jax.experimental import pallas as pl
from jax.experimental.pallas import tpu as pltpu

F32 = jnp.float32
BF16 = jnp.bfloat16

D_MODEL = 1024
DEPTH = 1
CHUNK = 64
SSM_WIDTH = 512
SSM_GROUP = 16
SSM_GROUPS = 32
SSM_STATE = 64
MLA_HEADS = 4
QK_NOPE = 128
QK_ROPE = 64
V_HEAD = 128
MLA_WIDTH = MLA_HEADS * V_HEAD
Q_LORA = 384
KV_LORA = 256
ROPE_THETA = 10000.0
MLA_SCALE = (QK_NOPE + QK_ROPE) ** -0.5
N_MEM = 256
X_HEADS = 4
X_HEAD_DIM = D_MODEL // X_HEADS
D_FF = 4 * D_MODEL
ALPHA = (2 * DEPTH) ** 0.25
EPS = 1e-5
NEG_INF = -1e30

QK_PAD = 256
S5_T = 16
S5_TW = S5_T * SSM_GROUP
ROPE_HALF = QK_ROPE // 2
VMEM_LIMIT = 56 * 1024 * 1024

_NT = (((1,), (1,)), ((), ()))


def _rms(x, g):
    return x * lax.rsqrt(jnp.mean(x * x, -1, keepdims=True) + EPS) * g


def _ln(x, g, b):
    mu = jnp.mean(x, -1, keepdims=True)
    xc = x - mu
    var = jnp.mean(xc * xc, -1, keepdims=True)
    return xc * lax.rsqrt(var + EPS) * g + b


def _dot(a, b):
    return jnp.dot(a, b, preferred_element_type=F32)


def _dot_nt(a, b):
    return lax.dot_general(a, b, _NT, preferred_element_type=F32)


def _dot_hi(a, b):
    return jnp.dot(a, b, preferred_element_type=F32, precision=lax.Precision.HIGHEST)


def _params(*sem):
    return pltpu.CompilerParams(dimension_semantics=sem, vmem_limit_bytes=VMEM_LIMIT)


def _const(shape):
    n = len(shape)
    return pl.BlockSpec(shape, lambda *_: (0,) * n)


def _proj_body(x_ref, pos_ref, inv_ref, w_in_ref, gq_ref, wq_ref, gkv_ref, wkv_ref,
               u_ref, q_ref, k_ref, v_ref, ckv_ref, kpe_ref):
    x = x_ref[...].astype(BF16)
    proj = _dot(x, w_in_ref[...])
    u_ref[...] = proj[:, :SSM_WIDTH]
    cq = _rms(proj[:, SSM_WIDTH:SSM_WIDTH + Q_LORA], gq_ref[...])
    q = _dot(cq.astype(BF16), wq_ref[...])
    c0 = SSM_WIDTH + Q_LORA
    ckv = _rms(proj[:, c0:c0 + KV_LORA], gkv_ref[...])
    ckv_ref[...] = ckv
    kv = _dot(ckv.astype(BF16), wkv_ref[...])

    ang = pos_ref[...] * inv_ref[...]
    lane = lax.broadcasted_iota(jnp.int32, (1, 128), 1)
    cs, sn = jnp.cos(ang), jnp.sin(ang)
    cos_t = jnp.where(lane < QK_ROPE, cs, 0.0)
    sin_a = jnp.where(lane < ROPE_HALF, -sn, 0.0)
    sin_b = jnp.where((lane >= ROPE_HALF) & (lane < QK_ROPE), sn, 0.0)

    def rope(c2):
        return (c2 * cos_t + pltpu.roll(c2, 128 - ROPE_HALF, 1) * sin_a
                + pltpu.roll(c2, ROPE_HALF, 1) * sin_b)

    kpe = rope(proj[:, c0 + KV_LORA:])
    kpe_ref[...] = kpe[:, :QK_ROPE]
    kpe_b = kpe.astype(BF16)
    for h in range(MLA_HEADS):
        a = h * QK_PAD
        q_ref[:, a:a + QK_NOPE] = q[:, a:a + QK_NOPE].astype(BF16)
        q_ref[:, a + QK_NOPE:a + QK_PAD] = rope(q[:, a + QK_NOPE:a + QK_PAD]).astype(BF16)
        k_ref[:, a:a + QK_NOPE] = kv[:, h * QK_NOPE:(h + 1) * QK_NOPE].astype(BF16)
        k_ref[:, a + QK_NOPE:a + QK_PAD] = kpe_b
    v_ref[...] = kv[:, MLA_HEADS * QK_NOPE:].astype(BF16)


def _project(x2, pos, inv, w_in, gq, wq, gkv, wkv, tn):
    n = x2.shape[0]
    row = lambda w: pl.BlockSpec((tn, w), lambda i: (i, 0))
    return pl.pallas_call(
        _proj_body,
        grid=(n // tn,),
        in_specs=[row(D_MODEL), row(1), _const(inv.shape), _const(w_in.shape), _const(gq.shape),
                  _const(wq.shape), _const(gkv.shape), _const(wkv.shape)],
        out_specs=[row(SSM_WIDTH), row(MLA_HEADS * QK_PAD), row(MLA_HEADS * QK_PAD),
                   row(MLA_WIDTH), row(KV_LORA), row(QK_ROPE)],
        out_shape=[jax.ShapeDtypeStruct((n, SSM_WIDTH), F32),
                   jax.ShapeDtypeStruct((n, MLA_HEADS * QK_PAD), BF16),
                   jax.ShapeDtypeStruct((n, MLA_HEADS * QK_PAD), BF16),
                   jax.ShapeDtypeStruct((n, MLA_WIDTH), BF16),
                   jax.ShapeDtypeStruct((n, KV_LORA), F32),
                   jax.ShapeDtypeStruct((n, QK_ROPE), F32)],
        compiler_params=_params("parallel"),
        name="proj",
    )(x2, pos, inv, w_in, gq, wq, gkv, wkv)


def _s5_prep_body(ar_row, ai_row, ar_col, ai_col, ldt_ref, bt_re, bt_im, ct_re, ct_im,
                  w_ref, vre_ref, vim_ref, zr_ref, zi_ref, are_ref, aim_ref):
    dt = jnp.exp(ldt_ref[0])
    arr, air = ar_row[0], ai_row[0]
    arc, aic = ar_col[0], ai_col[0]

    mag = jnp.exp(arr * dt)
    lr, li = mag * jnp.cos(air * dt), mag * jnp.sin(air * dt)
    nr, ni = lr - 1.0, li
    den = arr * arr + air * air
    f_re, f_im = (nr * arr + ni * air) / den, (ni * arr - nr * air) / den
    br, bi = bt_re[0], bt_im[0]
    bb_re = f_re * br - f_im * bi
    bb_im = f_re * bi + f_im * br

    lag = (lax.broadcasted_iota(jnp.int32, (1, S5_TW), 1) // SSM_GROUP).astype(F32)
    magc = jnp.exp(arc * dt * lag)
    pw_re, pw_im = magc * jnp.cos(aic * dt * lag), magc * jnp.sin(aic * dt * lag)
    cr, ci = ct_re[0], ct_im[0]
    q_re = cr * pw_re - ci * pw_im
    q_im = cr * pw_im + ci * pw_re

    magl = jnp.exp(arc * dt)
    lrc, lic = magl * jnp.cos(aic * dt), magl * jnp.sin(aic * dt)
    zr_ref[0] = (q_re * lrc - q_im * lic).astype(BF16)
    zi_ref[0] = (-(q_re * lic + q_im * lrc)).astype(BF16)

    lane = lax.broadcasted_iota(jnp.int32, (1, S5_TW), 1)
    for i in range(S5_T):
        if i == 0:
            s_re, s_im = q_re, q_im
        else:
            keep = lane >= i * SSM_GROUP
            s_re = jnp.where(keep, pltpu.roll(q_re, i * SSM_GROUP, 1), 0.0)
            s_im = jnp.where(keep, pltpu.roll(q_im, i * SSM_GROUP, 1), 0.0)
        blk = _dot_hi(bb_re, s_re) - _dot_hi(bb_im, s_im)
        w_ref[0, i * SSM_GROUP:(i + 1) * SSM_GROUP, :] = blk.astype(BF16)

    back = (S5_T - 1 - lax.broadcasted_iota(jnp.int32, (S5_TW, 1), 0) // SSM_GROUP).astype(F32)
    magv = jnp.exp(arr * dt * back)
    pv_re, pv_im = magv * jnp.cos(air * dt * back), magv * jnp.sin(air * dt * back)
    bbt_re = jnp.concatenate([bb_re] * S5_T, axis=0)
    bbt_im = jnp.concatenate([bb_im] * S5_T, axis=0)
    vre_ref[0] = (pv_re * bbt_re - pv_im * bbt_im).astype(BF16)
    vim_ref[0] = (pv_re * bbt_im + pv_im * bbt_re).astype(BF16)

    magt = jnp.exp(arr * dt * float(S5_T))
    are_ref[0] = magt * jnp.cos(air * dt * float(S5_T))
    aim_ref[0] = magt * jnp.sin(air * dt * float(S5_T))


def _s5_prep(a_re, a_im, b_re, b_im, c_re, c_im, log_dt):
    g, p, h = SSM_GROUPS, SSM_STATE, SSM_GROUP
    tile = lambda c: jnp.tile(jnp.transpose(c, (0, 2, 1)), (1, 1, S5_T))
    args = (a_re.reshape(g, 1, p), a_im.reshape(g, 1, p), a_re.reshape(g, p, 1), a_im.reshape(g, p, 1),
            log_dt.reshape(g, 1, 1), jnp.transpose(b_re, (0, 2, 1)), jnp.transpose(b_im, (0, 2, 1)),
            tile(c_re), tile(c_im))
    blk = lambda a: pl.BlockSpec((1,) + a.shape[1:], lambda i: (i, 0, 0))
    outs = [((g, S5_TW, S5_TW), BF16), ((g, S5_TW, p), BF16), ((g, S5_TW, p), BF16),
            ((g, p, S5_TW), BF16), ((g, p, S5_TW), BF16), ((g, 1, p), F32), ((g, 1, p), F32)]
    return pl.pallas_call(
        _s5_prep_body,
        grid=(g,),
        in_specs=[blk(a) for a in args],
        out_specs=[pl.BlockSpec((1,) + s[1:], lambda i: (i, 0, 0)) for s, _ in outs],
        out_shape=[jax.ShapeDtypeStruct(s, d) for s, d in outs],
        compiler_params=_params("parallel"),
        name="s5_prep",
    )(*args)


def _s5_scan_body(up_ref, w_ref, vre_ref, vim_ref, zr_ref, zi_ref, are_ref, aim_ref,
                  h0r_ref, h0i_ref, y_ref, fr_ref, fi_ref,
                  sre, sim, xre, xim, cr, ci, *, nseq, ct):
    t = pl.program_id(0)

    @pl.when(t == 0)
    def _():
        cr[...] = h0r_ref[...]
        ci[...] = h0i_ref[...]

    def to_state(g, c):
        u = up_ref[g]
        sre[g] = _dot(u, vre_ref[g])
        sim[g] = _dot(u, vim_ref[g])
        return c

    lax.fori_loop(0, SSM_GROUPS, to_state, 0)

    a_re, a_im = are_ref[...], aim_ref[...]

    def rows(c):
        return pl.ds(c, 1) if nseq == 1 else pl.ds(c, nseq, stride=ct)

    def step(c, carry):
        x_re, x_im = carry
        r = rows(c)
        xre[:, r, :] = x_re
        xim[:, r, :] = x_im
        n_re = a_re * x_re - a_im * x_im + sre[:, r, :]
        n_im = a_re * x_im + a_im * x_re + sim[:, r, :]
        return n_re, n_im

    x_re, x_im = lax.fori_loop(0, ct, step, (cr[...], ci[...]))
    cr[...] = x_re
    ci[...] = x_im
    fr_ref[...] = x_re
    fi_ref[...] = x_im

    def to_out(g, c):
        y = _dot(up_ref[g], w_ref[g])
        y += _dot(xre[g].astype(BF16), zr_ref[g])
        y += _dot(xim[g].astype(BF16), zi_ref[g])
        y_ref[g] = y
        return c

    lax.fori_loop(0, SSM_GROUPS, to_out, 0)


def _s5_scan(up, ops, h0r, h0i, nseq, ct):
    w, vre, vim, zr, zi, are, aim = ops
    g, nc, _ = up.shape
    cps = nc // nseq
    assert nseq == 1 or ct == cps
    rows = nseq * ct
    tile = pl.BlockSpec((g, rows, S5_TW), lambda t: (0, t, 0))
    st = pl.BlockSpec((g, nseq, SSM_STATE), lambda t: (0, 0, 0))
    kern = functools.partial(_s5_scan_body, nseq=nseq, ct=ct)
    return pl.pallas_call(
        kern,
        grid=(cps // ct,),
        in_specs=[tile] + [_const(a.shape) for a in (w, vre, vim, zr, zi, are, aim)] + [st, st],
        out_specs=[tile, st, st],
        out_shape=[jax.ShapeDtypeStruct(up.shape, F32),
                   jax.ShapeDtypeStruct((g, nseq, SSM_STATE), F32),
                   jax.ShapeDtypeStruct((g, nseq, SSM_STATE), F32)],
        scratch_shapes=[pltpu.VMEM((g, rows, SSM_STATE), F32)] * 4
        + [pltpu.VMEM((g, nseq, SSM_STATE), F32)] * 2,
        compiler_params=_params("arbitrary"),
        name="s5_scan",
    )(up, w, vre, vim, zr, zi, are, aim, h0r, h0i)


def _glu_body(y_ref, u_ref, d_ref, w_ref, g_ref, o_ref):
    y = y_ref[...] + d_ref[...] * u_ref[...]
    gl = _dot(jax.nn.gelu(y).astype(BF16), w_ref[...])
    o = gl[:, :SSM_WIDTH] * jax.nn.sigmoid(gl[:, SSM_WIDTH:])
    o_ref[...] = _rms(o, g_ref[...]).astype(BF16)


def _glu(y, u, d, w, g, tn):
    n = y.shape[0]
    row = pl.BlockSpec((tn, SSM_WIDTH), lambda i: (i, 0))
    return pl.pallas_call(
        _glu_body,
        grid=(n // tn,),
        in_specs=[row, row, _const(d.shape), _const(w.shape), _const(g.shape)],
        out_specs=row,
        out_shape=jax.ShapeDtypeStruct((n, SSM_WIDTH), BF16),
        compiler_params=_params("parallel"),
        name="glu",
    )(y, u, d, w, g)


def _attn_body(q_ref, k_ref, v_ref, o_ref, m_sc, l_sc, acc_sc, *, tq):
    i = pl.program_id(1)
    q = q_ref[...]
    m_sc[...] = jnp.full(m_sc.shape, -jnp.inf, F32)
    l_sc[...] = jnp.zeros(l_sc.shape, F32)
    acc_sc[...] = jnp.zeros(acc_sc.shape, F32)

    def step(j, masked):
        r = pl.ds(pl.multiple_of(j * tq, tq), tq)
        s = _dot_nt(q, k_ref[r, :]) * MLA_SCALE
        if masked:
            qc = lax.broadcasted_iota(jnp.int32, s.shape, 0) // CHUNK
            kc = lax.broadcasted_iota(jnp.int32, s.shape, 1) // CHUNK
            s = jnp.where(kc <= qc, s, NEG_INF)
        m_prev = m_sc[...]
        m_new = jnp.maximum(m_prev, jnp.max(s, -1, keepdims=True))
        p = jnp.exp(s - m_new)
        alpha = jnp.exp(m_prev - m_new)
        l_sc[...] = alpha * l_sc[...] + jnp.sum(p, -1, keepdims=True)
        acc_sc[...] = alpha * acc_sc[...] + _dot(p.astype(BF16), v_ref[r, :])
        m_sc[...] = m_new

    def full_step(j, c):
        step(j, False)
        return c

    lax.fori_loop(0, i, full_step, 0)
    step(i, True)
    o_ref[...] = acc_sc[...] / l_sc[...]


def _attention(q, k, v, tq):
    n = q.shape[0]
    assert tq % CHUNK == 0
    kern = functools.partial(_attn_body, tq=tq)
    return pl.pallas_call(
        kern,
        grid=(MLA_HEADS, n // tq),
        in_specs=[pl.BlockSpec((tq, QK_PAD), lambda h, i: (i, h)),
                  pl.BlockSpec((n, QK_PAD), lambda h, i: (0, h)),
                  pl.BlockSpec((n, V_HEAD), lambda h, i: (0, h))],
        out_specs=pl.BlockSpec((tq, V_HEAD), lambda h, i: (i, h)),
        out_shape=jax.ShapeDtypeStruct((n, MLA_WIDTH), F32),
        scratch_shapes=[pltpu.VMEM((tq, 1), F32), pltpu.VMEM((tq, 1), F32),
                        pltpu.VMEM((tq, V_HEAD), F32)],
        compiler_params=_params("arbitrary", "arbitrary"),
        name="attn",
    )(q, k, v)


def _attn_cached_body(q_ref, kn_ref, vn_ref, ckv_ref, kpe_ref, wkv_ref, o_ref, *, past, seq):
    kvp = _dot(ckv_ref[0].astype(BF16), wkv_ref[...])
    kpe = kpe_ref[0].astype(BF16)
    qc = (past + lax.broadcasted_iota(jnp.int32, (seq, 1), 0)) // CHUNK
    kc_past = lax.broadcasted_iota(jnp.int32, (1, past), 1) // CHUNK
    kc_new = (past + lax.broadcasted_iota(jnp.int32, (1, seq), 1)) // CHUNK
    for h in range(MLA_HEADS):
        a = h * QK_PAD
        qh = q_ref[:, a:a + QK_PAD]
        kn_h = kvp[:, h * QK_NOPE:(h + 1) * QK_NOPE].astype(BF16)
        s_p = (_dot_nt(qh[:, :QK_NOPE], kn_h) + _dot_nt(qh[:, QK_NOPE:], kpe)) * MLA_SCALE
        s_n = _dot_nt(qh, kn_ref[:, a:a + QK_PAD]) * MLA_SCALE
        s_p = jnp.where(kc_past <= qc, s_p, NEG_INF)
        s_n = jnp.where(kc_new <= qc, s_n, NEG_INF)
        m = jnp.maximum(jnp.max(s_p, -1, keepdims=True), jnp.max(s_n, -1, keepdims=True))
        e_p, e_n = jnp.exp(s_p - m), jnp.exp(s_n - m)
        l = jnp.sum(e_p, -1, keepdims=True) + jnp.sum(e_n, -1, keepdims=True)
        v0 = MLA_HEADS * QK_NOPE + h * V_HEAD
        o = _dot((e_p / l).astype(BF16), kvp[:, v0:v0 + V_HEAD].astype(BF16))
        o += _dot((e_n / l).astype(BF16), vn_ref[:, h * V_HEAD:(h + 1) * V_HEAD])
        o_ref[:, h * V_HEAD:(h + 1) * V_HEAD] = o


def _attention_cached(q, k, v, ckv_cache, kpe_cache, wkv, seq):
    n = q.shape[0]
    nb, past, _ = ckv_cache.shape
    row = lambda w: pl.BlockSpec((seq, w), lambda b: (b, 0))
    kern = functools.partial(_attn_cached_body, past=past, seq=seq)
    return pl.pallas_call(
        kern,
        grid=(nb,),
        in_specs=[row(MLA_HEADS * QK_PAD), row(MLA_HEADS * QK_PAD), row(MLA_WIDTH),
                  pl.BlockSpec((1, past, KV_LORA), lambda b: (b, 0, 0)),
                  pl.BlockSpec((1, past, 128), lambda b: (b, 0, 0)),
                  _const(wkv.shape)],
        out_specs=row(MLA_WIDTH),
        out_shape=jax.ShapeDtypeStruct((n, MLA_WIDTH), F32),
        compiler_params=_params("parallel"),
        name="attn_cached",
    )(q, k, v, ckv_cache, kpe_cache, wkv)


def _mix_body(os_ref, om_ref, x_ref, gm_ref, wo_ref, lng_ref, lnb_ref, wxq_ref, h1_ref, qx_ref):
    om = _rms(om_ref[...], gm_ref[...]).astype(BF16)
    a = _dot(os_ref[...], wo_ref[:SSM_WIDTH, :]) + _dot(om, wo_ref[SSM_WIDTH:, :])
    h1 = _ln(ALPHA * x_ref[...] + a, lng_ref[0:1, :], lnb_ref[0:1, :])
    h1_ref[...] = h1
    qx_ref[...] = _dot(h1.astype(BF16), wxq_ref[...]).astype(BF16)


def _mix(o_ssm, o_mla, x2, g_mla, w_o, ln_g, ln_b, w_xq, tn):
    n = x2.shape[0]
    row = lambda w: pl.BlockSpec((tn, w), lambda i: (i, 0))
    return pl.pallas_call(
        _mix_body,
        grid=(n // tn,),
        in_specs=[row(SSM_WIDTH), row(MLA_WIDTH), row(D_MODEL), _const(g_mla.shape), _const(w_o.shape),
                  _const(ln_g.shape), _const(ln_b.shape), _const(w_xq.shape)],
        out_specs=[row(D_MODEL), row(D_MODEL)],
        out_shape=[jax.ShapeDtypeStruct((n, D_MODEL), F32), jax.ShapeDtypeStruct((n, D_MODEL), BF16)],
        compiler_params=_params("parallel"),
        name="mix",
    )(o_ssm, o_mla, x2, g_mla, w_o, ln_g, ln_b, w_xq)


def _mem_attn_body(qx_ref, mk_ref, mv_ref, o_ref):
    mk = mk_ref[0].astype(BF16)
    mv = mv_ref[0].astype(BF16)
    for h in range(X_HEADS):
        a = h * X_HEAD_DIM
        s = _dot_nt(qx_ref[:, a:a + X_HEAD_DIM], mk[:, a:a + X_HEAD_DIM]) * (X_HEAD_DIM ** -0.5)
        e = jnp.exp(s - jnp.max(s, -1, keepdims=True))
        p = e / jnp.sum(e, -1, keepdims=True)
        o_ref[:, a:a + X_HEAD_DIM] = _dot(p.astype(BF16), mv[:, a:a + X_HEAD_DIM]).astype(BF16)


def _mem_attn(qx, mem_k, mem_v, tq):
    n = qx.shape[0]
    nb = mem_k.shape[0]
    tiles = n // nb // tq
    row = pl.BlockSpec((tq, D_MODEL), lambda b, i: (b * tiles + i, 0))
    mem = pl.BlockSpec((1, N_MEM, D_MODEL), lambda b, i: (b, 0, 0))
    return pl.pallas_call(
        _mem_attn_body,
        grid=(nb, tiles),
        in_specs=[row, mem, mem],
        out_specs=row,
        out_shape=jax.ShapeDtypeStruct((n, D_MODEL), BF16),
        compiler_params=_params("parallel", "parallel"),
        name="mem_attn",
    )(qx, mem_k, mem_v)


def _mlp_body(h1_ref, ox_ref, wxo_ref, lng_ref, lnb_ref, w1_ref, w2_ref, y_ref, *, ff_blk):
    h2 = _ln(ALPHA * h1_ref[...] + _dot(ox_ref[...], wxo_ref[...]), lng_ref[1:2, :], lnb_ref[1:2, :])
    hb = h2.astype(BF16)
    acc = jnp.zeros(h2.shape, F32)
    for c in range(0, D_FF, ff_blk):
        z = jnp.maximum(_dot(hb, w1_ref[:, c:c + ff_blk]), 0.0)
        acc += _dot((z * z).astype(BF16), w2_ref[c:c + ff_blk, :])
    y_ref[...] = _ln(ALPHA * h2 + acc, lng_ref[2:3, :], lnb_ref[2:3, :])


def _mlp(h1, ox, w_xo, ln_g, ln_b, w1, w2, tn):
    n = h1.shape[0]
    row = pl.BlockSpec((tn, D_MODEL), lambda i: (i, 0))
    kern = functools.partial(_mlp_body, ff_blk=1024)
    return pl.pallas_call(
        kern,
        grid=(n // tn,),
        in_specs=[row, row, _const(w_xo.shape), _const(ln_g.shape), _const(ln_b.shape),
                  _const(w1.shape), _const(w2.shape)],
        out_specs=row,
        out_shape=jax.ShapeDtypeStruct((n, D_MODEL), F32),
        compiler_params=_params("parallel"),
        name="mlp",
    )(h1, ox, w_xo, ln_g, ln_b, w1, w2)


def _mem_kv_body(mem_ref, wk_ref, wv_ref, k_ref, v_ref):
    m = mem_ref[...].astype(BF16)
    k_ref[...] = _dot(m, wk_ref[...])
    v_ref[...] = _dot(m, wv_ref[...])


def _mem_kv(mem2, wk, wv):
    n = mem2.shape[0]
    out = jax.ShapeDtypeStruct((n, D_MODEL), F32)
    return pl.pallas_call(
        _mem_kv_body,
        grid=(1,),
        in_specs=[_const(mem2.shape), _const(wk.shape), _const(wv.shape)],
        out_specs=[_const((n, D_MODEL))] * 2,
        out_shape=[out, out],
        compiler_params=_params("arbitrary"),
        name="mem_kv",
    )(mem2, wk, wv)


def _to_chunks(u):
    n = u.shape[0]
    u4 = u.reshape(n // S5_T, S5_T, SSM_GROUPS, SSM_GROUP)
    return jnp.transpose(u4, (2, 0, 1, 3)).reshape(SSM_GROUPS, n // S5_T, S5_TW)


def _from_chunks(y):
    g, nc, _ = y.shape
    y4 = y.reshape(g, nc, S5_T, SSM_GROUP)
    return jnp.transpose(y4, (1, 2, 0, 3)).reshape(nc * S5_T, g * SSM_GROUP)


def _layer(x2, pos, nseq, h0r, h0i, mem_k, mem_v, caches, wts, s5_ops, tn, scan_ct, mem_tq):
    n = x2.shape[0]
    seq = n // nseq
    u, q, k, v, ckv, kpe = _project(x2, pos, wts["inv"], wts["w_in"], wts["g_q"], wts["w_q"],
                                    wts["g_kv"], wts["w_kv"], tn)
    yc, fr, fi = _s5_scan(_to_chunks(u.astype(BF16)), s5_ops, h0r, h0i, nseq, scan_ct)
    o_ssm = _glu(_from_chunks(yc), u, wts["d_skip"], wts["w_glu"], wts["g_out_ssm"], tn)
    if caches is None:
        o_mla = _attention(q, k, v, tq=512)
    else:
        o_mla = _attention_cached(q, k, v, caches[0], caches[1], wts["w_kv"], seq)
    h1, qx = _mix(o_ssm, o_mla, x2, wts["g_out_mla"], wts["w_o"], wts["ln_g"], wts["ln_b"], wts["w_xq"], tn)
    ox = _mem_attn(qx, mem_k, mem_v, mem_tq)
    y = _mlp(h1, ox, wts["w_xo"], wts["ln_g"], wts["ln_b"], wts["w_ff1"], wts["w_ff2"], tn)
    state = lambda f: jnp.transpose(f, (1, 0, 2))
    return y, ckv, kpe, state(fr), state(fi)


def kernel(x_prompt, x_sample, mem_prompt, cache_mla_ckv, cache_mla_kpe, state_ssm_re, state_ssm_im, cache_mem_k, cache_mem_v, w_in, g_q, w_q_up, g_kv, w_kv_up, a_re, a_im, b_re, b_im, c_re, c_im, d_skip, log_dt, w_glu, g_out_ssm, g_out_mla, w_o, w_xq, w_xk, w_xv, w_xo, w_ff1, w_ff2, ln_g, ln_b):
    assert w_in.shape[0] == DEPTH == 1
    nbp, sp, _ = x_prompt.shape
    nbs, sd, _ = x_sample.shape
    past = cache_mla_ckv.shape[2]
    assert nbp == 1

    wq = jnp.pad(w_q_up[0], ((0, 0), (0, 0), (0, QK_PAD - QK_NOPE - QK_ROPE)))
    wkv = jnp.concatenate([w_kv_up[0][:, :, :QK_NOPE].reshape(KV_LORA, -1),
                           w_kv_up[0][:, :, QK_NOPE:].reshape(KV_LORA, -1)], axis=1)
    inv = ROPE_THETA ** (-jnp.arange(ROPE_HALF, dtype=F32) / ROPE_HALF)
    wts = {
        "inv": jnp.tile(inv, 128 // ROPE_HALF).reshape(1, 128),
        "w_in": jnp.pad(w_in[0], ((0, 0), (0, 128 - QK_ROPE))).astype(BF16),
        "g_q": g_q[0].reshape(1, -1),
        "w_q": wq.reshape(Q_LORA, MLA_HEADS * QK_PAD).astype(BF16),
        "g_kv": g_kv[0].reshape(1, -1),
        "w_kv": wkv.astype(BF16),
        "d_skip": d_skip[0].reshape(1, -1),
        "w_glu": w_glu[0].astype(BF16),
        "g_out_ssm": g_out_ssm[0].reshape(1, -1),
        "g_out_mla": g_out_mla[0].reshape(1, -1),
        "w_o": w_o[0].astype(BF16),
        "w_xq": w_xq[0].reshape(D_MODEL, D_MODEL).astype(BF16),
        "w_xo": w_xo[0].reshape(D_MODEL, D_MODEL).astype(BF16),
        "w_ff1": w_ff1[0].astype(BF16),
        "w_ff2": w_ff2[0].astype(BF16),
        "ln_g": ln_g[0],
        "ln_b": ln_b[0],
    }
    s5_ops = _s5_prep(a_re[0], a_im[0], b_re[0], b_im[0], c_re[0], c_im[0], log_dt[0])

    mk, mv = _mem_kv(mem_prompt.reshape(nbp * N_MEM, D_MODEL),
                     w_xk[0].reshape(D_MODEL, D_MODEL).astype(BF16),
                     w_xv[0].reshape(D_MODEL, D_MODEL).astype(BF16))
    zero = jnp.zeros((SSM_GROUPS, nbp, SSM_STATE), F32)
    pos_p = jnp.arange(sp, dtype=F32).reshape(sp, 1)
    yp, ckv_p, kpe_p, sre_p, sim_p = _layer(
        x_prompt.reshape(sp, D_MODEL), pos_p, nbp, zero, zero,
        mk.reshape(nbp, N_MEM, D_MODEL), mv.reshape(nbp, N_MEM, D_MODEL), None,
        wts, s5_ops, tn=512, scan_ct=128, mem_tq=512)

    pos_s = jnp.tile(past + jnp.arange(sd, dtype=F32), nbs).reshape(nbs * sd, 1)
    caches = (cache_mla_ckv[0], jnp.pad(cache_mla_kpe[0], ((0, 0), (0, 0), (0, 128 - QK_ROPE))))
    ys, ckv_s, kpe_s, sre_s, sim_s = _layer(
        x_sample.reshape(nbs * sd, D_MODEL), pos_s, nbs,
        jnp.transpose(state_ssm_re[0], (1, 0, 2)), jnp.transpose(state_ssm_im[0], (1, 0, 2)),
        cache_mem_k[0].reshape(nbs, N_MEM, D_MODEL), cache_mem_v[0].reshape(nbs, N_MEM, D_MODEL), caches,
        wts, s5_ops, tn=512, scan_ct=sd // S5_T, mem_tq=sd)

    return (yp.reshape(nbp, sp, D_MODEL), ys.reshape(nbs, sd, D_MODEL),
            ckv_p.reshape(1, nbp, sp, KV_LORA), kpe_p.reshape(1, nbp, sp, QK_ROPE),
            sre_p.reshape(1, nbp, SSM_GROUPS, SSM_STATE), sim_p.reshape(1, nbp, SSM_GROUPS, SSM_STATE),
            mk.reshape(1, nbp, N_MEM, X_HEADS, X_HEAD_DIM), mv.reshape(1, nbp, N_MEM, X_HEADS, X_HEAD_DIM),
            ckv_s.reshape(1, nbs, sd, KV_LORA), kpe_s.reshape(1, nbs, sd, QK_ROPE),
            sre_s.reshape(1, nbs, SSM_GROUPS, SSM_STATE), sim_s.reshape(1, nbs, SSM_GROUPS, SSM_STATE))
```

```python
import functools
import math

import jax
import jax.numpy as jnp
from jax import lax
from jax.experimental import pallas as pl
from jax.experimental.pallas import tpu as pltpu

F32 = jnp.float32
BF16 = jnp.bfloat16

D_MODEL = 1024
DEPTH = 1
CHUNK = 64
SSM_WIDTH = 512
SSM_GROUP = 16
SSM_GROUPS = 32
SSM_STATE = 64
MLA_HEADS = 4
QK_NOPE = 128
QK_ROPE = 64
V_HEAD = 128
MLA_WIDTH = MLA_HEADS * V_HEAD
Q_LORA = 384
KV_LORA = 256
ROPE_THETA = 10000.0
MLA_SCALE = (QK_NOPE + QK_ROPE) ** -0.5
N_MEM = 256
X_HEADS = 4
X_HEAD_DIM = D_MODEL // X_HEADS
D_FF = 4 * D_MODEL
ALPHA = (2 * DEPTH) ** 0.25
EPS = 1e-5
NEG_INF = -1e30

QK_PAD = 256
S5_T = 16
S5_TW = S5_T * SSM_GROUP
ROPE_HALF = QK_ROPE // 2
Q_SCALE = MLA_SCALE * math.log2(math.e)
ATTN_T = 512
ATTN_HEADS = 2
VMEM_LIMIT = 56 * 1024 * 1024

_NT = (((1,), (1,)), ((), ()))


def _rms(x, g):
    return x * lax.rsqrt(jnp.mean(x * x, -1, keepdims=True) + EPS) * g


def _ln(x, g, b):
    mu = jnp.mean(x, -1, keepdims=True)
    xc = x - mu
    var = jnp.mean(xc * xc, -1, keepdims=True)
    return xc * lax.rsqrt(var + EPS) * g + b


def _dot(a, b):
    return jnp.dot(a, b, preferred_element_type=F32)


def _dot_nt(a, b):
    return lax.dot_general(a, b, _NT, preferred_element_type=F32)


def _dot_hi(a, b):
    return jnp.dot(a, b, preferred_element_type=F32, precision=lax.Precision.HIGHEST)


def _params(*sem):
    return pltpu.CompilerParams(dimension_semantics=sem, vmem_limit_bytes=VMEM_LIMIT)


def _const(shape):
    n = len(shape)
    return pl.BlockSpec(shape, lambda *_: (0,) * n)


def _proj_body(x_ref, pos_ref, inv_ref, w_in_ref, gq_ref, wq_ref, gkv_ref, wk_ref, wv_ref,
               u_ref, q_ref, k_ref, v_ref, ckv_ref, kpe_ref, *, v_transposed):
    x = x_ref[...].astype(BF16)
    proj = _dot(x, w_in_ref[...])
    u_ref[...] = proj[:, :SSM_WIDTH]
    cq = _rms(proj[:, SSM_WIDTH:SSM_WIDTH + Q_LORA], gq_ref[...])
    q = _dot(cq.astype(BF16), wq_ref[...]) * Q_SCALE
    c0 = SSM_WIDTH + Q_LORA
    ckv = _rms(proj[:, c0:c0 + KV_LORA], gkv_ref[...])
    ckv_ref[...] = ckv
    ckv_b = ckv.astype(BF16)
    kn = _dot(ckv_b, wk_ref[...])
    if v_transposed:
        for c in range(v_ref.shape[0]):
            rows = ckv_b[c * ATTN_T:(c + 1) * ATTN_T, :]
            v_ref[c] = _dot_nt(wv_ref[...], rows).astype(BF16)
    else:
        v_ref[...] = _dot(ckv_b, wv_ref[...]).astype(BF16)

    ang = pos_ref[...] * inv_ref[...]
    lane = lax.broadcasted_iota(jnp.int32, (1, 128), 1)
    cs, sn = jnp.cos(ang), jnp.sin(ang)
    cos_t = jnp.where(lane < QK_ROPE, cs, 0.0)
    sin_a = jnp.where(lane < ROPE_HALF, -sn, 0.0)
    sin_b = jnp.where((lane >= ROPE_HALF) & (lane < QK_ROPE), sn, 0.0)

    def rope(c2):
        return (c2 * cos_t + pltpu.roll(c2, 128 - ROPE_HALF, 1) * sin_a
                + pltpu.roll(c2, ROPE_HALF, 1) * sin_b)

    kpe = rope(proj[:, c0 + KV_LORA:])
    kpe_ref[...] = kpe[:, :QK_ROPE]
    kpe_b = kpe.astype(BF16)
    for h in range(MLA_HEADS):
        a = h * QK_PAD
        q_ref[:, a:a + QK_NOPE] = q[:, a:a + QK_NOPE].astype(BF16)
        q_ref[:, a + QK_NOPE:a + QK_PAD] = rope(q[:, a + QK_NOPE:a + QK_PAD]).astype(BF16)
        k_ref[:, a:a + QK_NOPE] = kn[:, h * QK_NOPE:(h + 1) * QK_NOPE].astype(BF16)
        k_ref[:, a + QK_NOPE:a + QK_PAD] = kpe_b


def _project(x2, pos, inv, w_in, gq, wq, gkv, wk, wv, tn, v_transposed):
    n = x2.shape[0]
    row = lambda w: pl.BlockSpec((tn, w), lambda i: (i, 0))
    if v_transposed:
        assert tn % ATTN_T == 0
        v_spec = pl.BlockSpec((tn // ATTN_T, MLA_WIDTH, ATTN_T), lambda i: (i, 0, 0))
        v_shape = jax.ShapeDtypeStruct((n // ATTN_T, MLA_WIDTH, ATTN_T), BF16)
    else:
        v_spec, v_shape = row(MLA_WIDTH), jax.ShapeDtypeStruct((n, MLA_WIDTH), BF16)
    return pl.pallas_call(
        functools.partial(_proj_body, v_transposed=v_transposed),
        grid=(n // tn,),
        in_specs=[row(D_MODEL), row(1), _const(inv.shape), _const(w_in.shape), _const(gq.shape),
                  _const(wq.shape), _const(gkv.shape), _const(wk.shape), _const(wv.shape)],
        out_specs=[row(SSM_WIDTH), row(MLA_HEADS * QK_PAD), row(MLA_HEADS * QK_PAD),
                   v_spec, row(KV_LORA), row(QK_ROPE)],
        out_shape=[jax.ShapeDtypeStruct((n, SSM_WIDTH), F32),
                   jax.ShapeDtypeStruct((n, MLA_HEADS * QK_PAD), BF16),
                   jax.ShapeDtypeStruct((n, MLA_HEADS * QK_PAD), BF16),
                   v_shape,
                   jax.ShapeDtypeStruct((n, KV_LORA), F32),
                   jax.ShapeDtypeStruct((n, QK_ROPE), F32)],
        compiler_params=_params("parallel"),
        name="proj",
    )(x2, pos, inv, w_in, gq, wq, gkv, wk, wv)


def _s5_prep_body(ar_row, ai_row, ar_col, ai_col, ldt_ref, bt_re, bt_im, ct_re, ct_im,
                  w_ref, vre_ref, vim_ref, zr_ref, zi_ref, are_ref, aim_ref):
    dt = jnp.exp(ldt_ref[0])
    arr, air = ar_row[0], ai_row[0]
    arc, aic = ar_col[0], ai_col[0]

    mag = jnp.exp(arr * dt)
    lr, li = mag * jnp.cos(air * dt), mag * jnp.sin(air * dt)
    nr, ni = lr - 1.0, li
    den = arr * arr + air * air
    f_re, f_im = (nr * arr + ni * air) / den, (ni * arr - nr * air) / den
    br, bi = bt_re[0], bt_im[0]
    bb_re = f_re * br - f_im * bi
    bb_im = f_re * bi + f_im * br

    lag = (lax.broadcasted_iota(jnp.int32, (1, S5_TW), 1) // SSM_GROUP).astype(F32)
    magc = jnp.exp(arc * dt * lag)
    pw_re, pw_im = magc * jnp.cos(aic * dt * lag), magc * jnp.sin(aic * dt * lag)
    cr, ci = ct_re[0], ct_im[0]
    q_re = cr * pw_re - ci * pw_im
    q_im = cr * pw_im + ci * pw_re

    magl = jnp.exp(arc * dt)
    lrc, lic = magl * jnp.cos(aic * dt), magl * jnp.sin(aic * dt)
    zr_ref[0] = (q_re * lrc - q_im * lic).astype(BF16)
    zi_ref[0] = (-(q_re * lic + q_im * lrc)).astype(BF16)

    lane = lax.broadcasted_iota(jnp.int32, (1, S5_TW), 1)
    for i in range(S5_T):
        if i == 0:
            s_re, s_im = q_re, q_im
        else:
            keep = lane >= i * SSM_GROUP
            s_re = jnp.where(keep, pltpu.roll(q_re, i * SSM_GROUP, 1), 0.0)
            s_im = jnp.where(keep, pltpu.roll(q_im, i * SSM_GROUP, 1), 0.0)
        blk = _dot_hi(bb_re, s_re) - _dot_hi(bb_im, s_im)
        w_ref[0, i * SSM_GROUP:(i + 1) * SSM_GROUP, :] = blk.astype(BF16)

    back = (S5_T - 1 - lax.broadcasted_iota(jnp.int32, (S5_TW, 1), 0) // SSM_GROUP).astype(F32)
    magv = jnp.exp(arr * dt * back)
    pv_re, pv_im = magv * jnp.cos(air * dt * back), magv * jnp.sin(air * dt * back)
    bbt_re = jnp.concatenate([bb_re] * S5_T, axis=0)
    bbt_im = jnp.concatenate([bb_im] * S5_T, axis=0)
    vre_ref[0] = (pv_re * bbt_re - pv_im * bbt_im).astype(BF16)
    vim_ref[0] = (pv_re * bbt_im + pv_im * bbt_re).astype(BF16)

    magt = jnp.exp(arr * dt * float(S5_T))
    are_ref[0] = magt * jnp.cos(air * dt * float(S5_T))
    aim_ref[0] = magt * jnp.sin(air * dt * float(S5_T))


def _s5_prep(a_re, a_im, b_re, b_im, c_re, c_im, log_dt):
    g, p, h = SSM_GROUPS, SSM_STATE, SSM_GROUP
    tile = lambda c: jnp.tile(jnp.transpose(c, (0, 2, 1)), (1, 1, S5_T))
    args = (a_re.reshape(g, 1, p), a_im.reshape(g, 1, p), a_re.reshape(g, p, 1), a_im.reshape(g, p, 1),
            log_dt.reshape(g, 1, 1), jnp.transpose(b_re, (0, 2, 1)), jnp.transpose(b_im, (0, 2, 1)),
            tile(c_re), tile(c_im))
    blk = lambda a: pl.BlockSpec((1,) + a.shape[1:], lambda i: (i, 0, 0))
    outs = [((g, S5_TW, S5_TW), BF16), ((g, S5_TW, p), BF16), ((g, S5_TW, p), BF16),
            ((g, p, S5_TW), BF16), ((g, p, S5_TW), BF16), ((g, 1, p), F32), ((g, 1, p), F32)]
    return pl.pallas_call(
        _s5_prep_body,
        grid=(g,),
        in_specs=[blk(a) for a in args],
        out_specs=[pl.BlockSpec((1,) + s[1:], lambda i: (i, 0, 0)) for s, _ in outs],
        out_shape=[jax.ShapeDtypeStruct(s, d) for s, d in outs],
        compiler_params=_params("parallel"),
        name="s5_prep",
    )(*args)


def _s5_scan_body(up_ref, w_ref, vre_ref, vim_ref, zr_ref, zi_ref, are_ref, aim_ref,
                  h0r_ref, h0i_ref, y_ref, fr_ref, fi_ref,
                  sre, sim, xre, xim, cr, ci, *, nseq, ct):
    t = pl.program_id(0)

    @pl.when(t == 0)
    def _():
        cr[...] = h0r_ref[...]
        ci[...] = h0i_ref[...]

    def to_state(g, c):
        u = up_ref[g]
        sre[g] = _dot(u, vre_ref[g])
        sim[g] = _dot(u, vim_ref[g])
        return c

    lax.fori_loop(0, SSM_GROUPS, to_state, 0)

    a_re, a_im = are_ref[...], aim_ref[...]

    def rows(c):
        return pl.ds(c, 1) if nseq == 1 else pl.ds(c, nseq, stride=ct)

    def step(c, carry):
        x_re, x_im = carry
        r = rows(c)
        xre[:, r, :] = x_re
        xim[:, r, :] = x_im
        n_re = a_re * x_re - a_im * x_im + sre[:, r, :]
        n_im = a_re * x_im + a_im * x_re + sim[:, r, :]
        return n_re, n_im

    x_re, x_im = lax.fori_loop(0, ct, step, (cr[...], ci[...]))
    cr[...] = x_re
    ci[...] = x_im
    fr_ref[...] = x_re
    fi_ref[...] = x_im

    def to_out(g, c):
        y = _dot(up_ref[g], w_ref[g])
        y += _dot(xre[g].astype(BF16), zr_ref[g])
        y += _dot(xim[g].astype(BF16), zi_ref[g])
        y_ref[g] = y
        return c

    lax.fori_loop(0, SSM_GROUPS, to_out, 0)


def _s5_scan(up, ops, h0r, h0i, nseq, ct):
    w, vre, vim, zr, zi, are, aim = ops
    g, nc, _ = up.shape
    cps = nc // nseq
    assert nseq == 1 or ct == cps
    rows = nseq * ct
    tile = pl.BlockSpec((g, rows, S5_TW), lambda t: (0, t, 0))
    st = pl.BlockSpec((g, nseq, SSM_STATE), lambda t: (0, 0, 0))
    kern = functools.partial(_s5_scan_body, nseq=nseq, ct=ct)
    return pl.pallas_call(
        kern,
        grid=(cps // ct,),
        in_specs=[tile] + [_const(a.shape) for a in (w, vre, vim, zr, zi, are, aim)] + [st, st],
        out_specs=[tile, st, st],
        out_shape=[jax.ShapeDtypeStruct(up.shape, F32),
                   jax.ShapeDtypeStruct((g, nseq, SSM_STATE), F32),
                   jax.ShapeDtypeStruct((g, nseq, SSM_STATE), F32)],
        scratch_shapes=[pltpu.VMEM((g, rows, SSM_STATE), F32)] * 4
        + [pltpu.VMEM((g, nseq, SSM_STATE), F32)] * 2,
        compiler_params=_params("arbitrary"),
        name="s5_scan",
    )(up, w, vre, vim, zr, zi, are, aim, h0r, h0i)


def _glu_body(y_ref, u_ref, d_ref, w_ref, g_ref, o_ref):
    y = y_ref[...] + d_ref[...] * u_ref[...]
    gl = _dot(jax.nn.gelu(y).astype(BF16), w_ref[...])
    o = gl[:, :SSM_WIDTH] * jax.nn.sigmoid(gl[:, SSM_WIDTH:])
    o_ref[...] = _rms(o, g_ref[...]).astype(BF16)


def _glu(y, u, d, w, g, tn):
    n = y.shape[0]
    row = pl.BlockSpec((tn, SSM_WIDTH), lambda i: (i, 0))
    return pl.pallas_call(
        _glu_body,
        grid=(n // tn,),
        in_specs=[row, row, _const(d.shape), _const(w.shape), _const(g.shape)],
        out_specs=row,
        out_shape=jax.ShapeDtypeStruct((n, SSM_WIDTH), BF16),
        compiler_params=_params("parallel"),
        name="glu",
    )(y, u, d, w, g)


def _attn_body(q_ref, k_ref, vt_ref, o_ref, m_sc, l_sc, acc_sc, *, heads):
    i = pl.program_id(1)
    t = ATTN_T
    m_sc[...] = jnp.full(m_sc.shape, -jnp.inf, F32)
    l_sc[...] = jnp.zeros(l_sc.shape, F32)
    acc_sc[...] = jnp.zeros(acc_sc.shape, F32)

    def step(j, masked):
        r = pl.ds(pl.multiple_of(j * t, t), t)
        for h in range(heads):
            s = _dot_nt(k_ref[r, h * QK_PAD:(h + 1) * QK_PAD],
                        q_ref[:, h * QK_PAD:(h + 1) * QK_PAD])
            if masked:
                kc = lax.broadcasted_iota(jnp.int32, s.shape, 0) // CHUNK
                qc = lax.broadcasted_iota(jnp.int32, s.shape, 1) // CHUNK
                s = jnp.where(kc <= qc, s, NEG_INF)
            m_prev = m_sc[h]
            m_new = jnp.maximum(m_prev, jnp.max(s, 0, keepdims=True))
            p = jnp.exp2(s - m_new)
            alpha = jnp.exp2(m_prev - m_new)
            l_sc[h] = alpha * l_sc[h] + jnp.sum(p, 0, keepdims=True)
            pv = _dot(vt_ref[j, h * V_HEAD:(h + 1) * V_HEAD, :], p.astype(BF16))
            acc_sc[h] = alpha * acc_sc[h] + pv
            m_sc[h] = m_new

    def full_step(j, c):
        step(j, False)
        return c

    lax.fori_loop(0, i, full_step, 0)
    step(i, True)
    for h in range(heads):
        o_ref[:, h * V_HEAD:(h + 1) * V_HEAD] = (acc_sc[h] / l_sc[h]).T


def _attention(q, k, vt, heads):
    n = q.shape[0]
    t = ATTN_T
    assert t % CHUNK == 0 and n % t == 0 and MLA_HEADS % heads == 0
    once = pl.Buffered(1)
    return pl.pallas_call(
        functools.partial(_attn_body, heads=heads),
        grid=(MLA_HEADS // heads, n // t),
        in_specs=[pl.BlockSpec((t, heads * QK_PAD), lambda g, i: (i, g)),
                  pl.BlockSpec((n, heads * QK_PAD), lambda g, i: (0, g), pipeline_mode=once),
                  pl.BlockSpec((n // t, heads * V_HEAD, t), lambda g, i: (0, g, 0), pipeline_mode=once)],
        out_specs=pl.BlockSpec((t, heads * V_HEAD), lambda g, i: (i, g)),
        out_shape=jax.ShapeDtypeStruct((n, MLA_WIDTH), F32),
        scratch_shapes=[pltpu.VMEM((heads, 1, t), F32), pltpu.VMEM((heads, 1, t), F32),
                        pltpu.VMEM((heads, V_HEAD, t), F32)],
        compiler_params=_params("arbitrary", "arbitrary"),
        name="attn",
    )(q, k, vt)


def _attn_cached_body(q_ref, kn_ref, vn_ref, ckv_ref, kpe_ref, wk_ref, wv_ref, o_ref, *, past, seq):
    ckv_b = ckv_ref[0].astype(BF16)
    knp = _dot(ckv_b, wk_ref[...])
    vp = _dot(ckv_b, wv_ref[...])
    kpe = kpe_ref[0].astype(BF16)
    qc = (past + lax.broadcasted_iota(jnp.int32, (seq, 1), 0)) // CHUNK
    kc_past = lax.broadcasted_iota(jnp.int32, (1, past), 1) // CHUNK
    kc_new = (past + lax.broadcasted_iota(jnp.int32, (1, seq), 1)) // CHUNK
    for h in range(MLA_HEADS):
        a = h * QK_PAD
        qh = q_ref[:, a:a + QK_PAD]
        kn_h = knp[:, h * QK_NOPE:(h + 1) * QK_NOPE].astype(BF16)
        s_p = _dot_nt(qh[:, :QK_NOPE], kn_h) + _dot_nt(qh[:, QK_NOPE:], kpe)
        s_n = _dot_nt(qh, kn_ref[:, a:a + QK_PAD])
        s_p = jnp.where(kc_past <= qc, s_p, NEG_INF)
        s_n = jnp.where(kc_new <= qc, s_n, NEG_INF)
        m = jnp.maximum(jnp.max(s_p, -1, keepdims=True), jnp.max(s_n, -1, keepdims=True))
        e_p, e_n = jnp.exp2(s_p - m), jnp.exp2(s_n - m)
        l = jnp.sum(e_p, -1, keepdims=True) + jnp.sum(e_n, -1, keepdims=True)
        o = _dot((e_p / l).astype(BF16), vp[:, h * V_HEAD:(h + 1) * V_HEAD].astype(BF16))
        o += _dot((e_n / l).astype(BF16), vn_ref[:, h * V_HEAD:(h + 1) * V_HEAD])
        o_ref[:, h * V_HEAD:(h + 1) * V_HEAD] = o


def _attention_cached(q, k, v, ckv_cache, kpe_cache, wk, wv, seq):
    n = q.shape[0]
    nb, past, _ = ckv_cache.shape
    row = lambda w: pl.BlockSpec((seq, w), lambda b: (b, 0))
    kern = functools.partial(_attn_cached_body, past=past, seq=seq)
    return pl.pallas_call(
        kern,
        grid=(nb,),
        in_specs=[row(MLA_HEADS * QK_PAD), row(MLA_HEADS * QK_PAD), row(MLA_WIDTH),
                  pl.BlockSpec((1, past, KV_LORA), lambda b: (b, 0, 0)),
                  pl.BlockSpec((1, past, 128), lambda b: (b, 0, 0)),
                  _const(wk.shape), _const(wv.shape)],
        out_specs=row(MLA_WIDTH),
        out_shape=jax.ShapeDtypeStruct((n, MLA_WIDTH), F32),
        compiler_params=_params("parallel"),
        name="attn_cached",
    )(q, k, v, ckv_cache, kpe_cache, wk, wv)


def _mix_body(os_ref, om_ref, x_ref, gm_ref, wo_ref, lng_ref, lnb_ref, wxq_ref, h1_ref, qx_ref):
    om = _rms(om_ref[...], gm_ref[...]).astype(BF16)
    a = _dot(os_ref[...], wo_ref[:SSM_WIDTH, :]) + _dot(om, wo_ref[SSM_WIDTH:, :])
    h1 = _ln(ALPHA * x_ref[...] + a, lng_ref[0:1, :], lnb_ref[0:1, :])
    h1_ref[...] = h1
    qx_ref[...] = _dot(h1.astype(BF16), wxq_ref[...]).astype(BF16)


def _mix(o_ssm, o_mla, x2, g_mla, w_o, ln_g, ln_b, w_xq, tn):
    n = x2.shape[0]
    row = lambda w: pl.BlockSpec((tn, w), lambda i: (i, 0))
    return pl.pallas_call(
        _mix_body,
        grid=(n // tn,),
        in_specs=[row(SSM_WIDTH), row(MLA_WIDTH), row(D_MODEL), _const(g_mla.shape), _const(w_o.shape),
                  _const(ln_g.shape), _const(ln_b.shape), _const(w_xq.shape)],
        out_specs=[row(D_MODEL), row(D_MODEL)],
        out_shape=[jax.ShapeDtypeStruct((n, D_MODEL), F32), jax.ShapeDtypeStruct((n, D_MODEL), BF16)],
        compiler_params=_params("parallel"),
        name="mix",
    )(o_ssm, o_mla, x2, g_mla, w_o, ln_g, ln_b, w_xq)


def _mem_attn_body(qx_ref, mk_ref, mv_ref, o_ref):
    mk = mk_ref[0].astype(BF16)
    mv = mv_ref[0].astype(BF16)
    for h in range(X_HEADS):
        a = h * X_HEAD_DIM
        s = _dot_nt(qx_ref[:, a:a + X_HEAD_DIM], mk[:, a:a + X_HEAD_DIM]) * (X_HEAD_DIM ** -0.5)
        e = jnp.exp(s - jnp.max(s, -1, keepdims=True))
        p = e / jnp.sum(e, -1, keepdims=True)
        o_ref[:, a:a + X_HEAD_DIM] = _dot(p.astype(BF16), mv[:, a:a + X_HEAD_DIM]).astype(BF16)


def _mem_attn(qx, mem_k, mem_v, tq):
    n = qx.shape[0]
    nb = mem_k.shape[0]
    tiles = n // nb // tq
    row = pl.BlockSpec((tq, D_MODEL), lambda b, i: (b * tiles + i, 0))
    mem = pl.BlockSpec((1, N_MEM, D_MODEL), lambda b, i: (b, 0, 0))
    return pl.pallas_call(
        _mem_attn_body,
        grid=(nb, tiles),
        in_specs=[row, mem, mem],
        out_specs=row,
        out_shape=jax.ShapeDtypeStruct((n, D_MODEL), BF16),
        compiler_params=_params("parallel", "parallel"),
        name="mem_attn",
    )(qx, mem_k, mem_v)


def _mlp_body(h1_ref, ox_ref, wxo_ref, lng_ref, lnb_ref, w1_ref, w2_ref, y_ref, *, ff_blk):
    h2 = _ln(ALPHA * h1_ref[...] + _dot(ox_ref[...], wxo_ref[...]), lng_ref[1:2, :], lnb_ref[1:2, :])
    hb = h2.astype(BF16)
    acc = jnp.zeros(h2.shape, F32)
    for c in range(0, D_FF, ff_blk):
        z = jnp.maximum(_dot(hb, w1_ref[:, c:c + ff_blk]), 0.0)
        acc += _dot((z * z).astype(BF16), w2_ref[c:c + ff_blk, :])
    y_ref[...] = _ln(ALPHA * h2 + acc, lng_ref[2:3, :], lnb_ref[2:3, :])


def _mlp(h1, ox, w_xo, ln_g, ln_b, w1, w2, tn):
    n = h1.shape[0]
    row = pl.BlockSpec((tn, D_MODEL), lambda i: (i, 0))
    kern = functools.partial(_mlp_body, ff_blk=1024)
    return pl.pallas_call(
        kern,
        grid=(n // tn,),
        in_specs=[row, row, _const(w_xo.shape), _const(ln_g.shape), _const(ln_b.shape),
                  _const(w1.shape), _const(w2.shape)],
        out_specs=row,
        out_shape=jax.ShapeDtypeStruct((n, D_MODEL), F32),
        compiler_params=_params("parallel"),
        name="mlp",
    )(h1, ox, w_xo, ln_g, ln_b, w1, w2)


def _mem_kv_body(mem_ref, wk_ref, wv_ref, k_ref, v_ref):
    m = mem_ref[...].astype(BF16)
    k_ref[...] = _dot(m, wk_ref[...])
    v_ref[...] = _dot(m, wv_ref[...])


def _mem_kv(mem2, wk, wv):
    n = mem2.shape[0]
    out = jax.ShapeDtypeStruct((n, D_MODEL), F32)
    return pl.pallas_call(
        _mem_kv_body,
        grid=(1,),
        in_specs=[_const(mem2.shape), _const(wk.shape), _const(wv.shape)],
        out_specs=[_const((n, D_MODEL))] * 2,
        out_shape=[out, out],
        compiler_params=_params("arbitrary"),
        name="mem_kv",
    )(mem2, wk, wv)


def _to_chunks(u):
    n = u.shape[0]
    u4 = u.reshape(n // S5_T, S5_T, SSM_GROUPS, SSM_GROUP)
    return jnp.transpose(u4, (2, 0, 1, 3)).reshape(SSM_GROUPS, n // S5_T, S5_TW)


def _from_chunks(y):
    g, nc, _ = y.shape
    y4 = y.reshape(g, nc, S5_T, SSM_GROUP)
    return jnp.transpose(y4, (1, 2, 0, 3)).reshape(nc * S5_T, g * SSM_GROUP)


def _layer(x2, pos, nseq, h0r, h0i, mem_k, mem_v, caches, wts, s5_ops, tn, scan_ct, mem_tq):
    n = x2.shape[0]
    seq = n // nseq
    prompt = caches is None
    u, q, k, v, ckv, kpe = _project(x2, pos, wts["inv"], wts["w_in"], wts["g_q"], wts["w_q"], wts["g_kv"],
                                    wts["w_k"], wts["w_vt"] if prompt else wts["w_v"], tn, v_transposed=prompt)
    yc, fr, fi = _s5_scan(_to_chunks(u.astype(BF16)), s5_ops, h0r, h0i, nseq, scan_ct)
    o_ssm = _glu(_from_chunks(yc), u, wts["d_skip"], wts["w_glu"], wts["g_out_ssm"], tn)
    if prompt:
        o_mla = _attention(q, k, v, ATTN_HEADS)
    else:
        o_mla = _attention_cached(q, k, v, caches[0], caches[1], wts["w_k"], wts["w_v"], seq)
    h1, qx = _mix(o_ssm, o_mla, x2, wts["g_out_mla"], wts["w_o"], wts["ln_g"], wts["ln_b"], wts["w_xq"], tn)
    ox = _mem_attn(qx, mem_k, mem_v, mem_tq)
    y = _mlp(h1, ox, wts["w_xo"], wts["ln_g"], wts["ln_b"], wts["w_ff1"], wts["w_ff2"], tn)
    state = lambda f: jnp.transpose(f, (1, 0, 2))
    return y, ckv, kpe, state(fr), state(fi)


def kernel(x_prompt, x_sample, mem_prompt, cache_mla_ckv, cache_mla_kpe, state_ssm_re, state_ssm_im, cache_mem_k, cache_mem_v, w_in, g_q, w_q_up, g_kv, w_kv_up, a_re, a_im, b_re, b_im, c_re, c_im, d_skip, log_dt, w_glu, g_out_ssm, g_out_mla, w_o, w_xq, w_xk, w_xv, w_xo, w_ff1, w_ff2, ln_g, ln_b):
    assert w_in.shape[0] == DEPTH == 1
    nbp, sp, _ = x_prompt.shape
    nbs, sd, _ = x_sample.shape
    past = cache_mla_ckv.shape[2]
    assert nbp == 1

    wq = jnp.pad(w_q_up[0], ((0, 0), (0, 0), (0, QK_PAD - QK_NOPE - QK_ROPE)))
    wk = w_kv_up[0][:, :, :QK_NOPE].reshape(KV_LORA, -1).astype(BF16)
    wv = w_kv_up[0][:, :, QK_NOPE:].reshape(KV_LORA, -1).astype(BF16)
    inv = ROPE_THETA ** (-jnp.arange(ROPE_HALF, dtype=F32) / ROPE_HALF)
    wts = {
        "inv": jnp.tile(inv, 128 // ROPE_HALF).reshape(1, 128),
        "w_in": jnp.pad(w_in[0], ((0, 0), (0, 128 - QK_ROPE))).astype(BF16),
        "g_q": g_q[0].reshape(1, -1),
        "w_q": wq.reshape(Q_LORA, MLA_HEADS * QK_PAD).astype(BF16),
        "g_kv": g_kv[0].reshape(1, -1),
        "w_k": wk,
        "w_v": wv,
        "w_vt": wv.T,
        "d_skip": d_skip[0].reshape(1, -1),
        "w_glu": w_glu[0].astype(BF16),
        "g_out_ssm": g_out_ssm[0].reshape(1, -1),
        "g_out_mla": g_out_mla[0].reshape(1, -1),
        "w_o": w_o[0].astype(BF16),
        "w_xq": w_xq[0].reshape(D_MODEL, D_MODEL).astype(BF16),
        "w_xo": w_xo[0].reshape(D_MODEL, D_MODEL).astype(BF16),
        "w_ff1": w_ff1[0].astype(BF16),
        "w_ff2": w_ff2[0].astype(BF16),
        "ln_g": ln_g[0],
        "ln_b": ln_b[0],
    }
    s5_ops = _s5_prep(a_re[0], a_im[0], b_re[0], b_im[0], c_re[0], c_im[0], log_dt[0])

    mk, mv = _mem_kv(mem_prompt.reshape(nbp * N_MEM, D_MODEL),
                     w_xk[0].reshape(D_MODEL, D_MODEL).astype(BF16),
                     w_xv[0].reshape(D_MODEL, D_MODEL).astype(BF16))
    zero = jnp.zeros((SSM_GROUPS, nbp, SSM_STATE), F32)
    pos_p = jnp.arange(sp, dtype=F32).reshape(sp, 1)
    yp, ckv_p, kpe_p, sre_p, sim_p = _layer(
        x_prompt.reshape(sp, D_MODEL), pos_p, nbp, zero, zero,
        mk.reshape(nbp, N_MEM, D_MODEL), mv.reshape(nbp, N_MEM, D_MODEL), None,
        wts, s5_ops, tn=512, scan_ct=128, mem_tq=512)

    pos_s = jnp.tile(past + jnp.arange(sd, dtype=F32), nbs).reshape(nbs * sd, 1)
    caches = (cache_mla_ckv[0], jnp.pad(cache_mla_kpe[0], ((0, 0), (0, 0), (0, 128 - QK_ROPE))))
    ys, ckv_s, kpe_s, sre_s, sim_s = _layer(
        x_sample.reshape(nbs * sd, D_MODEL), pos_s, nbs,
        jnp.transpose(state_ssm_re[0], (1, 0, 2)), jnp.transpose(state_ssm_im[0], (1, 0, 2)),
        cache_mem_k[0].reshape(nbs, N_MEM, D_MODEL), cache_mem_v[0].reshape(nbs, N_MEM, D_MODEL), caches,
        wts, s5_ops, tn=512, scan_ct=sd // S5_T, mem_tq=sd)

    return (yp.reshape(nbp, sp, D_MODEL), ys.reshape(nbs, sd, D_MODEL),
            ckv_p.reshape(1, nbp, sp, KV_LORA), kpe_p.reshape(1, nbp, sp, QK_ROPE),
            sre_p.reshape(1, nbp, SSM_GROUPS, SSM_STATE), sim_p.reshape(1, nbp, SSM_GROUPS, SSM_STATE),
            mk.reshape(1, nbp, N_MEM, X_HEADS, X_HEAD_DIM), mv.reshape(1, nbp, N_MEM, X_HEADS, X_HEAD_DIM),
            ckv_s.reshape(1, nbs, sd, KV_LORA), kpe_s.reshape(1, nbs, sd, QK_ROPE),
            sre_s.reshape(1, nbs, SSM_GROUPS, SSM_STATE), sim_s.reshape(1, nbs, SSM_GROUPS, SSM_STATE))
```

```python
import functools
import math

import jax
import jax.numpy as jnp
from jax import lax
from jax.experimental import pallas as pl
from jax.experimental.pallas import tpu as pltpu

F32 = jnp.float32
BF16 = jnp.bfloat16

D_MODEL = 1024
DEPTH = 1
CHUNK = 64
SSM_WIDTH = 512
SSM_GROUP = 16
SSM_GROUPS = 32
SSM_STATE = 64
MLA_HEADS = 4
QK_NOPE = 128
QK_ROPE = 64
V_HEAD = 128
MLA_WIDTH = MLA_HEADS * V_HEAD
Q_LORA = 384
KV_LORA = 256
ROPE_THETA = 10000.0
MLA_SCALE = (QK_NOPE + QK_ROPE) ** -0.5
N_MEM = 256
X_HEADS = 4
X_HEAD_DIM = D_MODEL // X_HEADS
D_FF = 4 * D_MODEL
ALPHA = (2 * DEPTH) ** 0.25
EPS = 1e-5
NEG_INF = -1e30

QK_PAD = 256
S5_T = 16
S5_TW = S5_T * SSM_GROUP
S5_COL_GROUPS = 128 // SSM_GROUP
S5_COLS = SSM_GROUPS // S5_COL_GROUPS
ROPE_HALF = QK_ROPE // 2
Q_SCALE = MLA_SCALE * math.log2(math.e)
ATTN_T = 512
ATTN_HEADS = 2
VMEM_LIMIT = 56 * 1024 * 1024

_NT = (((1,), (1,)), ((), ()))


def _rms(x, g):
    return x * lax.rsqrt(jnp.mean(x * x, -1, keepdims=True) + EPS) * g


def _ln(x, g, b):
    mu = jnp.mean(x, -1, keepdims=True)
    xc = x - mu
    var = jnp.mean(xc * xc, -1, keepdims=True)
    return xc * lax.rsqrt(var + EPS) * g + b


def _dot(a, b):
    return jnp.dot(a, b, preferred_element_type=F32)


def _dot_nt(a, b):
    return lax.dot_general(a, b, _NT, preferred_element_type=F32)


def _dot_hi(a, b):
    return jnp.dot(a, b, preferred_element_type=F32, precision=lax.Precision.HIGHEST)


def _params(*sem, flags=None):
    return pltpu.CompilerParams(dimension_semantics=sem, vmem_limit_bytes=VMEM_LIMIT, flags=flags)


def _const(shape):
    n = len(shape)
    return pl.BlockSpec(shape, lambda *_: (0,) * n)


def _proj_body(x_ref, pos_ref, inv_ref, w_in_ref, gq_ref, wq_ref, gkv_ref, wk_ref, wv_ref,
               u_ref, q_ref, k_ref, v_ref, ckv_ref, kpe_ref, *, v_transposed):
    x = x_ref[...].astype(BF16)
    proj = _dot(x, w_in_ref[...])
    u_ref[...] = proj[:, :SSM_WIDTH]
    cq = _rms(proj[:, SSM_WIDTH:SSM_WIDTH + Q_LORA], gq_ref[...])
    q = _dot(cq.astype(BF16), wq_ref[...]) * Q_SCALE
    c0 = SSM_WIDTH + Q_LORA
    ckv = _rms(proj[:, c0:c0 + KV_LORA], gkv_ref[...])
    ckv_ref[...] = ckv
    ckv_b = ckv.astype(BF16)
    kn = _dot(ckv_b, wk_ref[...])
    if v_transposed:
        for c in range(v_ref.shape[0]):
            rows = ckv_b[c * ATTN_T:(c + 1) * ATTN_T, :]
            v_ref[c] = _dot_nt(wv_ref[...], rows).astype(BF16)
    else:
        v_ref[...] = _dot(ckv_b, wv_ref[...]).astype(BF16)

    ang = pos_ref[...] * inv_ref[...]
    lane = lax.broadcasted_iota(jnp.int32, (1, 128), 1)
    cs, sn = jnp.cos(ang), jnp.sin(ang)
    cos_t = jnp.where(lane < QK_ROPE, cs, 0.0)
    sin_a = jnp.where(lane < ROPE_HALF, -sn, 0.0)
    sin_b = jnp.where((lane >= ROPE_HALF) & (lane < QK_ROPE), sn, 0.0)

    def rope(c2):
        return (c2 * cos_t + pltpu.roll(c2, 128 - ROPE_HALF, 1) * sin_a
                + pltpu.roll(c2, ROPE_HALF, 1) * sin_b)

    kpe = rope(proj[:, c0 + KV_LORA:])
    kpe_ref[...] = kpe[:, :QK_ROPE]
    kpe_b = kpe.astype(BF16)
    for h in range(MLA_HEADS):
        a = h * QK_PAD
        q_ref[:, a:a + QK_NOPE] = q[:, a:a + QK_NOPE].astype(BF16)
        q_ref[:, a + QK_NOPE:a + QK_PAD] = rope(q[:, a + QK_NOPE:a + QK_PAD]).astype(BF16)
        k_ref[:, a:a + QK_NOPE] = kn[:, h * QK_NOPE:(h + 1) * QK_NOPE].astype(BF16)
        k_ref[:, a + QK_NOPE:a + QK_PAD] = kpe_b


def _project(x2, pos, inv, w_in, gq, wq, gkv, wk, wv, tn, v_transposed):
    n = x2.shape[0]
    row = lambda w: pl.BlockSpec((tn, w), lambda i: (i, 0))
    if v_transposed:
        assert tn % ATTN_T == 0
        v_spec = pl.BlockSpec((tn // ATTN_T, MLA_WIDTH, ATTN_T), lambda i: (i, 0, 0))
        v_shape = jax.ShapeDtypeStruct((n // ATTN_T, MLA_WIDTH, ATTN_T), BF16)
    else:
        v_spec, v_shape = row(MLA_WIDTH), jax.ShapeDtypeStruct((n, MLA_WIDTH), BF16)
    return pl.pallas_call(
        functools.partial(_proj_body, v_transposed=v_transposed),
        grid=(n // tn,),
        in_specs=[row(D_MODEL), row(1), _const(inv.shape), _const(w_in.shape), _const(gq.shape),
                  _const(wq.shape), _const(gkv.shape), _const(wk.shape), _const(wv.shape)],
        out_specs=[row(SSM_WIDTH), row(MLA_HEADS * QK_PAD), row(MLA_HEADS * QK_PAD),
                   v_spec, row(KV_LORA), row(QK_ROPE)],
        out_shape=[jax.ShapeDtypeStruct((n, SSM_WIDTH), F32),
                   jax.ShapeDtypeStruct((n, MLA_HEADS * QK_PAD), BF16),
                   jax.ShapeDtypeStruct((n, MLA_HEADS * QK_PAD), BF16),
                   v_shape,
                   jax.ShapeDtypeStruct((n, KV_LORA), F32),
                   jax.ShapeDtypeStruct((n, QK_ROPE), F32)],
        compiler_params=_params("parallel"),
        name="proj",
    )(x2, pos, inv, w_in, gq, wq, gkv, wk, wv)


def _s5_prep_body(ar_row, ai_row, ar_col, ai_col, ldt_ref, bt_re, bt_im, ct_re, ct_im,
                  w_ref, vre_ref, vim_ref, zr_ref, zi_ref, are_ref, aim_ref):
    dt = jnp.exp(ldt_ref[0])
    arr, air = ar_row[0], ai_row[0]
    arc, aic = ar_col[0], ai_col[0]

    mag = jnp.exp(arr * dt)
    lr, li = mag * jnp.cos(air * dt), mag * jnp.sin(air * dt)
    nr, ni = lr - 1.0, li
    den = arr * arr + air * air
    f_re, f_im = (nr * arr + ni * air) / den, (ni * arr - nr * air) / den
    br, bi = bt_re[0], bt_im[0]
    bb_re = f_re * br - f_im * bi
    bb_im = f_re * bi + f_im * br

    lag = (lax.broadcasted_iota(jnp.int32, (1, S5_TW), 1) // SSM_GROUP).astype(F32)
    magc = jnp.exp(arc * dt * lag)
    pw_re, pw_im = magc * jnp.cos(aic * dt * lag), magc * jnp.sin(aic * dt * lag)
    cr, ci = ct_re[0], ct_im[0]
    q_re = cr * pw_re - ci * pw_im
    q_im = cr * pw_im + ci * pw_re

    magl = jnp.exp(arc * dt)
    lrc, lic = magl * jnp.cos(aic * dt), magl * jnp.sin(aic * dt)
    zr_ref[0] = (q_re * lrc - q_im * lic).astype(BF16)
    zi_ref[0] = (-(q_re * lic + q_im * lrc)).astype(BF16)

    lane = lax.broadcasted_iota(jnp.int32, (1, S5_TW), 1)
    for i in range(2):
        if i == 0:
            s_re, s_im = q_re, q_im
        else:
            keep = lane >= i * SSM_GROUP
            s_re = jnp.where(keep, pltpu.roll(q_re, i * SSM_GROUP, 1), 0.0)
            s_im = jnp.where(keep, pltpu.roll(q_im, i * SSM_GROUP, 1), 0.0)
        blk = _dot_hi(bb_re, s_re) - _dot_hi(bb_im, s_im)
        w_ref[0, i * SSM_GROUP:(i + 1) * SSM_GROUP, :] = blk.astype(BF16)

    back = (S5_T - 1 - lax.broadcasted_iota(jnp.int32, (S5_TW, 1), 0) // SSM_GROUP).astype(F32)
    magv = jnp.exp(arr * dt * back)
    pv_re, pv_im = magv * jnp.cos(air * dt * back), magv * jnp.sin(air * dt * back)
    bbt_re = jnp.concatenate([bb_re] * S5_T, axis=0)
    bbt_im = jnp.concatenate([bb_im] * S5_T, axis=0)
    vre_ref[0] = (pv_re * bbt_re - pv_im * bbt_im).astype(BF16)
    vim_ref[0] = (pv_re * bbt_im + pv_im * bbt_re).astype(BF16)

    magt = jnp.exp(arr * dt * float(S5_T))
    are_ref[0] = magt * jnp.cos(air * dt * float(S5_T))
    aim_ref[0] = magt * jnp.sin(air * dt * float(S5_T))


def _s5_prep(a_re, a_im, b_re, b_im, c_re, c_im, log_dt):
    g, p, h = SSM_GROUPS, SSM_STATE, SSM_GROUP
    tile = lambda c: jnp.tile(jnp.transpose(c, (0, 2, 1)), (1, 1, S5_T))
    args = (a_re.reshape(g, 1, p), a_im.reshape(g, 1, p), a_re.reshape(g, p, 1), a_im.reshape(g, p, 1),
            log_dt.reshape(g, 1, 1), jnp.transpose(b_re, (0, 2, 1)), jnp.transpose(b_im, (0, 2, 1)),
            tile(c_re), tile(c_im))
    blk = lambda a: pl.BlockSpec((1,) + a.shape[1:], lambda i: (i, 0, 0))
    outs = [((g, 2 * h, S5_TW), BF16), ((g, S5_TW, p), BF16), ((g, S5_TW, p), BF16),
            ((g, p, S5_TW), BF16), ((g, p, S5_TW), BF16), ((g, 1, p), F32), ((g, 1, p), F32)]
    return pl.pallas_call(
        _s5_prep_body,
        grid=(g,),
        in_specs=[blk(a) for a in args],
        out_specs=[pl.BlockSpec((1,) + s[1:], lambda i: (i, 0, 0)) for s, _ in outs],
        out_shape=[jax.ShapeDtypeStruct(s, d) for s, d in outs],
        compiler_params=_params("parallel"),
        name="s5_prep",
    )(*args)


def _s5_column_ops(ops):
    w, vre, vim, zr, zi, are, aim = ops
    nc, r, t, h, p = S5_COLS, S5_COL_GROUPS, S5_T, SSM_GROUP, SSM_STATE
    eye = jnp.eye(r, dtype=bool)

    w7 = jnp.transpose(w.reshape(nc, r, 2, h, t // 2, 2, h), (0, 4, 2, 1, 3, 5, 6))
    wt = jnp.where(eye[None, None, None, :, None, None, :, None], w7[..., None, :], 0)
    wt = wt.reshape(nc, t // 2, 2 * r * h, 2 * r * h)

    def to_v(v):
        v5 = jnp.transpose(v.reshape(nc, r, t, h, p), (0, 2, 1, 3, 4))
        return jnp.where(eye[None, None, :, None, :, None], v5[:, :, :, :, None, :], 0).reshape(nc, t * r * h, r * p)

    v = jnp.concatenate([to_v(vre), to_v(vim)], axis=-1)

    def to_z(z):
        z5 = z.reshape(nc, r, p, t, h)
        return jnp.where(eye[None, :, None, None, :, None], z5[:, :, :, :, None, :], 0).reshape(nc, r * p, t * r * h)

    z = jnp.concatenate([to_z(zr), to_z(zi)], axis=1)
    return wt, v, z, are.reshape(nc, 1, r * p), aim.reshape(nc, 1, r * p)


def _s5_scan_body(u_ref, wt_ref, v_ref, z_ref, are_ref, aim_ref, h0r_ref, h0i_ref,
                  y_ref, fr_ref, fi_ref, lhs, sre, sim, xre, xim, cr, ci, *, nseq, ct):
    t = pl.program_id(1)
    nc = nseq * ct
    sw = S5_COL_GROUPS * SSM_STATE

    @pl.when(t == 0)
    def _():
        cr[...] = h0r_ref[0]
        ci[...] = h0i_ref[0]

    seq = ct * S5_T

    def token_rows(cc, i):
        if nseq == 1:
            return pl.ds(i, ct, stride=S5_T), slice(None)
        return pl.ds(cc * S5_T + i, nseq, stride=seq), slice(cc * nseq, (cc + 1) * nseq)

    for cc in range(1 if nseq == 1 else ct):
        for i in range(S5_T):
            tok, crow = token_rows(cc, i)
            lhs[crow, i * 128:(i + 1) * 128] = u_ref[tok, :].astype(BF16)

    s = _dot(lhs[...], v_ref[0])
    sre[...] = s[:, :sw]
    sim[...] = s[:, sw:]

    a_re, a_im = are_ref[0], aim_ref[0]

    def step(c, carry):
        x_re, x_im = carry
        r = pl.ds(c * nseq, nseq)
        xre[r, :] = x_re
        xim[r, :] = x_im
        n_re = a_re * x_re - a_im * x_im + sre[r, :]
        n_im = a_re * x_im + a_im * x_re + sim[r, :]
        return n_re, n_im

    x_re, x_im = lax.fori_loop(0, ct, step, (cr[...], ci[...]))
    cr[...] = x_re
    ci[...] = x_im
    fr_ref[0] = x_re
    fi_ref[0] = x_im

    xb_re, xb_im = xre[...].astype(BF16), xim[...].astype(BF16)
    for jp in range(S5_T // 2):
        cols = slice(jp * 256, (jp + 1) * 256)
        acc = _dot(xb_re, z_ref[0, :sw, cols]) + _dot(xb_im, z_ref[0, sw:, cols])
        for ip in range(jp + 1):
            acc += _dot(lhs[:, ip * 256:(ip + 1) * 256], wt_ref[0, jp - ip])
        for cc in range(1 if nseq == 1 else ct):
            for jj in range(2):
                tok, crow = token_rows(cc, 2 * jp + jj)
                y_ref[tok, :] = acc[crow, jj * 128:(jj + 1) * 128]


def _s5_scan(u, col_ops, h0r, h0i, nseq, ct):
    wt, v, z, are, aim = col_ops
    n = u.shape[0]
    cps = n // nseq // S5_T
    assert nseq == 1 or ct == cps
    nc = nseq * ct
    sw = S5_COL_GROUPS * SSM_STATE
    tile = pl.BlockSpec((nc * S5_T, 128), lambda c, t: (t, c))
    op = lambda a: pl.BlockSpec((1,) + a.shape[1:], lambda c, t: (c,) + (0,) * (a.ndim - 1))
    st = pl.BlockSpec((1, nseq, sw), lambda c, t: (c, 0, 0))
    st_shape = jax.ShapeDtypeStruct((S5_COLS, nseq, sw), F32)
    return pl.pallas_call(
        functools.partial(_s5_scan_body, nseq=nseq, ct=ct),
        grid=(S5_COLS, cps // ct),
        in_specs=[tile, op(wt), op(v), op(z), op(are), op(aim), st, st],
        out_specs=[tile, st, st],
        out_shape=[jax.ShapeDtypeStruct(u.shape, F32), st_shape, st_shape],
        scratch_shapes=[pltpu.VMEM((nc, S5_T * 128), BF16)] + [pltpu.VMEM((nc, sw), F32)] * 4
        + [pltpu.VMEM((nseq, sw), F32)] * 2,
        compiler_params=_params("arbitrary", "arbitrary"),
        name="s5_scan",
    )(u, wt, v, z, are, aim, h0r, h0i)


def _glu_body(y_ref, u_ref, d_ref, w_ref, g_ref, o_ref):
    y = y_ref[...] + d_ref[...] * u_ref[...]
    gl = _dot(jax.nn.gelu(y).astype(BF16), w_ref[...])
    o = gl[:, :SSM_WIDTH] * jax.nn.sigmoid(gl[:, SSM_WIDTH:])
    o_ref[...] = _rms(o, g_ref[...]).astype(BF16)


def _glu(y, u, d, w, g, tn):
    n = y.shape[0]
    row = pl.BlockSpec((tn, SSM_WIDTH), lambda i: (i, 0))
    return pl.pallas_call(
        _glu_body,
        grid=(n // tn,),
        in_specs=[row, row, _const(d.shape), _const(w.shape), _const(g.shape)],
        out_specs=row,
        out_shape=jax.ShapeDtypeStruct((n, SSM_WIDTH), BF16),
        compiler_params=_params("parallel"),
        name="glu",
    )(y, u, d, w, g)


def _attn_body(q_ref, k_ref, vt_ref, o_ref, m_sc, l_sc, acc_sc, *, heads):
    i = pl.program_id(1)
    t = ATTN_T
    m_sc[...] = jnp.full(m_sc.shape, -jnp.inf, F32)
    l_sc[...] = jnp.zeros(l_sc.shape, F32)
    acc_sc[...] = jnp.zeros(acc_sc.shape, F32)

    def step(j, masked):
        r = pl.ds(pl.multiple_of(j * t, t), t)
        for h in range(heads):
            s = _dot_nt(k_ref[r, h * QK_PAD:(h + 1) * QK_PAD],
                        q_ref[:, h * QK_PAD:(h + 1) * QK_PAD])
            if masked:
                kc = lax.broadcasted_iota(jnp.int32, s.shape, 0) // CHUNK
                qc = lax.broadcasted_iota(jnp.int32, s.shape, 1) // CHUNK
                s = jnp.where(kc <= qc, s, NEG_INF)
            m_prev = m_sc[h]
            m_new = jnp.maximum(m_prev, jnp.max(s, 0, keepdims=True))
            p = jnp.exp2(s - m_new)
            alpha = jnp.exp2(m_prev - m_new)
            l_sc[h] = alpha * l_sc[h] + jnp.sum(p, 0, keepdims=True)
            pv = _dot(vt_ref[j, h * V_HEAD:(h + 1) * V_HEAD, :], p.astype(BF16))
            acc_sc[h] = alpha * acc_sc[h] + pv
            m_sc[h] = m_new

    def full_step(j, c):
        step(j, False)
        return c

    lax.fori_loop(0, i, full_step, 0)
    step(i, True)
    for h in range(heads):
        o_ref[:, h * V_HEAD:(h + 1) * V_HEAD] = (acc_sc[h] / l_sc[h]).T


def _attention(q, k, vt, heads):
    n = q.shape[0]
    t = ATTN_T
    assert t % CHUNK == 0 and n % t == 0 and MLA_HEADS % heads == 0
    once = pl.Buffered(1)
    return pl.pallas_call(
        functools.partial(_attn_body, heads=heads),
        grid=(MLA_HEADS // heads, n // t),
        in_specs=[pl.BlockSpec((t, heads * QK_PAD), lambda g, i: (i, g)),
                  pl.BlockSpec((n, heads * QK_PAD), lambda g, i: (0, g), pipeline_mode=once),
                  pl.BlockSpec((n // t, heads * V_HEAD, t), lambda g, i: (0, g, 0), pipeline_mode=once)],
        out_specs=pl.BlockSpec((t, heads * V_HEAD), lambda g, i: (i, g)),
        out_shape=jax.ShapeDtypeStruct((n, MLA_WIDTH), F32),
        scratch_shapes=[pltpu.VMEM((heads, 1, t), F32), pltpu.VMEM((heads, 1, t), F32),
                        pltpu.VMEM((heads, V_HEAD, t), F32)],
        compiler_params=_params("arbitrary", "arbitrary"),
        name="attn",
    )(q, k, vt)


def _attn_cached_body(q_ref, kn_ref, vn_ref, ckv_ref, kpe_ref, wk_ref, wv_ref, o_ref, *, past, seq):
    ckv_b = ckv_ref[0].astype(BF16)
    knp = _dot(ckv_b, wk_ref[...])
    vp = _dot(ckv_b, wv_ref[...])
    kpe = kpe_ref[0].astype(BF16)
    qc = (past + lax.broadcasted_iota(jnp.int32, (seq, 1), 0)) // CHUNK
    kc_past = lax.broadcasted_iota(jnp.int32, (1, past), 1) // CHUNK
    kc_new = (past + lax.broadcasted_iota(jnp.int32, (1, seq), 1)) // CHUNK
    for h in range(MLA_HEADS):
        a = h * QK_PAD
        qh = q_ref[:, a:a + QK_PAD]
        kn_h = knp[:, h * QK_NOPE:(h + 1) * QK_NOPE].astype(BF16)
        s_p = _dot_nt(qh[:, :QK_NOPE], kn_h) + _dot_nt(qh[:, QK_NOPE:], kpe)
        s_n = _dot_nt(qh, kn_ref[:, a:a + QK_PAD])
        s_p = jnp.where(kc_past <= qc, s_p, NEG_INF)
        s_n = jnp.where(kc_new <= qc, s_n, NEG_INF)
        m = jnp.maximum(jnp.max(s_p, -1, keepdims=True), jnp.max(s_n, -1, keepdims=True))
        e_p, e_n = jnp.exp2(s_p - m), jnp.exp2(s_n - m)
        l = jnp.sum(e_p, -1, keepdims=True) + jnp.sum(e_n, -1, keepdims=True)
        o = _dot((e_p / l).astype(BF16), vp[:, h * V_HEAD:(h + 1) * V_HEAD].astype(BF16))
        o += _dot((e_n / l).astype(BF16), vn_ref[:, h * V_HEAD:(h + 1) * V_HEAD])
        o_ref[:, h * V_HEAD:(h + 1) * V_HEAD] = o


def _attention_cached(q, k, v, ckv_cache, kpe_cache, wk, wv, seq):
    n = q.shape[0]
    nb, past, _ = ckv_cache.shape
    row = lambda w: pl.BlockSpec((seq, w), lambda b: (b, 0))
    kern = functools.partial(_attn_cached_body, past=past, seq=seq)
    return pl.pallas_call(
        kern,
        grid=(nb,),
        in_specs=[row(MLA_HEADS * QK_PAD), row(MLA_HEADS * QK_PAD), row(MLA_WIDTH),
                  pl.BlockSpec((1, past, KV_LORA), lambda b: (b, 0, 0)),
                  pl.BlockSpec((1, past, 128), lambda b: (b, 0, 0)),
                  _const(wk.shape), _const(wv.shape)],
        out_specs=row(MLA_WIDTH),
        out_shape=jax.ShapeDtypeStruct((n, MLA_WIDTH), F32),
        compiler_params=_params("parallel"),
        name="attn_cached",
    )(q, k, v, ckv_cache, kpe_cache, wk, wv)


def _mix_body(os_ref, om_ref, x_ref, gm_ref, wo_ref, lng_ref, lnb_ref, wxq_ref, h1_ref, qx_ref):
    om = _rms(om_ref[...], gm_ref[...]).astype(BF16)
    a = _dot(os_ref[...], wo_ref[:SSM_WIDTH, :]) + _dot(om, wo_ref[SSM_WIDTH:, :])
    h1 = _ln(ALPHA * x_ref[...] + a, lng_ref[0:1, :], lnb_ref[0:1, :])
    h1_ref[...] = h1
    qx_ref[...] = _dot(h1.astype(BF16), wxq_ref[...]).astype(BF16)


def _mix(o_ssm, o_mla, x2, g_mla, w_o, ln_g, ln_b, w_xq, tn):
    n = x2.shape[0]
    row = lambda w: pl.BlockSpec((tn, w), lambda i: (i, 0))
    return pl.pallas_call(
        _mix_body,
        grid=(n // tn,),
        in_specs=[row(SSM_WIDTH), row(MLA_WIDTH), row(D_MODEL), _const(g_mla.shape), _const(w_o.shape),
                  _const(ln_g.shape), _const(ln_b.shape), _const(w_xq.shape)],
        out_specs=[row(D_MODEL), row(D_MODEL)],
        out_shape=[jax.ShapeDtypeStruct((n, D_MODEL), F32), jax.ShapeDtypeStruct((n, D_MODEL), BF16)],
        compiler_params=_params("parallel"),
        name="mix",
    )(o_ssm, o_mla, x2, g_mla, w_o, ln_g, ln_b, w_xq)


def _mem_attn_body(qx_ref, mk_ref, mv_ref, o_ref):
    mk = mk_ref[0].astype(BF16)
    mv = mv_ref[0].astype(BF16)
    for h in range(X_HEADS):
        a = h * X_HEAD_DIM
        s = _dot_nt(qx_ref[:, a:a + X_HEAD_DIM], mk[:, a:a + X_HEAD_DIM]) * (X_HEAD_DIM ** -0.5)
        e = jnp.exp(s - jnp.max(s, -1, keepdims=True))
        p = e / jnp.sum(e, -1, keepdims=True)
        o_ref[:, a:a + X_HEAD_DIM] = _dot(p.astype(BF16), mv[:, a:a + X_HEAD_DIM]).astype(BF16)


def _mem_attn(qx, mem_k, mem_v, tq):
    n = qx.shape[0]
    nb = mem_k.shape[0]
    tiles = n // nb // tq
    row = pl.BlockSpec((tq, D_MODEL), lambda b, i: (b * tiles + i, 0))
    mem = pl.BlockSpec((1, N_MEM, D_MODEL), lambda b, i: (b, 0, 0))
    return pl.pallas_call(
        _mem_attn_body,
        grid=(nb, tiles),
        in_specs=[row, mem, mem],
        out_specs=row,
        out_shape=jax.ShapeDtypeStruct((n, D_MODEL), BF16),
        compiler_params=_params("parallel", "parallel"),
        name="mem_attn",
    )(qx, mem_k, mem_v)


def _mlp_body(h1_ref, ox_ref, wxo_ref, lng_ref, lnb_ref, w1_ref, w2_ref, y_ref, *, ff_blk):
    h2 = _ln(ALPHA * h1_ref[...] + _dot(ox_ref[...], wxo_ref[...]), lng_ref[1:2, :], lnb_ref[1:2, :])
    hb = h2.astype(BF16)
    acc = jnp.zeros(h2.shape, F32)
    for c in range(0, D_FF, ff_blk):
        z = jnp.maximum(_dot(hb, w1_ref[:, c:c + ff_blk]), 0.0)
        acc += _dot((z * z).astype(BF16), w2_ref[c:c + ff_blk, :])
    y_ref[...] = _ln(ALPHA * h2 + acc, lng_ref[2:3, :], lnb_ref[2:3, :])


def _mlp(h1, ox, w_xo, ln_g, ln_b, w1, w2, tn):
    n = h1.shape[0]
    row = pl.BlockSpec((tn, D_MODEL), lambda i: (i, 0))
    kern = functools.partial(_mlp_body, ff_blk=1024)
    return pl.pallas_call(
        kern,
        grid=(n // tn,),
        in_specs=[row, row, _const(w_xo.shape), _const(ln_g.shape), _const(ln_b.shape),
                  _const(w1.shape), _const(w2.shape)],
        out_specs=row,
        out_shape=jax.ShapeDtypeStruct((n, D_MODEL), F32),
        compiler_params=_params("parallel"),
        name="mlp",
    )(h1, ox, w_xo, ln_g, ln_b, w1, w2)


def _mem_kv_body(mem_ref, wk_ref, wv_ref, k_ref, v_ref):
    m = mem_ref[...].astype(BF16)
    k_ref[...] = _dot(m, wk_ref[...])
    v_ref[...] = _dot(m, wv_ref[...])


def _mem_kv(mem2, wk, wv):
    n = mem2.shape[0]
    out = jax.ShapeDtypeStruct((n, D_MODEL), F32)
    return pl.pallas_call(
        _mem_kv_body,
        grid=(1,),
        in_specs=[_const(mem2.shape), _const(wk.shape), _const(wv.shape)],
        out_specs=[_const((n, D_MODEL))] * 2,
        out_shape=[out, out],
        compiler_params=_params("arbitrary"),
        name="mem_kv",
    )(mem2, wk, wv)


def _state_to_cols(s):
    return jnp.transpose(s.reshape(s.shape[0], S5_COLS, -1), (1, 0, 2))


def _state_from_cols(s):
    return jnp.transpose(s, (1, 0, 2)).reshape(s.shape[1], SSM_GROUPS, SSM_STATE)


def _layer(x2, pos, nseq, h0r, h0i, mem_k, mem_v, caches, wts, s5_ops, tn, scan_ct, mem_tq):
    n = x2.shape[0]
    seq = n // nseq
    prompt = caches is None
    u, q, k, v, ckv, kpe = _project(x2, pos, wts["inv"], wts["w_in"], wts["g_q"], wts["w_q"], wts["g_kv"],
                                    wts["w_k"], wts["w_vt"] if prompt else wts["w_v"], tn, v_transposed=prompt)
    y_ssm, fr, fi = _s5_scan(u, s5_ops, _state_to_cols(h0r), _state_to_cols(h0i), nseq, scan_ct)
    o_ssm = _glu(y_ssm, u, wts["d_skip"], wts["w_glu"], wts["g_out_ssm"], tn)
    if prompt:
        o_mla = _attention(q, k, v, ATTN_HEADS)
    else:
        o_mla = _attention_cached(q, k, v, caches[0], caches[1], wts["w_k"], wts["w_v"], seq)
    h1, qx = _mix(o_ssm, o_mla, x2, wts["g_out_mla"], wts["w_o"], wts["ln_g"], wts["ln_b"], wts["w_xq"], tn)
    ox = _mem_attn(qx, mem_k, mem_v, mem_tq)
    y = _mlp(h1, ox, wts["w_xo"], wts["ln_g"], wts["ln_b"], wts["w_ff1"], wts["w_ff2"], tn)
    return y, ckv, kpe, _state_from_cols(fr), _state_from_cols(fi)


def kernel(x_prompt, x_sample, mem_prompt, cache_mla_ckv, cache_mla_kpe, state_ssm_re, state_ssm_im, cache_mem_k, cache_mem_v, w_in, g_q, w_q_up, g_kv, w_kv_up, a_re, a_im, b_re, b_im, c_re, c_im, d_skip, log_dt, w_glu, g_out_ssm, g_out_mla, w_o, w_xq, w_xk, w_xv, w_xo, w_ff1, w_ff2, ln_g, ln_b):
    assert w_in.shape[0] == DEPTH == 1
    nbp, sp, _ = x_prompt.shape
    nbs, sd, _ = x_sample.shape
    past = cache_mla_ckv.shape[2]
    assert nbp == 1

    wq = jnp.pad(w_q_up[0], ((0, 0), (0, 0), (0, QK_PAD - QK_NOPE - QK_ROPE)))
    wk = w_kv_up[0][:, :, :QK_NOPE].reshape(KV_LORA, -1).astype(BF16)
    wv = w_kv_up[0][:, :, QK_NOPE:].reshape(KV_LORA, -1).astype(BF16)
    inv = ROPE_THETA ** (-jnp.arange(ROPE_HALF, dtype=F32) / ROPE_HALF)
    wts = {
        "inv": jnp.tile(inv, 128 // ROPE_HALF).reshape(1, 128),
        "w_in": jnp.pad(w_in[0], ((0, 0), (0, 128 - QK_ROPE))).astype(BF16),
        "g_q": g_q[0].reshape(1, -1),
        "w_q": wq.reshape(Q_LORA, MLA_HEADS * QK_PAD).astype(BF16),
        "g_kv": g_kv[0].reshape(1, -1),
        "w_k": wk,
        "w_v": wv,
        "w_vt": wv.T,
        "d_skip": d_skip[0].reshape(1, -1),
        "w_glu": w_glu[0].astype(BF16),
        "g_out_ssm": g_out_ssm[0].reshape(1, -1),
        "g_out_mla": g_out_mla[0].reshape(1, -1),
        "w_o": w_o[0].astype(BF16),
        "w_xq": w_xq[0].reshape(D_MODEL, D_MODEL).astype(BF16),
        "w_xo": w_xo[0].reshape(D_MODEL, D_MODEL).astype(BF16),
        "w_ff1": w_ff1[0].astype(BF16),
        "w_ff2": w_ff2[0].astype(BF16),
        "ln_g": ln_g[0],
        "ln_b": ln_b[0],
    }
    s5_ops = _s5_column_ops(_s5_prep(a_re[0], a_im[0], b_re[0], b_im[0], c_re[0], c_im[0], log_dt[0]))

    mk, mv = _mem_kv(mem_prompt.reshape(nbp * N_MEM, D_MODEL),
                     w_xk[0].reshape(D_MODEL, D_MODEL).astype(BF16),
                     w_xv[0].reshape(D_MODEL, D_MODEL).astype(BF16))
    zero = jnp.zeros((nbp, SSM_GROUPS, SSM_STATE), F32)
    pos_p = jnp.arange(sp, dtype=F32).reshape(sp, 1)
    yp, ckv_p, kpe_p, sre_p, sim_p = _layer(
        x_prompt.reshape(sp, D_MODEL), pos_p, nbp, zero, zero,
        mk.reshape(nbp, N_MEM, D_MODEL), mv.reshape(nbp, N_MEM, D_MODEL), None,
        wts, s5_ops, tn=512, scan_ct=512, mem_tq=512)

    pos_s = jnp.tile(past + jnp.arange(sd, dtype=F32), nbs).reshape(nbs * sd, 1)
    caches = (cache_mla_ckv[0], jnp.pad(cache_mla_kpe[0], ((0, 0), (0, 0), (0, 128 - QK_ROPE))))
    ys, ckv_s, kpe_s, sre_s, sim_s = _layer(
        x_sample.reshape(nbs * sd, D_MODEL), pos_s, nbs,
        state_ssm_re[0], state_ssm_im[0],
        cache_mem_k[0].reshape(nbs, N_MEM, D_MODEL), cache_mem_v[0].reshape(nbs, N_MEM, D_MODEL), caches,
        wts, s5_ops, tn=512, scan_ct=sd // S5_T, mem_tq=sd)

    return (yp.reshape(nbp, sp, D_MODEL), ys.reshape(nbs, sd, D_MODEL),
            ckv_p.reshape(1, nbp, sp, KV_LORA), kpe_p.reshape(1, nbp, sp, QK_ROPE),
            sre_p.reshape(1, nbp, SSM_GROUPS, SSM_STATE), sim_p.reshape(1, nbp, SSM_GROUPS, SSM_STATE),
            mk.reshape(1, nbp, N_MEM, X_HEADS, X_HEAD_DIM), mv.reshape(1, nbp, N_MEM, X_HEADS, X_HEAD_DIM),
            ckv_s.reshape(1, nbs, sd, KV_LORA), kpe_s.reshape(1, nbs, sd, QK_ROPE),
            sre_s.reshape(1, nbs, SSM_GROUPS, SSM_STATE), sim_s.reshape(1, nbs, SSM_GROUPS, SSM_STATE))
```

```python
import functools
import math

import jax
import jax.numpy as jnp
from jax import lax
from jax.experimental import pallas as pl
from jax.experimental.pallas import tpu as pltpu

F32 = jnp.float32
BF16 = jnp.bfloat16

D_MODEL = 1024
DEPTH = 1
CHUNK = 64
SSM_WIDTH = 512
SSM_GROUP = 16
SSM_GROUPS = 32
SSM_STATE = 64
MLA_HEADS = 4
QK_NOPE = 128
QK_ROPE = 64
V_HEAD = 128
MLA_WIDTH = MLA_HEADS * V_HEAD
Q_LORA = 384
KV_LORA = 256
ROPE_THETA = 10000.0
MLA_SCALE = (QK_NOPE + QK_ROPE) ** -0.5
N_MEM = 256
X_HEADS = 4
X_HEAD_DIM = D_MODEL // X_HEADS
D_FF = 4 * D_MODEL
ALPHA = (2 * DEPTH) ** 0.25
EPS = 1e-5
NEG_INF = -1e30

LANES = 128
QK_PAD = 256
S5_T = 16
S5_COL_GROUPS = LANES // SSM_GROUP
S5_COLS = SSM_GROUPS // S5_COL_GROUPS
S5_SW = S5_COL_GROUPS * SSM_STATE
ROPE_HALF = QK_ROPE // 2
Q_SCALE = MLA_SCALE * math.log2(math.e)
ATTN_T = 512
ATTN_HEADS = 2
VMEM_LIMIT = 56 * 1024 * 1024

_NT = (((1,), (1,)), ((), ()))


def _rms(x, g):
    return x * lax.rsqrt(jnp.mean(x * x, -1, keepdims=True) + EPS) * g


def _ln(x, g, b):
    mu = jnp.mean(x, -1, keepdims=True)
    xc = x - mu
    var = jnp.mean(xc * xc, -1, keepdims=True)
    return xc * lax.rsqrt(var + EPS) * g + b


def _dot(a, b):
    return jnp.dot(a, b, preferred_element_type=F32)


def _dot_nt(a, b):
    return lax.dot_general(a, b, _NT, preferred_element_type=F32)


def _dot_hi(a, b):
    return jnp.dot(a, b, preferred_element_type=F32, precision=lax.Precision.HIGHEST)


def _params(*sem):
    return pltpu.CompilerParams(dimension_semantics=sem, vmem_limit_bytes=VMEM_LIMIT)


def _const(shape):
    n = len(shape)
    return pl.BlockSpec(shape, lambda *_: (0,) * n)


def _proj_body(x_ref, pos_ref, inv_ref, w_in_ref, gq_ref, wq_ref, gkv_ref, wk_ref, wv_ref,
               u_ref, q_ref, k_ref, v_ref, ckv_ref, kpe_ref, *, v_transposed):
    x = x_ref[...].astype(BF16)
    proj = _dot(x, w_in_ref[...])
    u_ref[...] = proj[:, :SSM_WIDTH]
    cq = _rms(proj[:, SSM_WIDTH:SSM_WIDTH + Q_LORA], gq_ref[...])
    q = _dot(cq.astype(BF16), wq_ref[...]) * Q_SCALE
    c0 = SSM_WIDTH + Q_LORA
    ckv = _rms(proj[:, c0:c0 + KV_LORA], gkv_ref[...])
    ckv_ref[...] = ckv
    ckv_b = ckv.astype(BF16)
    kn = _dot(ckv_b, wk_ref[...])
    if v_transposed:
        for c in range(v_ref.shape[0]):
            rows = ckv_b[c * ATTN_T:(c + 1) * ATTN_T, :]
            v_ref[c] = _dot_nt(wv_ref[...], rows).astype(BF16)
    else:
        v_ref[...] = _dot(ckv_b, wv_ref[...]).astype(BF16)

    ang = pos_ref[...] * inv_ref[...]
    lane = lax.broadcasted_iota(jnp.int32, (1, LANES), 1)
    cs, sn = jnp.cos(ang), jnp.sin(ang)
    cos_t = jnp.where(lane < QK_ROPE, cs, 0.0)
    sin_a = jnp.where(lane < ROPE_HALF, -sn, 0.0)
    sin_b = jnp.where((lane >= ROPE_HALF) & (lane < QK_ROPE), sn, 0.0)

    def rope(c2):
        return (c2 * cos_t + pltpu.roll(c2, LANES - ROPE_HALF, 1) * sin_a
                + pltpu.roll(c2, ROPE_HALF, 1) * sin_b)

    kpe = rope(proj[:, c0 + KV_LORA:])
    kpe_ref[...] = kpe[:, :QK_ROPE]
    kpe_b = kpe.astype(BF16)
    for h in range(MLA_HEADS):
        a = h * QK_PAD
        q_ref[:, a:a + QK_NOPE] = q[:, a:a + QK_NOPE].astype(BF16)
        q_ref[:, a + QK_NOPE:a + QK_PAD] = rope(q[:, a + QK_NOPE:a + QK_PAD]).astype(BF16)
        k_ref[:, a:a + QK_NOPE] = kn[:, h * QK_NOPE:(h + 1) * QK_NOPE].astype(BF16)
        k_ref[:, a + QK_NOPE:a + QK_PAD] = kpe_b


def _project(x2, pos, inv, w_in, gq, wq, gkv, wk, wv, tn, v_transposed):
    n = x2.shape[0]
    row = lambda w: pl.BlockSpec((tn, w), lambda i: (i, 0))
    if v_transposed:
        assert tn % ATTN_T == 0
        v_spec = pl.BlockSpec((tn // ATTN_T, MLA_WIDTH, ATTN_T), lambda i: (i, 0, 0))
        v_shape = jax.ShapeDtypeStruct((n // ATTN_T, MLA_WIDTH, ATTN_T), BF16)
    else:
        v_spec, v_shape = row(MLA_WIDTH), jax.ShapeDtypeStruct((n, MLA_WIDTH), BF16)
    return pl.pallas_call(
        functools.partial(_proj_body, v_transposed=v_transposed),
        grid=(n // tn,),
        in_specs=[row(D_MODEL), row(1), _const(inv.shape), _const(w_in.shape), _const(gq.shape),
                  _const(wq.shape), _const(gkv.shape), _const(wk.shape), _const(wv.shape)],
        out_specs=[row(SSM_WIDTH), row(MLA_HEADS * QK_PAD), row(MLA_HEADS * QK_PAD),
                   v_spec, row(KV_LORA), row(QK_ROPE)],
        out_shape=[jax.ShapeDtypeStruct((n, SSM_WIDTH), F32),
                   jax.ShapeDtypeStruct((n, MLA_HEADS * QK_PAD), BF16),
                   jax.ShapeDtypeStruct((n, MLA_HEADS * QK_PAD), BF16),
                   v_shape,
                   jax.ShapeDtypeStruct((n, KV_LORA), F32),
                   jax.ShapeDtypeStruct((n, QK_ROPE), F32)],
        compiler_params=_params("parallel"),
        name="proj",
    )(x2, pos, inv, w_in, gq, wq, gkv, wk, wv)


def _s5_prep_body(ar_row, ai_row, ldt_row, ar_col, ai_col, ldt_col, bt_re, bt_im, ct_re, ct_im,
                  wt_ref, v_ref, z_ref, are_ref, aim_ref):
    sw = S5_SW
    arr, air, dtr = ar_row[0], ai_row[0], jnp.exp(ldt_row[0])
    arc, aic, dtc = ar_col[0], ai_col[0], jnp.exp(ldt_col[0])

    mag = jnp.exp(arr * dtr)
    lr, li = mag * jnp.cos(air * dtr), mag * jnp.sin(air * dtr)
    nr, ni = lr - 1.0, li
    den = arr * arr + air * air
    f_re, f_im = (nr * arr + ni * air) / den, (ni * arr - nr * air) / den

    same_b = (lax.broadcasted_iota(jnp.int32, (LANES, sw), 0) // SSM_GROUP
              == lax.broadcasted_iota(jnp.int32, (LANES, sw), 1) // SSM_STATE)
    br = jnp.where(same_b, bt_re[0], 0.0)
    bi = jnp.where(same_b, bt_im[0], 0.0)
    bb_re = f_re * br - f_im * bi
    bb_im = f_re * bi + f_im * br
    same_c = (lax.broadcasted_iota(jnp.int32, (sw, LANES), 0) // SSM_STATE
              == lax.broadcasted_iota(jnp.int32, (sw, LANES), 1) // SSM_GROUP)
    cr = jnp.where(same_c, ct_re[0], 0.0)
    ci = jnp.where(same_c, ct_im[0], 0.0)

    lag_ops = []
    for lag in range(S5_T):
        e = float(lag)
        pm = jnp.exp(arr * dtr * e)
        p_re, p_im = pm * jnp.cos(air * dtr * e), pm * jnp.sin(air * dtr * e)
        k_re = p_re * bb_re - p_im * bb_im
        k_im = p_re * bb_im + p_im * bb_re
        i = S5_T - 1 - lag
        v_ref[0, i * LANES:(i + 1) * LANES, :sw] = k_re.astype(BF16)
        v_ref[0, i * LANES:(i + 1) * LANES, sw:] = k_im.astype(BF16)
        lag_ops.append(_dot_hi(k_re, cr) - _dot_hi(k_im, ci))

        e1 = float(lag + 1)
        qm = jnp.exp(arc * dtc * e1)
        q_re, q_im = qm * jnp.cos(aic * dtc * e1), qm * jnp.sin(aic * dtc * e1)
        z_ref[0, :sw, lag * LANES:(lag + 1) * LANES] = (cr * q_re - ci * q_im).astype(BF16)
        z_ref[0, sw:, lag * LANES:(lag + 1) * LANES] = (-(cr * q_im + ci * q_re)).astype(BF16)

    zero = jnp.zeros((LANES, LANES), BF16)
    for d in range(S5_T // 2):
        wt_ref[0, d, :LANES, :LANES] = lag_ops[2 * d].astype(BF16)
        wt_ref[0, d, :LANES, LANES:] = lag_ops[2 * d + 1].astype(BF16)
        wt_ref[0, d, LANES:, :LANES] = lag_ops[2 * d - 1].astype(BF16) if d else zero
        wt_ref[0, d, LANES:, LANES:] = lag_ops[2 * d].astype(BF16)

    magt = jnp.exp(arr * dtr * float(S5_T))
    are_ref[0] = magt * jnp.cos(air * dtr * float(S5_T))
    aim_ref[0] = magt * jnp.sin(air * dtr * float(S5_T))


def _s5_prep(a_re, a_im, b_re, b_im, c_re, c_im, log_dt):
    nc, r, t, h, p, sw = S5_COLS, S5_COL_GROUPS, S5_T, SSM_GROUP, SSM_STATE, S5_SW
    ldt = jnp.repeat(log_dt, p)
    bt = lambda b: jnp.tile(jnp.transpose(b.reshape(nc, r, p, h), (0, 1, 3, 2)).reshape(nc, r * h, p), (1, 1, r))
    ct = lambda c: jnp.tile(jnp.transpose(c.reshape(nc, r, h, p), (0, 1, 3, 2)).reshape(nc, r * p, h), (1, 1, r))
    args = (a_re.reshape(nc, 1, sw), a_im.reshape(nc, 1, sw), ldt.reshape(nc, 1, sw),
            a_re.reshape(nc, sw, 1), a_im.reshape(nc, sw, 1), ldt.reshape(nc, sw, 1),
            bt(b_re), bt(b_im), ct(c_re), ct(c_im))
    blk = lambda s: pl.BlockSpec((1,) + s[1:], lambda i: (i,) + (0,) * (len(s) - 1))
    outs = [((nc, t // 2, 2 * LANES, 2 * LANES), BF16), ((nc, t * LANES, 2 * sw), BF16),
            ((nc, 2 * sw, t * LANES), BF16), ((nc, 1, sw), F32), ((nc, 1, sw), F32)]
    return pl.pallas_call(
        _s5_prep_body,
        grid=(nc,),
        in_specs=[blk(a.shape) for a in args],
        out_specs=[blk(s) for s, _ in outs],
        out_shape=[jax.ShapeDtypeStruct(s, d) for s, d in outs],
        compiler_params=_params("parallel"),
        name="s5_prep",
    )(*args)


def _s5_scan_body(u_ref, wt_ref, v_ref, z_ref, are_ref, aim_ref, h0r_ref, h0i_ref,
                  y_ref, fr_ref, fi_ref, lhs, sre, sim, xre, xim, cr, ci, *, nseq, ct):
    t = pl.program_id(1)
    sw = S5_SW
    seq = ct * S5_T

    @pl.when(t == 0)
    def _():
        cr[...] = h0r_ref[0]
        ci[...] = h0i_ref[0]

    def token_rows(cc, i):
        if nseq == 1:
            return pl.ds(i, ct, stride=S5_T), slice(None)
        return pl.ds(cc * S5_T + i, nseq, stride=seq), slice(cc * nseq, (cc + 1) * nseq)

    for cc in range(1 if nseq == 1 else ct):
        for i in range(S5_T):
            tok, crow = token_rows(cc, i)
            lhs[crow, i * LANES:(i + 1) * LANES] = u_ref[tok, :].astype(BF16)

    s = _dot(lhs[...], v_ref[0])
    sre[...] = s[:, :sw]
    sim[...] = s[:, sw:]

    a_re, a_im = are_ref[0], aim_ref[0]

    def step(c, carry):
        x_re, x_im = carry
        r = pl.ds(c * nseq, nseq)
        xre[r, :] = x_re
        xim[r, :] = x_im
        n_re = a_re * x_re - a_im * x_im + sre[r, :]
        n_im = a_re * x_im + a_im * x_re + sim[r, :]
        return n_re, n_im

    x_re, x_im = lax.fori_loop(0, ct, step, (cr[...], ci[...]))
    cr[...] = x_re
    ci[...] = x_im
    fr_ref[0] = x_re
    fi_ref[0] = x_im

    xb_re, xb_im = xre[...].astype(BF16), xim[...].astype(BF16)
    for jp in range(S5_T // 2):
        cols = slice(jp * 2 * LANES, (jp + 1) * 2 * LANES)
        acc = _dot(xb_re, z_ref[0, :sw, cols]) + _dot(xb_im, z_ref[0, sw:, cols])
        for ip in range(jp + 1):
            acc += _dot(lhs[:, ip * 2 * LANES:(ip + 1) * 2 * LANES], wt_ref[0, jp - ip])
        for cc in range(1 if nseq == 1 else ct):
            for jj in range(2):
                tok, crow = token_rows(cc, 2 * jp + jj)
                y_ref[tok, :] = acc[crow, jj * LANES:(jj + 1) * LANES]


def _s5_scan(u, col_ops, h0r, h0i, nseq, ct):
    wt, v, z, are, aim = col_ops
    n = u.shape[0]
    cps = n // nseq // S5_T
    assert nseq == 1 or ct == cps
    nc = nseq * ct
    sw = S5_SW
    tile = pl.BlockSpec((nc * S5_T, LANES), lambda c, t: (t, c))
    op = lambda a: pl.BlockSpec((1,) + a.shape[1:], lambda c, t: (c,) + (0,) * (a.ndim - 1))
    st = pl.BlockSpec((1, nseq, sw), lambda c, t: (c, 0, 0))
    st_shape = jax.ShapeDtypeStruct((S5_COLS, nseq, sw), F32)
    return pl.pallas_call(
        functools.partial(_s5_scan_body, nseq=nseq, ct=ct),
        grid=(S5_COLS, cps // ct),
        in_specs=[tile, op(wt), op(v), op(z), op(are), op(aim), st, st],
        out_specs=[tile, st, st],
        out_shape=[jax.ShapeDtypeStruct(u.shape, F32), st_shape, st_shape],
        scratch_shapes=[pltpu.VMEM((nc, S5_T * LANES), BF16)] + [pltpu.VMEM((nc, sw), F32)] * 4
        + [pltpu.VMEM((nseq, sw), F32)] * 2,
        compiler_params=_params("arbitrary", "arbitrary"),
        name="s5_scan",
    )(u, wt, v, z, are, aim, h0r, h0i)


def _glu_body(y_ref, u_ref, d_ref, w_ref, g_ref, o_ref):
    y = y_ref[...] + d_ref[...] * u_ref[...]
    gl = _dot(jax.nn.gelu(y).astype(BF16), w_ref[...])
    o = gl[:, :SSM_WIDTH] * jax.nn.sigmoid(gl[:, SSM_WIDTH:])
    o_ref[...] = _rms(o, g_ref[...]).astype(BF16)


def _glu(y, u, d, w, g, tn):
    n = y.shape[0]
    row = pl.BlockSpec((tn, SSM_WIDTH), lambda i: (i, 0))
    return pl.pallas_call(
        _glu_body,
        grid=(n // tn,),
        in_specs=[row, row, _const(d.shape), _const(w.shape), _const(g.shape)],
        out_specs=row,
        out_shape=jax.ShapeDtypeStruct((n, SSM_WIDTH), BF16),
        compiler_params=_params("parallel"),
        name="glu",
    )(y, u, d, w, g)


def _attn_body(q_ref, k_ref, vt_ref, o_ref, s0, s1, mt0, mt1, m_sc, l_sc, acc_sc, *, heads):
    i = pl.program_id(1)
    t = ATTN_T
    m_sc[...] = jnp.full(m_sc.shape, -jnp.inf, F32)
    l_sc[...] = jnp.zeros(l_sc.shape, F32)
    acc_sc[...] = jnp.zeros(acc_sc.shape, F32)

    def tile_of(k):
        return jnp.where(k == 0, i, k - 1)

    def scores(k, s_buf, mt_buf, masked=False):
        r = pl.ds(pl.multiple_of(tile_of(k) * t, t), t)
        for h in range(heads):
            s = _dot_nt(k_ref[r, h * QK_PAD:(h + 1) * QK_PAD],
                        q_ref[:, h * QK_PAD:(h + 1) * QK_PAD])
            if masked:
                kc = lax.broadcasted_iota(jnp.int32, s.shape, 0) // CHUNK
                qc = lax.broadcasted_iota(jnp.int32, s.shape, 1) // CHUNK
                s = jnp.where(kc <= qc, s, NEG_INF)
            s_buf[h] = s
            mt_buf[h] = jnp.max(s, 0, keepdims=True)

    def absorb(k, s_buf, mt_buf):
        j = tile_of(k)
        for h in range(heads):
            m_prev = m_sc[h]
            m_new = jnp.maximum(m_prev, mt_buf[h])
            p = jnp.exp2(s_buf[h] - m_new)
            alpha = jnp.exp2(m_prev - m_new)
            l_sc[h] = alpha * l_sc[h] + jnp.sum(p, 0, keepdims=True)
            pv = _dot(vt_ref[j, h * V_HEAD:(h + 1) * V_HEAD, :], p.astype(BF16))
            acc_sc[h] = alpha * acc_sc[h] + pv
            m_sc[h] = m_new

    scores(0, s0, mt0, masked=True)

    def pair(jj, c):
        scores(2 * jj + 1, s1, mt1)
        absorb(2 * jj, s0, mt0)
        scores(2 * jj + 2, s0, mt0)
        absorb(2 * jj + 1, s1, mt1)
        return c

    lax.fori_loop(0, i // 2, pair, 0)

    @pl.when(i % 2 == 1)
    def _():
        scores(i, s1, mt1)
        absorb(i - 1, s0, mt0)
        absorb(i, s1, mt1)

    @pl.when(i % 2 == 0)
    def _():
        absorb(i, s0, mt0)

    for h in range(heads):
        o_ref[:, h * V_HEAD:(h + 1) * V_HEAD] = (acc_sc[h] / l_sc[h]).T


def _attention(q, k, vt, heads):
    n = q.shape[0]
    t = ATTN_T
    assert t % CHUNK == 0 and n % t == 0 and MLA_HEADS % heads == 0
    once = pl.Buffered(1)
    return pl.pallas_call(
        functools.partial(_attn_body, heads=heads),
        grid=(MLA_HEADS // heads, n // t),
        in_specs=[pl.BlockSpec((t, heads * QK_PAD), lambda g, i: (i, g)),
                  pl.BlockSpec((n, heads * QK_PAD), lambda g, i: (0, g), pipeline_mode=once),
                  pl.BlockSpec((n // t, heads * V_HEAD, t), lambda g, i: (0, g, 0), pipeline_mode=once)],
        out_specs=pl.BlockSpec((t, heads * V_HEAD), lambda g, i: (i, g)),
        out_shape=jax.ShapeDtypeStruct((n, MLA_WIDTH), F32),
        scratch_shapes=[pltpu.VMEM((heads, t, t), F32)] * 2 + [pltpu.VMEM((heads, 1, t), F32)] * 4
        + [pltpu.VMEM((heads, V_HEAD, t), F32)],
        compiler_params=_params("arbitrary", "arbitrary"),
        name="attn",
    )(q, k, vt)


def _attn_cached_body(q_ref, kn_ref, vn_ref, ckv_ref, kpe_ref, wk_ref, wv_ref, o_ref, *, past, seq):
    ckv_b = ckv_ref[0].astype(BF16)
    knp = _dot(ckv_b, wk_ref[...])
    vp = _dot(ckv_b, wv_ref[...])
    kpe = kpe_ref[0].astype(BF16)
    qc = (past + lax.broadcasted_iota(jnp.int32, (seq, 1), 0)) // CHUNK
    kc_past = lax.broadcasted_iota(jnp.int32, (1, past), 1) // CHUNK
    kc_new = (past + lax.broadcasted_iota(jnp.int32, (1, seq), 1)) // CHUNK
    for h in range(MLA_HEADS):
        a = h * QK_PAD
        qh = q_ref[:, a:a + QK_PAD]
        kn_h = knp[:, h * QK_NOPE:(h + 1) * QK_NOPE].astype(BF16)
        s_p = _dot_nt(qh[:, :QK_NOPE], kn_h) + _dot_nt(qh[:, QK_NOPE:], kpe)
        s_n = _dot_nt(qh, kn_ref[:, a:a + QK_PAD])
        s_p = jnp.where(kc_past <= qc, s_p, NEG_INF)
        s_n = jnp.where(kc_new <= qc, s_n, NEG_INF)
        m = jnp.maximum(jnp.max(s_p, -1, keepdims=True), jnp.max(s_n, -1, keepdims=True))
        e_p, e_n = jnp.exp2(s_p - m), jnp.exp2(s_n - m)
        l = jnp.sum(e_p, -1, keepdims=True) + jnp.sum(e_n, -1, keepdims=True)
        o = _dot((e_p / l).astype(BF16), vp[:, h * V_HEAD:(h + 1) * V_HEAD].astype(BF16))
        o += _dot((e_n / l).astype(BF16), vn_ref[:, h * V_HEAD:(h + 1) * V_HEAD])
        o_ref[:, h * V_HEAD:(h + 1) * V_HEAD] = o


def _attention_cached(q, k, v, ckv_cache, kpe_cache, wk, wv, seq):
    n = q.shape[0]
    nb, past, _ = ckv_cache.shape
    row = lambda w: pl.BlockSpec((seq, w), lambda b: (b, 0))
    kern = functools.partial(_attn_cached_body, past=past, seq=seq)
    return pl.pallas_call(
        kern,
        grid=(nb,),
        in_specs=[row(MLA_HEADS * QK_PAD), row(MLA_HEADS * QK_PAD), row(MLA_WIDTH),
                  pl.BlockSpec((1, past, KV_LORA), lambda b: (b, 0, 0)),
                  pl.BlockSpec((1, past, LANES), lambda b: (b, 0, 0)),
                  _const(wk.shape), _const(wv.shape)],
        out_specs=row(MLA_WIDTH),
        out_shape=jax.ShapeDtypeStruct((n, MLA_WIDTH), F32),
        compiler_params=_params("parallel"),
        name="attn_cached",
    )(q, k, v, ckv_cache, kpe_cache, wk, wv)


def _mix_body(os_ref, om_ref, x_ref, gm_ref, wo_ref, lng_ref, lnb_ref, wxq_ref, h1_ref, qx_ref):
    om = _rms(om_ref[...], gm_ref[...]).astype(BF16)
    a = _dot(os_ref[...], wo_ref[:SSM_WIDTH, :]) + _dot(om, wo_ref[SSM_WIDTH:, :])
    h1 = _ln(ALPHA * x_ref[...] + a, lng_ref[0:1, :], lnb_ref[0:1, :])
    h1_ref[...] = h1
    qx_ref[...] = _dot(h1.astype(BF16), wxq_ref[...]).astype(BF16)


def _mix(o_ssm, o_mla, x2, g_mla, w_o, ln_g, ln_b, w_xq, tn):
    n = x2.shape[0]
    row = lambda w: pl.BlockSpec((tn, w), lambda i: (i, 0))
    return pl.pallas_call(
        _mix_body,
        grid=(n // tn,),
        in_specs=[row(SSM_WIDTH), row(MLA_WIDTH), row(D_MODEL), _const(g_mla.shape), _const(w_o.shape),
                  _const(ln_g.shape), _const(ln_b.shape), _const(w_xq.shape)],
        out_specs=[row(D_MODEL), row(D_MODEL)],
        out_shape=[jax.ShapeDtypeStruct((n, D_MODEL), F32), jax.ShapeDtypeStruct((n, D_MODEL), BF16)],
        compiler_params=_params("parallel"),
        name="mix",
    )(o_ssm, o_mla, x2, g_mla, w_o, ln_g, ln_b, w_xq)


def _mem_attn_body(qx_ref, mk_ref, mv_ref, o_ref):
    for h in range(X_HEADS):
        a = h * X_HEAD_DIM
        mk = mk_ref[0, :, h, :].astype(BF16)
        mv = mv_ref[0, :, h, :].astype(BF16)
        s = _dot_nt(qx_ref[:, a:a + X_HEAD_DIM], mk) * (X_HEAD_DIM ** -0.5)
        e = jnp.exp(s - jnp.max(s, -1, keepdims=True))
        p = e / jnp.sum(e, -1, keepdims=True)
        o_ref[:, a:a + X_HEAD_DIM] = _dot(p.astype(BF16), mv).astype(BF16)


def _mem_attn(qx, mem_k, mem_v, tq):
    n = qx.shape[0]
    nb = mem_k.shape[0]
    tiles = n // nb // tq
    row = pl.BlockSpec((tq, D_MODEL), lambda b, i: (b * tiles + i, 0))
    mem = pl.BlockSpec((1, N_MEM, X_HEADS, X_HEAD_DIM), lambda b, i: (b, 0, 0, 0))
    return pl.pallas_call(
        _mem_attn_body,
        grid=(nb, tiles),
        in_specs=[row, mem, mem],
        out_specs=row,
        out_shape=jax.ShapeDtypeStruct((n, D_MODEL), BF16),
        compiler_params=_params("parallel", "parallel"),
        name="mem_attn",
    )(qx, mem_k, mem_v)


def _mlp_body(h1_ref, ox_ref, wxo_ref, lng_ref, lnb_ref, w1_ref, w2_ref, y_ref, *, ff_blk):
    h2 = _ln(ALPHA * h1_ref[...] + _dot(ox_ref[...], wxo_ref[...]), lng_ref[1:2, :], lnb_ref[1:2, :])
    hb = h2.astype(BF16)
    acc = jnp.zeros(h2.shape, F32)
    for c in range(0, D_FF, ff_blk):
        z = jnp.maximum(_dot(hb, w1_ref[:, c:c + ff_blk]), 0.0)
        acc += _dot((z * z).astype(BF16), w2_ref[c:c + ff_blk, :])
    y_ref[...] = _ln(ALPHA * h2 + acc, lng_ref[2:3, :], lnb_ref[2:3, :])


def _mlp(h1, ox, w_xo, ln_g, ln_b, w1, w2, tn):
    n = h1.shape[0]
    row = pl.BlockSpec((tn, D_MODEL), lambda i: (i, 0))
    kern = functools.partial(_mlp_body, ff_blk=1024)
    return pl.pallas_call(
        kern,
        grid=(n // tn,),
        in_specs=[row, row, _const(w_xo.shape), _const(ln_g.shape), _const(ln_b.shape),
                  _const(w1.shape), _const(w2.shape)],
        out_specs=row,
        out_shape=jax.ShapeDtypeStruct((n, D_MODEL), F32),
        compiler_params=_params("parallel"),
        name="mlp",
    )(h1, ox, w_xo, ln_g, ln_b, w1, w2)


def _mem_kv_body(mem_ref, wk_ref, wv_ref, k_ref, v_ref):
    m = mem_ref[...].astype(BF16)
    k_ref[...] = _dot(m, wk_ref[...])
    v_ref[...] = _dot(m, wv_ref[...])


def _mem_kv(mem2, wk, wv):
    n = mem2.shape[0]
    out = jax.ShapeDtypeStruct((n, D_MODEL), F32)
    return pl.pallas_call(
        _mem_kv_body,
        grid=(1,),
        in_specs=[_const(mem2.shape), _const(wk.shape), _const(wv.shape)],
        out_specs=[_const((n, D_MODEL))] * 2,
        out_shape=[out, out],
        compiler_params=_params("arbitrary"),
        name="mem_kv",
    )(mem2, wk, wv)


def _state_to_cols(s):
    return jnp.transpose(s.reshape(s.shape[0], S5_COLS, -1), (1, 0, 2))


def _state_from_cols(s):
    return jnp.transpose(s, (1, 0, 2)).reshape(s.shape[1], SSM_GROUPS, SSM_STATE)


def _layer(x2, pos, nseq, h0r, h0i, mem_k, mem_v, caches, wts, s5_ops, tn, scan_ct, mem_tq):
    n = x2.shape[0]
    seq = n // nseq
    prompt = caches is None
    u, q, k, v, ckv, kpe = _project(x2, pos, wts["inv"], wts["w_in"], wts["g_q"], wts["w_q"], wts["g_kv"],
                                    wts["w_k"], wts["w_vt"] if prompt else wts["w_v"], tn, v_transposed=prompt)
    y_ssm, fr, fi = _s5_scan(u, s5_ops, _state_to_cols(h0r), _state_to_cols(h0i), nseq, scan_ct)
    o_ssm = _glu(y_ssm, u, wts["d_skip"], wts["w_glu"], wts["g_out_ssm"], tn)
    if prompt:
        o_mla = _attention(q, k, v, ATTN_HEADS)
    else:
        o_mla = _attention_cached(q, k, v, caches[0], caches[1], wts["w_k"], wts["w_v"], seq)
    h1, qx = _mix(o_ssm, o_mla, x2, wts["g_out_mla"], wts["w_o"], wts["ln_g"], wts["ln_b"], wts["w_xq"], tn)
    ox = _mem_attn(qx, mem_k, mem_v, mem_tq)
    y = _mlp(h1, ox, wts["w_xo"], wts["ln_g"], wts["ln_b"], wts["w_ff1"], wts["w_ff2"], tn)
    return y, ckv, kpe, _state_from_cols(fr), _state_from_cols(fi)


def kernel(x_prompt, x_sample, mem_prompt, cache_mla_ckv, cache_mla_kpe, state_ssm_re, state_ssm_im, cache_mem_k, cache_mem_v, w_in, g_q, w_q_up, g_kv, w_kv_up, a_re, a_im, b_re, b_im, c_re, c_im, d_skip, log_dt, w_glu, g_out_ssm, g_out_mla, w_o, w_xq, w_xk, w_xv, w_xo, w_ff1, w_ff2, ln_g, ln_b):
    assert w_in.shape[0] == DEPTH == 1
    nbp, sp, _ = x_prompt.shape
    nbs, sd, _ = x_sample.shape
    past = cache_mla_ckv.shape[2]
    assert nbp == 1

    wq = jnp.pad(w_q_up[0], ((0, 0), (0, 0), (0, QK_PAD - QK_NOPE - QK_ROPE)))
    wk = w_kv_up[0][:, :, :QK_NOPE].reshape(KV_LORA, -1).astype(BF16)
    wv = w_kv_up[0][:, :, QK_NOPE:].reshape(KV_LORA, -1).astype(BF16)
    inv = ROPE_THETA ** (-jnp.arange(ROPE_HALF, dtype=F32) / ROPE_HALF)
    wts = {
        "inv": jnp.tile(inv, LANES // ROPE_HALF).reshape(1, LANES),
        "w_in": jnp.pad(w_in[0], ((0, 0), (0, LANES - QK_ROPE))).astype(BF16),
        "g_q": g_q[0].reshape(1, -1),
        "w_q": wq.reshape(Q_LORA, MLA_HEADS * QK_PAD).astype(BF16),
        "g_kv": g_kv[0].reshape(1, -1),
        "w_k": wk,
        "w_v": wv,
        "w_vt": wv.T,
        "d_skip": d_skip[0].reshape(1, -1),
        "w_glu": w_glu[0].astype(BF16),
        "g_out_ssm": g_out_ssm[0].reshape(1, -1),
        "g_out_mla": g_out_mla[0].reshape(1, -1),
        "w_o": w_o[0].astype(BF16),
        "w_xq": w_xq[0].reshape(D_MODEL, D_MODEL).astype(BF16),
        "w_xo": w_xo[0].reshape(D_MODEL, D_MODEL).astype(BF16),
        "w_ff1": w_ff1[0].astype(BF16),
        "w_ff2": w_ff2[0].astype(BF16),
        "ln_g": ln_g[0],
        "ln_b": ln_b[0],
    }
    s5_ops = _s5_prep(a_re[0], a_im[0], b_re[0], b_im[0], c_re[0], c_im[0], log_dt[0])

    mk, mv = _mem_kv(mem_prompt.reshape(nbp * N_MEM, D_MODEL),
                     w_xk[0].reshape(D_MODEL, D_MODEL).astype(BF16),
                     w_xv[0].reshape(D_MODEL, D_MODEL).astype(BF16))
    mk = mk.reshape(nbp, N_MEM, X_HEADS, X_HEAD_DIM)
    mv = mv.reshape(nbp, N_MEM, X_HEADS, X_HEAD_DIM)
    zero = jnp.zeros((nbp, SSM_GROUPS, SSM_STATE), F32)
    pos_p = jnp.arange(sp, dtype=F32).reshape(sp, 1)
    yp, ckv_p, kpe_p, sre_p, sim_p = _layer(
        x_prompt.reshape(sp, D_MODEL), pos_p, nbp, zero, zero, mk, mv, None,
        wts, s5_ops, tn=512, scan_ct=512, mem_tq=512)

    pos_s = jnp.tile(past + jnp.arange(sd, dtype=F32), nbs).reshape(nbs * sd, 1)
    caches = (cache_mla_ckv[0], jnp.pad(cache_mla_kpe[0], ((0, 0), (0, 0), (0, LANES - QK_ROPE))))
    ys, ckv_s, kpe_s, sre_s, sim_s = _layer(
        x_sample.reshape(nbs * sd, D_MODEL), pos_s, nbs,
        state_ssm_re[0], state_ssm_im[0], cache_mem_k[0], cache_mem_v[0], caches,
        wts, s5_ops, tn=512, scan_ct=sd // S5_T, mem_tq=sd)

    return (yp.reshape(nbp, sp, D_MODEL), ys.reshape(nbs, sd, D_MODEL),
            ckv_p.reshape(1, nbp, sp, KV_LORA), kpe_p.reshape(1, nbp, sp, QK_ROPE),
            sre_p.reshape(1, nbp, SSM_GROUPS, SSM_STATE), sim_p.reshape(1, nbp, SSM_GROUPS, SSM_STATE),
            mk.reshape(1, nbp, N_MEM, X_HEADS, X_HEAD_DIM), mv.reshape(1, nbp, N_MEM, X_HEADS, X_HEAD_DIM),
            ckv_s.reshape(1, nbs, sd, KV_LORA), kpe_s.reshape(1, nbs, sd, QK_ROPE),
            sre_s.reshape(1, nbs, SSM_GROUPS, SSM_STATE), sim_s.reshape(1, nbs, SSM_GROUPS, SSM_STATE))
```

```python
import functools
import math

import jax
import jax.numpy as jnp
from jax import lax
from jax.experimental import pallas as pl
from jax.experimental.pallas import tpu as pltpu

F32 = jnp.float32
BF16 = jnp.bfloat16

D_MODEL = 1024
DEPTH = 1
CHUNK = 64
SSM_WIDTH = 512
SSM_GROUP = 16
SSM_GROUPS = 32
SSM_STATE = 64
MLA_HEADS = 4
QK_NOPE = 128
QK_ROPE = 64
V_HEAD = 128
MLA_WIDTH = MLA_HEADS * V_HEAD
Q_LORA = 384
KV_LORA = 256
ROPE_THETA = 10000.0
MLA_SCALE = (QK_NOPE + QK_ROPE) ** -0.5
N_MEM = 256
X_HEADS = 4
X_HEAD_DIM = D_MODEL // X_HEADS
D_FF = 4 * D_MODEL
ALPHA = (2 * DEPTH) ** 0.25
EPS = 1e-5
NEG_INF = -1e30

LANES = 128
QK_PAD = 256
S5_T = 16
S5_COL_GROUPS = LANES // SSM_GROUP
S5_COLS = SSM_GROUPS // S5_COL_GROUPS
S5_SW = S5_COL_GROUPS * SSM_STATE
ROPE_HALF = QK_ROPE // 2
Q_SCALE = MLA_SCALE * math.log2(math.e)
ATTN_T = 512
ATTN_HEADS = 2
VMEM_LIMIT = 56 * 1024 * 1024

_NT = (((1,), (1,)), ((), ()))


def _rms(x, g):
    return x * lax.rsqrt(jnp.mean(x * x, -1, keepdims=True) + EPS) * g


def _ln(x, g, b):
    mu = jnp.mean(x, -1, keepdims=True)
    xc = x - mu
    var = jnp.mean(xc * xc, -1, keepdims=True)
    return xc * lax.rsqrt(var + EPS) * g + b


def _dot(a, b):
    return jnp.dot(a, b, preferred_element_type=F32)


def _dot_nt(a, b):
    return lax.dot_general(a, b, _NT, preferred_element_type=F32)


def _dot_hi(a, b):
    return jnp.dot(a, b, preferred_element_type=F32, precision=lax.Precision.HIGHEST)


def _params(*sem):
    return pltpu.CompilerParams(dimension_semantics=sem, vmem_limit_bytes=VMEM_LIMIT)


def _const(shape):
    n = len(shape)
    return pl.BlockSpec(shape, lambda *_: (0,) * n)


def _proj_body(x_ref, pos_ref, inv_ref, w_in_ref, gq_ref, wq_ref, gkv_ref, wk_ref, wv_ref,
               u_ref, q_ref, k_ref, v_ref, ckv_ref, kpe_ref, *, v_transposed):
    x = x_ref[...].astype(BF16)
    proj = _dot(x, w_in_ref[...])
    u_ref[...] = proj[:, :SSM_WIDTH]
    cq = _rms(proj[:, SSM_WIDTH:SSM_WIDTH + Q_LORA], gq_ref[...])
    q = _dot(cq.astype(BF16), wq_ref[...]) * Q_SCALE
    c0 = SSM_WIDTH + Q_LORA
    ckv = _rms(proj[:, c0:c0 + KV_LORA], gkv_ref[...])
    ckv_ref[...] = ckv
    ckv_b = ckv.astype(BF16)
    kn = _dot(ckv_b, wk_ref[...])
    if v_transposed:
        for c in range(v_ref.shape[0]):
            rows = ckv_b[c * ATTN_T:(c + 1) * ATTN_T, :]
            v_ref[c] = _dot_nt(wv_ref[...], rows).astype(BF16)
    else:
        v_ref[...] = _dot(ckv_b, wv_ref[...]).astype(BF16)

    ang = pos_ref[...] * inv_ref[...]
    lane = lax.broadcasted_iota(jnp.int32, (1, LANES), 1)
    cs, sn = jnp.cos(ang), jnp.sin(ang)
    cos_t = jnp.where(lane < QK_ROPE, cs, 0.0)
    sin_a = jnp.where(lane < ROPE_HALF, -sn, 0.0)
    sin_b = jnp.where((lane >= ROPE_HALF) & (lane < QK_ROPE), sn, 0.0)

    def rope(c2):
        return (c2 * cos_t + pltpu.roll(c2, LANES - ROPE_HALF, 1) * sin_a
                + pltpu.roll(c2, ROPE_HALF, 1) * sin_b)

    kpe = rope(proj[:, c0 + KV_LORA:])
    kpe_ref[...] = kpe[:, :QK_ROPE]
    kpe_b = kpe.astype(BF16)
    for h in range(MLA_HEADS):
        a = h * QK_PAD
        q_ref[:, a:a + QK_NOPE] = q[:, a:a + QK_NOPE].astype(BF16)
        q_ref[:, a + QK_NOPE:a + QK_PAD] = rope(q[:, a + QK_NOPE:a + QK_PAD]).astype(BF16)
        k_ref[:, a:a + QK_NOPE] = kn[:, h * QK_NOPE:(h + 1) * QK_NOPE].astype(BF16)
        k_ref[:, a + QK_NOPE:a + QK_PAD] = kpe_b


def _project(x2, pos, inv, w_in, gq, wq, gkv, wk, wv, tn, v_transposed):
    n = x2.shape[0]
    row = lambda w: pl.BlockSpec((tn, w), lambda i: (i, 0))
    if v_transposed:
        assert tn % ATTN_T == 0
        v_spec = pl.BlockSpec((tn // ATTN_T, MLA_WIDTH, ATTN_T), lambda i: (i, 0, 0))
        v_shape = jax.ShapeDtypeStruct((n // ATTN_T, MLA_WIDTH, ATTN_T), BF16)
    else:
        v_spec, v_shape = row(MLA_WIDTH), jax.ShapeDtypeStruct((n, MLA_WIDTH), BF16)
    return pl.pallas_call(
        functools.partial(_proj_body, v_transposed=v_transposed),
        grid=(n // tn,),
        in_specs=[row(D_MODEL), row(1), _const(inv.shape), _const(w_in.shape), _const(gq.shape),
                  _const(wq.shape), _const(gkv.shape), _const(wk.shape), _const(wv.shape)],
        out_specs=[row(SSM_WIDTH), row(MLA_HEADS * QK_PAD), row(MLA_HEADS * QK_PAD),
                   v_spec, row(KV_LORA), row(QK_ROPE)],
        out_shape=[jax.ShapeDtypeStruct((n, SSM_WIDTH), F32),
                   jax.ShapeDtypeStruct((n, MLA_HEADS * QK_PAD), BF16),
                   jax.ShapeDtypeStruct((n, MLA_HEADS * QK_PAD), BF16),
                   v_shape,
                   jax.ShapeDtypeStruct((n, KV_LORA), F32),
                   jax.ShapeDtypeStruct((n, QK_ROPE), F32)],
        compiler_params=_params("parallel"),
        name="proj",
    )(x2, pos, inv, w_in, gq, wq, gkv, wk, wv)


def _s5_prep_body(ar_row, ai_row, ldt_row, bt_re, bt_im, ct_re, ct_im,
                  wt_ref, v_ref, z_ref, are_ref, aim_ref):
    sw = S5_SW
    arr, air, dtr = ar_row[0], ai_row[0], jnp.exp(ldt_row[0])

    mag = jnp.exp(arr * dtr)
    lr, li = mag * jnp.cos(air * dtr), mag * jnp.sin(air * dtr)
    nr, ni = lr - 1.0, li
    den = arr * arr + air * air
    f_re, f_im = (nr * arr + ni * air) / den, (ni * arr - nr * air) / den

    same_b = (lax.broadcasted_iota(jnp.int32, (LANES, sw), 0) // SSM_GROUP
              == lax.broadcasted_iota(jnp.int32, (LANES, sw), 1) // SSM_STATE)
    br = jnp.where(same_b, bt_re[0], 0.0)
    bi = jnp.where(same_b, bt_im[0], 0.0)
    bb_re = f_re * br - f_im * bi
    bb_im = f_re * bi + f_im * br
    same_c = (lax.broadcasted_iota(jnp.int32, (sw, LANES), 0) // SSM_STATE
              == lax.broadcasted_iota(jnp.int32, (sw, LANES), 1) // SSM_GROUP)
    cr = jnp.where(same_c, ct_re[0], 0.0)
    ci = jnp.where(same_c, ct_im[0], 0.0)

    e = lax.broadcasted_iota(jnp.int32, (2 * S5_T, 1), 0).astype(F32)
    pm = jnp.exp(arr * dtr * e)
    pw_re, pw_im = pm * jnp.cos(air * dtr * e), pm * jnp.sin(air * dtr * e)
    pt_re, pt_im = pw_re.T, pw_im.T

    lag_ops = []
    for lag in range(S5_T):
        p_re, p_im = pw_re[lag:lag + 1, :], pw_im[lag:lag + 1, :]
        k_re = p_re * bb_re - p_im * bb_im
        k_im = p_re * bb_im + p_im * bb_re
        i = S5_T - 1 - lag
        v_ref[0, i * LANES:(i + 1) * LANES, :sw] = k_re.astype(BF16)
        v_ref[0, i * LANES:(i + 1) * LANES, sw:] = k_im.astype(BF16)
        lag_ops.append(_dot_hi(k_re, cr) - _dot_hi(k_im, ci))

        q_re, q_im = pt_re[:, lag + 1:lag + 2], pt_im[:, lag + 1:lag + 2]
        z_ref[0, :sw, lag * LANES:(lag + 1) * LANES] = (cr * q_re - ci * q_im).astype(BF16)
        z_ref[0, sw:, lag * LANES:(lag + 1) * LANES] = (-(cr * q_im + ci * q_re)).astype(BF16)

    zero = jnp.zeros((LANES, LANES), BF16)
    for d in range(S5_T // 2):
        wt_ref[0, d, :LANES, :LANES] = lag_ops[2 * d].astype(BF16)
        wt_ref[0, d, :LANES, LANES:] = lag_ops[2 * d + 1].astype(BF16)
        wt_ref[0, d, LANES:, :LANES] = lag_ops[2 * d - 1].astype(BF16) if d else zero
        wt_ref[0, d, LANES:, LANES:] = lag_ops[2 * d].astype(BF16)

    are_ref[0] = pw_re[S5_T:S5_T + 1, :]
    aim_ref[0] = pw_im[S5_T:S5_T + 1, :]


def _s5_prep(a_re, a_im, b_re, b_im, c_re, c_im, log_dt):
    nc, r, t, h, p, sw = S5_COLS, S5_COL_GROUPS, S5_T, SSM_GROUP, SSM_STATE, S5_SW
    ldt = jnp.repeat(log_dt, p)
    bt = lambda b: jnp.tile(jnp.transpose(b.reshape(nc, r, p, h), (0, 1, 3, 2)).reshape(nc, r * h, p), (1, 1, r))
    ct = lambda c: jnp.tile(jnp.transpose(c.reshape(nc, r, h, p), (0, 1, 3, 2)).reshape(nc, r * p, h), (1, 1, r))
    args = (a_re.reshape(nc, 1, sw), a_im.reshape(nc, 1, sw), ldt.reshape(nc, 1, sw),
            bt(b_re), bt(b_im), ct(c_re), ct(c_im))
    blk = lambda s: pl.BlockSpec((1,) + s[1:], lambda i: (i,) + (0,) * (len(s) - 1))
    outs = [((nc, t // 2, 2 * LANES, 2 * LANES), BF16), ((nc, t * LANES, 2 * sw), BF16),
            ((nc, 2 * sw, t * LANES), BF16), ((nc, 1, sw), F32), ((nc, 1, sw), F32)]
    return pl.pallas_call(
        _s5_prep_body,
        grid=(nc,),
        in_specs=[blk(a.shape) for a in args],
        out_specs=[blk(s) for s, _ in outs],
        out_shape=[jax.ShapeDtypeStruct(s, d) for s, d in outs],
        compiler_params=_params("parallel"),
        name="s5_prep",
    )(*args)


def _s5_scan_body(u_ref, wt_ref, v_ref, z_ref, are_ref, aim_ref, h0r_ref, h0i_ref,
                  y_ref, fr_ref, fi_ref, lhs, sre, sim, xre, xim, cr, ci, *, nseq, ct):
    t = pl.program_id(1)
    sw = S5_SW
    seq = ct * S5_T

    @pl.when(t == 0)
    def _():
        cr[...] = h0r_ref[0]
        ci[...] = h0i_ref[0]

    def token_rows(cc, i):
        if nseq == 1:
            return pl.ds(i, ct, stride=S5_T), slice(None)
        return pl.ds(cc * S5_T + i, nseq, stride=seq), slice(cc * nseq, (cc + 1) * nseq)

    for cc in range(1 if nseq == 1 else ct):
        for i in range(S5_T):
            tok, crow = token_rows(cc, i)
            lhs[crow, i * LANES:(i + 1) * LANES] = u_ref[tok, :].astype(BF16)

    s = _dot(lhs[...], v_ref[0])
    sre[...] = s[:, :sw]
    sim[...] = s[:, sw:]

    a_re, a_im = are_ref[0], aim_ref[0]

    def step(c, carry):
        x_re, x_im = carry
        r = pl.ds(c * nseq, nseq)
        xre[r, :] = x_re
        xim[r, :] = x_im
        n_re = a_re * x_re - a_im * x_im + sre[r, :]
        n_im = a_re * x_im + a_im * x_re + sim[r, :]
        return n_re, n_im

    x_re, x_im = lax.fori_loop(0, ct, step, (cr[...], ci[...]))
    cr[...] = x_re
    ci[...] = x_im
    fr_ref[0] = x_re
    fi_ref[0] = x_im

    xb_re, xb_im = xre[...].astype(BF16), xim[...].astype(BF16)
    for jp in range(S5_T // 2):
        cols = slice(jp * 2 * LANES, (jp + 1) * 2 * LANES)
        acc = _dot(xb_re, z_ref[0, :sw, cols]) + _dot(xb_im, z_ref[0, sw:, cols])
        for ip in range(jp + 1):
            acc += _dot(lhs[:, ip * 2 * LANES:(ip + 1) * 2 * LANES], wt_ref[0, jp - ip])
        for cc in range(1 if nseq == 1 else ct):
            for jj in range(2):
                tok, crow = token_rows(cc, 2 * jp + jj)
                y_ref[tok, :] = acc[crow, jj * LANES:(jj + 1) * LANES]


def _s5_scan(u, col_ops, h0r, h0i, nseq, ct):
    wt, v, z, are, aim = col_ops
    n = u.shape[0]
    cps = n // nseq // S5_T
    assert nseq == 1 or ct == cps
    nc = nseq * ct
    sw = S5_SW
    tile = pl.BlockSpec((nc * S5_T, LANES), lambda c, t: (t, c))
    op = lambda a: pl.BlockSpec((1,) + a.shape[1:], lambda c, t: (c,) + (0,) * (a.ndim - 1))
    st = pl.BlockSpec((1, nseq, sw), lambda c, t: (c, 0, 0))
    st_shape = jax.ShapeDtypeStruct((S5_COLS, nseq, sw), F32)
    return pl.pallas_call(
        functools.partial(_s5_scan_body, nseq=nseq, ct=ct),
        grid=(S5_COLS, cps // ct),
        in_specs=[tile, op(wt), op(v), op(z), op(are), op(aim), st, st],
        out_specs=[tile, st, st],
        out_shape=[jax.ShapeDtypeStruct(u.shape, F32), st_shape, st_shape],
        scratch_shapes=[pltpu.VMEM((nc, S5_T * LANES), BF16)] + [pltpu.VMEM((nc, sw), F32)] * 4
        + [pltpu.VMEM((nseq, sw), F32)] * 2,
        compiler_params=_params("arbitrary", "arbitrary"),
        name="s5_scan",
    )(u, wt, v, z, are, aim, h0r, h0i)


def _glu_body(y_ref, u_ref, d_ref, w_ref, g_ref, o_ref):
    y = y_ref[...] + d_ref[...] * u_ref[...]
    gl = _dot(jax.nn.gelu(y).astype(BF16), w_ref[...])
    o = gl[:, :SSM_WIDTH] * jax.nn.sigmoid(gl[:, SSM_WIDTH:])
    o_ref[...] = _rms(o, g_ref[...]).astype(BF16)


def _glu(y, u, d, w, g, tn):
    n = y.shape[0]
    row = pl.BlockSpec((tn, SSM_WIDTH), lambda i: (i, 0))
    return pl.pallas_call(
        _glu_body,
        grid=(n // tn,),
        in_specs=[row, row, _const(d.shape), _const(w.shape), _const(g.shape)],
        out_specs=row,
        out_shape=jax.ShapeDtypeStruct((n, SSM_WIDTH), BF16),
        compiler_params=_params("parallel"),
        name="glu",
    )(y, u, d, w, g)


def _attn_body(q_ref, k_ref, vt_ref, o_ref, s0, s1, mt0, mt1, m_sc, l_sc, acc_sc, *, heads):
    i = pl.program_id(1)
    t = ATTN_T
    m_sc[...] = jnp.full(m_sc.shape, -jnp.inf, F32)
    l_sc[...] = jnp.zeros(l_sc.shape, F32)
    acc_sc[...] = jnp.zeros(acc_sc.shape, F32)

    def tile_of(k):
        return jnp.where(k == 0, i, k - 1)

    def scores(k, s_buf, mt_buf, masked=False):
        r = pl.ds(pl.multiple_of(tile_of(k) * t, t), t)
        for h in range(heads):
            s = _dot_nt(k_ref[r, h * QK_PAD:(h + 1) * QK_PAD],
                        q_ref[:, h * QK_PAD:(h + 1) * QK_PAD])
            if masked:
                kc = lax.broadcasted_iota(jnp.int32, s.shape, 0) // CHUNK
                qc = lax.broadcasted_iota(jnp.int32, s.shape, 1) // CHUNK
                s = jnp.where(kc <= qc, s, NEG_INF)
            s_buf[h] = s
            mt_buf[h] = jnp.max(s, 0, keepdims=True)

    def absorb(k, s_buf, mt_buf):
        j = tile_of(k)
        for h in range(heads):
            m_prev = m_sc[h]
            m_new = jnp.maximum(m_prev, mt_buf[h])
            p = jnp.exp2(s_buf[h] - m_new)
            alpha = jnp.exp2(m_prev - m_new)
            l_sc[h] = alpha * l_sc[h] + jnp.sum(p, 0, keepdims=True)
            pv = _dot(vt_ref[j, h * V_HEAD:(h + 1) * V_HEAD, :], p.astype(BF16))
            acc_sc[h] = alpha * acc_sc[h] + pv
            m_sc[h] = m_new

    scores(0, s0, mt0, masked=True)

    def pair(jj, c):
        scores(2 * jj + 1, s1, mt1)
        absorb(2 * jj, s0, mt0)
        scores(2 * jj + 2, s0, mt0)
        absorb(2 * jj + 1, s1, mt1)
        return c

    lax.fori_loop(0, i // 2, pair, 0)

    @pl.when(i % 2 == 1)
    def _():
        scores(i, s1, mt1)
        absorb(i - 1, s0, mt0)
        absorb(i, s1, mt1)

    @pl.when(i % 2 == 0)
    def _():
        absorb(i, s0, mt0)

    for h in range(heads):
        o_ref[:, h * V_HEAD:(h + 1) * V_HEAD] = (acc_sc[h] / l_sc[h]).T


def _attention(q, k, vt, heads):
    n = q.shape[0]
    t = ATTN_T
    assert t % CHUNK == 0 and n % t == 0 and MLA_HEADS % heads == 0
    once = pl.Buffered(1)
    return pl.pallas_call(
        functools.partial(_attn_body, heads=heads),
        grid=(MLA_HEADS // heads, n // t),
        in_specs=[pl.BlockSpec((t, heads * QK_PAD), lambda g, i: (i, g)),
                  pl.BlockSpec((n, heads * QK_PAD), lambda g, i: (0, g), pipeline_mode=once),
                  pl.BlockSpec((n // t, heads * V_HEAD, t), lambda g, i: (0, g, 0), pipeline_mode=once)],
        out_specs=pl.BlockSpec((t, heads * V_HEAD), lambda g, i: (i, g)),
        out_shape=jax.ShapeDtypeStruct((n, MLA_WIDTH), F32),
        scratch_shapes=[pltpu.VMEM((heads, t, t), F32)] * 2 + [pltpu.VMEM((heads, 1, t), F32)] * 4
        + [pltpu.VMEM((heads, V_HEAD, t), F32)],
        compiler_params=_params("arbitrary", "arbitrary"),
        name="attn",
    )(q, k, vt)


def _attn_cached_body(q_ref, kn_ref, vn_ref, ckv_ref, kpe_ref, wk_ref, wv_ref, o_ref, *, past, seq):
    ckv_b = ckv_ref[0].astype(BF16)
    knp = _dot(ckv_b, wk_ref[...])
    vp = _dot(ckv_b, wv_ref[...])
    kpe = kpe_ref[0].astype(BF16)
    qc = (past + lax.broadcasted_iota(jnp.int32, (seq, 1), 0)) // CHUNK
    kc_past = lax.broadcasted_iota(jnp.int32, (1, past), 1) // CHUNK
    kc_new = (past + lax.broadcasted_iota(jnp.int32, (1, seq), 1)) // CHUNK
    for h in range(MLA_HEADS):
        a = h * QK_PAD
        qh = q_ref[:, a:a + QK_PAD]
        kn_h = knp[:, h * QK_NOPE:(h + 1) * QK_NOPE].astype(BF16)
        s_p = _dot_nt(qh[:, :QK_NOPE], kn_h) + _dot_nt(qh[:, QK_NOPE:], kpe)
        s_n = _dot_nt(qh, kn_ref[:, a:a + QK_PAD])
        s_p = jnp.where(kc_past <= qc, s_p, NEG_INF)
        s_n = jnp.where(kc_new <= qc, s_n, NEG_INF)
        m = jnp.maximum(jnp.max(s_p, -1, keepdims=True), jnp.max(s_n, -1, keepdims=True))
        e_p, e_n = jnp.exp2(s_p - m), jnp.exp2(s_n - m)
        l = jnp.sum(e_p, -1, keepdims=True) + jnp.sum(e_n, -1, keepdims=True)
        o = _dot((e_p / l).astype(BF16), vp[:, h * V_HEAD:(h + 1) * V_HEAD].astype(BF16))
        o += _dot((e_n / l).astype(BF16), vn_ref[:, h * V_HEAD:(h + 1) * V_HEAD])
        o_ref[:, h * V_HEAD:(h + 1) * V_HEAD] = o


def _attention_cached(q, k, v, ckv_cache, kpe_cache, wk, wv, seq):
    n = q.shape[0]
    nb, past, _ = ckv_cache.shape
    row = lambda w: pl.BlockSpec((seq, w), lambda b: (b, 0))
    kern = functools.partial(_attn_cached_body, past=past, seq=seq)
    return pl.pallas_call(
        kern,
        grid=(nb,),
        in_specs=[row(MLA_HEADS * QK_PAD), row(MLA_HEADS * QK_PAD), row(MLA_WIDTH),
                  pl.BlockSpec((1, past, KV_LORA), lambda b: (b, 0, 0)),
                  pl.BlockSpec((1, past, LANES), lambda b: (b, 0, 0)),
                  _const(wk.shape), _const(wv.shape)],
        out_specs=row(MLA_WIDTH),
        out_shape=jax.ShapeDtypeStruct((n, MLA_WIDTH), F32),
        compiler_params=_params("parallel"),
        name="attn_cached",
    )(q, k, v, ckv_cache, kpe_cache, wk, wv)


def _mix_body(os_ref, om_ref, x_ref, gm_ref, wo_ref, lng_ref, lnb_ref, wxq_ref, h1_ref, qx_ref):
    om = _rms(om_ref[...], gm_ref[...]).astype(BF16)
    a = _dot(os_ref[...], wo_ref[:SSM_WIDTH, :]) + _dot(om, wo_ref[SSM_WIDTH:, :])
    h1 = _ln(ALPHA * x_ref[...] + a, lng_ref[0:1, :], lnb_ref[0:1, :])
    h1_ref[...] = h1
    qx_ref[...] = _dot(h1.astype(BF16), wxq_ref[...]).astype(BF16)


def _mix(o_ssm, o_mla, x2, g_mla, w_o, ln_g, ln_b, w_xq, tn):
    n = x2.shape[0]
    row = lambda w: pl.BlockSpec((tn, w), lambda i: (i, 0))
    return pl.pallas_call(
        _mix_body,
        grid=(n // tn,),
        in_specs=[row(SSM_WIDTH), row(MLA_WIDTH), row(D_MODEL), _const(g_mla.shape), _const(w_o.shape),
                  _const(ln_g.shape), _const(ln_b.shape), _const(w_xq.shape)],
        out_specs=[row(D_MODEL), row(D_MODEL)],
        out_shape=[jax.ShapeDtypeStruct((n, D_MODEL), F32), jax.ShapeDtypeStruct((n, D_MODEL), BF16)],
        compiler_params=_params("parallel"),
        name="mix",
    )(o_ssm, o_mla, x2, g_mla, w_o, ln_g, ln_b, w_xq)


def _mem_attn_heads(qx_ref, o_ref, head_kv):
    for h in range(X_HEADS):
        a = h * X_HEAD_DIM
        mk, mv = head_kv(h)
        s = _dot_nt(qx_ref[:, a:a + X_HEAD_DIM], mk.astype(BF16)) * (X_HEAD_DIM ** -0.5)
        e = jnp.exp(s - jnp.max(s, -1, keepdims=True))
        p = e / jnp.sum(e, -1, keepdims=True)
        o_ref[:, a:a + X_HEAD_DIM] = _dot(p.astype(BF16), mv.astype(BF16)).astype(BF16)


def _mem_attn_body(qx_ref, mk_ref, mv_ref, o_ref):
    def head_kv(h):
        a = h * X_HEAD_DIM
        return mk_ref[0, :, a:a + X_HEAD_DIM], mv_ref[0, :, a:a + X_HEAD_DIM]

    _mem_attn_heads(qx_ref, o_ref, head_kv)


def _mem_attn(qx, mem_k, mem_v, tq):
    n = qx.shape[0]
    nb = mem_k.shape[0]
    tiles = n // nb // tq
    row = pl.BlockSpec((tq, D_MODEL), lambda b, i: (b * tiles + i, 0))
    mem = pl.BlockSpec((1, N_MEM, D_MODEL), lambda b, i: (b, 0, 0))
    return pl.pallas_call(
        _mem_attn_body,
        grid=(nb, tiles),
        in_specs=[row, mem, mem],
        out_specs=row,
        out_shape=jax.ShapeDtypeStruct((n, D_MODEL), BF16),
        compiler_params=_params("parallel", "parallel"),
        name="mem_attn",
    )(qx, mem_k, mem_v)


def _mem_attn_cache_body(qx_ref, mk_hbm, mv_hbm, o_ref, kbuf, vbuf, sems):
    b = pl.program_id(0)
    nb = pl.num_programs(0)

    def copies(seq, slot):
        return [pltpu.make_async_copy(src.at[seq, :, h, :], buf.at[slot, h], sems.at[slot, t, h])
                for t, (src, buf) in enumerate(((mk_hbm, kbuf), (mv_hbm, vbuf))) for h in range(X_HEADS)]

    @pl.when(b == 0)
    def _():
        for c in copies(0, 0):
            c.start()

    @pl.when(b + 1 < nb)
    def _():
        for c in copies(b + 1, (b + 1) % 2):
            c.start()

    slot = b % 2
    for c in copies(b, slot):
        c.wait()
    _mem_attn_heads(qx_ref, o_ref, lambda h: (kbuf[slot, h], vbuf[slot, h]))


def _mem_attn_cache(qx, mem_k, mem_v, seq):
    n = qx.shape[0]
    nb = mem_k.shape[0]
    row = pl.BlockSpec((seq, D_MODEL), lambda b: (b, 0))
    hbm = pl.BlockSpec(memory_space=pl.ANY)
    buf = pltpu.VMEM((2, X_HEADS, N_MEM, X_HEAD_DIM), F32)
    return pl.pallas_call(
        _mem_attn_cache_body,
        grid=(nb,),
        in_specs=[row, hbm, hbm],
        out_specs=row,
        out_shape=jax.ShapeDtypeStruct((n, D_MODEL), BF16),
        scratch_shapes=[buf, buf, pltpu.SemaphoreType.DMA((2, 2, X_HEADS))],
        compiler_params=_params("arbitrary"),
        name="mem_attn_cache",
    )(qx, mem_k, mem_v)


def _mlp_body(h1_ref, ox_ref, wxo_ref, lng_ref, lnb_ref, w1_ref, w2_ref, y_ref, *, ff_blk):
    h2 = _ln(ALPHA * h1_ref[...] + _dot(ox_ref[...], wxo_ref[...]), lng_ref[1:2, :], lnb_ref[1:2, :])
    hb = h2.astype(BF16)
    acc = jnp.zeros(h2.shape, F32)
    for c in range(0, D_FF, ff_blk):
        z = jnp.maximum(_dot(hb, w1_ref[:, c:c + ff_blk]), 0.0)
        acc += _dot((z * z).astype(BF16), w2_ref[c:c + ff_blk, :])
    y_ref[...] = _ln(ALPHA * h2 + acc, lng_ref[2:3, :], lnb_ref[2:3, :])


def _mlp(h1, ox, w_xo, ln_g, ln_b, w1, w2, tn):
    n = h1.shape[0]
    row = pl.BlockSpec((tn, D_MODEL), lambda i: (i, 0))
    kern = functools.partial(_mlp_body, ff_blk=1024)
    return pl.pallas_call(
        kern,
        grid=(n // tn,),
        in_specs=[row, row, _const(w_xo.shape), _const(ln_g.shape), _const(ln_b.shape),
                  _const(w1.shape), _const(w2.shape)],
        out_specs=row,
        out_shape=jax.ShapeDtypeStruct((n, D_MODEL), F32),
        compiler_params=_params("parallel"),
        name="mlp",
    )(h1, ox, w_xo, ln_g, ln_b, w1, w2)


def _mem_kv_body(mem_ref, wk_ref, wv_ref, k_ref, v_ref):
    m = mem_ref[...].astype(BF16)
    k_ref[...] = _dot(m, wk_ref[...])
    v_ref[...] = _dot(m, wv_ref[...])


def _mem_kv(mem2, wk, wv):
    n = mem2.shape[0]
    out = jax.ShapeDtypeStruct((n, D_MODEL), F32)
    return pl.pallas_call(
        _mem_kv_body,
        grid=(1,),
        in_specs=[_const(mem2.shape), _const(wk.shape), _const(wv.shape)],
        out_specs=[_const((n, D_MODEL))] * 2,
        out_shape=[out, out],
        compiler_params=_params("arbitrary"),
        name="mem_kv",
    )(mem2, wk, wv)


def _state_to_cols(s):
    return jnp.transpose(s.reshape(s.shape[0], S5_COLS, -1), (1, 0, 2))


def _state_from_cols(s):
    return jnp.transpose(s, (1, 0, 2)).reshape(s.shape[1], SSM_GROUPS, SSM_STATE)


def _layer(x2, pos, nseq, h0r, h0i, mem_k, mem_v, caches, wts, s5_ops, tn, scan_ct, mem_tq):
    n = x2.shape[0]
    seq = n // nseq
    prompt = caches is None
    u, q, k, v, ckv, kpe = _project(x2, pos, wts["inv"], wts["w_in"], wts["g_q"], wts["w_q"], wts["g_kv"],
                                    wts["w_k"], wts["w_vt"] if prompt else wts["w_v"], tn, v_transposed=prompt)
    y_ssm, fr, fi = _s5_scan(u, s5_ops, _state_to_cols(h0r), _state_to_cols(h0i), nseq, scan_ct)
    o_ssm = _glu(y_ssm, u, wts["d_skip"], wts["w_glu"], wts["g_out_ssm"], tn)
    if prompt:
        o_mla = _attention(q, k, v, ATTN_HEADS)
    else:
        o_mla = _attention_cached(q, k, v, caches[0], caches[1], wts["w_k"], wts["w_v"], seq)
    h1, qx = _mix(o_ssm, o_mla, x2, wts["g_out_mla"], wts["w_o"], wts["ln_g"], wts["ln_b"], wts["w_xq"], tn)
    ox = _mem_attn(qx, mem_k, mem_v, mem_tq) if prompt else _mem_attn_cache(qx, mem_k, mem_v, seq)
    y = _mlp(h1, ox, wts["w_xo"], wts["ln_g"], wts["ln_b"], wts["w_ff1"], wts["w_ff2"], tn)
    return y, ckv, kpe, _state_from_cols(fr), _state_from_cols(fi)


def kernel(x_prompt, x_sample, mem_prompt, cache_mla_ckv, cache_mla_kpe, state_ssm_re, state_ssm_im, cache_mem_k, cache_mem_v, w_in, g_q, w_q_up, g_kv, w_kv_up, a_re, a_im, b_re, b_im, c_re, c_im, d_skip, log_dt, w_glu, g_out_ssm, g_out_mla, w_o, w_xq, w_xk, w_xv, w_xo, w_ff1, w_ff2, ln_g, ln_b):
    assert w_in.shape[0] == DEPTH == 1
    nbp, sp, _ = x_prompt.shape
    nbs, sd, _ = x_sample.shape
    past = cache_mla_ckv.shape[2]
    assert nbp == 1

    wq = jnp.pad(w_q_up[0], ((0, 0), (0, 0), (0, QK_PAD - QK_NOPE - QK_ROPE)))
    wk = w_kv_up[0][:, :, :QK_NOPE].reshape(KV_LORA, -1).astype(BF16)
    wv = w_kv_up[0][:, :, QK_NOPE:].reshape(KV_LORA, -1).astype(BF16)
    inv = ROPE_THETA ** (-jnp.arange(ROPE_HALF, dtype=F32) / ROPE_HALF)
    wts = {
        "inv": jnp.tile(inv, LANES // ROPE_HALF).reshape(1, LANES),
        "w_in": jnp.pad(w_in[0], ((0, 0), (0, LANES - QK_ROPE))).astype(BF16),
        "g_q": g_q[0].reshape(1, -1),
        "w_q": wq.reshape(Q_LORA, MLA_HEADS * QK_PAD).astype(BF16),
        "g_kv": g_kv[0].reshape(1, -1),
        "w_k": wk,
        "w_v": wv,
        "w_vt": wv.T,
        "d_skip": d_skip[0].reshape(1, -1),
        "w_glu": w_glu[0].astype(BF16),
        "g_out_ssm": g_out_ssm[0].reshape(1, -1),
        "g_out_mla": g_out_mla[0].reshape(1, -1),
        "w_o": w_o[0].astype(BF16),
        "w_xq": w_xq[0].reshape(D_MODEL, D_MODEL).astype(BF16),
        "w_xo": w_xo[0].reshape(D_MODEL, D_MODEL).astype(BF16),
        "w_ff1": w_ff1[0].astype(BF16),
        "w_ff2": w_ff2[0].astype(BF16),
        "ln_g": ln_g[0],
        "ln_b": ln_b[0],
    }
    s5_ops = _s5_prep(a_re[0], a_im[0], b_re[0], b_im[0], c_re[0], c_im[0], log_dt[0])

    mk, mv = _mem_kv(mem_prompt.reshape(nbp * N_MEM, D_MODEL),
                     w_xk[0].reshape(D_MODEL, D_MODEL).astype(BF16),
                     w_xv[0].reshape(D_MODEL, D_MODEL).astype(BF16))
    mk = mk.reshape(nbp, N_MEM, D_MODEL)
    mv = mv.reshape(nbp, N_MEM, D_MODEL)
    zero = jnp.zeros((nbp, SSM_GROUPS, SSM_STATE), F32)
    pos_p = jnp.arange(sp, dtype=F32).reshape(sp, 1)
    yp, ckv_p, kpe_p, sre_p, sim_p = _layer(
        x_prompt.reshape(sp, D_MODEL), pos_p, nbp, zero, zero, mk, mv, None,
        wts, s5_ops, tn=512, scan_ct=512, mem_tq=512)

    pos_s = jnp.tile(past + jnp.arange(sd, dtype=F32), nbs).reshape(nbs * sd, 1)
    caches = (cache_mla_ckv[0], jnp.pad(cache_mla_kpe[0], ((0, 0), (0, 0), (0, LANES - QK_ROPE))))
    ys, ckv_s, kpe_s, sre_s, sim_s = _layer(
        x_sample.reshape(nbs * sd, D_MODEL), pos_s, nbs,
        state_ssm_re[0], state_ssm_im[0], cache_mem_k[0], cache_mem_v[0], caches,
        wts, s5_ops, tn=512, scan_ct=sd // S5_T, mem_tq=sd)

    return (yp.reshape(nbp, sp, D_MODEL), ys.reshape(nbs, sd, D_MODEL),
            ckv_p.reshape(1, nbp, sp, KV_LORA), kpe_p.reshape(1, nbp, sp, QK_ROPE),
            sre_p.reshape(1, nbp, SSM_GROUPS, SSM_STATE), sim_p.reshape(1, nbp, SSM_GROUPS, SSM_STATE),
            mk.reshape(1, nbp, N_MEM, X_HEADS, X_HEAD_DIM), mv.reshape(1, nbp, N_MEM, X_HEADS, X_HEAD_DIM),
            ckv_s.reshape(1, nbs, sd, KV_LORA), kpe_s.reshape(1, nbs, sd, QK_ROPE),
            sre_s.reshape(1, nbs, SSM_GROUPS, SSM_STATE), sim_s.reshape(1, nbs, SSM_GROUPS, SSM_STATE))
```

```python
import functools
import math

import jax
import jax.numpy as jnp
from jax import lax
from jax.experimental import pallas as pl
from jax.experimental.pallas import tpu as pltpu

F32 = jnp.float32
BF16 = jnp.bfloat16

D_MODEL = 1024
DEPTH = 1
CHUNK = 64
SSM_WIDTH = 512
SSM_GROUP = 16
SSM_GROUPS = 32
SSM_STATE = 64
MLA_HEADS = 4
QK_NOPE = 128
QK_ROPE = 64
V_HEAD = 128
MLA_WIDTH = MLA_HEADS * V_HEAD
Q_LORA = 384
KV_LORA = 256
ROPE_THETA = 10000.0
MLA_SCALE = (QK_NOPE + QK_ROPE) ** -0.5
N_MEM = 256
X_HEADS = 4
X_HEAD_DIM = D_MODEL // X_HEADS
D_FF = 4 * D_MODEL
ALPHA = (2 * DEPTH) ** 0.25
EPS = 1e-5
NEG_INF = -1e30

LANES = 128
QK_PAD = 256
S5_T = 16
S5_COL_GROUPS = LANES // SSM_GROUP
S5_COLS = SSM_GROUPS // S5_COL_GROUPS
S5_SW = S5_COL_GROUPS * SSM_STATE
ROPE_HALF = QK_ROPE // 2
Q_SCALE = MLA_SCALE * math.log2(math.e)
ATTN_T = 512
ATTN_HEADS = 2
VMEM_LIMIT = 56 * 1024 * 1024

_NT = (((1,), (1,)), ((), ()))


def _rms(x, g):
    return x * lax.rsqrt(jnp.mean(x * x, -1, keepdims=True) + EPS) * g


def _ln(x, g, b):
    mu = jnp.mean(x, -1, keepdims=True)
    xc = x - mu
    var = jnp.mean(xc * xc, -1, keepdims=True)
    return xc * lax.rsqrt(var + EPS) * g + b


def _dot(a, b):
    return jnp.dot(a, b, preferred_element_type=F32)


def _dot_nt(a, b):
    return lax.dot_general(a, b, _NT, preferred_element_type=F32)


def _dot_hi(a, b):
    return jnp.dot(a, b, preferred_element_type=F32, precision=lax.Precision.HIGHEST)


def _params(*sem):
    return pltpu.CompilerParams(dimension_semantics=sem, vmem_limit_bytes=VMEM_LIMIT)


def _const(shape):
    n = len(shape)
    return pl.BlockSpec(shape, lambda *_: (0,) * n)


def _proj_body(x_ref, pos_ref, inv_ref, w_in_ref, gq_ref, wq_ref, gkv_ref, wk_ref, wv_ref,
               u_ref, q_ref, k_ref, v_ref, ckv_ref, kpe_ref, tc_ref, ts_ref, *, v_transposed, consecutive):
    x = x_ref[...].astype(BF16)
    proj = _dot(x, w_in_ref[...])
    u_ref[...] = proj[:, :SSM_WIDTH]
    cq = _rms(proj[:, SSM_WIDTH:SSM_WIDTH + Q_LORA], gq_ref[...])
    q = _dot(cq.astype(BF16), wq_ref[...]) * Q_SCALE
    c0 = SSM_WIDTH + Q_LORA
    ckv = _rms(proj[:, c0:c0 + KV_LORA], gkv_ref[...])
    ckv_ref[...] = ckv
    ckv_b = ckv.astype(BF16)
    kn = _dot(ckv_b, wk_ref[...])
    if v_transposed:
        for c in range(v_ref.shape[0]):
            rows = ckv_b[c * ATTN_T:(c + 1) * ATTN_T, :]
            v_ref[c] = _dot_nt(wv_ref[...], rows).astype(BF16)
    else:
        v_ref[...] = _dot(ckv_b, wv_ref[...]).astype(BF16)

    lane = lax.broadcasted_iota(jnp.int32, (1, LANES), 1)
    live = lane < QK_ROPE
    sign = jnp.where(lane < ROPE_HALF, -1.0, 1.0)
    inv = inv_ref[...]
    if consecutive:
        @pl.when(pl.program_id(0) == 0)
        def _():
            r = lax.broadcasted_iota(jnp.int32, (x_ref.shape[0], 1), 0).astype(F32)
            tc_ref[...] = jnp.where(live, jnp.cos(r * inv), 0.0)
            ts_ref[...] = jnp.where(live, jnp.sin(r * inv), 0.0)

        base = pos_ref[0:1, :] * inv
        ca, sa = jnp.cos(base), jnp.sin(base)
        tc, ts = tc_ref[...], ts_ref[...]
        cos_t = ca * tc - sa * ts
        sin_t = (sa * sign) * tc + (ca * sign) * ts
    else:
        ang = pos_ref[...] * inv
        cos_t = jnp.where(live, jnp.cos(ang), 0.0)
        sin_t = jnp.where(live, jnp.sin(ang) * sign, 0.0)

    def rope(c2):
        swapped = jnp.where(lane < ROPE_HALF, pltpu.roll(c2, LANES - ROPE_HALF, 1), pltpu.roll(c2, ROPE_HALF, 1))
        return c2 * cos_t + swapped * sin_t

    kpe = rope(proj[:, c0 + KV_LORA:])
    kpe_ref[...] = kpe[:, :QK_ROPE]
    kpe_b = kpe.astype(BF16)
    for h in range(MLA_HEADS):
        a = h * QK_PAD
        q_ref[:, a:a + QK_NOPE] = q[:, a:a + QK_NOPE].astype(BF16)
        q_ref[:, a + QK_NOPE:a + QK_PAD] = rope(q[:, a + QK_NOPE:a + QK_PAD]).astype(BF16)
        k_ref[:, a:a + QK_NOPE] = kn[:, h * QK_NOPE:(h + 1) * QK_NOPE].astype(BF16)
        k_ref[:, a + QK_NOPE:a + QK_PAD] = kpe_b


def _project(x2, pos, inv, w_in, gq, wq, gkv, wk, wv, tn, v_transposed, consecutive):
    n = x2.shape[0]
    row = lambda w: pl.BlockSpec((tn, w), lambda i: (i, 0))
    if v_transposed:
        assert tn % ATTN_T == 0
        v_spec = pl.BlockSpec((tn // ATTN_T, MLA_WIDTH, ATTN_T), lambda i: (i, 0, 0))
        v_shape = jax.ShapeDtypeStruct((n // ATTN_T, MLA_WIDTH, ATTN_T), BF16)
    else:
        v_spec, v_shape = row(MLA_WIDTH), jax.ShapeDtypeStruct((n, MLA_WIDTH), BF16)
    return pl.pallas_call(
        functools.partial(_proj_body, v_transposed=v_transposed, consecutive=consecutive),
        grid=(n // tn,),
        scratch_shapes=[pltpu.VMEM((tn, LANES), F32)] * 2,
        in_specs=[row(D_MODEL), row(1), _const(inv.shape), _const(w_in.shape), _const(gq.shape),
                  _const(wq.shape), _const(gkv.shape), _const(wk.shape), _const(wv.shape)],
        out_specs=[row(SSM_WIDTH), row(MLA_HEADS * QK_PAD), row(MLA_HEADS * QK_PAD),
                   v_spec, row(KV_LORA), row(QK_ROPE)],
        out_shape=[jax.ShapeDtypeStruct((n, SSM_WIDTH), F32),
                   jax.ShapeDtypeStruct((n, MLA_HEADS * QK_PAD), BF16),
                   jax.ShapeDtypeStruct((n, MLA_HEADS * QK_PAD), BF16),
                   v_shape,
                   jax.ShapeDtypeStruct((n, KV_LORA), F32),
                   jax.ShapeDtypeStruct((n, QK_ROPE), F32)],
        compiler_params=_params("arbitrary"),
        name="proj",
    )(x2, pos, inv, w_in, gq, wq, gkv, wk, wv)


def _s5_prep_body(ar_row, ai_row, ldt_row, bt_re, bt_im, ct_re, ct_im,
                  wt_ref, v_ref, z_ref, are_ref, aim_ref):
    sw = S5_SW
    arr, air, dtr = ar_row[0], ai_row[0], jnp.exp(ldt_row[0])

    mag = jnp.exp(arr * dtr)
    lr, li = mag * jnp.cos(air * dtr), mag * jnp.sin(air * dtr)
    nr, ni = lr - 1.0, li
    den = arr * arr + air * air
    f_re, f_im = (nr * arr + ni * air) / den, (ni * arr - nr * air) / den

    same_b = (lax.broadcasted_iota(jnp.int32, (LANES, sw), 0) // SSM_GROUP
              == lax.broadcasted_iota(jnp.int32, (LANES, sw), 1) // SSM_STATE)
    br = jnp.where(same_b, bt_re[0], 0.0)
    bi = jnp.where(same_b, bt_im[0], 0.0)
    bb_re = f_re * br - f_im * bi
    bb_im = f_re * bi + f_im * br
    same_c = (lax.broadcasted_iota(jnp.int32, (sw, LANES), 0) // SSM_STATE
              == lax.broadcasted_iota(jnp.int32, (sw, LANES), 1) // SSM_GROUP)
    cr = jnp.where(same_c, ct_re[0], 0.0)
    ci = jnp.where(same_c, ct_im[0], 0.0)

    e = lax.broadcasted_iota(jnp.int32, (2 * S5_T, 1), 0).astype(F32)
    pm = jnp.exp(arr * dtr * e)
    pw_re, pw_im = pm * jnp.cos(air * dtr * e), pm * jnp.sin(air * dtr * e)
    pt_re, pt_im = pw_re.T, pw_im.T

    lag_ops = []
    for lag in range(S5_T):
        p_re, p_im = pw_re[lag:lag + 1, :], pw_im[lag:lag + 1, :]
        k_re = p_re * bb_re - p_im * bb_im
        k_im = p_re * bb_im + p_im * bb_re
        i = S5_T - 1 - lag
        v_ref[0, i * LANES:(i + 1) * LANES, :sw] = k_re.astype(BF16)
        v_ref[0, i * LANES:(i + 1) * LANES, sw:] = k_im.astype(BF16)
        lag_ops.append(_dot_hi(k_re, cr) - _dot_hi(k_im, ci))

        q_re, q_im = pt_re[:, lag + 1:lag + 2], pt_im[:, lag + 1:lag + 2]
        z_ref[0, :sw, lag * LANES:(lag + 1) * LANES] = (cr * q_re - ci * q_im).astype(BF16)
        z_ref[0, sw:, lag * LANES:(lag + 1) * LANES] = (-(cr * q_im + ci * q_re)).astype(BF16)

    zero = jnp.zeros((LANES, LANES), BF16)
    for d in range(S5_T // 2):
        wt_ref[0, d, :LANES, :LANES] = lag_ops[2 * d].astype(BF16)
        wt_ref[0, d, :LANES, LANES:] = lag_ops[2 * d + 1].astype(BF16)
        wt_ref[0, d, LANES:, :LANES] = lag_ops[2 * d - 1].astype(BF16) if d else zero
        wt_ref[0, d, LANES:, LANES:] = lag_ops[2 * d].astype(BF16)

    are_ref[0] = pw_re[S5_T:S5_T + 1, :]
    aim_ref[0] = pw_im[S5_T:S5_T + 1, :]


def _s5_prep(a_re, a_im, b_re, b_im, c_re, c_im, log_dt):
    nc, r, t, h, p, sw = S5_COLS, S5_COL_GROUPS, S5_T, SSM_GROUP, SSM_STATE, S5_SW
    ldt = jnp.repeat(log_dt, p)
    bt = lambda b: jnp.tile(jnp.transpose(b.reshape(nc, r, p, h), (0, 1, 3, 2)).reshape(nc, r * h, p), (1, 1, r))
    ct = lambda c: jnp.tile(jnp.transpose(c.reshape(nc, r, h, p), (0, 1, 3, 2)).reshape(nc, r * p, h), (1, 1, r))
    args = (a_re.reshape(nc, 1, sw), a_im.reshape(nc, 1, sw), ldt.reshape(nc, 1, sw),
            bt(b_re), bt(b_im), ct(c_re), ct(c_im))
    blk = lambda s: pl.BlockSpec((1,) + s[1:], lambda i: (i,) + (0,) * (len(s) - 1))
    outs = [((nc, t // 2, 2 * LANES, 2 * LANES), BF16), ((nc, t * LANES, 2 * sw), BF16),
            ((nc, 2 * sw, t * LANES), BF16), ((nc, 1, sw), F32), ((nc, 1, sw), F32)]
    return pl.pallas_call(
        _s5_prep_body,
        grid=(nc,),
        in_specs=[blk(a.shape) for a in args],
        out_specs=[blk(s) for s, _ in outs],
        out_shape=[jax.ShapeDtypeStruct(s, d) for s, d in outs],
        compiler_params=_params("parallel"),
        name="s5_prep",
    )(*args)


def _s5_scan_body(u_ref, wt_ref, v_ref, z_ref, are_ref, aim_ref, h0r_ref, h0i_ref,
                  y_ref, fr_ref, fi_ref, lhs, sre, sim, xre, xim, cr, ci, *, nseq, ct):
    t = pl.program_id(1)
    sw = S5_SW
    seq = ct * S5_T

    @pl.when(t == 0)
    def _():
        cr[...] = h0r_ref[0]
        ci[...] = h0i_ref[0]

    def token_rows(cc, i):
        if nseq == 1:
            return pl.ds(i, ct, stride=S5_T), slice(None)
        return pl.ds(cc * S5_T + i, nseq, stride=seq), slice(cc * nseq, (cc + 1) * nseq)

    for cc in range(1 if nseq == 1 else ct):
        for i in range(S5_T):
            tok, crow = token_rows(cc, i)
            lhs[crow, i * LANES:(i + 1) * LANES] = u_ref[tok, :].astype(BF16)

    s = _dot(lhs[...], v_ref[0])
    sre[...] = s[:, :sw]
    sim[...] = s[:, sw:]

    a_re, a_im = are_ref[0], aim_ref[0]

    def step(c, carry):
        x_re, x_im = carry
        r = pl.ds(c * nseq, nseq)
        xre[r, :] = x_re
        xim[r, :] = x_im
        n_re = a_re * x_re - a_im * x_im + sre[r, :]
        n_im = a_re * x_im + a_im * x_re + sim[r, :]
        return n_re, n_im

    x_re, x_im = lax.fori_loop(0, ct, step, (cr[...], ci[...]))
    cr[...] = x_re
    ci[...] = x_im
    fr_ref[0] = x_re
    fi_ref[0] = x_im

    xb_re, xb_im = xre[...].astype(BF16), xim[...].astype(BF16)
    for jp in range(S5_T // 2):
        cols = slice(jp * 2 * LANES, (jp + 1) * 2 * LANES)
        acc = _dot(xb_re, z_ref[0, :sw, cols]) + _dot(xb_im, z_ref[0, sw:, cols])
        for ip in range(jp + 1):
            acc += _dot(lhs[:, ip * 2 * LANES:(ip + 1) * 2 * LANES], wt_ref[0, jp - ip])
        for cc in range(1 if nseq == 1 else ct):
            for jj in range(2):
                tok, crow = token_rows(cc, 2 * jp + jj)
                y_ref[tok, :] = acc[crow, jj * LANES:(jj + 1) * LANES]


def _s5_scan(u, col_ops, h0r, h0i, nseq, ct):
    wt, v, z, are, aim = col_ops
    n = u.shape[0]
    cps = n // nseq // S5_T
    assert nseq == 1 or ct == cps
    nc = nseq * ct
    sw = S5_SW
    tile = pl.BlockSpec((nc * S5_T, LANES), lambda c, t: (t, c))
    op = lambda a: pl.BlockSpec((1,) + a.shape[1:], lambda c, t: (c,) + (0,) * (a.ndim - 1))
    st = pl.BlockSpec((1, nseq, sw), lambda c, t: (c, 0, 0))
    st_shape = jax.ShapeDtypeStruct((S5_COLS, nseq, sw), F32)
    return pl.pallas_call(
        functools.partial(_s5_scan_body, nseq=nseq, ct=ct),
        grid=(S5_COLS, cps // ct),
        in_specs=[tile, op(wt), op(v), op(z), op(are), op(aim), st, st],
        out_specs=[tile, st, st],
        out_shape=[jax.ShapeDtypeStruct(u.shape, F32), st_shape, st_shape],
        scratch_shapes=[pltpu.VMEM((nc, S5_T * LANES), BF16)] + [pltpu.VMEM((nc, sw), F32)] * 4
        + [pltpu.VMEM((nseq, sw), F32)] * 2,
        compiler_params=_params("arbitrary", "arbitrary"),
        name="s5_scan",
    )(u, wt, v, z, are, aim, h0r, h0i)


def _glu_body(y_ref, u_ref, d_ref, w_ref, g_ref, o_ref):
    y = y_ref[...] + d_ref[...] * u_ref[...]
    gl = _dot(jax.nn.gelu(y).astype(BF16), w_ref[...])
    o = gl[:, :SSM_WIDTH] * jax.nn.sigmoid(gl[:, SSM_WIDTH:])
    o_ref[...] = _rms(o, g_ref[...]).astype(BF16)


def _glu(y, u, d, w, g, tn):
    n = y.shape[0]
    row = pl.BlockSpec((tn, SSM_WIDTH), lambda i: (i, 0))
    return pl.pallas_call(
        _glu_body,
        grid=(n // tn,),
        in_specs=[row, row, _const(d.shape), _const(w.shape), _const(g.shape)],
        out_specs=row,
        out_shape=jax.ShapeDtypeStruct((n, SSM_WIDTH), BF16),
        compiler_params=_params("parallel"),
        name="glu",
    )(y, u, d, w, g)


def _attn_body(q_ref, k_ref, vt_ref, o_ref, s0, s1, mt0, mt1, m_sc, l_sc, acc_sc, *, heads):
    i = pl.program_id(1)
    t = ATTN_T
    m_sc[...] = jnp.full(m_sc.shape, -jnp.inf, F32)
    l_sc[...] = jnp.zeros(l_sc.shape, F32)
    acc_sc[...] = jnp.zeros(acc_sc.shape, F32)

    def scores(tile, s_buf, mt_buf, diag=None):
        r = pl.ds(pl.multiple_of(tile * t, t), t)
        for h in range(heads):
            s = _dot_nt(k_ref[r, h * QK_PAD:(h + 1) * QK_PAD],
                        q_ref[:, h * QK_PAD:(h + 1) * QK_PAD])
            if diag is not None:
                kc = (diag * t + lax.broadcasted_iota(jnp.int32, s.shape, 0)) // CHUNK
                qc = lax.broadcasted_iota(jnp.int32, s.shape, 1) // CHUNK
                s = jnp.where(kc <= qc, s, NEG_INF)
            s_buf[h] = s
            mt_buf[h] = jnp.max(s, 0, keepdims=True)

    def absorb(tile, s_buf, mt_buf):
        for h in range(heads):
            m_prev = m_sc[h]
            m_new = jnp.maximum(m_prev, mt_buf[h])
            p = jnp.exp2(s_buf[h] - m_new)
            alpha = jnp.exp2(m_prev - m_new)
            l_sc[h] = alpha * l_sc[h] + jnp.sum(p, 0, keepdims=True)
            pv = _dot(vt_ref[tile, h * V_HEAD:(h + 1) * V_HEAD, :], p.astype(BF16))
            acc_sc[h] = alpha * acc_sc[h] + pv
            m_sc[h] = m_new

    scores(2 * i, s0, mt0, diag=0)
    scores(2 * i + 1, s1, mt1, diag=1)
    absorb(2 * i, s0, mt0)

    def pair(jj, c):
        scores(2 * jj, s0, mt0)
        absorb(jnp.where(jj == 0, 2 * i + 1, 2 * jj - 1), s1, mt1)
        scores(2 * jj + 1, s1, mt1)
        absorb(2 * jj, s0, mt0)
        return c

    lax.fori_loop(0, i, pair, 0)
    absorb(jnp.where(i == 0, 1, 2 * i - 1), s1, mt1)

    for h in range(heads):
        o_ref[:, h * V_HEAD:(h + 1) * V_HEAD] = (acc_sc[h] / l_sc[h]).T


def _attention(q, k, vt, heads):
    n = q.shape[0]
    t = ATTN_T
    tq = 2 * t
    assert t % CHUNK == 0 and n % tq == 0 and MLA_HEADS % heads == 0
    once = pl.Buffered(1)
    return pl.pallas_call(
        functools.partial(_attn_body, heads=heads),
        grid=(MLA_HEADS // heads, n // tq),
        in_specs=[pl.BlockSpec((tq, heads * QK_PAD), lambda g, i: (i, g)),
                  pl.BlockSpec((n, heads * QK_PAD), lambda g, i: (0, g), pipeline_mode=once),
                  pl.BlockSpec((n // t, heads * V_HEAD, t), lambda g, i: (0, g, 0), pipeline_mode=once)],
        out_specs=pl.BlockSpec((tq, heads * V_HEAD), lambda g, i: (i, g)),
        out_shape=jax.ShapeDtypeStruct((n, MLA_WIDTH), F32),
        scratch_shapes=[pltpu.VMEM((heads, t, tq), F32)] * 2 + [pltpu.VMEM((heads, 1, tq), F32)] * 4
        + [pltpu.VMEM((heads, V_HEAD, tq), F32)],
        compiler_params=_params("arbitrary", "arbitrary"),
        name="attn",
    )(q, k, vt)


def _attn_cached_body(q_ref, kn_ref, vn_ref, ckv_ref, kpe_ref, wk_ref, wv_ref, o_ref, *, past, seq, nsub):
    heads = range(MLA_HEADS)
    pos = past + lax.broadcasted_iota(jnp.int32, (seq, 1), 0)
    qc = jnp.concatenate([pos] * MLA_HEADS, axis=0) // CHUNK
    kc_past = lax.broadcasted_iota(jnp.int32, (1, past), 1) // CHUNK
    kc_new = (past + lax.broadcasted_iota(jnp.int32, (1, seq), 1)) // CHUNK
    for b in range(nsub):
        rows = slice(b * seq, (b + 1) * seq)
        ckv_b = ckv_ref[b].astype(BF16)
        kpe_b = kpe_ref[b].astype(BF16)
        qa = jnp.concatenate(
            [_dot_nt(q_ref[rows, h * QK_PAD:h * QK_PAD + QK_NOPE], wk_ref[:, h * QK_NOPE:(h + 1) * QK_NOPE])
             for h in heads], axis=0)
        qpe = jnp.concatenate(
            [q_ref[rows, h * QK_PAD + QK_NOPE:h * QK_PAD + QK_NOPE + QK_ROPE] for h in heads], axis=0)
        s_p = _dot_nt(qa.astype(BF16), ckv_b) + _dot_nt(qpe, kpe_b)
        s_n = jnp.concatenate(
            [_dot_nt(q_ref[rows, h * QK_PAD:(h + 1) * QK_PAD], kn_ref[rows, h * QK_PAD:(h + 1) * QK_PAD])
             for h in heads], axis=0)
        s_p = jnp.where(kc_past <= qc, s_p, NEG_INF)
        s_n = jnp.where(kc_new <= qc, s_n, NEG_INF)
        m = jnp.maximum(jnp.max(s_p, -1, keepdims=True), jnp.max(s_n, -1, keepdims=True))
        e_p, e_n = jnp.exp2(s_p - m), jnp.exp2(s_n - m)
        l = jnp.sum(e_p, -1, keepdims=True) + jnp.sum(e_n, -1, keepdims=True)
        p_n = (e_n / l).astype(BF16)
        o_lat = _dot((e_p / l).astype(BF16), ckv_b).astype(BF16)
        for h in heads:
            hr = slice(h * seq, (h + 1) * seq)
            cols = slice(h * V_HEAD, (h + 1) * V_HEAD)
            o_ref[rows, cols] = _dot(o_lat[hr], wv_ref[:, cols]) + _dot(p_n[hr], vn_ref[rows, cols])


def _attention_cached(q, k, v, ckv_cache, kpe_cache, wk, wv, seq, nsub):
    n = q.shape[0]
    nb, past, _ = ckv_cache.shape
    assert nb % nsub == 0
    row = lambda w: pl.BlockSpec((nsub * seq, w), lambda b: (b, 0))
    kern = functools.partial(_attn_cached_body, past=past, seq=seq, nsub=nsub)
    return pl.pallas_call(
        kern,
        grid=(nb // nsub,),
        in_specs=[row(MLA_HEADS * QK_PAD), row(MLA_HEADS * QK_PAD), row(MLA_WIDTH),
                  pl.BlockSpec((nsub, past, KV_LORA), lambda b: (b, 0, 0)),
                  pl.BlockSpec((nsub, past, QK_ROPE), lambda b: (b, 0, 0)),
                  _const(wk.shape), _const(wv.shape)],
        out_specs=row(MLA_WIDTH),
        out_shape=jax.ShapeDtypeStruct((n, MLA_WIDTH), F32),
        compiler_params=_params("parallel"),
        name="attn_cached",
    )(q, k, v, ckv_cache, kpe_cache, wk, wv)


def _mix_body(os_ref, om_ref, x_ref, gm_ref, wo_ref, lng_ref, lnb_ref, wxq_ref, h1_ref, qx_ref):
    om = _rms(om_ref[...], gm_ref[...]).astype(BF16)
    a = _dot(os_ref[...], wo_ref[:SSM_WIDTH, :]) + _dot(om, wo_ref[SSM_WIDTH:, :])
    h1 = _ln(ALPHA * x_ref[...] + a, lng_ref[0:1, :], lnb_ref[0:1, :])
    h1_ref[...] = h1
    qx_ref[...] = _dot(h1.astype(BF16), wxq_ref[...]).astype(BF16)


def _mix(o_ssm, o_mla, x2, g_mla, w_o, ln_g, ln_b, w_xq, tn):
    n = x2.shape[0]
    row = lambda w: pl.BlockSpec((tn, w), lambda i: (i, 0))
    return pl.pallas_call(
        _mix_body,
        grid=(n // tn,),
        in_specs=[row(SSM_WIDTH), row(MLA_WIDTH), row(D_MODEL), _const(g_mla.shape), _const(w_o.shape),
                  _const(ln_g.shape), _const(ln_b.shape), _const(w_xq.shape)],
        out_specs=[row(D_MODEL), row(D_MODEL)],
        out_shape=[jax.ShapeDtypeStruct((n, D_MODEL), F32), jax.ShapeDtypeStruct((n, D_MODEL), BF16)],
        compiler_params=_params("parallel"),
        name="mix",
    )(o_ssm, o_mla, x2, g_mla, w_o, ln_g, ln_b, w_xq)


def _mem_attn_heads(qx_ref, o_ref, head_kv):
    for h in range(X_HEADS):
        a = h * X_HEAD_DIM
        mk, mv = head_kv(h)
        s = _dot_nt(qx_ref[:, a:a + X_HEAD_DIM], mk.astype(BF16)) * (X_HEAD_DIM ** -0.5)
        e = jnp.exp(s - jnp.max(s, -1, keepdims=True))
        p = e / jnp.sum(e, -1, keepdims=True)
        o_ref[:, a:a + X_HEAD_DIM] = _dot(p.astype(BF16), mv.astype(BF16)).astype(BF16)


def _mem_attn_body(qx_ref, mk_ref, mv_ref, o_ref):
    def head_kv(h):
        a = h * X_HEAD_DIM
        return mk_ref[0, :, a:a + X_HEAD_DIM], mv_ref[0, :, a:a + X_HEAD_DIM]

    _mem_attn_heads(qx_ref, o_ref, head_kv)


def _mem_attn(qx, mem_k, mem_v, tq):
    n = qx.shape[0]
    nb = mem_k.shape[0]
    tiles = n // nb // tq
    row = pl.BlockSpec((tq, D_MODEL), lambda b, i: (b * tiles + i, 0))
    mem = pl.BlockSpec((1, N_MEM, D_MODEL), lambda b, i: (b, 0, 0))
    return pl.pallas_call(
        _mem_attn_body,
        grid=(nb, tiles),
        in_specs=[row, mem, mem],
        out_specs=row,
        out_shape=jax.ShapeDtypeStruct((n, D_MODEL), BF16),
        compiler_params=_params("parallel", "parallel"),
        name="mem_attn",
    )(qx, mem_k, mem_v)


def _mem_attn_cache_body(qx_ref, mk_hbm, mv_hbm, o_ref, kbuf, vbuf, sems):
    b = pl.program_id(0)
    nb = pl.num_programs(0)

    def copies(seq, slot):
        return [pltpu.make_async_copy(src.at[seq, :, h, :], buf.at[slot, h], sems.at[slot, t, h])
                for t, (src, buf) in enumerate(((mk_hbm, kbuf), (mv_hbm, vbuf))) for h in range(X_HEADS)]

    @pl.when(b == 0)
    def _():
        for c in copies(0, 0):
            c.start()

    @pl.when(b + 1 < nb)
    def _():
        for c in copies(b + 1, (b + 1) % 2):
            c.start()

    slot = b % 2
    for c in copies(b, slot):
        c.wait()
    _mem_attn_heads(qx_ref, o_ref, lambda h: (kbuf[slot, h], vbuf[slot, h]))


def _mem_attn_cache(qx, mem_k, mem_v, seq):
    n = qx.shape[0]
    nb = mem_k.shape[0]
    row = pl.BlockSpec((seq, D_MODEL), lambda b: (b, 0))
    hbm = pl.BlockSpec(memory_space=pl.ANY)
    buf = pltpu.VMEM((2, X_HEADS, N_MEM, X_HEAD_DIM), F32)
    return pl.pallas_call(
        _mem_attn_cache_body,
        grid=(nb,),
        in_specs=[row, hbm, hbm],
        out_specs=row,
        out_shape=jax.ShapeDtypeStruct((n, D_MODEL), BF16),
        scratch_shapes=[buf, buf, pltpu.SemaphoreType.DMA((2, 2, X_HEADS))],
        compiler_params=_params("arbitrary"),
        name="mem_attn_cache",
    )(qx, mem_k, mem_v)


def _mlp_body(h1_ref, ox_ref, wxo_ref, lng_ref, lnb_ref, w1_ref, w2_ref, y_ref, *, ff_blk):
    h2 = _ln(ALPHA * h1_ref[...] + _dot(ox_ref[...], wxo_ref[...]), lng_ref[1:2, :], lnb_ref[1:2, :])
    hb = h2.astype(BF16)
    acc = jnp.zeros(h2.shape, F32)
    for c in range(0, D_FF, ff_blk):
        z = jnp.maximum(_dot(hb, w1_ref[:, c:c + ff_blk]), 0.0)
        acc += _dot((z * z).astype(BF16), w2_ref[c:c + ff_blk, :])
    y_ref[...] = _ln(ALPHA * h2 + acc, lng_ref[2:3, :], lnb_ref[2:3, :])


def _mlp(h1, ox, w_xo, ln_g, ln_b, w1, w2, tn):
    n = h1.shape[0]
    row = pl.BlockSpec((tn, D_MODEL), lambda i: (i, 0))
    kern = functools.partial(_mlp_body, ff_blk=1024)
    return pl.pallas_call(
        kern,
        grid=(n // tn,),
        in_specs=[row, row, _const(w_xo.shape), _const(ln_g.shape), _const(ln_b.shape),
                  _const(w1.shape), _const(w2.shape)],
        out_specs=row,
        out_shape=jax.ShapeDtypeStruct((n, D_MODEL), F32),
        compiler_params=_params("parallel"),
        name="mlp",
    )(h1, ox, w_xo, ln_g, ln_b, w1, w2)


def _mem_kv_body(mem_ref, wk_ref, wv_ref, k_ref, v_ref):
    m = mem_ref[...].astype(BF16)
    k_ref[...] = _dot(m, wk_ref[...])
    v_ref[...] = _dot(m, wv_ref[...])


def _mem_kv(mem2, wk, wv):
    n = mem2.shape[0]
    out = jax.ShapeDtypeStruct((n, D_MODEL), F32)
    return pl.pallas_call(
        _mem_kv_body,
        grid=(1,),
        in_specs=[_const(mem2.shape), _const(wk.shape), _const(wv.shape)],
        out_specs=[_const((n, D_MODEL))] * 2,
        out_shape=[out, out],
        compiler_params=_params("arbitrary"),
        name="mem_kv",
    )(mem2, wk, wv)


def _state_to_cols(s):
    return jnp.transpose(s.reshape(s.shape[0], S5_COLS, -1), (1, 0, 2))


def _state_from_cols(s):
    return jnp.transpose(s, (1, 0, 2)).reshape(s.shape[1], SSM_GROUPS, SSM_STATE)


def _layer(x2, pos, nseq, h0r, h0i, mem_k, mem_v, caches, wts, s5_ops, tn, scan_ct, mem_tq):
    n = x2.shape[0]
    seq = n // nseq
    prompt = caches is None
    u, q, k, v, ckv, kpe = _project(x2, pos, wts["inv"], wts["w_in"], wts["g_q"], wts["w_q"], wts["g_kv"],
                                    wts["w_k"], wts["w_vt"] if prompt else wts["w_v"], tn,
                                    v_transposed=prompt, consecutive=prompt)
    y_ssm, fr, fi = _s5_scan(u, s5_ops, _state_to_cols(h0r), _state_to_cols(h0i), nseq, scan_ct)
    o_ssm = _glu(y_ssm, u, wts["d_skip"], wts["w_glu"], wts["g_out_ssm"], tn)
    if prompt:
        o_mla = _attention(q, k, v, ATTN_HEADS)
    else:
        o_mla = _attention_cached(q, k, v, caches[0], caches[1], wts["w_k"], wts["w_v"], seq, nsub=2)
    h1, qx = _mix(o_ssm, o_mla, x2, wts["g_out_mla"], wts["w_o"], wts["ln_g"], wts["ln_b"], wts["w_xq"], tn)
    ox = _mem_attn(qx, mem_k, mem_v, mem_tq) if prompt else _mem_attn_cache(qx, mem_k, mem_v, seq)
    y = _mlp(h1, ox, wts["w_xo"], wts["ln_g"], wts["ln_b"], wts["w_ff1"], wts["w_ff2"], tn)
    return y, ckv, kpe, _state_from_cols(fr), _state_from_cols(fi)


def kernel(x_prompt, x_sample, mem_prompt, cache_mla_ckv, cache_mla_kpe, state_ssm_re, state_ssm_im, cache_mem_k, cache_mem_v, w_in, g_q, w_q_up, g_kv, w_kv_up, a_re, a_im, b_re, b_im, c_re, c_im, d_skip, log_dt, w_glu, g_out_ssm, g_out_mla, w_o, w_xq, w_xk, w_xv, w_xo, w_ff1, w_ff2, ln_g, ln_b):
    assert w_in.shape[0] == DEPTH == 1
    nbp, sp, _ = x_prompt.shape
    nbs, sd, _ = x_sample.shape
    past = cache_mla_ckv.shape[2]
    assert nbp == 1

    wq = jnp.pad(w_q_up[0], ((0, 0), (0, 0), (0, QK_PAD - QK_NOPE - QK_ROPE)))
    wk = w_kv_up[0][:, :, :QK_NOPE].reshape(KV_LORA, -1).astype(BF16)
    wv = w_kv_up[0][:, :, QK_NOPE:].reshape(KV_LORA, -1).astype(BF16)
    inv = ROPE_THETA ** (-jnp.arange(ROPE_HALF, dtype=F32) / ROPE_HALF)
    wts = {
        "inv": jnp.tile(inv, LANES // ROPE_HALF).reshape(1, LANES),
        "w_in": jnp.pad(w_in[0], ((0, 0), (0, LANES - QK_ROPE))).astype(BF16),
        "g_q": g_q[0].reshape(1, -1),
        "w_q": wq.reshape(Q_LORA, MLA_HEADS * QK_PAD).astype(BF16),
        "g_kv": g_kv[0].reshape(1, -1),
        "w_k": wk,
        "w_v": wv,
        "w_vt": wv.T,
        "d_skip": d_skip[0].reshape(1, -1),
        "w_glu": w_glu[0].astype(BF16),
        "g_out_ssm": g_out_ssm[0].reshape(1, -1),
        "g_out_mla": g_out_mla[0].reshape(1, -1),
        "w_o": w_o[0].astype(BF16),
        "w_xq": w_xq[0].reshape(D_MODEL, D_MODEL).astype(BF16),
        "w_xo": w_xo[0].reshape(D_MODEL, D_MODEL).astype(BF16),
        "w_ff1": w_ff1[0].astype(BF16),
        "w_ff2": w_ff2[0].astype(BF16),
        "ln_g": ln_g[0],
        "ln_b": ln_b[0],
    }
    s5_ops = _s5_prep(a_re[0], a_im[0], b_re[0], b_im[0], c_re[0], c_im[0], log_dt[0])

    mk, mv = _mem_kv(mem_prompt.reshape(nbp * N_MEM, D_MODEL),
                     w_xk[0].reshape(D_MODEL, D_MODEL).astype(BF16),
                     w_xv[0].reshape(D_MODEL, D_MODEL).astype(BF16))
    mk = mk.reshape(nbp, N_MEM, D_MODEL)
    mv = mv.reshape(nbp, N_MEM, D_MODEL)
    zero = jnp.zeros((nbp, SSM_GROUPS, SSM_STATE), F32)
    pos_p = jnp.arange(sp, dtype=F32).reshape(sp, 1)
    yp, ckv_p, kpe_p, sre_p, sim_p = _layer(
        x_prompt.reshape(sp, D_MODEL), pos_p, nbp, zero, zero, mk, mv, None,
        wts, s5_ops, tn=512, scan_ct=512, mem_tq=512)

    pos_s = jnp.tile(past + jnp.arange(sd, dtype=F32), nbs).reshape(nbs * sd, 1)
    caches = (cache_mla_ckv[0], cache_mla_kpe[0])
    ys, ckv_s, kpe_s, sre_s, sim_s = _layer(
        x_sample.reshape(nbs * sd, D_MODEL), pos_s, nbs,
        state_ssm_re[0], state_ssm_im[0], cache_mem_k[0], cache_mem_v[0], caches,
        wts, s5_ops, tn=512, scan_ct=sd // S5_T, mem_tq=sd)

    return (yp.reshape(nbp, sp, D_MODEL), ys.reshape(nbs, sd, D_MODEL),
            ckv_p.reshape(1, nbp, sp, KV_LORA), kpe_p.reshape(1, nbp, sp, QK_ROPE),
            sre_p.reshape(1, nbp, SSM_GROUPS, SSM_STATE), sim_p.reshape(1, nbp, SSM_GROUPS, SSM_STATE),
            mk.reshape(1, nbp, N_MEM, X_HEADS, X_HEAD_DIM), mv.reshape(1, nbp, N_MEM, X_HEADS, X_HEAD_DIM),
            ckv_s.reshape(1, nbs, sd, KV_LORA), kpe_s.reshape(1, nbs, sd, QK_ROPE),
            sre_s.reshape(1, nbs, SSM_GROUPS, SSM_STATE), sim_s.reshape(1, nbs, SSM_GROUPS, SSM_STATE))
```

```python
import functools
import math

import jax
import jax.numpy as jnp
from jax import lax
from jax.experimental import pallas as pl
from jax.experimental.pallas import tpu as pltpu

F32 = jnp.float32
BF16 = jnp.bfloat16

D_MODEL = 1024
DEPTH = 1
CHUNK = 64
SSM_WIDTH = 512
SSM_GROUP = 16
SSM_GROUPS = 32
SSM_STATE = 64
MLA_HEADS = 4
QK_NOPE = 128
QK_ROPE = 64
V_HEAD = 128
MLA_WIDTH = MLA_HEADS * V_HEAD
Q_LORA = 384
KV_LORA = 256
ROPE_THETA = 10000.0
MLA_SCALE = (QK_NOPE + QK_ROPE) ** -0.5
N_MEM = 256
X_HEADS = 4
X_HEAD_DIM = D_MODEL // X_HEADS
D_FF = 4 * D_MODEL
ALPHA = (2 * DEPTH) ** 0.25
EPS = 1e-5
NEG_INF = -1e30

LANES = 128
QK_PAD = 256
S5_T = 16
S5_COL_GROUPS = LANES // SSM_GROUP
S5_COLS = SSM_GROUPS // S5_COL_GROUPS
S5_SW = S5_COL_GROUPS * SSM_STATE
ROPE_HALF = QK_ROPE // 2
Q_SCALE = MLA_SCALE * math.log2(math.e)
ATTN_T = 512
ATTN_HEADS = 2
MIX_SPLIT = 2
VMEM_LIMIT = 56 * 1024 * 1024

_NT = (((1,), (1,)), ((), ()))


def _rms(x, g):
    return x * lax.rsqrt(jnp.mean(x * x, -1, keepdims=True) + EPS) * g


def _ln(x, g, b):
    mu = jnp.mean(x, -1, keepdims=True)
    xc = x - mu
    var = jnp.mean(xc * xc, -1, keepdims=True)
    return xc * lax.rsqrt(var + EPS) * g + b


def _dot(a, b):
    return jnp.dot(a, b, preferred_element_type=F32)


def _dot_nt(a, b):
    return lax.dot_general(a, b, _NT, preferred_element_type=F32)


def _dot_hi(a, b):
    return jnp.dot(a, b, preferred_element_type=F32, precision=lax.Precision.HIGHEST)


def _params(*sem):
    return pltpu.CompilerParams(dimension_semantics=sem, vmem_limit_bytes=VMEM_LIMIT)


def _const(shape):
    n = len(shape)
    return pl.BlockSpec(shape, lambda *_: (0,) * n)


def _proj_body(x_ref, pos_ref, inv_ref, w_in_ref, gq_ref, wq_ref, gkv_ref, wk_ref, wv_ref,
               u_ref, q_ref, k_ref, v_ref, ckv_ref, kpe_ref, tc_ref, ts_ref, *, v_transposed, consecutive):
    x = x_ref[...].astype(BF16)
    proj = _dot(x, w_in_ref[...])
    u_ref[...] = proj[:, :SSM_WIDTH]
    cq = _rms(proj[:, SSM_WIDTH:SSM_WIDTH + Q_LORA], gq_ref[...])
    q = _dot(cq.astype(BF16), wq_ref[...]) * Q_SCALE
    c0 = SSM_WIDTH + Q_LORA
    ckv = _rms(proj[:, c0:c0 + KV_LORA], gkv_ref[...])
    ckv_ref[...] = ckv
    ckv_b = ckv.astype(BF16)
    kn = _dot(ckv_b, wk_ref[...])
    if v_transposed:
        for c in range(v_ref.shape[0]):
            rows = ckv_b[c * ATTN_T:(c + 1) * ATTN_T, :]
            v_ref[c] = _dot_nt(wv_ref[...], rows).astype(BF16)
    else:
        v_ref[...] = _dot(ckv_b, wv_ref[...]).astype(BF16)

    lane = lax.broadcasted_iota(jnp.int32, (1, LANES), 1)
    live = lane < QK_ROPE
    sign = jnp.where(lane < ROPE_HALF, -1.0, 1.0)
    inv = inv_ref[...]
    if consecutive:
        @pl.when(pl.program_id(0) == 0)
        def _():
            r = lax.broadcasted_iota(jnp.int32, (x_ref.shape[0], 1), 0).astype(F32)
            tc_ref[...] = jnp.where(live, jnp.cos(r * inv), 0.0)
            ts_ref[...] = jnp.where(live, jnp.sin(r * inv), 0.0)

        base = pos_ref[0:1, :] * inv
        ca, sa = jnp.cos(base), jnp.sin(base)
        tc, ts = tc_ref[...], ts_ref[...]
        cos_t = ca * tc - sa * ts
        sin_t = (sa * sign) * tc + (ca * sign) * ts
    else:
        ang = pos_ref[...] * inv
        cos_t = jnp.where(live, jnp.cos(ang), 0.0)
        sin_t = jnp.where(live, jnp.sin(ang) * sign, 0.0)

    def rope(c2):
        swapped = jnp.where(lane < ROPE_HALF, pltpu.roll(c2, LANES - ROPE_HALF, 1), pltpu.roll(c2, ROPE_HALF, 1))
        return c2 * cos_t + swapped * sin_t

    kpe = rope(proj[:, c0 + KV_LORA:])
    kpe_ref[...] = kpe[:, :QK_ROPE]
    kpe_b = kpe.astype(BF16)
    for h in range(MLA_HEADS):
        a = h * QK_PAD
        q_ref[:, a:a + QK_NOPE] = q[:, a:a + QK_NOPE].astype(BF16)
        q_ref[:, a + QK_NOPE:a + QK_PAD] = rope(q[:, a + QK_NOPE:a + QK_PAD]).astype(BF16)
        k_ref[:, a:a + QK_NOPE] = kn[:, h * QK_NOPE:(h + 1) * QK_NOPE].astype(BF16)
        k_ref[:, a + QK_NOPE:a + QK_PAD] = kpe_b


def _project(x2, pos, inv, w_in, gq, wq, gkv, wk, wv, tn, v_transposed, consecutive):
    n = x2.shape[0]
    row = lambda w: pl.BlockSpec((tn, w), lambda i: (i, 0))
    if v_transposed:
        assert tn % ATTN_T == 0
        v_spec = pl.BlockSpec((tn // ATTN_T, MLA_WIDTH, ATTN_T), lambda i: (i, 0, 0))
        v_shape = jax.ShapeDtypeStruct((n // ATTN_T, MLA_WIDTH, ATTN_T), BF16)
    else:
        v_spec, v_shape = row(MLA_WIDTH), jax.ShapeDtypeStruct((n, MLA_WIDTH), BF16)
    return pl.pallas_call(
        functools.partial(_proj_body, v_transposed=v_transposed, consecutive=consecutive),
        grid=(n // tn,),
        scratch_shapes=[pltpu.VMEM((tn, LANES), F32)] * 2,
        in_specs=[row(D_MODEL), row(1), _const(inv.shape), _const(w_in.shape), _const(gq.shape),
                  _const(wq.shape), _const(gkv.shape), _const(wk.shape), _const(wv.shape)],
        out_specs=[row(SSM_WIDTH), row(MLA_HEADS * QK_PAD), row(MLA_HEADS * QK_PAD),
                   v_spec, row(KV_LORA), row(QK_ROPE)],
        out_shape=[jax.ShapeDtypeStruct((n, SSM_WIDTH), F32),
                   jax.ShapeDtypeStruct((n, MLA_HEADS * QK_PAD), BF16),
                   jax.ShapeDtypeStruct((n, MLA_HEADS * QK_PAD), BF16),
                   v_shape,
                   jax.ShapeDtypeStruct((n, KV_LORA), F32),
                   jax.ShapeDtypeStruct((n, QK_ROPE), F32)],
        compiler_params=_params("arbitrary"),
        name="proj",
    )(x2, pos, inv, w_in, gq, wq, gkv, wk, wv)


def _s5_prep_body(ar_row, ai_row, ldt_row, bt_re, bt_im, ct_re, ct_im,
                  wt_ref, v_ref, z_ref, are_ref, aim_ref):
    sw = S5_SW
    arr, air, dtr = ar_row[0], ai_row[0], jnp.exp(ldt_row[0])

    mag = jnp.exp(arr * dtr)
    lr, li = mag * jnp.cos(air * dtr), mag * jnp.sin(air * dtr)
    nr, ni = lr - 1.0, li
    den = arr * arr + air * air
    f_re, f_im = (nr * arr + ni * air) / den, (ni * arr - nr * air) / den

    same_b = (lax.broadcasted_iota(jnp.int32, (LANES, sw), 0) // SSM_GROUP
              == lax.broadcasted_iota(jnp.int32, (LANES, sw), 1) // SSM_STATE)
    br = jnp.where(same_b, bt_re[0], 0.0)
    bi = jnp.where(same_b, bt_im[0], 0.0)
    bb_re = f_re * br - f_im * bi
    bb_im = f_re * bi + f_im * br
    same_c = (lax.broadcasted_iota(jnp.int32, (sw, LANES), 0) // SSM_STATE
              == lax.broadcasted_iota(jnp.int32, (sw, LANES), 1) // SSM_GROUP)
    cr = jnp.where(same_c, ct_re[0], 0.0)
    ci = jnp.where(same_c, ct_im[0], 0.0)

    e = lax.broadcasted_iota(jnp.int32, (2 * S5_T, 1), 0).astype(F32)
    pm = jnp.exp(arr * dtr * e)
    pw_re, pw_im = pm * jnp.cos(air * dtr * e), pm * jnp.sin(air * dtr * e)
    pt_re, pt_im = pw_re.T, pw_im.T

    lag_ops = []
    for lag in range(S5_T):
        p_re, p_im = pw_re[lag:lag + 1, :], pw_im[lag:lag + 1, :]
        k_re = p_re * bb_re - p_im * bb_im
        k_im = p_re * bb_im + p_im * bb_re
        i = S5_T - 1 - lag
        v_ref[0, i * LANES:(i + 1) * LANES, :sw] = k_re.astype(BF16)
        v_ref[0, i * LANES:(i + 1) * LANES, sw:] = k_im.astype(BF16)
        lag_ops.append(_dot_hi(k_re, cr) - _dot_hi(k_im, ci))

        q_re, q_im = pt_re[:, lag + 1:lag + 2], pt_im[:, lag + 1:lag + 2]
        z_ref[0, :sw, lag * LANES:(lag + 1) * LANES] = (cr * q_re - ci * q_im).astype(BF16)
        z_ref[0, sw:, lag * LANES:(lag + 1) * LANES] = (-(cr * q_im + ci * q_re)).astype(BF16)

    zero = jnp.zeros((LANES, LANES), BF16)
    for d in range(S5_T // 2):
        wt_ref[0, d, :LANES, :LANES] = lag_ops[2 * d].astype(BF16)
        wt_ref[0, d, :LANES, LANES:] = lag_ops[2 * d + 1].astype(BF16)
        wt_ref[0, d, LANES:, :LANES] = lag_ops[2 * d - 1].astype(BF16) if d else zero
        wt_ref[0, d, LANES:, LANES:] = lag_ops[2 * d].astype(BF16)

    are_ref[0] = pw_re[S5_T:S5_T + 1, :]
    aim_ref[0] = pw_im[S5_T:S5_T + 1, :]


def _s5_prep(a_re, a_im, b_re, b_im, c_re, c_im, log_dt):
    nc, r, t, h, p, sw = S5_COLS, S5_COL_GROUPS, S5_T, SSM_GROUP, SSM_STATE, S5_SW
    ldt = jnp.repeat(log_dt, p)
    bt = lambda b: jnp.tile(jnp.transpose(b.reshape(nc, r, p, h), (0, 1, 3, 2)).reshape(nc, r * h, p), (1, 1, r))
    ct = lambda c: jnp.tile(jnp.transpose(c.reshape(nc, r, h, p), (0, 1, 3, 2)).reshape(nc, r * p, h), (1, 1, r))
    args = (a_re.reshape(nc, 1, sw), a_im.reshape(nc, 1, sw), ldt.reshape(nc, 1, sw),
            bt(b_re), bt(b_im), ct(c_re), ct(c_im))
    blk = lambda s: pl.BlockSpec((1,) + s[1:], lambda i: (i,) + (0,) * (len(s) - 1))
    outs = [((nc, t // 2, 2 * LANES, 2 * LANES), BF16), ((nc, t * LANES, 2 * sw), BF16),
            ((nc, 2 * sw, t * LANES), BF16), ((nc, 1, sw), F32), ((nc, 1, sw), F32)]
    return pl.pallas_call(
        _s5_prep_body,
        grid=(nc,),
        in_specs=[blk(a.shape) for a in args],
        out_specs=[blk(s) for s, _ in outs],
        out_shape=[jax.ShapeDtypeStruct(s, d) for s, d in outs],
        compiler_params=_params("parallel"),
        name="s5_prep",
    )(*args)


def _s5_scan_body(u_ref, wt_ref, v_ref, z_ref, are_ref, aim_ref, h0r_ref, h0i_ref,
                  y_ref, fr_ref, fi_ref, lhs, sre, sim, xre, xim, cr, ci, *, nseq, ct):
    t = pl.program_id(1)
    sw = S5_SW
    seq = ct * S5_T

    @pl.when(t == 0)
    def _():
        cr[...] = h0r_ref[0]
        ci[...] = h0i_ref[0]

    def token_rows(cc, i):
        if nseq == 1:
            return pl.ds(i, ct, stride=S5_T), slice(None)
        return pl.ds(cc * S5_T + i, nseq, stride=seq), slice(cc * nseq, (cc + 1) * nseq)

    for cc in range(1 if nseq == 1 else ct):
        for i in range(S5_T):
            tok, crow = token_rows(cc, i)
            lhs[crow, i * LANES:(i + 1) * LANES] = u_ref[tok, :].astype(BF16)

    s = _dot(lhs[...], v_ref[0])
    sre[...] = s[:, :sw]
    sim[...] = s[:, sw:]

    a_re, a_im = are_ref[0], aim_ref[0]

    def step(c, carry):
        x_re, x_im = carry
        r = pl.ds(c * nseq, nseq)
        xre[r, :] = x_re
        xim[r, :] = x_im
        n_re = a_re * x_re - a_im * x_im + sre[r, :]
        n_im = a_re * x_im + a_im * x_re + sim[r, :]
        return n_re, n_im

    x_re, x_im = lax.fori_loop(0, ct, step, (cr[...], ci[...]))
    cr[...] = x_re
    ci[...] = x_im
    fr_ref[0] = x_re
    fi_ref[0] = x_im

    xb_re, xb_im = xre[...].astype(BF16), xim[...].astype(BF16)
    for jp in range(S5_T // 2):
        cols = slice(jp * 2 * LANES, (jp + 1) * 2 * LANES)
        acc = _dot(xb_re, z_ref[0, :sw, cols]) + _dot(xb_im, z_ref[0, sw:, cols])
        for ip in range(jp + 1):
            acc += _dot(lhs[:, ip * 2 * LANES:(ip + 1) * 2 * LANES], wt_ref[0, jp - ip])
        for cc in range(1 if nseq == 1 else ct):
            for jj in range(2):
                tok, crow = token_rows(cc, 2 * jp + jj)
                y_ref[tok, :] = acc[crow, jj * LANES:(jj + 1) * LANES]


def _s5_scan(u, col_ops, h0r, h0i, nseq, ct):
    wt, v, z, are, aim = col_ops
    n = u.shape[0]
    cps = n // nseq // S5_T
    assert nseq == 1 or ct == cps
    nc = nseq * ct
    sw = S5_SW
    tile = pl.BlockSpec((nc * S5_T, LANES), lambda c, t: (t, c))
    op = lambda a: pl.BlockSpec((1,) + a.shape[1:], lambda c, t: (c,) + (0,) * (a.ndim - 1))
    st = pl.BlockSpec((1, nseq, sw), lambda c, t: (c, 0, 0))
    st_shape = jax.ShapeDtypeStruct((S5_COLS, nseq, sw), F32)
    return pl.pallas_call(
        functools.partial(_s5_scan_body, nseq=nseq, ct=ct),
        grid=(S5_COLS, cps // ct),
        in_specs=[tile, op(wt), op(v), op(z), op(are), op(aim), st, st],
        out_specs=[tile, st, st],
        out_shape=[jax.ShapeDtypeStruct(u.shape, F32), st_shape, st_shape],
        scratch_shapes=[pltpu.VMEM((nc, S5_T * LANES), BF16)] + [pltpu.VMEM((nc, sw), F32)] * 4
        + [pltpu.VMEM((nseq, sw), F32)] * 2,
        compiler_params=_params("arbitrary", "arbitrary"),
        name="s5_scan",
    )(u, wt, v, z, are, aim, h0r, h0i)


def _glu_body(y_ref, u_ref, d_ref, w_ref, g_ref, o_ref):
    y = y_ref[...] + d_ref[...] * u_ref[...]
    gl = _dot(jax.nn.gelu(y).astype(BF16), w_ref[...])
    o = gl[:, :SSM_WIDTH] * jax.nn.sigmoid(gl[:, SSM_WIDTH:])
    o_ref[...] = _rms(o, g_ref[...]).astype(BF16)


def _glu(y, u, d, w, g, tn):
    n = y.shape[0]
    row = pl.BlockSpec((tn, SSM_WIDTH), lambda i: (i, 0))
    return pl.pallas_call(
        _glu_body,
        grid=(n // tn,),
        in_specs=[row, row, _const(d.shape), _const(w.shape), _const(g.shape)],
        out_specs=row,
        out_shape=jax.ShapeDtypeStruct((n, SSM_WIDTH), BF16),
        compiler_params=_params("parallel"),
        name="glu",
    )(y, u, d, w, g)


def _attn_body(q_ref, k_ref, vt_ref, o_ref, s0, s1, mt0, mt1, m_sc, l_sc, acc_sc, *, heads):
    i = pl.program_id(1)
    t = ATTN_T
    m_sc[...] = jnp.full(m_sc.shape, -jnp.inf, F32)
    l_sc[...] = jnp.zeros(l_sc.shape, F32)
    acc_sc[...] = jnp.zeros(acc_sc.shape, F32)

    def scores(tile, s_buf, mt_buf, diag=None):
        r = pl.ds(pl.multiple_of(tile * t, t), t)
        for h in range(heads):
            s = _dot_nt(k_ref[r, h * QK_PAD:(h + 1) * QK_PAD],
                        q_ref[:, h * QK_PAD:(h + 1) * QK_PAD])
            if diag is not None:
                kc = (diag * t + lax.broadcasted_iota(jnp.int32, s.shape, 0)) // CHUNK
                qc = lax.broadcasted_iota(jnp.int32, s.shape, 1) // CHUNK
                s = jnp.where(kc <= qc, s, NEG_INF)
            s_buf[h] = s
            mt_buf[h] = jnp.max(s, 0, keepdims=True)

    def absorb(tile, s_buf, mt_buf):
        hs = range(heads)
        m_prev = [m_sc[h] for h in hs]
        m_new = [jnp.maximum(m_prev[h], mt_buf[h]) for h in hs]
        p = [jnp.exp2(s_buf[h] - m_new[h]) for h in hs]
        alpha = [jnp.exp2(m_prev[h] - m_new[h]) for h in hs]
        for h in hs:
            l_sc[h] = alpha[h] * l_sc[h] + jnp.sum(p[h], 0, keepdims=True)
            m_sc[h] = m_new[h]
        pv = [_dot(vt_ref[tile, h * V_HEAD:(h + 1) * V_HEAD, :], p[h].astype(BF16)) for h in hs]
        for h in hs:
            acc_sc[h] = alpha[h] * acc_sc[h] + pv[h]

    scores(2 * i, s0, mt0, diag=0)
    scores(2 * i + 1, s1, mt1, diag=1)
    absorb(2 * i, s0, mt0)

    def pair(jj, c):
        scores(2 * jj, s0, mt0)
        absorb(jnp.where(jj == 0, 2 * i + 1, 2 * jj - 1), s1, mt1)
        scores(2 * jj + 1, s1, mt1)
        absorb(2 * jj, s0, mt0)
        return c

    lax.fori_loop(0, i, pair, 0)
    absorb(jnp.where(i == 0, 1, 2 * i - 1), s1, mt1)

    for h in range(heads):
        o_ref[:, h * V_HEAD:(h + 1) * V_HEAD] = (acc_sc[h] / l_sc[h]).T


def _attention(q, k, vt, heads):
    n = q.shape[0]
    t = ATTN_T
    tq = 2 * t
    assert t % CHUNK == 0 and n % tq == 0 and MLA_HEADS % heads == 0
    once = pl.Buffered(1)
    return pl.pallas_call(
        functools.partial(_attn_body, heads=heads),
        grid=(MLA_HEADS // heads, n // tq),
        in_specs=[pl.BlockSpec((tq, heads * QK_PAD), lambda g, i: (i, g)),
                  pl.BlockSpec((n, heads * QK_PAD), lambda g, i: (0, g), pipeline_mode=once),
                  pl.BlockSpec((n // t, heads * V_HEAD, t), lambda g, i: (0, g, 0), pipeline_mode=once)],
        out_specs=pl.BlockSpec((tq, heads * V_HEAD), lambda g, i: (i, g)),
        out_shape=jax.ShapeDtypeStruct((n, MLA_WIDTH), F32),
        scratch_shapes=[pltpu.VMEM((heads, t, tq), F32)] * 2 + [pltpu.VMEM((heads, 1, tq), F32)] * 4
        + [pltpu.VMEM((heads, V_HEAD, tq), F32)],
        compiler_params=_params("arbitrary", "arbitrary"),
        name="attn",
    )(q, k, vt)


def _attn_cached_body(q_ref, kn_ref, vn_ref, ckv_ref, kpe_ref, wk_ref, wv_ref, o_ref, *, past, seq, nsub):
    heads = range(MLA_HEADS)
    pos = past + lax.broadcasted_iota(jnp.int32, (seq, 1), 0)
    qc = jnp.concatenate([pos] * MLA_HEADS, axis=0) // CHUNK
    kc_past = lax.broadcasted_iota(jnp.int32, (1, past), 1) // CHUNK
    kc_new = (past + lax.broadcasted_iota(jnp.int32, (1, seq), 1)) // CHUNK
    seqs = range(nsub)
    rows = [slice(b * seq, (b + 1) * seq) for b in seqs]
    ckv = [ckv_ref[b].astype(BF16) for b in seqs]
    kpe = [kpe_ref[b].astype(BF16) for b in seqs]
    qa = [jnp.concatenate(
        [_dot_nt(q_ref[r, h * QK_PAD:h * QK_PAD + QK_NOPE], wk_ref[:, h * QK_NOPE:(h + 1) * QK_NOPE])
         for h in heads], axis=0).astype(BF16) for r in rows]
    qpe = [jnp.concatenate(
        [q_ref[r, h * QK_PAD + QK_NOPE:h * QK_PAD + QK_NOPE + QK_ROPE] for h in heads], axis=0) for r in rows]
    s_p = [_dot_nt(qa[b], ckv[b]) + _dot_nt(qpe[b], kpe[b]) for b in seqs]
    s_n = [jnp.concatenate(
        [_dot_nt(q_ref[r, h * QK_PAD:(h + 1) * QK_PAD], kn_ref[r, h * QK_PAD:(h + 1) * QK_PAD])
         for h in heads], axis=0) for r in rows]
    s_p = [jnp.where(kc_past <= qc, s, NEG_INF) for s in s_p]
    s_n = [jnp.where(kc_new <= qc, s, NEG_INF) for s in s_n]
    m = [jnp.maximum(jnp.max(a, -1, keepdims=True), jnp.max(c, -1, keepdims=True)) for a, c in zip(s_p, s_n)]
    e_p = [jnp.exp2(s - mm) for s, mm in zip(s_p, m)]
    e_n = [jnp.exp2(s - mm) for s, mm in zip(s_n, m)]
    l = [jnp.sum(a, -1, keepdims=True) + jnp.sum(c, -1, keepdims=True) for a, c in zip(e_p, e_n)]
    p_n = [(e / ll).astype(BF16) for e, ll in zip(e_n, l)]
    o_lat = [_dot((e / ll).astype(BF16), c).astype(BF16) for e, ll, c in zip(e_p, l, ckv)]
    for b in seqs:
        for h in heads:
            hr = slice(h * seq, (h + 1) * seq)
            cols = slice(h * V_HEAD, (h + 1) * V_HEAD)
            o_ref[rows[b], cols] = _dot(o_lat[b][hr], wv_ref[:, cols]) + _dot(p_n[b][hr], vn_ref[rows[b], cols])


def _attention_cached(q, k, v, ckv_cache, kpe_cache, wk, wv, seq, nsub):
    n = q.shape[0]
    nb, past, _ = ckv_cache.shape
    assert nb % nsub == 0
    row = lambda w: pl.BlockSpec((nsub * seq, w), lambda b: (b, 0))
    kern = functools.partial(_attn_cached_body, past=past, seq=seq, nsub=nsub)
    return pl.pallas_call(
        kern,
        grid=(nb // nsub,),
        in_specs=[row(MLA_HEADS * QK_PAD), row(MLA_HEADS * QK_PAD), row(MLA_WIDTH),
                  pl.BlockSpec((nsub, past, KV_LORA), lambda b: (b, 0, 0)),
                  pl.BlockSpec((nsub, past, QK_ROPE), lambda b: (b, 0, 0)),
                  _const(wk.shape), _const(wv.shape)],
        out_specs=row(MLA_WIDTH),
        out_shape=jax.ShapeDtypeStruct((n, MLA_WIDTH), F32),
        compiler_params=_params("parallel"),
        name="attn_cached",
    )(q, k, v, ckv_cache, kpe_cache, wk, wv)


def _row_blocks(ref):
    rb = ref.shape[0] // MIX_SPLIT
    return [slice(i * rb, (i + 1) * rb) for i in range(MIX_SPLIT)]


def _mix_head(blocks, os_ref, om_ref, x_ref, gm_ref, wo_ref, lng_ref, lnb_ref, wxq_ref):
    om = [_rms(om_ref[r, :], gm_ref[...]).astype(BF16) for r in blocks]
    a = [_dot(os_ref[r, :], wo_ref[:SSM_WIDTH, :]) + _dot(o, wo_ref[SSM_WIDTH:, :]) for r, o in zip(blocks, om)]
    h1 = [_ln(ALPHA * x_ref[r, :] + aa, lng_ref[0:1, :], lnb_ref[0:1, :]) for r, aa in zip(blocks, a)]
    return h1, [_dot(h.astype(BF16), wxq_ref[...]).astype(BF16) for h in h1]


def _mix_body(os_ref, om_ref, x_ref, gm_ref, wo_ref, lng_ref, lnb_ref, wxq_ref, h1_ref, qx_ref):
    blocks = _row_blocks(x_ref)
    h1, qx = _mix_head(blocks, os_ref, om_ref, x_ref, gm_ref, wo_ref, lng_ref, lnb_ref, wxq_ref)
    for r, h, q in zip(blocks, h1, qx):
        h1_ref[r, :] = h
        qx_ref[r, :] = q


def _mix(o_ssm, o_mla, x2, g_mla, w_o, ln_g, ln_b, w_xq, tn):
    n = x2.shape[0]
    row = lambda w: pl.BlockSpec((tn, w), lambda i: (i, 0))
    return pl.pallas_call(
        _mix_body,
        grid=(n // tn,),
        in_specs=[row(SSM_WIDTH), row(MLA_WIDTH), row(D_MODEL), _const(g_mla.shape), _const(w_o.shape),
                  _const(ln_g.shape), _const(ln_b.shape), _const(w_xq.shape)],
        out_specs=[row(D_MODEL), row(D_MODEL)],
        out_shape=[jax.ShapeDtypeStruct((n, D_MODEL), F32), jax.ShapeDtypeStruct((n, D_MODEL), BF16)],
        compiler_params=_params("parallel"),
        name="mix",
    )(o_ssm, o_mla, x2, g_mla, w_o, ln_g, ln_b, w_xq)


def _mem_attn_heads(qs, mks, mvs):
    ss = [_dot_nt(q, mk.astype(BF16)) * (X_HEAD_DIM ** -0.5) for q, mk in zip(qs, mks)]
    es = [jnp.exp(s - jnp.max(s, -1, keepdims=True)) for s in ss]
    ps = [(e / jnp.sum(e, -1, keepdims=True)).astype(BF16) for e in es]
    return [_dot(p, mv.astype(BF16)).astype(BF16) for p, mv in zip(ps, mvs)]


def _mix_mem_body(os_ref, om_ref, x_ref, gm_ref, wo_ref, lng_ref, lnb_ref, wxq_ref, mk_ref, mv_ref, wxo_ref,
                  h2_ref):
    blocks = _row_blocks(x_ref)
    h1, qx = _mix_head(blocks, os_ref, om_ref, x_ref, gm_ref, wo_ref, lng_ref, lnb_ref, wxq_ref)
    cols = [slice(h * X_HEAD_DIM, (h + 1) * X_HEAD_DIM) for h in range(X_HEADS)]
    heads = _mem_attn_heads([q[:, c] for q in qx for c in cols], [mk_ref[:, c] for _ in qx for c in cols],
                            [mv_ref[:, c] for _ in qx for c in cols])
    ox = [jnp.concatenate(heads[i * X_HEADS:(i + 1) * X_HEADS], axis=1) for i in range(len(blocks))]
    att = [_dot(o, wxo_ref[...]) for o in ox]
    for r, h, a in zip(blocks, h1, att):
        h2_ref[r, :] = _ln(ALPHA * h + a, lng_ref[1:2, :], lnb_ref[1:2, :])


def _mix_mem(o_ssm, o_mla, x2, g_mla, w_o, ln_g, ln_b, w_xq, mem_k, mem_v, w_xo, tn):
    n = x2.shape[0]
    row = lambda w: pl.BlockSpec((tn, w), lambda i: (i, 0))
    consts = (g_mla, w_o, ln_g, ln_b, w_xq, mem_k, mem_v, w_xo)
    return pl.pallas_call(
        _mix_mem_body,
        grid=(n // tn,),
        in_specs=[row(SSM_WIDTH), row(MLA_WIDTH), row(D_MODEL)] + [_const(c.shape) for c in consts],
        out_specs=row(D_MODEL),
        out_shape=jax.ShapeDtypeStruct((n, D_MODEL), F32),
        compiler_params=_params("parallel"),
        name="mix_mem",
    )(o_ssm, o_mla, x2, *consts)


def _mem_attn_cache_body(qx_ref, mk_hbm, mv_hbm, o_ref, kbuf, vbuf, sems, *, seq, nsub):
    b = pl.program_id(0)
    nb = pl.num_programs(0)

    def copies(step, slot):
        return [pltpu.make_async_copy(src.at[step * nsub + j, :, h, :], buf.at[slot, j, h], sems.at[slot, j, t, h])
                for t, (src, buf) in enumerate(((mk_hbm, kbuf), (mv_hbm, vbuf)))
                for j in range(nsub) for h in range(X_HEADS)]

    @pl.when(b == 0)
    def _():
        for c in copies(0, 0):
            c.start()

    @pl.when(b + 1 < nb)
    def _():
        for c in copies(b + 1, (b + 1) % 2):
            c.start()

    slot = b % 2
    for c in copies(b, slot):
        c.wait()
    jh = [(j, h) for j in range(nsub) for h in range(X_HEADS)]
    where = [(slice(j * seq, (j + 1) * seq), slice(h * X_HEAD_DIM, (h + 1) * X_HEAD_DIM)) for j, h in jh]
    outs = _mem_attn_heads([qx_ref[r, c] for r, c in where], [kbuf[slot, j, h] for j, h in jh],
                           [vbuf[slot, j, h] for j, h in jh])
    for (r, c), o in zip(where, outs):
        o_ref[r, c] = o


def _mem_attn_cache(qx, mem_k, mem_v, seq, nsub):
    n = qx.shape[0]
    nb = mem_k.shape[0]
    assert nb % nsub == 0
    row = pl.BlockSpec((nsub * seq, D_MODEL), lambda b: (b, 0))
    hbm = pl.BlockSpec(memory_space=pl.ANY)
    buf = pltpu.VMEM((2, nsub, X_HEADS, N_MEM, X_HEAD_DIM), F32)
    return pl.pallas_call(
        functools.partial(_mem_attn_cache_body, seq=seq, nsub=nsub),
        grid=(nb // nsub,),
        in_specs=[row, hbm, hbm],
        out_specs=row,
        out_shape=jax.ShapeDtypeStruct((n, D_MODEL), BF16),
        scratch_shapes=[buf, buf, pltpu.SemaphoreType.DMA((2, nsub, 2, X_HEADS))],
        compiler_params=_params("arbitrary"),
        name="mem_attn_cache",
    )(qx, mem_k, mem_v)


def _mlp_body(*refs, ff_blk, second_norm):
    if second_norm:
        h1_ref, ox_ref, wxo_ref, lng_ref, lnb_ref, w1_ref, w2_ref, y_ref = refs
        h2 = _ln(ALPHA * h1_ref[...] + _dot(ox_ref[...], wxo_ref[...]), lng_ref[1:2, :], lnb_ref[1:2, :])
    else:
        h2_ref, lng_ref, lnb_ref, w1_ref, w2_ref, y_ref = refs
        h2 = h2_ref[...]
    hb = h2.astype(BF16)
    acc = jnp.zeros(h2.shape, F32)
    for c in range(0, D_FF, ff_blk):
        z = jnp.maximum(_dot(hb, w1_ref[:, c:c + ff_blk]), 0.0)
        acc += _dot((z * z).astype(BF16), w2_ref[c:c + ff_blk, :])
    y_ref[...] = _ln(ALPHA * h2 + acc, lng_ref[2:3, :], lnb_ref[2:3, :])


def _mlp(h, ox, w_xo, ln_g, ln_b, w1, w2, tn):
    n = h.shape[0]
    row = pl.BlockSpec((tn, D_MODEL), lambda i: (i, 0))
    second_norm = ox is not None
    rows = (h, ox) if second_norm else (h,)
    consts = ((w_xo,) if second_norm else ()) + (ln_g, ln_b, w1, w2)
    return pl.pallas_call(
        functools.partial(_mlp_body, ff_blk=1024, second_norm=second_norm),
        grid=(n // tn,),
        in_specs=[row] * len(rows) + [_const(c.shape) for c in consts],
        out_specs=row,
        out_shape=jax.ShapeDtypeStruct((n, D_MODEL), F32),
        compiler_params=_params("parallel"),
        name="mlp",
    )(*rows, *consts)


def _mem_kv_body(mem_ref, wk_ref, wv_ref, k_ref, v_ref):
    m = mem_ref[...].astype(BF16)
    k_ref[...] = _dot(m, wk_ref[...])
    v_ref[...] = _dot(m, wv_ref[...])


def _mem_kv(mem2, wk, wv):
    n = mem2.shape[0]
    out = jax.ShapeDtypeStruct((n, D_MODEL), F32)
    return pl.pallas_call(
        _mem_kv_body,
        grid=(1,),
        in_specs=[_const(mem2.shape), _const(wk.shape), _const(wv.shape)],
        out_specs=[_const((n, D_MODEL))] * 2,
        out_shape=[out, out],
        compiler_params=_params("arbitrary"),
        name="mem_kv",
    )(mem2, wk, wv)


def _state_to_cols(s):
    return jnp.transpose(s.reshape(s.shape[0], S5_COLS, -1), (1, 0, 2))


def _state_from_cols(s):
    return jnp.transpose(s, (1, 0, 2)).reshape(s.shape[1], SSM_GROUPS, SSM_STATE)


def _layer(x2, pos, nseq, h0r, h0i, mem_k, mem_v, caches, wts, s5_ops, tn, scan_ct):
    n = x2.shape[0]
    seq = n // nseq
    prompt = caches is None
    u, q, k, v, ckv, kpe = _project(x2, pos, wts["inv"], wts["w_in"], wts["g_q"], wts["w_q"], wts["g_kv"],
                                    wts["w_k"], wts["w_vt"] if prompt else wts["w_v"], tn,
                                    v_transposed=prompt, consecutive=prompt)
    y_ssm, fr, fi = _s5_scan(u, s5_ops, _state_to_cols(h0r), _state_to_cols(h0i), nseq, scan_ct)
    o_ssm = _glu(y_ssm, u, wts["d_skip"], wts["w_glu"], wts["g_out_ssm"], tn)
    if prompt:
        o_mla = _attention(q, k, v, ATTN_HEADS)
    else:
        o_mla = _attention_cached(q, k, v, caches[0], caches[1], wts["w_k"], wts["w_v"], seq, nsub=2)
    mix_args = (o_ssm, o_mla, x2, wts["g_out_mla"], wts["w_o"], wts["ln_g"], wts["ln_b"], wts["w_xq"])
    if prompt:
        h, ox = _mix_mem(*mix_args, mem_k, mem_v, wts["w_xo"], tn), None
    else:
        h, qx = _mix(*mix_args, tn)
        ox = _mem_attn_cache(qx, mem_k, mem_v, seq, nsub=2)
    y = _mlp(h, ox, wts["w_xo"], wts["ln_g"], wts["ln_b"], wts["w_ff1"], wts["w_ff2"], tn)
    return y, ckv, kpe, _state_from_cols(fr), _state_from_cols(fi)


def kernel(x_prompt, x_sample, mem_prompt, cache_mla_ckv, cache_mla_kpe, state_ssm_re, state_ssm_im, cache_mem_k, cache_mem_v, w_in, g_q, w_q_up, g_kv, w_kv_up, a_re, a_im, b_re, b_im, c_re, c_im, d_skip, log_dt, w_glu, g_out_ssm, g_out_mla, w_o, w_xq, w_xk, w_xv, w_xo, w_ff1, w_ff2, ln_g, ln_b):
    assert w_in.shape[0] == DEPTH == 1
    nbp, sp, _ = x_prompt.shape
    nbs, sd, _ = x_sample.shape
    past = cache_mla_ckv.shape[2]
    assert nbp == 1

    wq = jnp.pad(w_q_up[0], ((0, 0), (0, 0), (0, QK_PAD - QK_NOPE - QK_ROPE)))
    wk = w_kv_up[0][:, :, :QK_NOPE].reshape(KV_LORA, -1).astype(BF16)
    wv = w_kv_up[0][:, :, QK_NOPE:].reshape(KV_LORA, -1).astype(BF16)
    inv = ROPE_THETA ** (-jnp.arange(ROPE_HALF, dtype=F32) / ROPE_HALF)
    wts = {
        "inv": jnp.tile(inv, LANES // ROPE_HALF).reshape(1, LANES),
        "w_in": jnp.pad(w_in[0], ((0, 0), (0, LANES - QK_ROPE))).astype(BF16),
        "g_q": g_q[0].reshape(1, -1),
        "w_q": wq.reshape(Q_LORA, MLA_HEADS * QK_PAD).astype(BF16),
        "g_kv": g_kv[0].reshape(1, -1),
        "w_k": wk,
        "w_v": wv,
        "w_vt": wv.T,
        "d_skip": d_skip[0].reshape(1, -1),
        "w_glu": w_glu[0].astype(BF16),
        "g_out_ssm": g_out_ssm[0].reshape(1, -1),
        "g_out_mla": g_out_mla[0].reshape(1, -1),
        "w_o": w_o[0].astype(BF16),
        "w_xq": w_xq[0].reshape(D_MODEL, D_MODEL).astype(BF16),
        "w_xo": w_xo[0].reshape(D_MODEL, D_MODEL).astype(BF16),
        "w_ff1": w_ff1[0].astype(BF16),
        "w_ff2": w_ff2[0].astype(BF16),
        "ln_g": ln_g[0],
        "ln_b": ln_b[0],
    }
    s5_ops = _s5_prep(a_re[0], a_im[0], b_re[0], b_im[0], c_re[0], c_im[0], log_dt[0])

    mk, mv = _mem_kv(mem_prompt.reshape(nbp * N_MEM, D_MODEL),
                     w_xk[0].reshape(D_MODEL, D_MODEL).astype(BF16),
                     w_xv[0].reshape(D_MODEL, D_MODEL).astype(BF16))
    zero = jnp.zeros((nbp, SSM_GROUPS, SSM_STATE), F32)
    pos_p = jnp.arange(sp, dtype=F32).reshape(sp, 1)
    yp, ckv_p, kpe_p, sre_p, sim_p = _layer(
        x_prompt.reshape(sp, D_MODEL), pos_p, nbp, zero, zero, mk, mv, None,
        wts, s5_ops, tn=512, scan_ct=512)

    pos_s = jnp.tile(past + jnp.arange(sd, dtype=F32), nbs).reshape(nbs * sd, 1)
    caches = (cache_mla_ckv[0], cache_mla_kpe[0])
    ys, ckv_s, kpe_s, sre_s, sim_s = _layer(
        x_sample.reshape(nbs * sd, D_MODEL), pos_s, nbs,
        state_ssm_re[0], state_ssm_im[0], cache_mem_k[0], cache_mem_v[0], caches,
        wts, s5_ops, tn=512, scan_ct=sd // S5_T)

    return (yp.reshape(nbp, sp, D_MODEL), ys.reshape(nbs, sd, D_MODEL),
            ckv_p.reshape(1, nbp, sp, KV_LORA), kpe_p.reshape(1, nbp, sp, QK_ROPE),
            sre_p.reshape(1, nbp, SSM_GROUPS, SSM_STATE), sim_p.reshape(1, nbp, SSM_GROUPS, SSM_STATE),
            mk.reshape(1, nbp, N_MEM, X_HEADS, X_HEAD_DIM), mv.reshape(1, nbp, N_MEM, X_HEADS, X_HEAD_DIM),
            ckv_s.reshape(1, nbs, sd, KV_LORA), kpe_s.reshape(1, nbs, sd, QK_ROPE),
            sre_s.reshape(1, nbs, SSM_GROUPS, SSM_STATE), sim_s.reshape(1, nbs, SSM_GROUPS, SSM_STATE))
```

```python
import functools
import math

import jax
import jax.numpy as jnp
from jax import lax
from jax.experimental import pallas as pl
from jax.experimental.pallas import tpu as pltpu

F32 = jnp.float32
BF16 = jnp.bfloat16

D_MODEL = 1024
DEPTH = 1
CHUNK = 64
SSM_WIDTH = 512
SSM_GROUP = 16
SSM_GROUPS = 32
SSM_STATE = 64
MLA_HEADS = 4
QK_NOPE = 128
QK_ROPE = 64
V_HEAD = 128
MLA_WIDTH = MLA_HEADS * V_HEAD
Q_LORA = 384
KV_LORA = 256
ROPE_THETA = 10000.0
MLA_SCALE = (QK_NOPE + QK_ROPE) ** -0.5
N_MEM = 256
X_HEADS = 4
X_HEAD_DIM = D_MODEL // X_HEADS
D_FF = 4 * D_MODEL
ALPHA = (2 * DEPTH) ** 0.25
EPS = 1e-5
NEG_INF = -1e30

LANES = 128
QK_PAD = 256
S5_T = 16
S5_COL_GROUPS = LANES // SSM_GROUP
S5_COLS = SSM_GROUPS // S5_COL_GROUPS
S5_SW = S5_COL_GROUPS * SSM_STATE
ROPE_HALF = QK_ROPE // 2
Q_SCALE = MLA_SCALE * math.log2(math.e)
ATTN_T = 512
ATTN_HEADS = 2
ATTN_QSPLIT = 4
MIX_SPLIT = 2
PROJ_ROWS = 256
VMEM_LIMIT = 56 * 1024 * 1024

_NT = (((1,), (1,)), ((), ()))


def _rms(x, g):
    return x * lax.rsqrt(jnp.mean(x * x, -1, keepdims=True) + EPS) * g


def _ln(x, g, b):
    mu = jnp.mean(x, -1, keepdims=True)
    xc = x - mu
    var = jnp.mean(xc * xc, -1, keepdims=True)
    return xc * lax.rsqrt(var + EPS) * g + b


def _dot(a, b):
    return jnp.dot(a, b, preferred_element_type=F32)


def _dot_nt(a, b):
    return lax.dot_general(a, b, _NT, preferred_element_type=F32)


def _dot_hi(a, b):
    return jnp.dot(a, b, preferred_element_type=F32, precision=lax.Precision.HIGHEST)


def _params(*sem):
    return pltpu.CompilerParams(dimension_semantics=sem, vmem_limit_bytes=VMEM_LIMIT)


def _const(shape):
    n = len(shape)
    return pl.BlockSpec(shape, lambda *_: (0,) * n)


def _proj_body(x_ref, pos_ref, inv_ref, w_in_ref, gq_ref, wq_ref, gkv_ref, wk_ref, wv_ref,
               u_ref, q_ref, k_ref, v_ref, ckv_ref, kpe_ref, tc_ref, ts_ref, *, v_transposed, consecutive):
    rb = PROJ_ROWS
    assert x_ref.shape[0] % rb == 0 and ATTN_T % rb == 0
    blocks = [slice(r0, r0 + rb) for r0 in range(0, x_ref.shape[0], rb)]
    c0 = SSM_WIDTH + Q_LORA
    projs = [_dot(x_ref[r, :].astype(BF16), w_in_ref[...]) for r in blocks]
    cqs = [_rms(p[:, SSM_WIDTH:c0], gq_ref[...]).astype(BF16) for p in projs]
    ckvs = [_rms(p[:, c0:c0 + KV_LORA], gkv_ref[...]) for p in projs]
    qs = [_dot(c, wq_ref[...]) * Q_SCALE for c in cqs]
    kns = [_dot(c.astype(BF16), wk_ref[...]) for c in ckvs]
    for r, p, ckv in zip(blocks, projs, ckvs):
        u_ref[r, :] = p[:, :SSM_WIDTH]
        ckv_ref[r, :] = ckv
        if v_transposed:
            lanes = slice(r.start % ATTN_T, r.start % ATTN_T + rb)
            v_ref[r.start // ATTN_T, :, lanes] = _dot_nt(wv_ref[...], ckv.astype(BF16)).astype(BF16)
        else:
            v_ref[r, :] = _dot(ckv.astype(BF16), wv_ref[...]).astype(BF16)

    lane = lax.broadcasted_iota(jnp.int32, (1, LANES), 1)
    live = lane < QK_ROPE
    sign = jnp.where(lane < ROPE_HALF, -1.0, 1.0)
    inv = inv_ref[...]
    if consecutive:
        @pl.when(pl.program_id(0) == 0)
        def _():
            r = lax.broadcasted_iota(jnp.int32, (x_ref.shape[0], 1), 0).astype(F32)
            tc_ref[...] = jnp.where(live, jnp.cos(r * inv), 0.0)
            ts_ref[...] = jnp.where(live, jnp.sin(r * inv), 0.0)

        base = pos_ref[0:1, :] * inv
        ca, sa = jnp.cos(base), jnp.sin(base)
        cos_ts = [ca * tc_ref[r, :] - sa * ts_ref[r, :] for r in blocks]
        sin_ts = [(sa * sign) * tc_ref[r, :] + (ca * sign) * ts_ref[r, :] for r in blocks]
    else:
        angs = [pos_ref[r, :] * inv for r in blocks]
        cos_ts = [jnp.where(live, jnp.cos(a), 0.0) for a in angs]
        sin_ts = [jnp.where(live, jnp.sin(a) * sign, 0.0) for a in angs]

    def rope(c2, cos_t, sin_t):
        swapped = jnp.where(lane < ROPE_HALF, pltpu.roll(c2, LANES - ROPE_HALF, 1), pltpu.roll(c2, ROPE_HALF, 1))
        return c2 * cos_t + swapped * sin_t

    for r, p, q, kn, cos_t, sin_t in zip(blocks, projs, qs, kns, cos_ts, sin_ts):
        kpe = rope(p[:, c0 + KV_LORA:], cos_t, sin_t)
        kpe_ref[r, :] = kpe[:, :QK_ROPE]
        kpe_b = kpe.astype(BF16)
        for h in range(MLA_HEADS):
            a = h * QK_PAD
            q_ref[r, a:a + QK_NOPE] = q[:, a:a + QK_NOPE].astype(BF16)
            q_ref[r, a + QK_NOPE:a + QK_PAD] = rope(q[:, a + QK_NOPE:a + QK_PAD], cos_t, sin_t).astype(BF16)
            k_ref[r, a:a + QK_NOPE] = kn[:, h * QK_NOPE:(h + 1) * QK_NOPE].astype(BF16)
            k_ref[r, a + QK_NOPE:a + QK_PAD] = kpe_b


def _project(x2, pos, inv, w_in, gq, wq, gkv, wk, wv, tn, v_transposed, consecutive):
    n = x2.shape[0]
    row = lambda w: pl.BlockSpec((tn, w), lambda i: (i, 0))
    if v_transposed:
        assert tn % ATTN_T == 0
        v_spec = pl.BlockSpec((tn // ATTN_T, MLA_WIDTH, ATTN_T), lambda i: (i, 0, 0))
        v_shape = jax.ShapeDtypeStruct((n // ATTN_T, MLA_WIDTH, ATTN_T), BF16)
    else:
        v_spec, v_shape = row(MLA_WIDTH), jax.ShapeDtypeStruct((n, MLA_WIDTH), BF16)
    return pl.pallas_call(
        functools.partial(_proj_body, v_transposed=v_transposed, consecutive=consecutive),
        grid=(n // tn,),
        scratch_shapes=[pltpu.VMEM((tn, LANES), F32)] * 2,
        in_specs=[row(D_MODEL), row(1), _const(inv.shape), _const(w_in.shape), _const(gq.shape),
                  _const(wq.shape), _const(gkv.shape), _const(wk.shape), _const(wv.shape)],
        out_specs=[row(SSM_WIDTH), row(MLA_HEADS * QK_PAD), row(MLA_HEADS * QK_PAD),
                   v_spec, row(KV_LORA), row(QK_ROPE)],
        out_shape=[jax.ShapeDtypeStruct((n, SSM_WIDTH), F32),
                   jax.ShapeDtypeStruct((n, MLA_HEADS * QK_PAD), BF16),
                   jax.ShapeDtypeStruct((n, MLA_HEADS * QK_PAD), BF16),
                   v_shape,
                   jax.ShapeDtypeStruct((n, KV_LORA), F32),
                   jax.ShapeDtypeStruct((n, QK_ROPE), F32)],
        compiler_params=_params("arbitrary"),
        name="proj",
    )(x2, pos, inv, w_in, gq, wq, gkv, wk, wv)


def _s5_prep_body(ar_row, ai_row, ldt_row, bt_re, bt_im, ct_re, ct_im,
                  wt_ref, v_ref, z_ref, are_ref, aim_ref):
    sw = S5_SW
    arr, air, dtr = ar_row[0], ai_row[0], jnp.exp(ldt_row[0])

    mag = jnp.exp(arr * dtr)
    lr, li = mag * jnp.cos(air * dtr), mag * jnp.sin(air * dtr)
    nr, ni = lr - 1.0, li
    den = arr * arr + air * air
    f_re, f_im = (nr * arr + ni * air) / den, (ni * arr - nr * air) / den

    same_b = (lax.broadcasted_iota(jnp.int32, (LANES, sw), 0) // SSM_GROUP
              == lax.broadcasted_iota(jnp.int32, (LANES, sw), 1) // SSM_STATE)
    br = jnp.where(same_b, bt_re[0], 0.0)
    bi = jnp.where(same_b, bt_im[0], 0.0)
    bb_re = f_re * br - f_im * bi
    bb_im = f_re * bi + f_im * br
    same_c = (lax.broadcasted_iota(jnp.int32, (sw, LANES), 0) // SSM_STATE
              == lax.broadcasted_iota(jnp.int32, (sw, LANES), 1) // SSM_GROUP)
    cr = jnp.where(same_c, ct_re[0], 0.0)
    ci = jnp.where(same_c, ct_im[0], 0.0)

    e = lax.broadcasted_iota(jnp.int32, (2 * S5_T, 1), 0).astype(F32)
    pm = jnp.exp(arr * dtr * e)
    pw_re, pw_im = pm * jnp.cos(air * dtr * e), pm * jnp.sin(air * dtr * e)
    pt_re, pt_im = pw_re.T, pw_im.T

    lag_ops = []
    for lag in range(S5_T):
        p_re, p_im = pw_re[lag:lag + 1, :], pw_im[lag:lag + 1, :]
        k_re = p_re * bb_re - p_im * bb_im
        k_im = p_re * bb_im + p_im * bb_re
        i = S5_T - 1 - lag
        v_ref[0, i * LANES:(i + 1) * LANES, :sw] = k_re.astype(BF16)
        v_ref[0, i * LANES:(i + 1) * LANES, sw:] = k_im.astype(BF16)
        lag_ops.append(_dot_hi(k_re, cr) - _dot_hi(k_im, ci))

        q_re, q_im = pt_re[:, lag + 1:lag + 2], pt_im[:, lag + 1:lag + 2]
        z_ref[0, :sw, lag * LANES:(lag + 1) * LANES] = (cr * q_re - ci * q_im).astype(BF16)
        z_ref[0, sw:, lag * LANES:(lag + 1) * LANES] = (-(cr * q_im + ci * q_re)).astype(BF16)

    zero = jnp.zeros((LANES, LANES), BF16)
    for d in range(S5_T // 2):
        wt_ref[0, d, :LANES, :LANES] = lag_ops[2 * d].astype(BF16)
        wt_ref[0, d, :LANES, LANES:] = lag_ops[2 * d + 1].astype(BF16)
        wt_ref[0, d, LANES:, :LANES] = lag_ops[2 * d - 1].astype(BF16) if d else zero
        wt_ref[0, d, LANES:, LANES:] = lag_ops[2 * d].astype(BF16)

    are_ref[0] = pw_re[S5_T:S5_T + 1, :]
    aim_ref[0] = pw_im[S5_T:S5_T + 1, :]


def _s5_prep(a_re, a_im, b_re, b_im, c_re, c_im, log_dt):
    nc, r, t, h, p, sw = S5_COLS, S5_COL_GROUPS, S5_T, SSM_GROUP, SSM_STATE, S5_SW
    ldt = jnp.repeat(log_dt, p)
    bt = lambda b: jnp.tile(jnp.transpose(b.reshape(nc, r, p, h), (0, 1, 3, 2)).reshape(nc, r * h, p), (1, 1, r))
    ct = lambda c: jnp.tile(jnp.transpose(c.reshape(nc, r, h, p), (0, 1, 3, 2)).reshape(nc, r * p, h), (1, 1, r))
    args = (a_re.reshape(nc, 1, sw), a_im.reshape(nc, 1, sw), ldt.reshape(nc, 1, sw),
            bt(b_re), bt(b_im), ct(c_re), ct(c_im))
    blk = lambda s: pl.BlockSpec((1,) + s[1:], lambda i: (i,) + (0,) * (len(s) - 1))
    outs = [((nc, t // 2, 2 * LANES, 2 * LANES), BF16), ((nc, t * LANES, 2 * sw), BF16),
            ((nc, 2 * sw, t * LANES), BF16), ((nc, 1, sw), F32), ((nc, 1, sw), F32)]
    return pl.pallas_call(
        _s5_prep_body,
        grid=(nc,),
        in_specs=[blk(a.shape) for a in args],
        out_specs=[blk(s) for s, _ in outs],
        out_shape=[jax.ShapeDtypeStruct(s, d) for s, d in outs],
        compiler_params=_params("parallel"),
        name="s5_prep",
    )(*args)


def _s5_scan_body(u_ref, wt_ref, v_ref, z_ref, are_ref, aim_ref, h0r_ref, h0i_ref,
                  y_ref, fr_ref, fi_ref, lhs, sre, sim, xre, xim, cr, ci, *, nseq, ct):
    t = pl.program_id(1)
    sw = S5_SW
    seq = ct * S5_T

    @pl.when(t == 0)
    def _():
        cr[...] = h0r_ref[0]
        ci[...] = h0i_ref[0]

    def token_rows(cc, i):
        if nseq == 1:
            return pl.ds(i, ct, stride=S5_T), slice(None)
        return pl.ds(cc * S5_T + i, nseq, stride=seq), slice(cc * nseq, (cc + 1) * nseq)

    for cc in range(1 if nseq == 1 else ct):
        for i in range(S5_T):
            tok, crow = token_rows(cc, i)
            lhs[crow, i * LANES:(i + 1) * LANES] = u_ref[tok, :].astype(BF16)

    s = _dot(lhs[...], v_ref[0])
    sre[...] = s[:, :sw]
    sim[...] = s[:, sw:]

    a_re, a_im = are_ref[0], aim_ref[0]

    def step(c, carry):
        x_re, x_im = carry
        r = pl.ds(c * nseq, nseq)
        xre[r, :] = x_re
        xim[r, :] = x_im
        n_re = a_re * x_re - a_im * x_im + sre[r, :]
        n_im = a_re * x_im + a_im * x_re + sim[r, :]
        return n_re, n_im

    x_re, x_im = lax.fori_loop(0, ct, step, (cr[...], ci[...]))
    cr[...] = x_re
    ci[...] = x_im
    fr_ref[0] = x_re
    fi_ref[0] = x_im

    xb_re, xb_im = xre[...].astype(BF16), xim[...].astype(BF16)
    for jp in range(S5_T // 2):
        cols = slice(jp * 2 * LANES, (jp + 1) * 2 * LANES)
        acc = _dot(xb_re, z_ref[0, :sw, cols]) + _dot(xb_im, z_ref[0, sw:, cols])
        for ip in range(jp + 1):
            acc += _dot(lhs[:, ip * 2 * LANES:(ip + 1) * 2 * LANES], wt_ref[0, jp - ip])
        for cc in range(1 if nseq == 1 else ct):
            for jj in range(2):
                tok, crow = token_rows(cc, 2 * jp + jj)
                y_ref[tok, :] = acc[crow, jj * LANES:(jj + 1) * LANES]


def _s5_scan(u, col_ops, h0r, h0i, nseq, ct):
    wt, v, z, are, aim = col_ops
    n = u.shape[0]
    cps = n // nseq // S5_T
    assert nseq == 1 or ct == cps
    nc = nseq * ct
    sw = S5_SW
    tile = pl.BlockSpec((nc * S5_T, LANES), lambda c, t: (t, c))
    op = lambda a: pl.BlockSpec((1,) + a.shape[1:], lambda c, t: (c,) + (0,) * (a.ndim - 1))
    st = pl.BlockSpec((1, nseq, sw), lambda c, t: (c, 0, 0))
    st_shape = jax.ShapeDtypeStruct((S5_COLS, nseq, sw), F32)
    return pl.pallas_call(
        functools.partial(_s5_scan_body, nseq=nseq, ct=ct),
        grid=(S5_COLS, cps // ct),
        in_specs=[tile, op(wt), op(v), op(z), op(are), op(aim), st, st],
        out_specs=[tile, st, st],
        out_shape=[jax.ShapeDtypeStruct(u.shape, F32), st_shape, st_shape],
        scratch_shapes=[pltpu.VMEM((nc, S5_T * LANES), BF16)] + [pltpu.VMEM((nc, sw), F32)] * 4
        + [pltpu.VMEM((nseq, sw), F32)] * 2,
        compiler_params=_params("arbitrary", "arbitrary"),
        name="s5_scan",
    )(u, wt, v, z, are, aim, h0r, h0i)


def _glu_body(y_ref, u_ref, d_ref, w_ref, g_ref, o_ref):
    y = y_ref[...] + d_ref[...] * u_ref[...]
    gl = _dot(jax.nn.gelu(y).astype(BF16), w_ref[...])
    o = gl[:, :SSM_WIDTH] * jax.nn.sigmoid(gl[:, SSM_WIDTH:])
    o_ref[...] = _rms(o, g_ref[...]).astype(BF16)


def _glu(y, u, d, w, g, tn):
    n = y.shape[0]
    row = pl.BlockSpec((tn, SSM_WIDTH), lambda i: (i, 0))
    return pl.pallas_call(
        _glu_body,
        grid=(n // tn,),
        in_specs=[row, row, _const(d.shape), _const(w.shape), _const(g.shape)],
        out_specs=row,
        out_shape=jax.ShapeDtypeStruct((n, SSM_WIDTH), BF16),
        compiler_params=_params("parallel"),
        name="glu",
    )(y, u, d, w, g)


def _attn_body(q_ref, k_ref, vt_ref, o_ref, s0, s1, mt0, mt1, m_sc, l_sc, acc_sc, *, heads):
    i = pl.program_id(1)
    t = ATTN_T
    m_sc[...] = jnp.full(m_sc.shape, -jnp.inf, F32)
    l_sc[...] = jnp.zeros(l_sc.shape, F32)
    acc_sc[...] = jnp.zeros(acc_sc.shape, F32)

    qb = q_ref.shape[0] // ATTN_QSPLIT
    units = [(h, c * qb) for h in range(heads) for c in range(ATTN_QSPLIT)]

    def scores(tile, s_buf, mt_buf, unit, diag=None):
        h, c0 = unit
        r = pl.ds(pl.multiple_of(tile * t, t), t)
        s = _dot_nt(k_ref[r, h * QK_PAD:(h + 1) * QK_PAD],
                    q_ref[c0:c0 + qb, h * QK_PAD:(h + 1) * QK_PAD])
        if diag is not None:
            kc = (diag * t + lax.broadcasted_iota(jnp.int32, s.shape, 0)) // CHUNK
            qc = (c0 + lax.broadcasted_iota(jnp.int32, s.shape, 1)) // CHUNK
            s = jnp.where(kc <= qc, s, NEG_INF)
        s_buf[h, :, c0:c0 + qb] = s
        mt_buf[h, :, c0:c0 + qb] = jnp.max(s, 0, keepdims=True)

    def absorb(tile, s_buf, mt_buf, unit):
        h, c0 = unit
        cols = slice(c0, c0 + qb)
        m_prev = m_sc[h, :, cols]
        m_new = jnp.maximum(m_prev, mt_buf[h, :, cols])
        p = jnp.exp2(s_buf[h, :, cols] - m_new)
        alpha = jnp.exp2(m_prev - m_new)
        l_sc[h, :, cols] = alpha * l_sc[h, :, cols] + jnp.sum(p, 0, keepdims=True)
        m_sc[h, :, cols] = m_new
        pv = _dot(vt_ref[tile, h * V_HEAD:(h + 1) * V_HEAD, :], p.astype(BF16))
        acc_sc[h, :, cols] = alpha * acc_sc[h, :, cols] + pv

    for u in units:
        scores(2 * i, s0, mt0, u, diag=0)
    for u in units:
        scores(2 * i + 1, s1, mt1, u, diag=1)
        absorb(2 * i, s0, mt0, u)

    def pair(jj, c):
        prev = jnp.where(jj == 0, 2 * i + 1, 2 * jj - 1)
        for u in units:
            scores(2 * jj, s0, mt0, u)
            absorb(prev, s1, mt1, u)
        for u in units:
            scores(2 * jj + 1, s1, mt1, u)
            absorb(2 * jj, s0, mt0, u)
        return c

    lax.fori_loop(0, i, pair, 0)
    last = jnp.where(i == 0, 1, 2 * i - 1)
    for u in units:
        absorb(last, s1, mt1, u)

    for h in range(heads):
        o_ref[:, h * V_HEAD:(h + 1) * V_HEAD] = (acc_sc[h] / l_sc[h]).T


def _attention(q, k, vt, heads):
    n = q.shape[0]
    t = ATTN_T
    tq = 2 * t
    assert t % CHUNK == 0 and n % tq == 0 and MLA_HEADS % heads == 0
    once = pl.Buffered(1)
    return pl.pallas_call(
        functools.partial(_attn_body, heads=heads),
        grid=(MLA_HEADS // heads, n // tq),
        in_specs=[pl.BlockSpec((tq, heads * QK_PAD), lambda g, i: (i, g)),
                  pl.BlockSpec((n, heads * QK_PAD), lambda g, i: (0, g), pipeline_mode=once),
                  pl.BlockSpec((n // t, heads * V_HEAD, t), lambda g, i: (0, g, 0), pipeline_mode=once)],
        out_specs=pl.BlockSpec((tq, heads * V_HEAD), lambda g, i: (i, g)),
        out_shape=jax.ShapeDtypeStruct((n, MLA_WIDTH), F32),
        scratch_shapes=[pltpu.VMEM((heads, t, tq), F32)] * 2 + [pltpu.VMEM((heads, 1, tq), F32)] * 4
        + [pltpu.VMEM((heads, V_HEAD, tq), F32)],
        compiler_params=_params("arbitrary", "arbitrary"),
        name="attn",
    )(q, k, vt)


def _attn_cached_body(q_ref, kn_ref, vn_ref, ckv_ref, kpe_ref, wk_ref, wv_ref, o_ref, *, past, seq, nsub):
    heads = range(MLA_HEADS)
    pos = past + lax.broadcasted_iota(jnp.int32, (seq, 1), 0)
    qc = jnp.concatenate([pos] * MLA_HEADS, axis=0) // CHUNK
    kc_past = lax.broadcasted_iota(jnp.int32, (1, past), 1) // CHUNK
    kc_new = (past + lax.broadcasted_iota(jnp.int32, (1, seq), 1)) // CHUNK
    seqs = range(nsub)
    rows = [slice(b * seq, (b + 1) * seq) for b in seqs]
    ckv = [ckv_ref[b].astype(BF16) for b in seqs]
    kpe = [kpe_ref[b].astype(BF16) for b in seqs]
    qa = [jnp.concatenate(
        [_dot_nt(q_ref[r, h * QK_PAD:h * QK_PAD + QK_NOPE], wk_ref[:, h * QK_NOPE:(h + 1) * QK_NOPE])
         for h in heads], axis=0).astype(BF16) for r in rows]
    qpe = [jnp.concatenate(
        [q_ref[r, h * QK_PAD + QK_NOPE:h * QK_PAD + QK_NOPE + QK_ROPE] for h in heads], axis=0) for r in rows]
    s_p = [_dot_nt(qa[b], ckv[b]) + _dot_nt(qpe[b], kpe[b]) for b in seqs]
    s_n = [jnp.concatenate(
        [_dot_nt(q_ref[r, h * QK_PAD:(h + 1) * QK_PAD], kn_ref[r, h * QK_PAD:(h + 1) * QK_PAD])
         for h in heads], axis=0) for r in rows]
    s_p = [jnp.where(kc_past <= qc, s, NEG_INF) for s in s_p]
    s_n = [jnp.where(kc_new <= qc, s, NEG_INF) for s in s_n]
    m = [jnp.maximum(jnp.max(a, -1, keepdims=True), jnp.max(c, -1, keepdims=True)) for a, c in zip(s_p, s_n)]
    e_p = [jnp.exp2(s - mm) for s, mm in zip(s_p, m)]
    e_n = [jnp.exp2(s - mm) for s, mm in zip(s_n, m)]
    l = [jnp.sum(a, -1, keepdims=True) + jnp.sum(c, -1, keepdims=True) for a, c in zip(e_p, e_n)]
    p_n = [(e / ll).astype(BF16) for e, ll in zip(e_n, l)]
    o_lat = [_dot((e / ll).astype(BF16), c).astype(BF16) for e, ll, c in zip(e_p, l, ckv)]
    for b in seqs:
        for h in heads:
            hr = slice(h * seq, (h + 1) * seq)
            cols = slice(h * V_HEAD, (h + 1) * V_HEAD)
            o_ref[rows[b], cols] = _dot(o_lat[b][hr], wv_ref[:, cols]) + _dot(p_n[b][hr], vn_ref[rows[b], cols])


def _attention_cached(q, k, v, ckv_cache, kpe_cache, wk, wv, seq, nsub):
    n = q.shape[0]
    nb, past, _ = ckv_cache.shape
    assert nb % nsub == 0
    row = lambda w: pl.BlockSpec((nsub * seq, w), lambda b: (b, 0))
    kern = functools.partial(_attn_cached_body, past=past, seq=seq, nsub=nsub)
    return pl.pallas_call(
        kern,
        grid=(nb // nsub,),
        in_specs=[row(MLA_HEADS * QK_PAD), row(MLA_HEADS * QK_PAD), row(MLA_WIDTH),
                  pl.BlockSpec((nsub, past, KV_LORA), lambda b: (b, 0, 0)),
                  pl.BlockSpec((nsub, past, QK_ROPE), lambda b: (b, 0, 0)),
                  _const(wk.shape), _const(wv.shape)],
        out_specs=row(MLA_WIDTH),
        out_shape=jax.ShapeDtypeStruct((n, MLA_WIDTH), F32),
        compiler_params=_params("parallel"),
        name="attn_cached",
    )(q, k, v, ckv_cache, kpe_cache, wk, wv)


def _row_blocks(ref):
    rb = ref.shape[0] // MIX_SPLIT
    return [slice(i * rb, (i + 1) * rb) for i in range(MIX_SPLIT)]


def _mix_head(blocks, os_ref, om_ref, x_ref, gm_ref, wo_ref, lng_ref, lnb_ref, wxq_ref):
    om = [_rms(om_ref[r, :], gm_ref[...]).astype(BF16) for r in blocks]
    a = [_dot(os_ref[r, :], wo_ref[:SSM_WIDTH, :]) + _dot(o, wo_ref[SSM_WIDTH:, :]) for r, o in zip(blocks, om)]
    h1 = [_ln(ALPHA * x_ref[r, :] + aa, lng_ref[0:1, :], lnb_ref[0:1, :]) for r, aa in zip(blocks, a)]
    return h1, [_dot(h.astype(BF16), wxq_ref[...]).astype(BF16) for h in h1]


def _mix_body(os_ref, om_ref, x_ref, gm_ref, wo_ref, lng_ref, lnb_ref, wxq_ref, h1_ref, qx_ref):
    blocks = _row_blocks(x_ref)
    h1, qx = _mix_head(blocks, os_ref, om_ref, x_ref, gm_ref, wo_ref, lng_ref, lnb_ref, wxq_ref)
    for r, h, q in zip(blocks, h1, qx):
        h1_ref[r, :] = h
        qx_ref[r, :] = q


def _mix(o_ssm, o_mla, x2, g_mla, w_o, ln_g, ln_b, w_xq, tn):
    n = x2.shape[0]
    row = lambda w: pl.BlockSpec((tn, w), lambda i: (i, 0))
    return pl.pallas_call(
        _mix_body,
        grid=(n // tn,),
        in_specs=[row(SSM_WIDTH), row(MLA_WIDTH), row(D_MODEL), _const(g_mla.shape), _const(w_o.shape),
                  _const(ln_g.shape), _const(ln_b.shape), _const(w_xq.shape)],
        out_specs=[row(D_MODEL), row(D_MODEL)],
        out_shape=[jax.ShapeDtypeStruct((n, D_MODEL), F32), jax.ShapeDtypeStruct((n, D_MODEL), BF16)],
        compiler_params=_params("parallel"),
        name="mix",
    )(o_ssm, o_mla, x2, g_mla, w_o, ln_g, ln_b, w_xq)


def _mem_attn_heads(qs, mks, mvs):
    ss = [_dot_nt(q, mk.astype(BF16)) * (X_HEAD_DIM ** -0.5) for q, mk in zip(qs, mks)]
    es = [jnp.exp(s - jnp.max(s, -1, keepdims=True)) for s in ss]
    ps = [(e / jnp.sum(e, -1, keepdims=True)).astype(BF16) for e in es]
    return [_dot(p, mv.astype(BF16)).astype(BF16) for p, mv in zip(ps, mvs)]


def _mix_mem_body(os_ref, om_ref, x_ref, gm_ref, wo_ref, lng_ref, lnb_ref, wxq_ref, mk_ref, mv_ref, wxo_ref,
                  h2_ref):
    blocks = _row_blocks(x_ref)
    h1, qx = _mix_head(blocks, os_ref, om_ref, x_ref, gm_ref, wo_ref, lng_ref, lnb_ref, wxq_ref)
    cols = [slice(h * X_HEAD_DIM, (h + 1) * X_HEAD_DIM) for h in range(X_HEADS)]
    heads = _mem_attn_heads([q[:, c] for q in qx for c in cols], [mk_ref[:, c] for _ in qx for c in cols],
                            [mv_ref[:, c] for _ in qx for c in cols])
    ox = [jnp.concatenate(heads[i * X_HEADS:(i + 1) * X_HEADS], axis=1) for i in range(len(blocks))]
    att = [_dot(o, wxo_ref[...]) for o in ox]
    for r, h, a in zip(blocks, h1, att):
        h2_ref[r, :] = _ln(ALPHA * h + a, lng_ref[1:2, :], lnb_ref[1:2, :])


def _mix_mem(o_ssm, o_mla, x2, g_mla, w_o, ln_g, ln_b, w_xq, mem_k, mem_v, w_xo, tn):
    n = x2.shape[0]
    row = lambda w: pl.BlockSpec((tn, w), lambda i: (i, 0))
    consts = (g_mla, w_o, ln_g, ln_b, w_xq, mem_k, mem_v, w_xo)
    return pl.pallas_call(
        _mix_mem_body,
        grid=(n // tn,),
        in_specs=[row(SSM_WIDTH), row(MLA_WIDTH), row(D_MODEL)] + [_const(c.shape) for c in consts],
        out_specs=row(D_MODEL),
        out_shape=jax.ShapeDtypeStruct((n, D_MODEL), F32),
        compiler_params=_params("parallel"),
        name="mix_mem",
    )(o_ssm, o_mla, x2, *consts)


def _mem_attn_cache_body(qx_ref, mk_hbm, mv_hbm, o_ref, kbuf, vbuf, sems, *, seq, nsub):
    b = pl.program_id(0)
    nb = pl.num_programs(0)

    def copies(step, slot):
        return [pltpu.make_async_copy(src.at[step * nsub + j, :, h, :], buf.at[slot, j, h], sems.at[slot, j, t, h])
                for t, (src, buf) in enumerate(((mk_hbm, kbuf), (mv_hbm, vbuf)))
                for j in range(nsub) for h in range(X_HEADS)]

    @pl.when(b == 0)
    def _():
        for c in copies(0, 0):
            c.start()

    @pl.when(b + 1 < nb)
    def _():
        for c in copies(b + 1, (b + 1) % 2):
            c.start()

    slot = b % 2
    for c in copies(b, slot):
        c.wait()
    jh = [(j, h) for j in range(nsub) for h in range(X_HEADS)]
    where = [(slice(j * seq, (j + 1) * seq), slice(h * X_HEAD_DIM, (h + 1) * X_HEAD_DIM)) for j, h in jh]
    outs = _mem_attn_heads([qx_ref[r, c] for r, c in where], [kbuf[slot, j, h] for j, h in jh],
                           [vbuf[slot, j, h] for j, h in jh])
    for (r, c), o in zip(where, outs):
        o_ref[r, c] = o


def _mem_attn_cache(qx, mem_k, mem_v, seq, nsub):
    n = qx.shape[0]
    nb = mem_k.shape[0]
    assert nb % nsub == 0
    row = pl.BlockSpec((nsub * seq, D_MODEL), lambda b: (b, 0))
    hbm = pl.BlockSpec(memory_space=pl.ANY)
    buf = pltpu.VMEM((2, nsub, X_HEADS, N_MEM, X_HEAD_DIM), F32)
    return pl.pallas_call(
        functools.partial(_mem_attn_cache_body, seq=seq, nsub=nsub),
        grid=(nb // nsub,),
        in_specs=[row, hbm, hbm],
        out_specs=row,
        out_shape=jax.ShapeDtypeStruct((n, D_MODEL), BF16),
        scratch_shapes=[buf, buf, pltpu.SemaphoreType.DMA((2, nsub, 2, X_HEADS))],
        compiler_params=_params("arbitrary"),
        name="mem_attn_cache",
    )(qx, mem_k, mem_v)


def _mlp_body(*refs, ff_blk, second_norm):
    if second_norm:
        h1_ref, ox_ref, wxo_ref, lng_ref, lnb_ref, w1_ref, w2_ref, y_ref = refs
        h2 = _ln(ALPHA * h1_ref[...] + _dot(ox_ref[...], wxo_ref[...]), lng_ref[1:2, :], lnb_ref[1:2, :])
    else:
        h2_ref, lng_ref, lnb_ref, w1_ref, w2_ref, y_ref = refs
        h2 = h2_ref[...]
    hb = h2.astype(BF16)
    acc = jnp.zeros(h2.shape, F32)
    for c in range(0, D_FF, ff_blk):
        z = jnp.maximum(_dot(hb, w1_ref[:, c:c + ff_blk]), 0.0)
        acc += _dot((z * z).astype(BF16), w2_ref[c:c + ff_blk, :])
    y_ref[...] = _ln(ALPHA * h2 + acc, lng_ref[2:3, :], lnb_ref[2:3, :])


def _mlp(h, ox, w_xo, ln_g, ln_b, w1, w2, tn):
    n = h.shape[0]
    row = pl.BlockSpec((tn, D_MODEL), lambda i: (i, 0))
    second_norm = ox is not None
    rows = (h, ox) if second_norm else (h,)
    consts = ((w_xo,) if second_norm else ()) + (ln_g, ln_b, w1, w2)
    return pl.pallas_call(
        functools.partial(_mlp_body, ff_blk=1024, second_norm=second_norm),
        grid=(n // tn,),
        in_specs=[row] * len(rows) + [_const(c.shape) for c in consts],
        out_specs=row,
        out_shape=jax.ShapeDtypeStruct((n, D_MODEL), F32),
        compiler_params=_params("parallel"),
        name="mlp",
    )(*rows, *consts)


def _mem_kv_body(mem_ref, wk_ref, wv_ref, k_ref, v_ref):
    m = mem_ref[...].astype(BF16)
    k_ref[...] = _dot(m, wk_ref[...])
    v_ref[...] = _dot(m, wv_ref[...])


def _mem_kv(mem2, wk, wv):
    n = mem2.shape[0]
    out = jax.ShapeDtypeStruct((n, D_MODEL), F32)
    return pl.pallas_call(
        _mem_kv_body,
        grid=(1,),
        in_specs=[_const(mem2.shape), _const(wk.shape), _const(wv.shape)],
        out_specs=[_const((n, D_MODEL))] * 2,
        out_shape=[out, out],
        compiler_params=_params("arbitrary"),
        name="mem_kv",
    )(mem2, wk, wv)


def _state_to_cols(s):
    return jnp.transpose(s.reshape(s.shape[0], S5_COLS, -1), (1, 0, 2))


def _state_from_cols(s):
    return jnp.transpose(s, (1, 0, 2)).reshape(s.shape[1], SSM_GROUPS, SSM_STATE)


def _layer(x2, pos, nseq, h0r, h0i, mem_k, mem_v, caches, wts, s5_ops, tn, scan_ct):
    n = x2.shape[0]
    seq = n // nseq
    prompt = caches is None
    u, q, k, v, ckv, kpe = _project(x2, pos, wts["inv"], wts["w_in"], wts["g_q"], wts["w_q"], wts["g_kv"],
                                    wts["w_k"], wts["w_vt"] if prompt else wts["w_v"], tn,
                                    v_transposed=prompt, consecutive=prompt)
    y_ssm, fr, fi = _s5_scan(u, s5_ops, _state_to_cols(h0r), _state_to_cols(h0i), nseq, scan_ct)
    o_ssm = _glu(y_ssm, u, wts["d_skip"], wts["w_glu"], wts["g_out_ssm"], tn)
    if prompt:
        o_mla = _attention(q, k, v, ATTN_HEADS)
    else:
        o_mla = _attention_cached(q, k, v, caches[0], caches[1], wts["w_k"], wts["w_v"], seq, nsub=2)
    mix_args = (o_ssm, o_mla, x2, wts["g_out_mla"], wts["w_o"], wts["ln_g"], wts["ln_b"], wts["w_xq"])
    if prompt:
        h, ox = _mix_mem(*mix_args, mem_k, mem_v, wts["w_xo"], tn), None
    else:
        h, qx = _mix(*mix_args, tn)
        ox = _mem_attn_cache(qx, mem_k, mem_v, seq, nsub=2)
    y = _mlp(h, ox, wts["w_xo"], wts["ln_g"], wts["ln_b"], wts["w_ff1"], wts["w_ff2"], tn)
    return y, ckv, kpe, _state_from_cols(fr), _state_from_cols(fi)


def kernel(x_prompt, x_sample, mem_prompt, cache_mla_ckv, cache_mla_kpe, state_ssm_re, state_ssm_im, cache_mem_k, cache_mem_v, w_in, g_q, w_q_up, g_kv, w_kv_up, a_re, a_im, b_re, b_im, c_re, c_im, d_skip, log_dt, w_glu, g_out_ssm, g_out_mla, w_o, w_xq, w_xk, w_xv, w_xo, w_ff1, w_ff2, ln_g, ln_b):
    assert w_in.shape[0] == DEPTH == 1
    nbp, sp, _ = x_prompt.shape
    nbs, sd, _ = x_sample.shape
    past = cache_mla_ckv.shape[2]
    assert nbp == 1

    wq = jnp.pad(w_q_up[0], ((0, 0), (0, 0), (0, QK_PAD - QK_NOPE - QK_ROPE)))
    wk = w_kv_up[0][:, :, :QK_NOPE].reshape(KV_LORA, -1).astype(BF16)
    wv = w_kv_up[0][:, :, QK_NOPE:].reshape(KV_LORA, -1).astype(BF16)
    inv = ROPE_THETA ** (-jnp.arange(ROPE_HALF, dtype=F32) / ROPE_HALF)
    wts = {
        "inv": jnp.tile(inv, LANES // ROPE_HALF).reshape(1, LANES),
        "w_in": jnp.pad(w_in[0], ((0, 0), (0, LANES - QK_ROPE))).astype(BF16),
        "g_q": g_q[0].reshape(1, -1),
        "w_q": wq.reshape(Q_LORA, MLA_HEADS * QK_PAD).astype(BF16),
        "g_kv": g_kv[0].reshape(1, -1),
        "w_k": wk,
        "w_v": wv,
        "w_vt": wv.T,
        "d_skip": d_skip[0].reshape(1, -1),
        "w_glu": w_glu[0].astype(BF16),
        "g_out_ssm": g_out_ssm[0].reshape(1, -1),
        "g_out_mla": g_out_mla[0].reshape(1, -1),
        "w_o": w_o[0].astype(BF16),
        "w_xq": w_xq[0].reshape(D_MODEL, D_MODEL).astype(BF16),
        "w_xo": w_xo[0].reshape(D_MODEL, D_MODEL).astype(BF16),
        "w_ff1": w_ff1[0].astype(BF16),
        "w_ff2": w_ff2[0].astype(BF16),
        "ln_g": ln_g[0],
        "ln_b": ln_b[0],
    }
    s5_ops = _s5_prep(a_re[0], a_im[0], b_re[0], b_im[0], c_re[0], c_im[0], log_dt[0])

    mk, mv = _mem_kv(mem_prompt.reshape(nbp * N_MEM, D_MODEL),
                     w_xk[0].reshape(D_MODEL, D_MODEL).astype(BF16),
                     w_xv[0].reshape(D_MODEL, D_MODEL).astype(BF16))
    zero = jnp.zeros((nbp, SSM_GROUPS, SSM_STATE), F32)
    pos_p = jnp.arange(sp, dtype=F32).reshape(sp, 1)
    yp, ckv_p, kpe_p, sre_p, sim_p = _layer(
        x_prompt.reshape(sp, D_MODEL), pos_p, nbp, zero, zero, mk, mv, None,
        wts, s5_ops, tn=512, scan_ct=512)

    pos_s = jnp.tile(past + jnp.arange(sd, dtype=F32), nbs).reshape(nbs * sd, 1)
    caches = (cache_mla_ckv[0], cache_mla_kpe[0])
    ys, ckv_s, kpe_s, sre_s, sim_s = _layer(
        x_sample.reshape(nbs * sd, D_MODEL), pos_s, nbs,
        state_ssm_re[0], state_ssm_im[0], cache_mem_k[0], cache_mem_v[0], caches,
        wts, s5_ops, tn=512, scan_ct=sd // S5_T)

    return (yp.reshape(nbp, sp, D_MODEL), ys.reshape(nbs, sd, D_MODEL),
            ckv_p.reshape(1, nbp, sp, KV_LORA), kpe_p.reshape(1, nbp, sp, QK_ROPE),
            sre_p.reshape(1, nbp, SSM_GROUPS, SSM_STATE), sim_p.reshape(1, nbp, SSM_GROUPS, SSM_STATE),
            mk.reshape(1, nbp, N_MEM, X_HEADS, X_HEAD_DIM), mv.reshape(1, nbp, N_MEM, X_HEADS, X_HEAD_DIM),
            ckv_s.reshape(1, nbs, sd, KV_LORA), kpe_s.reshape(1, nbs, sd, QK_ROPE),
            sre_s.reshape(1, nbs, SSM_GROUPS, SSM_STATE), sim_s.reshape(1, nbs, SSM_GROUPS, SSM_STATE))
```

```python
import functools
import math

import jax
import jax.numpy as jnp
from jax import lax
from jax.experimental import pallas as pl
from jax.experimental.pallas import tpu as pltpu

F32 = jnp.float32
BF16 = jnp.bfloat16

D_MODEL = 1024
DEPTH = 1
CHUNK = 64
SSM_WIDTH = 512
SSM_GROUP = 16
SSM_GROUPS = 32
SSM_STATE = 64
MLA_HEADS = 4
QK_NOPE = 128
QK_ROPE = 64
V_HEAD = 128
MLA_WIDTH = MLA_HEADS * V_HEAD
Q_LORA = 384
KV_LORA = 256
ROPE_THETA = 10000.0
MLA_SCALE = (QK_NOPE + QK_ROPE) ** -0.5
N_MEM = 256
X_HEADS = 4
X_HEAD_DIM = D_MODEL // X_HEADS
D_FF = 4 * D_MODEL
ALPHA = (2 * DEPTH) ** 0.25
EPS = 1e-5
NEG_INF = -1e30

LANES = 128
QK_PAD = 256
S5_T = 16
S5_COL_GROUPS = LANES // SSM_GROUP
S5_COLS = SSM_GROUPS // S5_COL_GROUPS
S5_SW = S5_COL_GROUPS * SSM_STATE
ROPE_HALF = QK_ROPE // 2
Q_SCALE = MLA_SCALE * math.log2(math.e)
ATTN_T = 512
ATTN_HEADS = 2
ATTN_QSPLIT = 4
MIX_SPLIT = 2
PROJ_ROWS = 256
VMEM_LIMIT = 56 * 1024 * 1024

_NT = (((1,), (1,)), ((), ()))


def _rms(x, g):
    return x * lax.rsqrt(jnp.mean(x * x, -1, keepdims=True) + EPS) * g


def _ln(x, g, b):
    mu = jnp.mean(x, -1, keepdims=True)
    xc = x - mu
    var = jnp.mean(xc * xc, -1, keepdims=True)
    return xc * lax.rsqrt(var + EPS) * g + b


def _dot(a, b):
    return jnp.dot(a, b, preferred_element_type=F32)


def _dot_nt(a, b):
    return lax.dot_general(a, b, _NT, preferred_element_type=F32)


def _split(a):
    hi = a.astype(BF16)
    return hi, (a - hi.astype(F32)).astype(BF16)


def _dot_split(a, b):
    return _dot(a[0], b[0]) + _dot(a[0], b[1]) + _dot(a[1], b[0])


def _params(*sem):
    return pltpu.CompilerParams(dimension_semantics=sem, vmem_limit_bytes=VMEM_LIMIT)


def _const(shape):
    n = len(shape)
    return pl.BlockSpec(shape, lambda *_: (0,) * n)


def _proj_body(x_ref, pos_ref, inv_ref, w_in_ref, gq_ref, wq_ref, gkv_ref, wk_ref, wv_ref,
               u_ref, q_ref, k_ref, v_ref, ckv_ref, kpe_ref, tc_ref, ts_ref, *, v_transposed, consecutive):
    rb = PROJ_ROWS
    assert x_ref.shape[0] % rb == 0 and ATTN_T % rb == 0
    blocks = [slice(r0, r0 + rb) for r0 in range(0, x_ref.shape[0], rb)]
    c0 = SSM_WIDTH + Q_LORA
    projs = [_dot(x_ref[r, :].astype(BF16), w_in_ref[...]) for r in blocks]
    cqs = [_rms(p[:, SSM_WIDTH:c0], gq_ref[...]).astype(BF16) for p in projs]
    ckvs = [_rms(p[:, c0:c0 + KV_LORA], gkv_ref[...]) for p in projs]
    qs = [_dot(c, wq_ref[...]) * Q_SCALE for c in cqs]
    kns = [_dot(c.astype(BF16), wk_ref[...]) for c in ckvs]
    for r, p, ckv in zip(blocks, projs, ckvs):
        u_ref[r, :] = p[:, :SSM_WIDTH]
        ckv_ref[r, :] = ckv
        if v_transposed:
            lanes = slice(r.start % ATTN_T, r.start % ATTN_T + rb)
            v_ref[r.start // ATTN_T, :, lanes] = _dot_nt(wv_ref[...], ckv.astype(BF16)).astype(BF16)
        else:
            v_ref[r, :] = _dot(ckv.astype(BF16), wv_ref[...]).astype(BF16)

    lane = lax.broadcasted_iota(jnp.int32, (1, LANES), 1)
    live = lane < QK_ROPE
    sign = jnp.where(lane < ROPE_HALF, -1.0, 1.0)
    inv = inv_ref[...]
    if consecutive:
        @pl.when(pl.program_id(0) == 0)
        def _():
            r = lax.broadcasted_iota(jnp.int32, (x_ref.shape[0], 1), 0).astype(F32)
            tc_ref[...] = jnp.where(live, jnp.cos(r * inv), 0.0)
            ts_ref[...] = jnp.where(live, jnp.sin(r * inv), 0.0)

        base = pos_ref[0:1, :] * inv
        ca, sa = jnp.cos(base), jnp.sin(base)
        cos_ts = [ca * tc_ref[r, :] - sa * ts_ref[r, :] for r in blocks]
        sin_ts = [(sa * sign) * tc_ref[r, :] + (ca * sign) * ts_ref[r, :] for r in blocks]
    else:
        angs = [pos_ref[r, :] * inv for r in blocks]
        cos_ts = [jnp.where(live, jnp.cos(a), 0.0) for a in angs]
        sin_ts = [jnp.where(live, jnp.sin(a) * sign, 0.0) for a in angs]

    def rope(c2, cos_t, sin_t):
        swapped = jnp.where(lane < ROPE_HALF, pltpu.roll(c2, LANES - ROPE_HALF, 1), pltpu.roll(c2, ROPE_HALF, 1))
        return c2 * cos_t + swapped * sin_t

    for r, p, q, kn, cos_t, sin_t in zip(blocks, projs, qs, kns, cos_ts, sin_ts):
        kpe = rope(p[:, c0 + KV_LORA:], cos_t, sin_t)
        kpe_ref[r, :] = kpe[:, :QK_ROPE]
        kpe_b = kpe.astype(BF16)
        for h in range(MLA_HEADS):
            a = h * QK_PAD
            q_ref[r, a:a + QK_NOPE] = q[:, a:a + QK_NOPE].astype(BF16)
            q_ref[r, a + QK_NOPE:a + QK_PAD] = rope(q[:, a + QK_NOPE:a + QK_PAD], cos_t, sin_t).astype(BF16)
            k_ref[r, a:a + QK_NOPE] = kn[:, h * QK_NOPE:(h + 1) * QK_NOPE].astype(BF16)
            k_ref[r, a + QK_NOPE:a + QK_PAD] = kpe_b


def _project(x2, pos, inv, w_in, gq, wq, gkv, wk, wv, tn, v_transposed, consecutive):
    n = x2.shape[0]
    row = lambda w: pl.BlockSpec((tn, w), lambda i: (i, 0))
    if v_transposed:
        assert tn % ATTN_T == 0
        v_spec = pl.BlockSpec((tn // ATTN_T, MLA_WIDTH, ATTN_T), lambda i: (i, 0, 0))
        v_shape = jax.ShapeDtypeStruct((n // ATTN_T, MLA_WIDTH, ATTN_T), BF16)
    else:
        v_spec, v_shape = row(MLA_WIDTH), jax.ShapeDtypeStruct((n, MLA_WIDTH), BF16)
    return pl.pallas_call(
        functools.partial(_proj_body, v_transposed=v_transposed, consecutive=consecutive),
        grid=(n // tn,),
        scratch_shapes=[pltpu.VMEM((tn, LANES), F32)] * 2,
        in_specs=[row(D_MODEL), row(1), _const(inv.shape), _const(w_in.shape), _const(gq.shape),
                  _const(wq.shape), _const(gkv.shape), _const(wk.shape), _const(wv.shape)],
        out_specs=[row(SSM_WIDTH), row(MLA_HEADS * QK_PAD), row(MLA_HEADS * QK_PAD),
                   v_spec, row(KV_LORA), row(QK_ROPE)],
        out_shape=[jax.ShapeDtypeStruct((n, SSM_WIDTH), F32),
                   jax.ShapeDtypeStruct((n, MLA_HEADS * QK_PAD), BF16),
                   jax.ShapeDtypeStruct((n, MLA_HEADS * QK_PAD), BF16),
                   v_shape,
                   jax.ShapeDtypeStruct((n, KV_LORA), F32),
                   jax.ShapeDtypeStruct((n, QK_ROPE), F32)],
        compiler_params=_params("arbitrary"),
        name="proj",
    )(x2, pos, inv, w_in, gq, wq, gkv, wk, wv)


def _s5_prep_body(ar_row, ai_row, ldt_row, bt_re, bt_im, ct_re, ct_im,
                  wt_ref, v_ref, z_ref, are_ref, aim_ref):
    sw = S5_SW
    arr, air, dtr = ar_row[0], ai_row[0], jnp.exp(ldt_row[0])

    mag = jnp.exp(arr * dtr)
    lr, li = mag * jnp.cos(air * dtr), mag * jnp.sin(air * dtr)
    nr, ni = lr - 1.0, li
    den = arr * arr + air * air
    f_re, f_im = (nr * arr + ni * air) / den, (ni * arr - nr * air) / den

    same_b = (lax.broadcasted_iota(jnp.int32, (LANES, sw), 0) // SSM_GROUP
              == lax.broadcasted_iota(jnp.int32, (LANES, sw), 1) // SSM_STATE)
    br = jnp.where(same_b, bt_re[0], 0.0)
    bi = jnp.where(same_b, bt_im[0], 0.0)
    bb_re = f_re * br - f_im * bi
    bb_im = f_re * bi + f_im * br
    same_c = (lax.broadcasted_iota(jnp.int32, (sw, LANES), 0) // SSM_STATE
              == lax.broadcasted_iota(jnp.int32, (sw, LANES), 1) // SSM_GROUP)
    cr = jnp.where(same_c, ct_re[0], 0.0)
    ci = jnp.where(same_c, ct_im[0], 0.0)
    cr_s, ci_s = _split(cr), _split(ci)

    e = lax.broadcasted_iota(jnp.int32, (2 * S5_T, 1), 0).astype(F32)
    pm = jnp.exp(arr * dtr * e)
    pw_re, pw_im = pm * jnp.cos(air * dtr * e), pm * jnp.sin(air * dtr * e)
    pt_re, pt_im = pw_re.T, pw_im.T

    lag_ops = []
    for lag in range(S5_T):
        p_re, p_im = pw_re[lag:lag + 1, :], pw_im[lag:lag + 1, :]
        k_re = p_re * bb_re - p_im * bb_im
        k_im = p_re * bb_im + p_im * bb_re
        i = S5_T - 1 - lag
        v_ref[0, i * LANES:(i + 1) * LANES, :sw] = k_re.astype(BF16)
        v_ref[0, i * LANES:(i + 1) * LANES, sw:] = k_im.astype(BF16)
        lag_ops.append(_dot_split(_split(k_re), cr_s) - _dot_split(_split(k_im), ci_s))

        q_re, q_im = pt_re[:, lag + 1:lag + 2], pt_im[:, lag + 1:lag + 2]
        z_ref[0, :sw, lag * LANES:(lag + 1) * LANES] = (cr * q_re - ci * q_im).astype(BF16)
        z_ref[0, sw:, lag * LANES:(lag + 1) * LANES] = (-(cr * q_im + ci * q_re)).astype(BF16)

    zero = jnp.zeros((LANES, LANES), BF16)
    for d in range(S5_T // 2):
        wt_ref[0, d, :LANES, :LANES] = lag_ops[2 * d].astype(BF16)
        wt_ref[0, d, :LANES, LANES:] = lag_ops[2 * d + 1].astype(BF16)
        wt_ref[0, d, LANES:, :LANES] = lag_ops[2 * d - 1].astype(BF16) if d else zero
        wt_ref[0, d, LANES:, LANES:] = lag_ops[2 * d].astype(BF16)

    are_ref[0] = pw_re[S5_T:S5_T + 1, :]
    aim_ref[0] = pw_im[S5_T:S5_T + 1, :]


def _s5_prep(a_re, a_im, b_re, b_im, c_re, c_im, log_dt):
    nc, r, t, h, p, sw = S5_COLS, S5_COL_GROUPS, S5_T, SSM_GROUP, SSM_STATE, S5_SW
    ldt = jnp.repeat(log_dt, p)
    bt = lambda b: jnp.tile(jnp.transpose(b.reshape(nc, r, p, h), (0, 1, 3, 2)).reshape(nc, r * h, p), (1, 1, r))
    ct = lambda c: jnp.tile(jnp.transpose(c.reshape(nc, r, h, p), (0, 1, 3, 2)).reshape(nc, r * p, h), (1, 1, r))
    args = (a_re.reshape(nc, 1, sw), a_im.reshape(nc, 1, sw), ldt.reshape(nc, 1, sw),
            bt(b_re), bt(b_im), ct(c_re), ct(c_im))
    blk = lambda s: pl.BlockSpec((1,) + s[1:], lambda i: (i,) + (0,) * (len(s) - 1))
    outs = [((nc, t // 2, 2 * LANES, 2 * LANES), BF16), ((nc, t * LANES, 2 * sw), BF16),
            ((nc, 2 * sw, t * LANES), BF16), ((nc, 1, sw), F32), ((nc, 1, sw), F32)]
    return pl.pallas_call(
        _s5_prep_body,
        grid=(nc,),
        in_specs=[blk(a.shape) for a in args],
        out_specs=[blk(s) for s, _ in outs],
        out_shape=[jax.ShapeDtypeStruct(s, d) for s, d in outs],
        compiler_params=_params("parallel"),
        name="s5_prep",
    )(*args)


def _s5_scan_body(u_ref, wt_ref, v_ref, z_ref, are_ref, aim_ref, h0r_ref, h0i_ref,
                  y_ref, fr_ref, fi_ref, lhs, sre, sim, xre, xim, cr, ci, *, nseq, ct):
    t = pl.program_id(1)
    sw = S5_SW
    seq = ct * S5_T

    @pl.when(t == 0)
    def _():
        cr[...] = h0r_ref[0]
        ci[...] = h0i_ref[0]

    def token_rows(cc, i):
        if nseq == 1:
            return pl.ds(i, ct, stride=S5_T), slice(None)
        return pl.ds(cc * S5_T + i, nseq, stride=seq), slice(cc * nseq, (cc + 1) * nseq)

    for cc in range(1 if nseq == 1 else ct):
        for i in range(S5_T):
            tok, crow = token_rows(cc, i)
            lhs[crow, i * LANES:(i + 1) * LANES] = u_ref[tok, :].astype(BF16)

    s = _dot(lhs[...], v_ref[0])
    sre[...] = s[:, :sw]
    sim[...] = s[:, sw:]

    a_re, a_im = are_ref[0], aim_ref[0]

    def step(c, carry):
        x_re, x_im = carry
        r = pl.ds(c * nseq, nseq)
        xre[r, :] = x_re
        xim[r, :] = x_im
        n_re = a_re * x_re - a_im * x_im + sre[r, :]
        n_im = a_re * x_im + a_im * x_re + sim[r, :]
        return n_re, n_im

    x_re, x_im = lax.fori_loop(0, ct, step, (cr[...], ci[...]))
    cr[...] = x_re
    ci[...] = x_im
    fr_ref[0] = x_re
    fi_ref[0] = x_im

    xb_re, xb_im = xre[...].astype(BF16), xim[...].astype(BF16)
    for jp in range(S5_T // 2):
        cols = slice(jp * 2 * LANES, (jp + 1) * 2 * LANES)
        acc = _dot(xb_re, z_ref[0, :sw, cols]) + _dot(xb_im, z_ref[0, sw:, cols])
        for ip in range(jp + 1):
            acc += _dot(lhs[:, ip * 2 * LANES:(ip + 1) * 2 * LANES], wt_ref[0, jp - ip])
        for cc in range(1 if nseq == 1 else ct):
            for jj in range(2):
                tok, crow = token_rows(cc, 2 * jp + jj)
                y_ref[tok, :] = acc[crow, jj * LANES:(jj + 1) * LANES]


def _s5_scan(u, col_ops, h0r, h0i, nseq, ct):
    wt, v, z, are, aim = col_ops
    n = u.shape[0]
    cps = n // nseq // S5_T
    assert nseq == 1 or ct == cps
    nc = nseq * ct
    sw = S5_SW
    tile = pl.BlockSpec((nc * S5_T, LANES), lambda c, t: (t, c))
    op = lambda a: pl.BlockSpec((1,) + a.shape[1:], lambda c, t: (c,) + (0,) * (a.ndim - 1))
    st = pl.BlockSpec((1, nseq, sw), lambda c, t: (c, 0, 0))
    st_shape = jax.ShapeDtypeStruct((S5_COLS, nseq, sw), F32)
    return pl.pallas_call(
        functools.partial(_s5_scan_body, nseq=nseq, ct=ct),
        grid=(S5_COLS, cps // ct),
        in_specs=[tile, op(wt), op(v), op(z), op(are), op(aim), st, st],
        out_specs=[tile, st, st],
        out_shape=[jax.ShapeDtypeStruct(u.shape, F32), st_shape, st_shape],
        scratch_shapes=[pltpu.VMEM((nc, S5_T * LANES), BF16)] + [pltpu.VMEM((nc, sw), F32)] * 4
        + [pltpu.VMEM((nseq, sw), F32)] * 2,
        compiler_params=_params("arbitrary", "arbitrary"),
        name="s5_scan",
    )(u, wt, v, z, are, aim, h0r, h0i)


def _glu_rows(ys, us, d_ref, w_ref, g_ref):
    acts = [jax.nn.gelu(y + d_ref[...] * u).astype(BF16) for y, u in zip(ys, us)]
    gls = [_dot(a, w_ref[...]) for a in acts]
    os = [gl[:, :SSM_WIDTH] * jax.nn.sigmoid(gl[:, SSM_WIDTH:]) for gl in gls]
    return [_rms(o, g_ref[...]).astype(BF16) for o in os]


def _glu_body(y_ref, u_ref, d_ref, w_ref, g_ref, o_ref):
    o_ref[...] = _glu_rows([y_ref[...]], [u_ref[...]], d_ref, w_ref, g_ref)[0]


def _glu(y, u, d, w, g, tn):
    n = y.shape[0]
    row = pl.BlockSpec((tn, SSM_WIDTH), lambda i: (i, 0))
    return pl.pallas_call(
        _glu_body,
        grid=(n // tn,),
        in_specs=[row, row, _const(d.shape), _const(w.shape), _const(g.shape)],
        out_specs=row,
        out_shape=jax.ShapeDtypeStruct((n, SSM_WIDTH), BF16),
        compiler_params=_params("parallel"),
        name="glu",
    )(y, u, d, w, g)


def _attn_body(q_ref, k_ref, vt_ref, o_ref, s0, s1, mt0, mt1, m_sc, l_sc, acc_sc, *, heads):
    i = pl.program_id(1)
    t = ATTN_T
    m_sc[...] = jnp.full(m_sc.shape, -jnp.inf, F32)
    l_sc[...] = jnp.zeros(l_sc.shape, F32)
    acc_sc[...] = jnp.zeros(acc_sc.shape, F32)

    qb = q_ref.shape[0] // ATTN_QSPLIT
    units = [(h, c * qb) for c in range(ATTN_QSPLIT) for h in range(heads)]

    def scores(tile, s_buf, mt_buf, unit, diag=None):
        h, c0 = unit
        r = pl.ds(pl.multiple_of(tile * t, t), t)
        s = _dot_nt(k_ref[r, h * QK_PAD:(h + 1) * QK_PAD],
                    q_ref[c0:c0 + qb, h * QK_PAD:(h + 1) * QK_PAD])
        if diag is not None:
            kc = (diag * t + lax.broadcasted_iota(jnp.int32, s.shape, 0)) // CHUNK
            qc = (c0 + lax.broadcasted_iota(jnp.int32, s.shape, 1)) // CHUNK
            s = jnp.where(kc <= qc, s, NEG_INF)
        s_buf[h, :, c0:c0 + qb] = s
        mt_buf[h, :, c0:c0 + qb] = jnp.max(s, 0, keepdims=True)

    def absorb(tile, s_buf, mt_buf, unit):
        h, c0 = unit
        cols = slice(c0, c0 + qb)
        m_prev = m_sc[h, :, cols]
        m_new = jnp.maximum(m_prev, mt_buf[h, :, cols])
        p = jnp.exp2(s_buf[h, :, cols] - m_new)
        alpha = jnp.exp2(m_prev - m_new)
        l_sc[h, :, cols] = alpha * l_sc[h, :, cols] + jnp.sum(p, 0, keepdims=True)
        m_sc[h, :, cols] = m_new
        pv = _dot(vt_ref[tile, h * V_HEAD:(h + 1) * V_HEAD, :], p.astype(BF16))
        acc_sc[h, :, cols] = alpha * acc_sc[h, :, cols] + pv

    for u in units:
        scores(2 * i, s0, mt0, u, diag=0)
    for u in units:
        scores(2 * i + 1, s1, mt1, u, diag=1)
        absorb(2 * i, s0, mt0, u)

    def pair(jj, c):
        prev = jnp.where(jj == 0, 2 * i + 1, 2 * jj - 1)
        for u in units:
            scores(2 * jj, s0, mt0, u)
            absorb(prev, s1, mt1, u)
        for u in units:
            scores(2 * jj + 1, s1, mt1, u)
            absorb(2 * jj, s0, mt0, u)
        return c

    lax.fori_loop(0, i, pair, 0)
    last = jnp.where(i == 0, 1, 2 * i - 1)
    for u in units:
        absorb(last, s1, mt1, u)

    for h in range(heads):
        o_ref[:, h * V_HEAD:(h + 1) * V_HEAD] = (acc_sc[h] / l_sc[h]).T


def _attention(q, k, vt, heads):
    n = q.shape[0]
    t = ATTN_T
    tq = 2 * t
    assert t % CHUNK == 0 and n % tq == 0 and MLA_HEADS % heads == 0
    once = pl.Buffered(1)
    return pl.pallas_call(
        functools.partial(_attn_body, heads=heads),
        grid=(MLA_HEADS // heads, n // tq),
        in_specs=[pl.BlockSpec((tq, heads * QK_PAD), lambda g, i: (i, g)),
                  pl.BlockSpec((n, heads * QK_PAD), lambda g, i: (0, g), pipeline_mode=once),
                  pl.BlockSpec((n // t, heads * V_HEAD, t), lambda g, i: (0, g, 0), pipeline_mode=once)],
        out_specs=pl.BlockSpec((tq, heads * V_HEAD), lambda g, i: (i, g)),
        out_shape=jax.ShapeDtypeStruct((n, MLA_WIDTH), F32),
        scratch_shapes=[pltpu.VMEM((heads, t, tq), F32)] * 2 + [pltpu.VMEM((heads, 1, tq), F32)] * 4
        + [pltpu.VMEM((heads, V_HEAD, tq), F32)],
        compiler_params=_params("arbitrary", "arbitrary"),
        name="attn",
    )(q, k, vt)


def _attn_cached_body(q_ref, kn_ref, vn_ref, ckv_ref, kpe_ref, wk_ref, wv_ref, o_ref, *, past, seq, nsub):
    heads = range(MLA_HEADS)
    pos = past + lax.broadcasted_iota(jnp.int32, (seq, 1), 0)
    qc = jnp.concatenate([pos] * MLA_HEADS, axis=0) // CHUNK
    kc_past = lax.broadcasted_iota(jnp.int32, (1, past), 1) // CHUNK
    kc_new = (past + lax.broadcasted_iota(jnp.int32, (1, seq), 1)) // CHUNK
    seqs = range(nsub)
    rows = [slice(b * seq, (b + 1) * seq) for b in seqs]
    ckv = [ckv_ref[b].astype(BF16) for b in seqs]
    kpe = [kpe_ref[b].astype(BF16) for b in seqs]
    qa = [jnp.concatenate(
        [_dot_nt(q_ref[r, h * QK_PAD:h * QK_PAD + QK_NOPE], wk_ref[:, h * QK_NOPE:(h + 1) * QK_NOPE])
         for h in heads], axis=0).astype(BF16) for r in rows]
    qpe = [jnp.concatenate(
        [q_ref[r, h * QK_PAD + QK_NOPE:h * QK_PAD + QK_NOPE + QK_ROPE] for h in heads], axis=0) for r in rows]
    s_p = [_dot_nt(qa[b], ckv[b]) + _dot_nt(qpe[b], kpe[b]) for b in seqs]
    s_n = [jnp.concatenate(
        [_dot_nt(q_ref[r, h * QK_PAD:(h + 1) * QK_PAD], kn_ref[r, h * QK_PAD:(h + 1) * QK_PAD])
         for h in heads], axis=0) for r in rows]
    s_p = [jnp.where(kc_past <= qc, s, NEG_INF) for s in s_p]
    s_n = [jnp.where(kc_new <= qc, s, NEG_INF) for s in s_n]
    m = [jnp.maximum(jnp.max(a, -1, keepdims=True), jnp.max(c, -1, keepdims=True)) for a, c in zip(s_p, s_n)]
    e_p = [jnp.exp2(s - mm) for s, mm in zip(s_p, m)]
    e_n = [jnp.exp2(s - mm) for s, mm in zip(s_n, m)]
    l = [jnp.sum(a, -1, keepdims=True) + jnp.sum(c, -1, keepdims=True) for a, c in zip(e_p, e_n)]
    p_n = [(e / ll).astype(BF16) for e, ll in zip(e_n, l)]
    o_lat = [_dot((e / ll).astype(BF16), c).astype(BF16) for e, ll, c in zip(e_p, l, ckv)]
    for b in seqs:
        for h in heads:
            hr = slice(h * seq, (h + 1) * seq)
            cols = slice(h * V_HEAD, (h + 1) * V_HEAD)
            o_ref[rows[b], cols] = _dot(o_lat[b][hr], wv_ref[:, cols]) + _dot(p_n[b][hr], vn_ref[rows[b], cols])


def _attention_cached(q, k, v, ckv_cache, kpe_cache, wk, wv, seq, nsub):
    n = q.shape[0]
    nb, past, _ = ckv_cache.shape
    assert nb % nsub == 0
    row = lambda w: pl.BlockSpec((nsub * seq, w), lambda b: (b, 0))
    kern = functools.partial(_attn_cached_body, past=past, seq=seq, nsub=nsub)
    return pl.pallas_call(
        kern,
        grid=(nb // nsub,),
        in_specs=[row(MLA_HEADS * QK_PAD), row(MLA_HEADS * QK_PAD), row(MLA_WIDTH),
                  pl.BlockSpec((nsub, past, KV_LORA), lambda b: (b, 0, 0)),
                  pl.BlockSpec((nsub, past, QK_ROPE), lambda b: (b, 0, 0)),
                  _const(wk.shape), _const(wv.shape)],
        out_specs=row(MLA_WIDTH),
        out_shape=jax.ShapeDtypeStruct((n, MLA_WIDTH), F32),
        compiler_params=_params("parallel"),
        name="attn_cached",
    )(q, k, v, ckv_cache, kpe_cache, wk, wv)


def _row_blocks(ref):
    rb = ref.shape[0] // MIX_SPLIT
    return [slice(i * rb, (i + 1) * rb) for i in range(MIX_SPLIT)]


def _mix_head(blocks, oss, om_ref, x_ref, gm_ref, wo_ref, lng_ref, lnb_ref, wxq_ref):
    om = [_rms(om_ref[r, :], gm_ref[...]).astype(BF16) for r in blocks]
    a = [_dot(s, wo_ref[:SSM_WIDTH, :]) + _dot(o, wo_ref[SSM_WIDTH:, :]) for s, o in zip(oss, om)]
    h1 = [_ln(ALPHA * x_ref[r, :] + aa, lng_ref[0:1, :], lnb_ref[0:1, :]) for r, aa in zip(blocks, a)]
    return h1, [_dot(h.astype(BF16), wxq_ref[...]).astype(BF16) for h in h1]


def _mix_body(os_ref, om_ref, x_ref, gm_ref, wo_ref, lng_ref, lnb_ref, wxq_ref, h1_ref, qx_ref):
    blocks = _row_blocks(x_ref)
    oss = [os_ref[r, :] for r in blocks]
    h1, qx = _mix_head(blocks, oss, om_ref, x_ref, gm_ref, wo_ref, lng_ref, lnb_ref, wxq_ref)
    for r, h, q in zip(blocks, h1, qx):
        h1_ref[r, :] = h
        qx_ref[r, :] = q


def _mix(o_ssm, o_mla, x2, g_mla, w_o, ln_g, ln_b, w_xq, tn):
    n = x2.shape[0]
    row = lambda w: pl.BlockSpec((tn, w), lambda i: (i, 0))
    return pl.pallas_call(
        _mix_body,
        grid=(n // tn,),
        in_specs=[row(SSM_WIDTH), row(MLA_WIDTH), row(D_MODEL), _const(g_mla.shape), _const(w_o.shape),
                  _const(ln_g.shape), _const(ln_b.shape), _const(w_xq.shape)],
        out_specs=[row(D_MODEL), row(D_MODEL)],
        out_shape=[jax.ShapeDtypeStruct((n, D_MODEL), F32), jax.ShapeDtypeStruct((n, D_MODEL), BF16)],
        compiler_params=_params("parallel"),
        name="mix",
    )(o_ssm, o_mla, x2, g_mla, w_o, ln_g, ln_b, w_xq)


def _mem_attn_heads(qs, mks, mvs):
    ss = [_dot_nt(q, mk.astype(BF16)) * (X_HEAD_DIM ** -0.5) for q, mk in zip(qs, mks)]
    es = [jnp.exp(s - jnp.max(s, -1, keepdims=True)) for s in ss]
    ps = [(e / jnp.sum(e, -1, keepdims=True)).astype(BF16) for e in es]
    return [_dot(p, mv.astype(BF16)).astype(BF16) for p, mv in zip(ps, mvs)]


def _mix_mem_body(y_ref, u_ref, d_ref, wglu_ref, gs_ref, om_ref, x_ref, gm_ref, wo_ref, lng_ref, lnb_ref, wxq_ref,
                  mk_ref, mv_ref, wxo_ref, h2_ref):
    blocks = _row_blocks(x_ref)
    oss = _glu_rows([y_ref[r, :] for r in blocks], [u_ref[r, :] for r in blocks], d_ref, wglu_ref, gs_ref)
    h1, qx = _mix_head(blocks, oss, om_ref, x_ref, gm_ref, wo_ref, lng_ref, lnb_ref, wxq_ref)
    cols = [slice(h * X_HEAD_DIM, (h + 1) * X_HEAD_DIM) for h in range(X_HEADS)]
    heads = _mem_attn_heads([q[:, c] for q in qx for c in cols], [mk_ref[:, c] for _ in qx for c in cols],
                            [mv_ref[:, c] for _ in qx for c in cols])
    ox = [jnp.concatenate(heads[i * X_HEADS:(i + 1) * X_HEADS], axis=1) for i in range(len(blocks))]
    att = [_dot(o, wxo_ref[...]) for o in ox]
    for r, h, a in zip(blocks, h1, att):
        h2_ref[r, :] = _ln(ALPHA * h + a, lng_ref[1:2, :], lnb_ref[1:2, :])


def _mix_mem(y_ssm, u, d_skip, w_glu, g_ssm, o_mla, x2, g_mla, w_o, ln_g, ln_b, w_xq, mem_k, mem_v, w_xo, tn):
    n = x2.shape[0]
    row = lambda w: pl.BlockSpec((tn, w), lambda i: (i, 0))
    glu_consts = (d_skip, w_glu, g_ssm)
    consts = (g_mla, w_o, ln_g, ln_b, w_xq, mem_k, mem_v, w_xo)
    return pl.pallas_call(
        _mix_mem_body,
        grid=(n // tn,),
        in_specs=[row(SSM_WIDTH), row(SSM_WIDTH)] + [_const(c.shape) for c in glu_consts]
        + [row(MLA_WIDTH), row(D_MODEL)] + [_const(c.shape) for c in consts],
        out_specs=row(D_MODEL),
        out_shape=jax.ShapeDtypeStruct((n, D_MODEL), F32),
        compiler_params=_params("parallel"),
        name="mix_mem",
    )(y_ssm, u, *glu_consts, o_mla, x2, *consts)


def _mem_attn_cache_body(qx_ref, mk_hbm, mv_hbm, o_ref, kbuf, vbuf, sems, *, seq, nsub):
    b = pl.program_id(0)
    nb = pl.num_programs(0)

    def copies(step, slot):
        return [pltpu.make_async_copy(src.at[step * nsub + j, :, h, :], buf.at[slot, j, h], sems.at[slot, j, t, h])
                for t, (src, buf) in enumerate(((mk_hbm, kbuf), (mv_hbm, vbuf)))
                for j in range(nsub) for h in range(X_HEADS)]

    @pl.when(b == 0)
    def _():
        for c in copies(0, 0):
            c.start()

    @pl.when(b + 1 < nb)
    def _():
        for c in copies(b + 1, (b + 1) % 2):
            c.start()

    slot = b % 2
    for c in copies(b, slot):
        c.wait()
    jh = [(j, h) for j in range(nsub) for h in range(X_HEADS)]
    where = [(slice(j * seq, (j + 1) * seq), slice(h * X_HEAD_DIM, (h + 1) * X_HEAD_DIM)) for j, h in jh]
    outs = _mem_attn_heads([qx_ref[r, c] for r, c in where], [kbuf[slot, j, h] for j, h in jh],
                           [vbuf[slot, j, h] for j, h in jh])
    for (r, c), o in zip(where, outs):
        o_ref[r, c] = o


def _mem_attn_cache(qx, mem_k, mem_v, seq, nsub):
    n = qx.shape[0]
    nb = mem_k.shape[0]
    assert nb % nsub == 0
    row = pl.BlockSpec((nsub * seq, D_MODEL), lambda b: (b, 0))
    hbm = pl.BlockSpec(memory_space=pl.ANY)
    buf = pltpu.VMEM((2, nsub, X_HEADS, N_MEM, X_HEAD_DIM), F32)
    return pl.pallas_call(
        functools.partial(_mem_attn_cache_body, seq=seq, nsub=nsub),
        grid=(nb // nsub,),
        in_specs=[row, hbm, hbm],
        out_specs=row,
        out_shape=jax.ShapeDtypeStruct((n, D_MODEL), BF16),
        scratch_shapes=[buf, buf, pltpu.SemaphoreType.DMA((2, nsub, 2, X_HEADS))],
        compiler_params=_params("arbitrary"),
        name="mem_attn_cache",
    )(qx, mem_k, mem_v)


def _mlp_body(*refs, ff_blk, second_norm):
    if second_norm:
        h1_ref, ox_ref, wxo_ref, lng_ref, lnb_ref, w1_ref, w2_ref, y_ref = refs
        h2 = _ln(ALPHA * h1_ref[...] + _dot(ox_ref[...], wxo_ref[...]), lng_ref[1:2, :], lnb_ref[1:2, :])
    else:
        h2_ref, lng_ref, lnb_ref, w1_ref, w2_ref, y_ref = refs
        h2 = h2_ref[...]
    hb = h2.astype(BF16)
    acc = jnp.zeros(h2.shape, F32)
    for c in range(0, D_FF, ff_blk):
        z = jnp.maximum(_dot(hb, w1_ref[:, c:c + ff_blk]), 0.0)
        acc += _dot((z * z).astype(BF16), w2_ref[c:c + ff_blk, :])
    y_ref[...] = _ln(ALPHA * h2 + acc, lng_ref[2:3, :], lnb_ref[2:3, :])


def _mlp(h, ox, w_xo, ln_g, ln_b, w1, w2, tn):
    n = h.shape[0]
    row = pl.BlockSpec((tn, D_MODEL), lambda i: (i, 0))
    second_norm = ox is not None
    rows = (h, ox) if second_norm else (h,)
    consts = ((w_xo,) if second_norm else ()) + (ln_g, ln_b, w1, w2)
    return pl.pallas_call(
        functools.partial(_mlp_body, ff_blk=1024, second_norm=second_norm),
        grid=(n // tn,),
        in_specs=[row] * len(rows) + [_const(c.shape) for c in consts],
        out_specs=row,
        out_shape=jax.ShapeDtypeStruct((n, D_MODEL), F32),
        compiler_params=_params("parallel"),
        name="mlp",
    )(*rows, *consts)


def _mem_kv_body(mem_ref, wk_ref, wv_ref, k_ref, v_ref):
    m = mem_ref[...].astype(BF16)
    k_ref[...] = _dot(m, wk_ref[...])
    v_ref[...] = _dot(m, wv_ref[...])


def _mem_kv(mem2, wk, wv):
    n = mem2.shape[0]
    out = jax.ShapeDtypeStruct((n, D_MODEL), F32)
    return pl.pallas_call(
        _mem_kv_body,
        grid=(1,),
        in_specs=[_const(mem2.shape), _const(wk.shape), _const(wv.shape)],
        out_specs=[_const((n, D_MODEL))] * 2,
        out_shape=[out, out],
        compiler_params=_params("arbitrary"),
        name="mem_kv",
    )(mem2, wk, wv)


def _state_to_cols(s):
    return jnp.transpose(s.reshape(s.shape[0], S5_COLS, -1), (1, 0, 2))


def _state_from_cols(s):
    return jnp.transpose(s, (1, 0, 2)).reshape(s.shape[1], SSM_GROUPS, SSM_STATE)


def _layer(x2, pos, nseq, h0r, h0i, mem_k, mem_v, caches, wts, s5_ops, tn, scan_ct):
    n = x2.shape[0]
    seq = n // nseq
    prompt = caches is None
    u, q, k, v, ckv, kpe = _project(x2, pos, wts["inv"], wts["w_in"], wts["g_q"], wts["w_q"], wts["g_kv"],
                                    wts["w_k"], wts["w_vt"] if prompt else wts["w_v"], tn,
                                    v_transposed=prompt, consecutive=prompt)
    y_ssm, fr, fi = _s5_scan(u, s5_ops, _state_to_cols(h0r), _state_to_cols(h0i), nseq, scan_ct)
    glu_args = (y_ssm, u, wts["d_skip"], wts["w_glu"], wts["g_out_ssm"])
    if prompt:
        o_mla = _attention(q, k, v, ATTN_HEADS)
    else:
        o_mla = _attention_cached(q, k, v, caches[0], caches[1], wts["w_k"], wts["w_v"], seq, nsub=2)
    mix_args = (o_mla, x2, wts["g_out_mla"], wts["w_o"], wts["ln_g"], wts["ln_b"], wts["w_xq"])
    if prompt:
        h, ox = _mix_mem(*glu_args, *mix_args, mem_k, mem_v, wts["w_xo"], tn), None
    else:
        h, qx = _mix(_glu(*glu_args, tn), *mix_args, tn)
        ox = _mem_attn_cache(qx, mem_k, mem_v, seq, nsub=2)
    y = _mlp(h, ox, wts["w_xo"], wts["ln_g"], wts["ln_b"], wts["w_ff1"], wts["w_ff2"], tn)
    return y, ckv, kpe, _state_from_cols(fr), _state_from_cols(fi)


def kernel(x_prompt, x_sample, mem_prompt, cache_mla_ckv, cache_mla_kpe, state_ssm_re, state_ssm_im, cache_mem_k, cache_mem_v, w_in, g_q, w_q_up, g_kv, w_kv_up, a_re, a_im, b_re, b_im, c_re, c_im, d_skip, log_dt, w_glu, g_out_ssm, g_out_mla, w_o, w_xq, w_xk, w_xv, w_xo, w_ff1, w_ff2, ln_g, ln_b):
    assert w_in.shape[0] == DEPTH == 1
    nbp, sp, _ = x_prompt.shape
    nbs, sd, _ = x_sample.shape
    past = cache_mla_ckv.shape[2]
    assert nbp == 1

    wq = jnp.pad(w_q_up[0], ((0, 0), (0, 0), (0, QK_PAD - QK_NOPE - QK_ROPE)))
    wk = w_kv_up[0][:, :, :QK_NOPE].reshape(KV_LORA, -1).astype(BF16)
    wv = w_kv_up[0][:, :, QK_NOPE:].reshape(KV_LORA, -1).astype(BF16)
    inv = ROPE_THETA ** (-jnp.arange(ROPE_HALF, dtype=F32) / ROPE_HALF)
    wts = {
        "inv": jnp.tile(inv, LANES // ROPE_HALF).reshape(1, LANES),
        "w_in": jnp.pad(w_in[0], ((0, 0), (0, LANES - QK_ROPE))).astype(BF16),
        "g_q": g_q[0].reshape(1, -1),
        "w_q": wq.reshape(Q_LORA, MLA_HEADS * QK_PAD).astype(BF16),
        "g_kv": g_kv[0].reshape(1, -1),
        "w_k": wk,
        "w_v": wv,
        "w_vt": wv.T,
        "d_skip": d_skip[0].reshape(1, -1),
        "w_glu": w_glu[0].astype(BF16),
        "g_out_ssm": g_out_ssm[0].reshape(1, -1),
        "g_out_mla": g_out_mla[0].reshape(1, -1),
        "w_o": w_o[0].astype(BF16),
        "w_xq": w_xq[0].reshape(D_MODEL, D_MODEL).astype(BF16),
        "w_xo": w_xo[0].reshape(D_MODEL, D_MODEL).astype(BF16),
        "w_ff1": w_ff1[0].astype(BF16),
        "w_ff2": w_ff2[0].astype(BF16),
        "ln_g": ln_g[0],
        "ln_b": ln_b[0],
    }
    s5_ops = _s5_prep(a_re[0], a_im[0], b_re[0], b_im[0], c_re[0], c_im[0], log_dt[0])

    mk, mv = _mem_kv(mem_prompt.reshape(nbp * N_MEM, D_MODEL),
                     w_xk[0].reshape(D_MODEL, D_MODEL).astype(BF16),
                     w_xv[0].reshape(D_MODEL, D_MODEL).astype(BF16))
    zero = jnp.zeros((nbp, SSM_GROUPS, SSM_STATE), F32)
    pos_p = jnp.arange(sp, dtype=F32).reshape(sp, 1)
    yp, ckv_p, kpe_p, sre_p, sim_p = _layer(
        x_prompt.reshape(sp, D_MODEL), pos_p, nbp, zero, zero, mk, mv, None,
        wts, s5_ops, tn=512, scan_ct=512)

    pos_s = jnp.tile(past + jnp.arange(sd, dtype=F32), nbs).reshape(nbs * sd, 1)
    caches = (cache_mla_ckv[0], cache_mla_kpe[0])
    ys, ckv_s, kpe_s, sre_s, sim_s = _layer(
        x_sample.reshape(nbs * sd, D_MODEL), pos_s, nbs,
        state_ssm_re[0], state_ssm_im[0], cache_mem_k[0], cache_mem_v[0], caches,
        wts, s5_ops, tn=512, scan_ct=sd // S5_T)

    return (yp.reshape(nbp, sp, D_MODEL), ys.reshape(nbs, sd, D_MODEL),
            ckv_p.reshape(1, nbp, sp, KV_LORA), kpe_p.reshape(1, nbp, sp, QK_ROPE),
            sre_p.reshape(1, nbp, SSM_GROUPS, SSM_STATE), sim_p.reshape(1, nbp, SSM_GROUPS, SSM_STATE),
            mk.reshape(1, nbp, N_MEM, X_HEADS, X_HEAD_DIM), mv.reshape(1, nbp, N_MEM, X_HEADS, X_HEAD_DIM),
            ckv_s.reshape(1, nbs, sd, KV_LORA), kpe_s.reshape(1, nbs, sd, QK_ROPE),
            sre_s.reshape(1, nbs, SSM_GROUPS, SSM_STATE), sim_s.reshape(1, nbs, SSM_GROUPS, SSM_STATE))
```

```python
import functools
import math

import jax
import jax.numpy as jnp
from jax import lax
from jax.experimental import pallas as pl
from jax.experimental.pallas import tpu as pltpu

F32 = jnp.float32
BF16 = jnp.bfloat16

D_MODEL = 1024
DEPTH = 1
CHUNK = 64
SSM_WIDTH = 512
SSM_GROUP = 16
SSM_GROUPS = 32
SSM_STATE = 64
MLA_HEADS = 4
QK_NOPE = 128
QK_ROPE = 64
V_HEAD = 128
MLA_WIDTH = MLA_HEADS * V_HEAD
Q_LORA = 384
KV_LORA = 256
ROPE_THETA = 10000.0
MLA_SCALE = (QK_NOPE + QK_ROPE) ** -0.5
N_MEM = 256
X_HEADS = 4
X_HEAD_DIM = D_MODEL // X_HEADS
D_FF = 4 * D_MODEL
ALPHA = (2 * DEPTH) ** 0.25
EPS = 1e-5
NEG_INF = -1e30

LANES = 128
QK_PAD = 256
S5_T = 16
S5_COL_GROUPS = LANES // SSM_GROUP
S5_COLS = SSM_GROUPS // S5_COL_GROUPS
S5_SW = S5_COL_GROUPS * SSM_STATE
ROPE_HALF = QK_ROPE // 2
Q_SCALE = MLA_SCALE * math.log2(math.e)
ATTN_T = 512
ATTN_HEADS = 2
ATTN_TQ = 1024
ATTN_QSPLIT = 4
CACHE_SEQS = 4
MIX_TILE = 1024
MIX_SPLIT = 2
PROJ_TILE = 1024
PROJ_ROWS = 512
VMEM_LIMIT = 56 * 1024 * 1024

_NT = (((1,), (1,)), ((), ()))


def _rms(x, g):
    return x * lax.rsqrt(jnp.mean(x * x, -1, keepdims=True) + EPS) * g


def _ln(x, g, b):
    mu = jnp.mean(x, -1, keepdims=True)
    xc = x - mu
    var = jnp.mean(xc * xc, -1, keepdims=True)
    return xc * lax.rsqrt(var + EPS) * g + b


def _dot(a, b):
    return jnp.dot(a, b, preferred_element_type=F32)


def _dot_nt(a, b):
    return lax.dot_general(a, b, _NT, preferred_element_type=F32)


def _split(a):
    hi = a.astype(BF16)
    return hi, (a - hi.astype(F32)).astype(BF16)


def _dot_split(a, b):
    return _dot(a[0], b[0]) + _dot(a[0], b[1]) + _dot(a[1], b[0])


def _params(*sem):
    return pltpu.CompilerParams(dimension_semantics=sem, vmem_limit_bytes=VMEM_LIMIT)


def _const(shape):
    n = len(shape)
    return pl.BlockSpec(shape, lambda *_: (0,) * n)


def _proj_body(x_ref, pos_ref, inv_ref, w_in_ref, gq_ref, wq_ref, gkv_ref, wk_ref, wv_ref,
               u_ref, q_ref, k_ref, v_ref, ckv_ref, kpe_ref, tc_ref, ts_ref, *, v_transposed, consecutive):
    rb = PROJ_ROWS
    assert x_ref.shape[0] % rb == 0 and ATTN_T % rb == 0
    blocks = [slice(r0, r0 + rb) for r0 in range(0, x_ref.shape[0], rb)]
    c0 = SSM_WIDTH + Q_LORA
    projs = [_dot(x_ref[r, :].astype(BF16), w_in_ref[...]) for r in blocks]
    cqs = [_rms(p[:, SSM_WIDTH:c0], gq_ref[...]).astype(BF16) for p in projs]
    ckvs = [_rms(p[:, c0:c0 + KV_LORA], gkv_ref[...]) for p in projs]
    qs = [_dot(c, wq_ref[...]) * Q_SCALE for c in cqs]
    kns = [_dot(c.astype(BF16), wk_ref[...]) for c in ckvs]
    for r, p, ckv in zip(blocks, projs, ckvs):
        u_ref[r, :] = p[:, :SSM_WIDTH]
        ckv_ref[r, :] = ckv
        if v_transposed:
            lanes = slice(r.start % ATTN_T, r.start % ATTN_T + rb)
            v_ref[r.start // ATTN_T, :, lanes] = _dot_nt(wv_ref[...], ckv.astype(BF16)).astype(BF16)
        else:
            v_ref[r, :] = _dot(ckv.astype(BF16), wv_ref[...]).astype(BF16)

    lane = lax.broadcasted_iota(jnp.int32, (1, LANES), 1)
    live = lane < QK_ROPE
    sign = jnp.where(lane < ROPE_HALF, -1.0, 1.0)
    inv = inv_ref[...]
    if consecutive:
        @pl.when(pl.program_id(0) == 0)
        def _():
            r = lax.broadcasted_iota(jnp.int32, (x_ref.shape[0], 1), 0).astype(F32)
            tc_ref[...] = jnp.where(live, jnp.cos(r * inv), 0.0)
            ts_ref[...] = jnp.where(live, jnp.sin(r * inv), 0.0)

        base = pos_ref[0:1, :] * inv
        ca, sa = jnp.cos(base), jnp.sin(base)
        cos_ts = [ca * tc_ref[r, :] - sa * ts_ref[r, :] for r in blocks]
        sin_ts = [(sa * sign) * tc_ref[r, :] + (ca * sign) * ts_ref[r, :] for r in blocks]
    else:
        angs = [pos_ref[r, :] * inv for r in blocks]
        cos_ts = [jnp.where(live, jnp.cos(a), 0.0) for a in angs]
        sin_ts = [jnp.where(live, jnp.sin(a) * sign, 0.0) for a in angs]

    def rope(c2, cos_t, sin_t):
        swapped = jnp.where(lane < ROPE_HALF, pltpu.roll(c2, LANES - ROPE_HALF, 1), pltpu.roll(c2, ROPE_HALF, 1))
        return c2 * cos_t + swapped * sin_t

    for r, p, q, kn, cos_t, sin_t in zip(blocks, projs, qs, kns, cos_ts, sin_ts):
        kpe = rope(p[:, c0 + KV_LORA:], cos_t, sin_t)
        kpe_ref[r, :] = kpe[:, :QK_ROPE]
        kpe_b = kpe.astype(BF16)
        for h in range(MLA_HEADS):
            a = h * QK_PAD
            q_ref[r, a:a + QK_NOPE] = q[:, a:a + QK_NOPE].astype(BF16)
            q_ref[r, a + QK_NOPE:a + QK_PAD] = rope(q[:, a + QK_NOPE:a + QK_PAD], cos_t, sin_t).astype(BF16)
            k_ref[r, a:a + QK_NOPE] = kn[:, h * QK_NOPE:(h + 1) * QK_NOPE].astype(BF16)
            k_ref[r, a + QK_NOPE:a + QK_PAD] = kpe_b


def _project(x2, pos, inv, w_in, gq, wq, gkv, wk, wv, tn, v_transposed, consecutive, kpe_cache_shape):
    n = x2.shape[0]
    row = lambda w: pl.BlockSpec((tn, w), lambda i: (i, 0))
    if kpe_cache_shape:
        kpe_spec = pl.BlockSpec((None, None, tn, QK_ROPE), lambda i: (0, 0, i, 0))
        kpe_shape = jax.ShapeDtypeStruct((1, 1, n, QK_ROPE), F32)
    else:
        kpe_spec, kpe_shape = row(QK_ROPE), jax.ShapeDtypeStruct((n, QK_ROPE), F32)
    if v_transposed:
        assert tn % ATTN_T == 0
        v_spec = pl.BlockSpec((tn // ATTN_T, MLA_WIDTH, ATTN_T), lambda i: (i, 0, 0))
        v_shape = jax.ShapeDtypeStruct((n // ATTN_T, MLA_WIDTH, ATTN_T), BF16)
    else:
        v_spec, v_shape = row(MLA_WIDTH), jax.ShapeDtypeStruct((n, MLA_WIDTH), BF16)
    return pl.pallas_call(
        functools.partial(_proj_body, v_transposed=v_transposed, consecutive=consecutive),
        grid=(n // tn,),
        scratch_shapes=[pltpu.VMEM((tn, LANES), F32)] * 2,
        in_specs=[row(D_MODEL), row(1), _const(inv.shape), _const(w_in.shape), _const(gq.shape),
                  _const(wq.shape), _const(gkv.shape), _const(wk.shape), _const(wv.shape)],
        out_specs=[row(SSM_WIDTH), row(MLA_HEADS * QK_PAD), row(MLA_HEADS * QK_PAD),
                   v_spec, row(KV_LORA), kpe_spec],
        out_shape=[jax.ShapeDtypeStruct((n, SSM_WIDTH), F32),
                   jax.ShapeDtypeStruct((n, MLA_HEADS * QK_PAD), BF16),
                   jax.ShapeDtypeStruct((n, MLA_HEADS * QK_PAD), BF16),
                   v_shape,
                   jax.ShapeDtypeStruct((n, KV_LORA), F32),
                   kpe_shape],
        compiler_params=_params("arbitrary"),
        name="proj",
    )(x2, pos, inv, w_in, gq, wq, gkv, wk, wv)


def _s5_prep_body(ar_row, ai_row, ldt_row, bt_re, bt_im, ct_re, ct_im,
                  wt_ref, v_ref, z_ref, are_ref, aim_ref):
    sw = S5_SW
    arr, air, dtr = ar_row[0], ai_row[0], jnp.exp(ldt_row[0])

    mag = jnp.exp(arr * dtr)
    lr, li = mag * jnp.cos(air * dtr), mag * jnp.sin(air * dtr)
    nr, ni = lr - 1.0, li
    den = arr * arr + air * air
    f_re, f_im = (nr * arr + ni * air) / den, (ni * arr - nr * air) / den

    same_b = (lax.broadcasted_iota(jnp.int32, (LANES, sw), 0) // SSM_GROUP
              == lax.broadcasted_iota(jnp.int32, (LANES, sw), 1) // SSM_STATE)
    br = jnp.where(same_b, bt_re[0], 0.0)
    bi = jnp.where(same_b, bt_im[0], 0.0)
    bb_re = f_re * br - f_im * bi
    bb_im = f_re * bi + f_im * br
    same_c = (lax.broadcasted_iota(jnp.int32, (sw, LANES), 0) // SSM_STATE
              == lax.broadcasted_iota(jnp.int32, (sw, LANES), 1) // SSM_GROUP)
    cr = jnp.where(same_c, ct_re[0], 0.0)
    ci = jnp.where(same_c, ct_im[0], 0.0)
    cr_s, ci_s = _split(cr), _split(ci)

    e = lax.broadcasted_iota(jnp.int32, (2 * S5_T, 1), 0).astype(F32)
    pm = jnp.exp(arr * dtr * e)
    pw_re, pw_im = pm * jnp.cos(air * dtr * e), pm * jnp.sin(air * dtr * e)
    pt_re, pt_im = pw_re.T, pw_im.T

    lag_ops = []
    for lag in range(S5_T):
        p_re, p_im = pw_re[lag:lag + 1, :], pw_im[lag:lag + 1, :]
        k_re = p_re * bb_re - p_im * bb_im
        k_im = p_re * bb_im + p_im * bb_re
        i = S5_T - 1 - lag
        v_ref[0, i * LANES:(i + 1) * LANES, :sw] = k_re.astype(BF16)
        v_ref[0, i * LANES:(i + 1) * LANES, sw:] = k_im.astype(BF16)
        lag_ops.append(_dot_split(_split(k_re), cr_s) - _dot_split(_split(k_im), ci_s))

        q_re, q_im = pt_re[:, lag + 1:lag + 2], pt_im[:, lag + 1:lag + 2]
        z_ref[0, :sw, lag * LANES:(lag + 1) * LANES] = (cr * q_re - ci * q_im).astype(BF16)
        z_ref[0, sw:, lag * LANES:(lag + 1) * LANES] = (-(cr * q_im + ci * q_re)).astype(BF16)

    zero = jnp.zeros((LANES, LANES), BF16)
    for d in range(S5_T // 2):
        wt_ref[0, d, :LANES, :LANES] = lag_ops[2 * d].astype(BF16)
        wt_ref[0, d, :LANES, LANES:] = lag_ops[2 * d + 1].astype(BF16)
        wt_ref[0, d, LANES:, :LANES] = lag_ops[2 * d - 1].astype(BF16) if d else zero
        wt_ref[0, d, LANES:, LANES:] = lag_ops[2 * d].astype(BF16)

    are_ref[0] = pw_re[S5_T:S5_T + 1, :]
    aim_ref[0] = pw_im[S5_T:S5_T + 1, :]


def _s5_prep(a_re, a_im, b_re, b_im, c_re, c_im, log_dt):
    nc, r, t, h, p, sw = S5_COLS, S5_COL_GROUPS, S5_T, SSM_GROUP, SSM_STATE, S5_SW
    ldt = jnp.repeat(log_dt, p)
    bt = lambda b: jnp.tile(jnp.transpose(b.reshape(nc, r, p, h), (0, 1, 3, 2)).reshape(nc, r * h, p), (1, 1, r))
    ct = lambda c: jnp.tile(jnp.transpose(c.reshape(nc, r, h, p), (0, 1, 3, 2)).reshape(nc, r * p, h), (1, 1, r))
    args = (a_re.reshape(nc, 1, sw), a_im.reshape(nc, 1, sw), ldt.reshape(nc, 1, sw),
            bt(b_re), bt(b_im), ct(c_re), ct(c_im))
    blk = lambda s: pl.BlockSpec((1,) + s[1:], lambda i: (i,) + (0,) * (len(s) - 1))
    outs = [((nc, t // 2, 2 * LANES, 2 * LANES), BF16), ((nc, t * LANES, 2 * sw), BF16),
            ((nc, 2 * sw, t * LANES), BF16), ((nc, 1, sw), F32), ((nc, 1, sw), F32)]
    return pl.pallas_call(
        _s5_prep_body,
        grid=(nc,),
        in_specs=[blk(a.shape) for a in args],
        out_specs=[blk(s) for s, _ in outs],
        out_shape=[jax.ShapeDtypeStruct(s, d) for s, d in outs],
        compiler_params=_params("parallel"),
        name="s5_prep",
    )(*args)


def _s5_scan_body(u_ref, wt_ref, v_ref, z_ref, are_ref, aim_ref, h0r_ref, h0i_ref,
                  y_ref, fr_ref, fi_ref, lhs, sre, sim, xre, xim, cr, ci, *, nseq, ct):
    t = pl.program_id(1)
    sw = S5_SW
    seq = ct * S5_T

    @pl.when(t == 0)
    def _():
        cr[...] = h0r_ref[0]
        ci[...] = h0i_ref[0]

    def token_rows(cc, i):
        if nseq == 1:
            return pl.ds(i, ct, stride=S5_T), slice(None)
        return pl.ds(cc * S5_T + i, nseq, stride=seq), slice(cc * nseq, (cc + 1) * nseq)

    for cc in range(1 if nseq == 1 else ct):
        for i in range(S5_T):
            tok, crow = token_rows(cc, i)
            lhs[crow, i * LANES:(i + 1) * LANES] = u_ref[tok, :].astype(BF16)

    s = _dot(lhs[...], v_ref[0])
    sre[...] = s[:, :sw]
    sim[...] = s[:, sw:]

    a_re, a_im = are_ref[0], aim_ref[0]

    def step(c, carry):
        x_re, x_im = carry
        r = pl.ds(c * nseq, nseq)
        xre[r, :] = x_re
        xim[r, :] = x_im
        n_re = a_re * x_re - a_im * x_im + sre[r, :]
        n_im = a_re * x_im + a_im * x_re + sim[r, :]
        return n_re, n_im

    x_re, x_im = lax.fori_loop(0, ct, step, (cr[...], ci[...]))
    cr[...] = x_re
    ci[...] = x_im
    fr_ref[0] = x_re
    fi_ref[0] = x_im

    xb_re, xb_im = xre[...].astype(BF16), xim[...].astype(BF16)
    for jp in range(S5_T // 2):
        cols = slice(jp * 2 * LANES, (jp + 1) * 2 * LANES)
        acc = _dot(xb_re, z_ref[0, :sw, cols]) + _dot(xb_im, z_ref[0, sw:, cols])
        for ip in range(jp + 1):
            acc += _dot(lhs[:, ip * 2 * LANES:(ip + 1) * 2 * LANES], wt_ref[0, jp - ip])
        for cc in range(1 if nseq == 1 else ct):
            for jj in range(2):
                tok, crow = token_rows(cc, 2 * jp + jj)
                y_ref[tok, :] = acc[crow, jj * LANES:(jj + 1) * LANES]


def _s5_scan(u, col_ops, h0r, h0i, nseq, ct):
    wt, v, z, are, aim = col_ops
    n = u.shape[0]
    cps = n // nseq // S5_T
    assert nseq == 1 or ct == cps
    nc = nseq * ct
    sw = S5_SW
    tile = pl.BlockSpec((nc * S5_T, LANES), lambda c, t: (t, c))
    op = lambda a: pl.BlockSpec((1,) + a.shape[1:], lambda c, t: (c,) + (0,) * (a.ndim - 1))
    st = pl.BlockSpec((1, nseq, sw), lambda c, t: (c, 0, 0))
    st_shape = jax.ShapeDtypeStruct((S5_COLS, nseq, sw), F32)
    return pl.pallas_call(
        functools.partial(_s5_scan_body, nseq=nseq, ct=ct),
        grid=(S5_COLS, cps // ct),
        in_specs=[tile, op(wt), op(v), op(z), op(are), op(aim), st, st],
        out_specs=[tile, st, st],
        out_shape=[jax.ShapeDtypeStruct(u.shape, F32), st_shape, st_shape],
        scratch_shapes=[pltpu.VMEM((nc, S5_T * LANES), BF16)] + [pltpu.VMEM((nc, sw), F32)] * 4
        + [pltpu.VMEM((nseq, sw), F32)] * 2,
        compiler_params=_params("arbitrary", "arbitrary"),
        name="s5_scan",
    )(u, wt, v, z, are, aim, h0r, h0i)


def _glu_rows(ys, us, d_ref, w_ref, g_ref):
    acts = [jax.nn.gelu(y + d_ref[...] * u).astype(BF16) for y, u in zip(ys, us)]
    gls = [_dot(a, w_ref[...]) for a in acts]
    os = [gl[:, :SSM_WIDTH] * jax.nn.sigmoid(gl[:, SSM_WIDTH:]) for gl in gls]
    return [_rms(o, g_ref[...]).astype(BF16) for o in os]


def _glu_body(y_ref, u_ref, d_ref, w_ref, g_ref, o_ref):
    o_ref[...] = _glu_rows([y_ref[...]], [u_ref[...]], d_ref, w_ref, g_ref)[0]


def _glu(y, u, d, w, g, tn):
    n = y.shape[0]
    row = pl.BlockSpec((tn, SSM_WIDTH), lambda i: (i, 0))
    return pl.pallas_call(
        _glu_body,
        grid=(n // tn,),
        in_specs=[row, row, _const(d.shape), _const(w.shape), _const(g.shape)],
        out_specs=row,
        out_shape=jax.ShapeDtypeStruct((n, SSM_WIDTH), BF16),
        compiler_params=_params("parallel"),
        name="glu",
    )(y, u, d, w, g)


def _attn_body(q_ref, k_ref, vt_ref, o_ref, s0, s1, mt0, mt1, m_sc, l_sc, acc_sc, *, heads):
    i = pl.program_id(1)
    t = ATTN_T
    m_sc[...] = jnp.full(m_sc.shape, -jnp.inf, F32)
    l_sc[...] = jnp.zeros(l_sc.shape, F32)
    acc_sc[...] = jnp.zeros(acc_sc.shape, F32)

    qb = q_ref.shape[0] // ATTN_QSPLIT
    units = [(h, c * qb) for c in range(ATTN_QSPLIT) for h in range(heads)]

    def scores(tile, s_buf, mt_buf, unit, diag=None):
        h, c0 = unit
        r = pl.ds(pl.multiple_of(tile * t, t), t)
        s = _dot_nt(k_ref[r, h * QK_PAD:(h + 1) * QK_PAD],
                    q_ref[c0:c0 + qb, h * QK_PAD:(h + 1) * QK_PAD])
        if diag is not None:
            kc = (diag * t + lax.broadcasted_iota(jnp.int32, s.shape, 0)) // CHUNK
            qc = (c0 + lax.broadcasted_iota(jnp.int32, s.shape, 1)) // CHUNK
            s = jnp.where(kc <= qc, s, NEG_INF)
        s_buf[h, :, c0:c0 + qb] = s
        mt_buf[h, :, c0:c0 + qb] = jnp.max(s, 0, keepdims=True)

    def absorb(tile, s_buf, mt_buf, unit):
        h, c0 = unit
        cols = slice(c0, c0 + qb)
        m_prev = m_sc[h, :, cols]
        m_new = jnp.maximum(m_prev, mt_buf[h, :, cols])
        p = jnp.exp2(s_buf[h, :, cols] - m_new)
        alpha = jnp.exp2(m_prev - m_new)
        l_sc[h, :, cols] = alpha * l_sc[h, :, cols] + jnp.sum(p, 0, keepdims=True)
        m_sc[h, :, cols] = m_new
        pv = _dot(vt_ref[tile, h * V_HEAD:(h + 1) * V_HEAD, :], p.astype(BF16))
        acc_sc[h, :, cols] = alpha * acc_sc[h, :, cols] + pv

    nd = q_ref.shape[0] // t
    bufs = ((s0, mt0), (s1, mt1))
    for u in units:
        scores(nd * i, *bufs[0], u, diag=0)
    for d in range(1, nd):
        for u in units:
            scores(nd * i + d, *bufs[d % 2], u, diag=d)
            absorb(nd * i + d - 1, *bufs[(d - 1) % 2], u)

    def pair(jj, c):
        prev = jnp.where(jj == 0, nd * i + nd - 1, 2 * jj - 1)
        for u in units:
            scores(2 * jj, s0, mt0, u)
            absorb(prev, s1, mt1, u)
        for u in units:
            scores(2 * jj + 1, s1, mt1, u)
            absorb(2 * jj, s0, mt0, u)
        return c

    lax.fori_loop(0, (nd // 2) * i, pair, 0)
    last = jnp.where(i == 0, nd - 1, nd * i - 1)
    for u in units:
        absorb(last, s1, mt1, u)

    for h in range(heads):
        o_ref[:, h * V_HEAD:(h + 1) * V_HEAD] = (acc_sc[h] / l_sc[h]).T


def _attention(q, k, vt, heads):
    n = q.shape[0]
    t = ATTN_T
    tq = ATTN_TQ
    assert t % CHUNK == 0 and n % tq == 0 and MLA_HEADS % heads == 0 and (tq // t) % 2 == 0
    once = pl.Buffered(1)
    return pl.pallas_call(
        functools.partial(_attn_body, heads=heads),
        grid=(MLA_HEADS // heads, n // tq),
        in_specs=[pl.BlockSpec((tq, heads * QK_PAD), lambda g, i: (i, g)),
                  pl.BlockSpec((n, heads * QK_PAD), lambda g, i: (0, g), pipeline_mode=once),
                  pl.BlockSpec((n // t, heads * V_HEAD, t), lambda g, i: (0, g, 0), pipeline_mode=once)],
        out_specs=pl.BlockSpec((tq, heads * V_HEAD), lambda g, i: (i, g)),
        out_shape=jax.ShapeDtypeStruct((n, MLA_WIDTH), F32),
        scratch_shapes=[pltpu.VMEM((heads, t, tq), F32)] * 2 + [pltpu.VMEM((heads, 1, tq), F32)] * 4
        + [pltpu.VMEM((heads, V_HEAD, tq), F32)],
        compiler_params=_params("arbitrary", "arbitrary"),
        name="attn",
    )(q, k, vt)


def _attn_cached_body(q_ref, kn_ref, vn_ref, ckv_ref, kpe_ref, wk_ref, wv_ref, o_ref, *, past, seq, nsub):
    heads = range(MLA_HEADS)
    pos = past + lax.broadcasted_iota(jnp.int32, (seq, 1), 0)
    qc = jnp.concatenate([pos] * MLA_HEADS, axis=0) // CHUNK
    kc_past = lax.broadcasted_iota(jnp.int32, (1, past), 1) // CHUNK
    kc_new = (past + lax.broadcasted_iota(jnp.int32, (1, seq), 1)) // CHUNK
    seqs = range(nsub)
    rows = [slice(b * seq, (b + 1) * seq) for b in seqs]
    ckv = [ckv_ref[b].astype(BF16) for b in seqs]
    kpe = [kpe_ref[b].astype(BF16) for b in seqs]
    qa = [jnp.concatenate(
        [_dot_nt(q_ref[r, h * QK_PAD:h * QK_PAD + QK_NOPE], wk_ref[:, h * QK_NOPE:(h + 1) * QK_NOPE])
         for h in heads], axis=0).astype(BF16) for r in rows]
    qpe = [jnp.concatenate(
        [q_ref[r, h * QK_PAD + QK_NOPE:h * QK_PAD + QK_NOPE + QK_ROPE] for h in heads], axis=0) for r in rows]
    s_p = [_dot_nt(qa[b], ckv[b]) + _dot_nt(qpe[b], kpe[b]) for b in seqs]
    s_n = [jnp.concatenate(
        [_dot_nt(q_ref[r, h * QK_PAD:(h + 1) * QK_PAD], kn_ref[r, h * QK_PAD:(h + 1) * QK_PAD])
         for h in heads], axis=0) for r in rows]
    s_p = [jnp.where(kc_past <= qc, s, NEG_INF) for s in s_p]
    s_n = [jnp.where(kc_new <= qc, s, NEG_INF) for s in s_n]
    m = [jnp.maximum(jnp.max(a, -1, keepdims=True), jnp.max(c, -1, keepdims=True)) for a, c in zip(s_p, s_n)]
    e_p = [jnp.exp2(s - mm) for s, mm in zip(s_p, m)]
    e_n = [jnp.exp2(s - mm) for s, mm in zip(s_n, m)]
    l = [jnp.sum(a, -1, keepdims=True) + jnp.sum(c, -1, keepdims=True) for a, c in zip(e_p, e_n)]
    p_n = [(e / ll).astype(BF16) for e, ll in zip(e_n, l)]
    o_lat = [_dot((e / ll).astype(BF16), c).astype(BF16) for e, ll, c in zip(e_p, l, ckv)]
    for b in seqs:
        for h in heads:
            hr = slice(h * seq, (h + 1) * seq)
            cols = slice(h * V_HEAD, (h + 1) * V_HEAD)
            o_ref[rows[b], cols] = _dot(o_lat[b][hr], wv_ref[:, cols]) + _dot(p_n[b][hr], vn_ref[rows[b], cols])


def _attention_cached(q, k, v, ckv_cache, kpe_cache, wk, wv, seq, nsub):
    n = q.shape[0]
    nb, past, _ = ckv_cache.shape
    assert nb % nsub == 0
    row = lambda w: pl.BlockSpec((nsub * seq, w), lambda b: (b, 0))
    kern = functools.partial(_attn_cached_body, past=past, seq=seq, nsub=nsub)
    return pl.pallas_call(
        kern,
        grid=(nb // nsub,),
        in_specs=[row(MLA_HEADS * QK_PAD), row(MLA_HEADS * QK_PAD), row(MLA_WIDTH),
                  pl.BlockSpec((nsub, past, KV_LORA), lambda b: (b, 0, 0)),
                  pl.BlockSpec((nsub, past, QK_ROPE), lambda b: (b, 0, 0)),
                  _const(wk.shape), _const(wv.shape)],
        out_specs=row(MLA_WIDTH),
        out_shape=jax.ShapeDtypeStruct((n, MLA_WIDTH), F32),
        compiler_params=_params("parallel"),
        name="attn_cached",
    )(q, k, v, ckv_cache, kpe_cache, wk, wv)


def _row_blocks(ref):
    rb = ref.shape[0] // MIX_SPLIT
    return [slice(i * rb, (i + 1) * rb) for i in range(MIX_SPLIT)]


def _mix_head(blocks, oss, om_ref, x_ref, gm_ref, wo_ref, lng_ref, lnb_ref, wxq_ref):
    om = [_rms(om_ref[r, :], gm_ref[...]).astype(BF16) for r in blocks]
    a = [_dot(s, wo_ref[:SSM_WIDTH, :]) + _dot(o, wo_ref[SSM_WIDTH:, :]) for s, o in zip(oss, om)]
    h1 = [_ln(ALPHA * x_ref[r, :] + aa, lng_ref[0:1, :], lnb_ref[0:1, :]) for r, aa in zip(blocks, a)]
    return h1, [_dot(h.astype(BF16), wxq_ref[...]).astype(BF16) for h in h1]


def _mix_body(os_ref, om_ref, x_ref, gm_ref, wo_ref, lng_ref, lnb_ref, wxq_ref, h1_ref, qx_ref):
    blocks = _row_blocks(x_ref)
    oss = [os_ref[r, :] for r in blocks]
    h1, qx = _mix_head(blocks, oss, om_ref, x_ref, gm_ref, wo_ref, lng_ref, lnb_ref, wxq_ref)
    for r, h, q in zip(blocks, h1, qx):
        h1_ref[r, :] = h
        qx_ref[r, :] = q


def _mix(o_ssm, o_mla, x2, g_mla, w_o, ln_g, ln_b, w_xq, tn):
    n = x2.shape[0]
    row = lambda w: pl.BlockSpec((tn, w), lambda i: (i, 0))
    return pl.pallas_call(
        _mix_body,
        grid=(n // tn,),
        in_specs=[row(SSM_WIDTH), row(MLA_WIDTH), row(D_MODEL), _const(g_mla.shape), _const(w_o.shape),
                  _const(ln_g.shape), _const(ln_b.shape), _const(w_xq.shape)],
        out_specs=[row(D_MODEL), row(D_MODEL)],
        out_shape=[jax.ShapeDtypeStruct((n, D_MODEL), F32), jax.ShapeDtypeStruct((n, D_MODEL), BF16)],
        compiler_params=_params("parallel"),
        name="mix",
    )(o_ssm, o_mla, x2, g_mla, w_o, ln_g, ln_b, w_xq)


def _mem_attn_heads(qs, mks, mvs):
    ss = [_dot_nt(q, mk.astype(BF16)) * (X_HEAD_DIM ** -0.5) for q, mk in zip(qs, mks)]
    es = [jnp.exp(s - jnp.max(s, -1, keepdims=True)) for s in ss]
    ps = [(e / jnp.sum(e, -1, keepdims=True)).astype(BF16) for e in es]
    return [_dot(p, mv.astype(BF16)).astype(BF16) for p, mv in zip(ps, mvs)]


def _mix_mem_body(y_ref, u_ref, d_ref, wglu_ref, gs_ref, om_ref, x_ref, gm_ref, wo_ref, lng_ref, lnb_ref, wxq_ref,
                  mk_ref, mv_ref, wxo_ref, h2_ref):
    blocks = _row_blocks(x_ref)
    oss = _glu_rows([y_ref[r, :] for r in blocks], [u_ref[r, :] for r in blocks], d_ref, wglu_ref, gs_ref)
    h1, qx = _mix_head(blocks, oss, om_ref, x_ref, gm_ref, wo_ref, lng_ref, lnb_ref, wxq_ref)
    cols = [slice(h * X_HEAD_DIM, (h + 1) * X_HEAD_DIM) for h in range(X_HEADS)]
    heads = _mem_attn_heads([q[:, c] for q in qx for c in cols], [mk_ref[:, c] for _ in qx for c in cols],
                            [mv_ref[:, c] for _ in qx for c in cols])
    ox = [jnp.concatenate(heads[i * X_HEADS:(i + 1) * X_HEADS], axis=1) for i in range(len(blocks))]
    att = [_dot(o, wxo_ref[...]) for o in ox]
    for r, h, a in zip(blocks, h1, att):
        h2_ref[r, :] = _ln(ALPHA * h + a, lng_ref[1:2, :], lnb_ref[1:2, :])


def _mix_mem(y_ssm, u, d_skip, w_glu, g_ssm, o_mla, x2, g_mla, w_o, ln_g, ln_b, w_xq, mem_k, mem_v, w_xo, tn):
    n = x2.shape[0]
    row = lambda w: pl.BlockSpec((tn, w), lambda i: (i, 0))
    glu_consts = (d_skip, w_glu, g_ssm)
    consts = (g_mla, w_o, ln_g, ln_b, w_xq, mem_k, mem_v, w_xo)
    return pl.pallas_call(
        _mix_mem_body,
        grid=(n // tn,),
        in_specs=[row(SSM_WIDTH), row(SSM_WIDTH)] + [_const(c.shape) for c in glu_consts]
        + [row(MLA_WIDTH), row(D_MODEL)] + [_const(c.shape) for c in consts],
        out_specs=row(D_MODEL),
        out_shape=jax.ShapeDtypeStruct((n, D_MODEL), F32),
        compiler_params=_params("parallel"),
        name="mix_mem",
    )(y_ssm, u, *glu_consts, o_mla, x2, *consts)


def _mem_attn_cache_body(qx_ref, mk_hbm, mv_hbm, o_ref, kbuf, vbuf, sems, *, seq, nsub):
    b = pl.program_id(0)
    nb = pl.num_programs(0)

    def copies(step, slot):
        return [pltpu.make_async_copy(src.at[step * nsub + j, :, h, :], buf.at[slot, j, h], sems.at[slot, j, t, h])
                for t, (src, buf) in enumerate(((mk_hbm, kbuf), (mv_hbm, vbuf)))
                for j in range(nsub) for h in range(X_HEADS)]

    @pl.when(b == 0)
    def _():
        for c in copies(0, 0):
            c.start()

    @pl.when(b + 1 < nb)
    def _():
        for c in copies(b + 1, (b + 1) % 2):
            c.start()

    slot = b % 2
    for c in copies(b, slot):
        c.wait()
    jh = [(j, h) for j in range(nsub) for h in range(X_HEADS)]
    where = [(slice(j * seq, (j + 1) * seq), slice(h * X_HEAD_DIM, (h + 1) * X_HEAD_DIM)) for j, h in jh]
    outs = _mem_attn_heads([qx_ref[r, c] for r, c in where], [kbuf[slot, j, h] for j, h in jh],
                           [vbuf[slot, j, h] for j, h in jh])
    for (r, c), o in zip(where, outs):
        o_ref[r, c] = o


def _mem_attn_cache(qx, mem_k, mem_v, seq, nsub):
    n = qx.shape[0]
    nb = mem_k.shape[0]
    assert nb % nsub == 0
    row = pl.BlockSpec((nsub * seq, D_MODEL), lambda b: (b, 0))
    hbm = pl.BlockSpec(memory_space=pl.ANY)
    buf = pltpu.VMEM((2, nsub, X_HEADS, N_MEM, X_HEAD_DIM), F32)
    return pl.pallas_call(
        functools.partial(_mem_attn_cache_body, seq=seq, nsub=nsub),
        grid=(nb // nsub,),
        in_specs=[row, hbm, hbm],
        out_specs=row,
        out_shape=jax.ShapeDtypeStruct((n, D_MODEL), BF16),
        scratch_shapes=[buf, buf, pltpu.SemaphoreType.DMA((2, nsub, 2, X_HEADS))],
        compiler_params=_params("arbitrary"),
        name="mem_attn_cache",
    )(qx, mem_k, mem_v)


def _mlp_body(*refs, ff_blk, second_norm):
    if second_norm:
        h1_ref, ox_ref, wxo_ref, lng_ref, lnb_ref, w1_ref, w2_ref, y_ref = refs
        h2 = _ln(ALPHA * h1_ref[...] + _dot(ox_ref[...], wxo_ref[...]), lng_ref[1:2, :], lnb_ref[1:2, :])
    else:
        h2_ref, lng_ref, lnb_ref, w1_ref, w2_ref, y_ref = refs
        h2 = h2_ref[...]
    hb = h2.astype(BF16)
    acc = jnp.zeros(h2.shape, F32)
    for c in range(0, D_FF, ff_blk):
        z = jnp.maximum(_dot(hb, w1_ref[:, c:c + ff_blk]), 0.0)
        acc += _dot((z * z).astype(BF16), w2_ref[c:c + ff_blk, :])
    y_ref[...] = _ln(ALPHA * h2 + acc, lng_ref[2:3, :], lnb_ref[2:3, :])


def _mlp(h, ox, w_xo, ln_g, ln_b, w1, w2, tn):
    n = h.shape[0]
    row = pl.BlockSpec((tn, D_MODEL), lambda i: (i, 0))
    second_norm = ox is not None
    rows = (h, ox) if second_norm else (h,)
    consts = ((w_xo,) if second_norm else ()) + (ln_g, ln_b, w1, w2)
    return pl.pallas_call(
        functools.partial(_mlp_body, ff_blk=1024, second_norm=second_norm),
        grid=(n // tn,),
        in_specs=[row] * len(rows) + [_const(c.shape) for c in consts],
        out_specs=row,
        out_shape=jax.ShapeDtypeStruct((n, D_MODEL), F32),
        compiler_params=_params("parallel"),
        name="mlp",
    )(*rows, *consts)


def _mem_kv_body(mem_ref, wk_ref, wv_ref, k_ref, v_ref):
    m = mem_ref[...].astype(BF16)
    k_ref[...] = _dot(m, wk_ref[...])
    v_ref[...] = _dot(m, wv_ref[...])


def _mem_kv(mem2, wk, wv):
    n = mem2.shape[0]
    out = jax.ShapeDtypeStruct((n, D_MODEL), F32)
    return pl.pallas_call(
        _mem_kv_body,
        grid=(1,),
        in_specs=[_const(mem2.shape), _const(wk.shape), _const(wv.shape)],
        out_specs=[_const((n, D_MODEL))] * 2,
        out_shape=[out, out],
        compiler_params=_params("arbitrary"),
        name="mem_kv",
    )(mem2, wk, wv)


def _state_to_cols(s):
    return jnp.transpose(s.reshape(s.shape[0], S5_COLS, -1), (1, 0, 2))


def _state_from_cols(s):
    return jnp.transpose(s, (1, 0, 2)).reshape(s.shape[1], SSM_GROUPS, SSM_STATE)


def _layer(x2, pos, nseq, h0r, h0i, mem_k, mem_v, caches, wts, s5_ops, tn, scan_ct):
    n = x2.shape[0]
    seq = n // nseq
    prompt = caches is None
    u, q, k, v, ckv, kpe = _project(x2, pos, wts["inv"], wts["w_in"], wts["g_q"], wts["w_q"], wts["g_kv"],
                                    wts["w_k"], wts["w_vt"] if prompt else wts["w_v"], min(n, PROJ_TILE),
                                    v_transposed=prompt, consecutive=prompt, kpe_cache_shape=prompt)
    y_ssm, fr, fi = _s5_scan(u, s5_ops, _state_to_cols(h0r), _state_to_cols(h0i), nseq, scan_ct)
    glu_args = (y_ssm, u, wts["d_skip"], wts["w_glu"], wts["g_out_ssm"])
    if prompt:
        o_mla = _attention(q, k, v, ATTN_HEADS)
    else:
        o_mla = _attention_cached(q, k, v, caches[0], caches[1], wts["w_k"], wts["w_v"], seq, nsub=CACHE_SEQS)
    mix_args = (o_mla, x2, wts["g_out_mla"], wts["w_o"], wts["ln_g"], wts["ln_b"], wts["w_xq"])
    if prompt:
        h, ox = _mix_mem(*glu_args, *mix_args, mem_k, mem_v, wts["w_xo"], min(n, MIX_TILE)), None
    else:
        h, qx = _mix(_glu(*glu_args, tn), *mix_args, tn)
        ox = _mem_attn_cache(qx, mem_k, mem_v, seq, nsub=CACHE_SEQS)
    y = _mlp(h, ox, wts["w_xo"], wts["ln_g"], wts["ln_b"], wts["w_ff1"], wts["w_ff2"], tn)
    return y, ckv, kpe, _state_from_cols(fr), _state_from_cols(fi)


def kernel(x_prompt, x_sample, mem_prompt, cache_mla_ckv, cache_mla_kpe, state_ssm_re, state_ssm_im, cache_mem_k, cache_mem_v, w_in, g_q, w_q_up, g_kv, w_kv_up, a_re, a_im, b_re, b_im, c_re, c_im, d_skip, log_dt, w_glu, g_out_ssm, g_out_mla, w_o, w_xq, w_xk, w_xv, w_xo, w_ff1, w_ff2, ln_g, ln_b):
    assert w_in.shape[0] == DEPTH == 1
    nbp, sp, _ = x_prompt.shape
    nbs, sd, _ = x_sample.shape
    past = cache_mla_ckv.shape[2]
    assert nbp == 1

    wq = jnp.pad(w_q_up[0], ((0, 0), (0, 0), (0, QK_PAD - QK_NOPE - QK_ROPE)))
    wk = w_kv_up[0][:, :, :QK_NOPE].reshape(KV_LORA, -1).astype(BF16)
    wv = w_kv_up[0][:, :, QK_NOPE:].reshape(KV_LORA, -1).astype(BF16)
    inv = ROPE_THETA ** (-jnp.arange(ROPE_HALF, dtype=F32) / ROPE_HALF)
    wts = {
        "inv": jnp.tile(inv, LANES // ROPE_HALF).reshape(1, LANES),
        "w_in": jnp.pad(w_in[0], ((0, 0), (0, LANES - QK_ROPE))).astype(BF16),
        "g_q": g_q[0].reshape(1, -1),
        "w_q": wq.reshape(Q_LORA, MLA_HEADS * QK_PAD).astype(BF16),
        "g_kv": g_kv[0].reshape(1, -1),
        "w_k": wk,
        "w_v": wv,
        "w_vt": wv.T,
        "d_skip": d_skip[0].reshape(1, -1),
        "w_glu": w_glu[0].astype(BF16),
        "g_out_ssm": g_out_ssm[0].reshape(1, -1),
        "g_out_mla": g_out_mla[0].reshape(1, -1),
        "w_o": w_o[0].astype(BF16),
        "w_xq": w_xq[0].reshape(D_MODEL, D_MODEL).astype(BF16),
        "w_xo": w_xo[0].reshape(D_MODEL, D_MODEL).astype(BF16),
        "w_ff1": w_ff1[0].astype(BF16),
        "w_ff2": w_ff2[0].astype(BF16),
        "ln_g": ln_g[0],
        "ln_b": ln_b[0],
    }
    s5_ops = _s5_prep(a_re[0], a_im[0], b_re[0], b_im[0], c_re[0], c_im[0], log_dt[0])

    mk, mv = _mem_kv(mem_prompt.reshape(nbp * N_MEM, D_MODEL),
                     w_xk[0].reshape(D_MODEL, D_MODEL).astype(BF16),
                     w_xv[0].reshape(D_MODEL, D_MODEL).astype(BF16))
    zero = jnp.zeros((nbp, SSM_GROUPS, SSM_STATE), F32)
    pos_p = jnp.arange(sp, dtype=F32).reshape(sp, 1)
    yp, ckv_p, kpe_p, sre_p, sim_p = _layer(
        x_prompt.reshape(sp, D_MODEL), pos_p, nbp, zero, zero, mk, mv, None,
        wts, s5_ops, tn=512, scan_ct=512)

    pos_s = jnp.tile(past + jnp.arange(sd, dtype=F32), nbs).reshape(nbs * sd, 1)
    caches = (cache_mla_ckv[0], cache_mla_kpe[0])
    ys, ckv_s, kpe_s, sre_s, sim_s = _layer(
        x_sample.reshape(nbs * sd, D_MODEL), pos_s, nbs,
        state_ssm_re[0], state_ssm_im[0], cache_mem_k[0], cache_mem_v[0], caches,
        wts, s5_ops, tn=512, scan_ct=sd // S5_T)

    return (yp.reshape(nbp, sp, D_MODEL), ys.reshape(nbs, sd, D_MODEL),
            ckv_p.reshape(1, nbp, sp, KV_LORA), kpe_p.reshape(1, nbp, sp, QK_ROPE),
            sre_p.reshape(1, nbp, SSM_GROUPS, SSM_STATE), sim_p.reshape(1, nbp, SSM_GROUPS, SSM_STATE),
            mk.reshape(1, nbp, N_MEM, X_HEADS, X_HEAD_DIM), mv.reshape(1, nbp, N_MEM, X_HEADS, X_HEAD_DIM),
            ckv_s.reshape(1, nbs, sd, KV_LORA), kpe_s.reshape(1, nbs, sd, QK_ROPE),
            sre_s.reshape(1, nbs, SSM_GROUPS, SSM_STATE), sim_s.reshape(1, nbs, SSM_GROUPS, SSM_STATE))
```

```python
import functools
import math

import jax
import jax.numpy as jnp
from jax import lax
from jax.experimental import pallas as pl
from jax.experimental.pallas import tpu as pltpu

F32 = jnp.float32
BF16 = jnp.bfloat16

D_MODEL = 1024
DEPTH = 1
CHUNK = 64
SSM_WIDTH = 512
SSM_GROUP = 16
SSM_GROUPS = 32
SSM_STATE = 64
MLA_HEADS = 4
QK_NOPE = 128
QK_ROPE = 64
V_HEAD = 128
MLA_WIDTH = MLA_HEADS * V_HEAD
Q_LORA = 384
KV_LORA = 256
ROPE_THETA = 10000.0
MLA_SCALE = (QK_NOPE + QK_ROPE) ** -0.5
N_MEM = 256
X_HEADS = 4
X_HEAD_DIM = D_MODEL // X_HEADS
D_FF = 4 * D_MODEL
ALPHA = (2 * DEPTH) ** 0.25
EPS = 1e-5
NEG_INF = -1e30

LANES = 128
MXU_DEPTH = 256
VMEM_BYTES = 64 * 1024 * 1024
VMEM_LIMIT = VMEM_BYTES - 8 * 1024 * 1024

QK_PAD = MXU_DEPTH
S5_T = MXU_DEPTH // SSM_GROUP
S5_COL_GROUPS = LANES // SSM_GROUP
S5_COLS = SSM_GROUPS // S5_COL_GROUPS
S5_SW = S5_COL_GROUPS * SSM_STATE
ROPE_HALF = QK_ROPE // 2
Q_SCALE = MLA_SCALE * math.log2(math.e)
ATTN_T = 512
ATTN_HEADS = 2
ATTN_TQ = 1024
ATTN_QSPLIT = 4
MLP_FF_BLOCK = 1024
CACHE_SEQS = 4
MIX_TILE = 1024
MIX_SPLIT = 2
PROJ_TILE = 1024
PROJ_ROWS = 512
TOKEN_TILE = 512
SCAN_CHUNKS = 512

_NT = (((1,), (1,)), ((), ()))


def _rms(x, g):
    return x * lax.rsqrt(jnp.mean(x * x, -1, keepdims=True) + EPS) * g


def _ln(x, g, b):
    mu = jnp.mean(x, -1, keepdims=True)
    xc = x - mu
    var = jnp.mean(xc * xc, -1, keepdims=True)
    return xc * lax.rsqrt(var + EPS) * g + b


def _dot(a, b):
    return jnp.dot(a, b, preferred_element_type=F32)


def _dot_nt(a, b):
    return lax.dot_general(a, b, _NT, preferred_element_type=F32)


def _split(a):
    hi = a.astype(BF16)
    return hi, (a - hi.astype(F32)).astype(BF16)


def _dot_split(a, b):
    return _dot(a[0], b[0]) + _dot(a[0], b[1]) + _dot(a[1], b[0])


def _params(*sem):
    return pltpu.CompilerParams(dimension_semantics=sem, vmem_limit_bytes=VMEM_LIMIT)


def _const(shape):
    n = len(shape)
    return pl.BlockSpec(shape, lambda *_: (0,) * n)


def _proj_body(x_ref, pos_ref, inv_ref, w_in_ref, gq_ref, wq_ref, gkv_ref, wk_ref, wv_ref,
               u_ref, q_ref, k_ref, v_ref, ckv_ref, kpe_ref, tc_ref, ts_ref, *, v_transposed, consecutive):
    rb = PROJ_ROWS
    assert x_ref.shape[0] % rb == 0 and ATTN_T % rb == 0
    blocks = [slice(r0, r0 + rb) for r0 in range(0, x_ref.shape[0], rb)]
    c0 = SSM_WIDTH + Q_LORA
    projs = [_dot(x_ref[r, :].astype(BF16), w_in_ref[...]) for r in blocks]
    cqs = [_rms(p[:, SSM_WIDTH:c0], gq_ref[...]).astype(BF16) for p in projs]
    ckvs = [_rms(p[:, c0:c0 + KV_LORA], gkv_ref[...]) for p in projs]
    qs = [_dot(c, wq_ref[...]) * Q_SCALE for c in cqs]
    kns = [_dot(c.astype(BF16), wk_ref[...]) for c in ckvs]
    for r, p, ckv in zip(blocks, projs, ckvs):
        u_ref[r, :] = p[:, :SSM_WIDTH]
        ckv_ref[r, :] = ckv
        if v_transposed:
            lanes = slice(r.start % ATTN_T, r.start % ATTN_T + rb)
            v_ref[r.start // ATTN_T, :, lanes] = _dot_nt(wv_ref[...], ckv.astype(BF16)).astype(BF16)
        else:
            v_ref[r, :] = _dot(ckv.astype(BF16), wv_ref[...]).astype(BF16)

    lane = lax.broadcasted_iota(jnp.int32, (1, LANES), 1)
    live = lane < QK_ROPE
    sign = jnp.where(lane < ROPE_HALF, -1.0, 1.0)
    inv = inv_ref[...]
    if consecutive:
        @pl.when(pl.program_id(0) == 0)
        def _():
            r = lax.broadcasted_iota(jnp.int32, (x_ref.shape[0], 1), 0).astype(F32)
            tc_ref[...] = jnp.where(live, jnp.cos(r * inv), 0.0)
            ts_ref[...] = jnp.where(live, jnp.sin(r * inv), 0.0)

        base = pos_ref[0:1, :] * inv
        ca, sa = jnp.cos(base), jnp.sin(base)
        cos_ts = [ca * tc_ref[r, :] - sa * ts_ref[r, :] for r in blocks]
        sin_ts = [(sa * sign) * tc_ref[r, :] + (ca * sign) * ts_ref[r, :] for r in blocks]
    else:
        angs = [pos_ref[r, :] * inv for r in blocks]
        cos_ts = [jnp.where(live, jnp.cos(a), 0.0) for a in angs]
        sin_ts = [jnp.where(live, jnp.sin(a) * sign, 0.0) for a in angs]

    def rope(c2, cos_t, sin_t):
        swapped = jnp.where(lane < ROPE_HALF, pltpu.roll(c2, LANES - ROPE_HALF, 1), pltpu.roll(c2, ROPE_HALF, 1))
        return c2 * cos_t + swapped * sin_t

    for r, p, q, kn, cos_t, sin_t in zip(blocks, projs, qs, kns, cos_ts, sin_ts):
        kpe = rope(p[:, c0 + KV_LORA:], cos_t, sin_t)
        kpe_ref[r, :] = kpe[:, :QK_ROPE]
        kpe_b = kpe.astype(BF16)
        for h in range(MLA_HEADS):
            a = h * QK_PAD
            q_ref[r, a:a + QK_NOPE] = q[:, a:a + QK_NOPE].astype(BF16)
            q_ref[r, a + QK_NOPE:a + QK_PAD] = rope(q[:, a + QK_NOPE:a + QK_PAD], cos_t, sin_t).astype(BF16)
            k_ref[r, a:a + QK_NOPE] = kn[:, h * QK_NOPE:(h + 1) * QK_NOPE].astype(BF16)
            k_ref[r, a + QK_NOPE:a + QK_PAD] = kpe_b


def _project(x2, pos, inv, w_in, gq, wq, gkv, wk, wv, tn, v_transposed, consecutive, kpe_cache_shape):
    n = x2.shape[0]
    row = lambda w: pl.BlockSpec((tn, w), lambda i: (i, 0))
    if kpe_cache_shape:
        kpe_spec = pl.BlockSpec((None, None, tn, QK_ROPE), lambda i: (0, 0, i, 0))
        kpe_shape = jax.ShapeDtypeStruct((1, 1, n, QK_ROPE), F32)
    else:
        kpe_spec, kpe_shape = row(QK_ROPE), jax.ShapeDtypeStruct((n, QK_ROPE), F32)
    if v_transposed:
        assert tn % ATTN_T == 0
        v_spec = pl.BlockSpec((tn // ATTN_T, MLA_WIDTH, ATTN_T), lambda i: (i, 0, 0))
        v_shape = jax.ShapeDtypeStruct((n // ATTN_T, MLA_WIDTH, ATTN_T), BF16)
    else:
        v_spec, v_shape = row(MLA_WIDTH), jax.ShapeDtypeStruct((n, MLA_WIDTH), BF16)
    return pl.pallas_call(
        functools.partial(_proj_body, v_transposed=v_transposed, consecutive=consecutive),
        grid=(n // tn,),
        scratch_shapes=[pltpu.VMEM((tn, LANES), F32)] * 2,
        in_specs=[row(D_MODEL), row(1), _const(inv.shape), _const(w_in.shape), _const(gq.shape),
                  _const(wq.shape), _const(gkv.shape), _const(wk.shape), _const(wv.shape)],
        out_specs=[row(SSM_WIDTH), row(MLA_HEADS * QK_PAD), row(MLA_HEADS * QK_PAD),
                   v_spec, row(KV_LORA), kpe_spec],
        out_shape=[jax.ShapeDtypeStruct((n, SSM_WIDTH), F32),
                   jax.ShapeDtypeStruct((n, MLA_HEADS * QK_PAD), BF16),
                   jax.ShapeDtypeStruct((n, MLA_HEADS * QK_PAD), BF16),
                   v_shape,
                   jax.ShapeDtypeStruct((n, KV_LORA), F32),
                   kpe_shape],
        compiler_params=_params("arbitrary"),
        name="proj",
    )(x2, pos, inv, w_in, gq, wq, gkv, wk, wv)


def _s5_prep_body(ar_row, ai_row, ldt_row, bt_re, bt_im, ct_re, ct_im,
                  wt_ref, v_ref, z_ref, are_ref, aim_ref):
    sw = S5_SW
    arr, air, dtr = ar_row[0], ai_row[0], jnp.exp(ldt_row[0])

    mag = jnp.exp(arr * dtr)
    lr, li = mag * jnp.cos(air * dtr), mag * jnp.sin(air * dtr)
    nr, ni = lr - 1.0, li
    den = arr * arr + air * air
    f_re, f_im = (nr * arr + ni * air) / den, (ni * arr - nr * air) / den

    same_b = (lax.broadcasted_iota(jnp.int32, (LANES, sw), 0) // SSM_GROUP
              == lax.broadcasted_iota(jnp.int32, (LANES, sw), 1) // SSM_STATE)
    br = jnp.where(same_b, bt_re[0], 0.0)
    bi = jnp.where(same_b, bt_im[0], 0.0)
    bb_re = f_re * br - f_im * bi
    bb_im = f_re * bi + f_im * br
    same_c = (lax.broadcasted_iota(jnp.int32, (sw, LANES), 0) // SSM_STATE
              == lax.broadcasted_iota(jnp.int32, (sw, LANES), 1) // SSM_GROUP)
    cr = jnp.where(same_c, ct_re[0], 0.0)
    ci = jnp.where(same_c, ct_im[0], 0.0)
    cr_s, ci_s = _split(cr), _split(ci)

    e = lax.broadcasted_iota(jnp.int32, (2 * S5_T, 1), 0).astype(F32)
    pm = jnp.exp(arr * dtr * e)
    pw_re, pw_im = pm * jnp.cos(air * dtr * e), pm * jnp.sin(air * dtr * e)
    pt_re, pt_im = pw_re.T, pw_im.T

    lag_ops = []
    for lag in range(S5_T):
        p_re, p_im = pw_re[lag:lag + 1, :], pw_im[lag:lag + 1, :]
        k_re = p_re * bb_re - p_im * bb_im
        k_im = p_re * bb_im + p_im * bb_re
        i = S5_T - 1 - lag
        v_ref[0, i * LANES:(i + 1) * LANES, :sw] = k_re.astype(BF16)
        v_ref[0, i * LANES:(i + 1) * LANES, sw:] = k_im.astype(BF16)
        lag_ops.append(_dot_split(_split(k_re), cr_s) - _dot_split(_split(k_im), ci_s))

        q_re, q_im = pt_re[:, lag + 1:lag + 2], pt_im[:, lag + 1:lag + 2]
        z_ref[0, :sw, lag * LANES:(lag + 1) * LANES] = (cr * q_re - ci * q_im).astype(BF16)
        z_ref[0, sw:, lag * LANES:(lag + 1) * LANES] = (-(cr * q_im + ci * q_re)).astype(BF16)

    zero = jnp.zeros((LANES, LANES), BF16)
    for d in range(S5_T // 2):
        wt_ref[0, d, :LANES, :LANES] = lag_ops[2 * d].astype(BF16)
        wt_ref[0, d, :LANES, LANES:] = lag_ops[2 * d + 1].astype(BF16)
        wt_ref[0, d, LANES:, :LANES] = lag_ops[2 * d - 1].astype(BF16) if d else zero
        wt_ref[0, d, LANES:, LANES:] = lag_ops[2 * d].astype(BF16)

    are_ref[0] = pw_re[S5_T:S5_T + 1, :]
    aim_ref[0] = pw_im[S5_T:S5_T + 1, :]


def _s5_prep(a_re, a_im, b_re, b_im, c_re, c_im, log_dt):
    nc, r, t, h, p, sw = S5_COLS, S5_COL_GROUPS, S5_T, SSM_GROUP, SSM_STATE, S5_SW
    ldt = jnp.repeat(log_dt, p)
    bt = lambda b: jnp.tile(jnp.transpose(b.reshape(nc, r, p, h), (0, 1, 3, 2)).reshape(nc, r * h, p), (1, 1, r))
    ct = lambda c: jnp.tile(jnp.transpose(c.reshape(nc, r, h, p), (0, 1, 3, 2)).reshape(nc, r * p, h), (1, 1, r))
    args = (a_re.reshape(nc, 1, sw), a_im.reshape(nc, 1, sw), ldt.reshape(nc, 1, sw),
            bt(b_re), bt(b_im), ct(c_re), ct(c_im))
    blk = lambda s: pl.BlockSpec((1,) + s[1:], lambda i: (i,) + (0,) * (len(s) - 1))
    outs = [((nc, t // 2, 2 * LANES, 2 * LANES), BF16), ((nc, t * LANES, 2 * sw), BF16),
            ((nc, 2 * sw, t * LANES), BF16), ((nc, 1, sw), F32), ((nc, 1, sw), F32)]
    return pl.pallas_call(
        _s5_prep_body,
        grid=(nc,),
        in_specs=[blk(a.shape) for a in args],
        out_specs=[blk(s) for s, _ in outs],
        out_shape=[jax.ShapeDtypeStruct(s, d) for s, d in outs],
        compiler_params=_params("parallel"),
        name="s5_prep",
    )(*args)


def _s5_scan_body(u_ref, wt_ref, v_ref, z_ref, are_ref, aim_ref, h0r_ref, h0i_ref,
                  y_ref, fr_ref, fi_ref, lhs, sre, sim, xre, xim, cr, ci, *, nseq, ct):
    t = pl.program_id(1)
    sw = S5_SW
    seq = ct * S5_T

    @pl.when(t == 0)
    def _():
        cr[...] = h0r_ref[0]
        ci[...] = h0i_ref[0]

    def token_rows(cc, i):
        if nseq == 1:
            return pl.ds(i, ct, stride=S5_T), slice(None)
        return pl.ds(cc * S5_T + i, nseq, stride=seq), slice(cc * nseq, (cc + 1) * nseq)

    for cc in range(1 if nseq == 1 else ct):
        for i in range(S5_T):
            tok, crow = token_rows(cc, i)
            lhs[crow, i * LANES:(i + 1) * LANES] = u_ref[tok, :].astype(BF16)

    s = _dot(lhs[...], v_ref[0])
    sre[...] = s[:, :sw]
    sim[...] = s[:, sw:]

    a_re, a_im = are_ref[0], aim_ref[0]

    def step(c, carry):
        x_re, x_im = carry
        r = pl.ds(c * nseq, nseq)
        xre[r, :] = x_re
        xim[r, :] = x_im
        n_re = a_re * x_re - a_im * x_im + sre[r, :]
        n_im = a_re * x_im + a_im * x_re + sim[r, :]
        return n_re, n_im

    x_re, x_im = lax.fori_loop(0, ct, step, (cr[...], ci[...]))
    cr[...] = x_re
    ci[...] = x_im
    fr_ref[0] = x_re
    fi_ref[0] = x_im

    xb_re, xb_im = xre[...].astype(BF16), xim[...].astype(BF16)
    for jp in range(S5_T // 2):
        cols = slice(jp * 2 * LANES, (jp + 1) * 2 * LANES)
        acc = _dot(xb_re, z_ref[0, :sw, cols]) + _dot(xb_im, z_ref[0, sw:, cols])
        for ip in range(jp + 1):
            acc += _dot(lhs[:, ip * 2 * LANES:(ip + 1) * 2 * LANES], wt_ref[0, jp - ip])
        for cc in range(1 if nseq == 1 else ct):
            for jj in range(2):
                tok, crow = token_rows(cc, 2 * jp + jj)
                y_ref[tok, :] = acc[crow, jj * LANES:(jj + 1) * LANES]


def _s5_scan(u, col_ops, h0r, h0i, nseq, ct):
    wt, v, z, are, aim = col_ops
    n = u.shape[0]
    cps = n // nseq // S5_T
    assert nseq == 1 or ct == cps
    nc = nseq * ct
    sw = S5_SW
    tile = pl.BlockSpec((nc * S5_T, LANES), lambda c, t: (t, c))
    op = lambda a: pl.BlockSpec((1,) + a.shape[1:], lambda c, t: (c,) + (0,) * (a.ndim - 1))
    st = pl.BlockSpec((1, nseq, sw), lambda c, t: (c, 0, 0))
    st_shape = jax.ShapeDtypeStruct((S5_COLS, nseq, sw), F32)
    return pl.pallas_call(
        functools.partial(_s5_scan_body, nseq=nseq, ct=ct),
        grid=(S5_COLS, cps // ct),
        in_specs=[tile, op(wt), op(v), op(z), op(are), op(aim), st, st],
        out_specs=[tile, st, st],
        out_shape=[jax.ShapeDtypeStruct(u.shape, F32), st_shape, st_shape],
        scratch_shapes=[pltpu.VMEM((nc, S5_T * LANES), BF16)] + [pltpu.VMEM((nc, sw), F32)] * 4
        + [pltpu.VMEM((nseq, sw), F32)] * 2,
        compiler_params=_params("arbitrary", "arbitrary"),
        name="s5_scan",
    )(u, wt, v, z, are, aim, h0r, h0i)


def _glu_rows(ys, us, d_ref, w_ref, g_ref):
    acts = [jax.nn.gelu(y + d_ref[...] * u).astype(BF16) for y, u in zip(ys, us)]
    gls = [_dot(a, w_ref[...]) for a in acts]
    os = [gl[:, :SSM_WIDTH] * jax.nn.sigmoid(gl[:, SSM_WIDTH:]) for gl in gls]
    return [_rms(o, g_ref[...]).astype(BF16) for o in os]


def _glu_body(y_ref, u_ref, d_ref, w_ref, g_ref, o_ref):
    o_ref[...] = _glu_rows([y_ref[...]], [u_ref[...]], d_ref, w_ref, g_ref)[0]


def _glu(y, u, d, w, g, tn):
    n = y.shape[0]
    row = pl.BlockSpec((tn, SSM_WIDTH), lambda i: (i, 0))
    return pl.pallas_call(
        _glu_body,
        grid=(n // tn,),
        in_specs=[row, row, _const(d.shape), _const(w.shape), _const(g.shape)],
        out_specs=row,
        out_shape=jax.ShapeDtypeStruct((n, SSM_WIDTH), BF16),
        compiler_params=_params("parallel"),
        name="glu",
    )(y, u, d, w, g)


def _attn_body(q_ref, k_ref, vt_ref, o_ref, s0, s1, mt0, mt1, m_sc, l_sc, acc_sc, *, heads):
    i = pl.program_id(1)
    t = ATTN_T
    m_sc[...] = jnp.full(m_sc.shape, -jnp.inf, F32)
    l_sc[...] = jnp.zeros(l_sc.shape, F32)
    acc_sc[...] = jnp.zeros(acc_sc.shape, F32)

    qb = q_ref.shape[0] // ATTN_QSPLIT
    units = [(h, c * qb) for c in range(ATTN_QSPLIT) for h in range(heads)]

    def scores(tile, s_buf, mt_buf, unit, diag=None):
        h, c0 = unit
        r = pl.ds(pl.multiple_of(tile * t, t), t)
        s = _dot_nt(k_ref[r, h * QK_PAD:(h + 1) * QK_PAD],
                    q_ref[c0:c0 + qb, h * QK_PAD:(h + 1) * QK_PAD])
        if diag is not None:
            kc = (diag * t + lax.broadcasted_iota(jnp.int32, s.shape, 0)) // CHUNK
            qc = (c0 + lax.broadcasted_iota(jnp.int32, s.shape, 1)) // CHUNK
            s = jnp.where(kc <= qc, s, NEG_INF)
        s_buf[h, :, c0:c0 + qb] = s
        mt_buf[h, :, c0:c0 + qb] = jnp.max(s, 0, keepdims=True)

    def absorb(tile, s_buf, mt_buf, unit):
        h, c0 = unit
        cols = slice(c0, c0 + qb)
        m_prev = m_sc[h, :, cols]
        m_new = jnp.maximum(m_prev, mt_buf[h, :, cols])
        p = jnp.exp2(s_buf[h, :, cols] - m_new)
        alpha = jnp.exp2(m_prev - m_new)
        l_sc[h, :, cols] = alpha * l_sc[h, :, cols] + jnp.sum(p, 0, keepdims=True)
        m_sc[h, :, cols] = m_new
        pv = _dot(vt_ref[tile, h * V_HEAD:(h + 1) * V_HEAD, :], p.astype(BF16))
        acc_sc[h, :, cols] = alpha * acc_sc[h, :, cols] + pv

    nd = q_ref.shape[0] // t
    bufs = ((s0, mt0), (s1, mt1))
    for u in units:
        scores(nd * i, *bufs[0], u, diag=0)
    for d in range(1, nd):
        for u in units:
            scores(nd * i + d, *bufs[d % 2], u, diag=d)
            absorb(nd * i + d - 1, *bufs[(d - 1) % 2], u)

    def pair(jj, c):
        prev = jnp.where(jj == 0, nd * i + nd - 1, 2 * jj - 1)
        for u in units:
            scores(2 * jj, s0, mt0, u)
            absorb(prev, s1, mt1, u)
        for u in units:
            scores(2 * jj + 1, s1, mt1, u)
            absorb(2 * jj, s0, mt0, u)
        return c

    lax.fori_loop(0, (nd // 2) * i, pair, 0)
    last = jnp.where(i == 0, nd - 1, nd * i - 1)
    for u in units:
        absorb(last, s1, mt1, u)

    for h in range(heads):
        o_ref[:, h * V_HEAD:(h + 1) * V_HEAD] = (acc_sc[h] / l_sc[h]).T


def _attention(q, k, vt, heads):
    n = q.shape[0]
    t = ATTN_T
    tq = ATTN_TQ
    assert t % CHUNK == 0 and n % tq == 0 and MLA_HEADS % heads == 0 and (tq // t) % 2 == 0
    once = pl.Buffered(1)
    return pl.pallas_call(
        functools.partial(_attn_body, heads=heads),
        grid=(MLA_HEADS // heads, n // tq),
        in_specs=[pl.BlockSpec((tq, heads * QK_PAD), lambda g, i: (i, g)),
                  pl.BlockSpec((n, heads * QK_PAD), lambda g, i: (0, g), pipeline_mode=once),
                  pl.BlockSpec((n // t, heads * V_HEAD, t), lambda g, i: (0, g, 0), pipeline_mode=once)],
        out_specs=pl.BlockSpec((tq, heads * V_HEAD), lambda g, i: (i, g)),
        out_shape=jax.ShapeDtypeStruct((n, MLA_WIDTH), F32),
        scratch_shapes=[pltpu.VMEM((heads, t, tq), F32)] * 2 + [pltpu.VMEM((heads, 1, tq), F32)] * 4
        + [pltpu.VMEM((heads, V_HEAD, tq), F32)],
        compiler_params=_params("arbitrary", "arbitrary"),
        name="attn",
    )(q, k, vt)


def _attn_cached_body(q_ref, kn_ref, vn_ref, ckv_ref, kpe_ref, wk_ref, wv_ref, o_ref, *, past, seq, nsub):
    heads = range(MLA_HEADS)
    pos = past + lax.broadcasted_iota(jnp.int32, (seq, 1), 0)
    qc = jnp.concatenate([pos] * MLA_HEADS, axis=0) // CHUNK
    kc_past = lax.broadcasted_iota(jnp.int32, (1, past), 1) // CHUNK
    kc_new = (past + lax.broadcasted_iota(jnp.int32, (1, seq), 1)) // CHUNK
    seqs = range(nsub)
    rows = [slice(b * seq, (b + 1) * seq) for b in seqs]
    ckv = [ckv_ref[b].astype(BF16) for b in seqs]
    kpe = [kpe_ref[b].astype(BF16) for b in seqs]
    qa = [jnp.concatenate(
        [_dot_nt(q_ref[r, h * QK_PAD:h * QK_PAD + QK_NOPE], wk_ref[:, h * QK_NOPE:(h + 1) * QK_NOPE])
         for h in heads], axis=0).astype(BF16) for r in rows]
    qpe = [jnp.concatenate(
        [q_ref[r, h * QK_PAD + QK_NOPE:h * QK_PAD + QK_NOPE + QK_ROPE] for h in heads], axis=0) for r in rows]
    s_p = [_dot_nt(qa[b], ckv[b]) + _dot_nt(qpe[b], kpe[b]) for b in seqs]
    s_n = [jnp.concatenate(
        [_dot_nt(q_ref[r, h * QK_PAD:(h + 1) * QK_PAD], kn_ref[r, h * QK_PAD:(h + 1) * QK_PAD])
         for h in heads], axis=0) for r in rows]
    s_p = [jnp.where(kc_past <= qc, s, NEG_INF) for s in s_p]
    s_n = [jnp.where(kc_new <= qc, s, NEG_INF) for s in s_n]
    m = [jnp.maximum(jnp.max(a, -1, keepdims=True), jnp.max(c, -1, keepdims=True)) for a, c in zip(s_p, s_n)]
    e_p = [jnp.exp2(s - mm) for s, mm in zip(s_p, m)]
    e_n = [jnp.exp2(s - mm) for s, mm in zip(s_n, m)]
    l = [jnp.sum(a, -1, keepdims=True) + jnp.sum(c, -1, keepdims=True) for a, c in zip(e_p, e_n)]
    p_n = [(e / ll).astype(BF16) for e, ll in zip(e_n, l)]
    o_lat = [_dot((e / ll).astype(BF16), c).astype(BF16) for e, ll, c in zip(e_p, l, ckv)]
    for b in seqs:
        for h in heads:
            hr = slice(h * seq, (h + 1) * seq)
            cols = slice(h * V_HEAD, (h + 1) * V_HEAD)
            o_ref[rows[b], cols] = _dot(o_lat[b][hr], wv_ref[:, cols]) + _dot(p_n[b][hr], vn_ref[rows[b], cols])


def _attention_cached(q, k, v, ckv_cache, kpe_cache, wk, wv, seq, nsub):
    n = q.shape[0]
    nb, past, _ = ckv_cache.shape
    assert nb % nsub == 0
    row = lambda w: pl.BlockSpec((nsub * seq, w), lambda b: (b, 0))
    kern = functools.partial(_attn_cached_body, past=past, seq=seq, nsub=nsub)
    return pl.pallas_call(
        kern,
        grid=(nb // nsub,),
        in_specs=[row(MLA_HEADS * QK_PAD), row(MLA_HEADS * QK_PAD), row(MLA_WIDTH),
                  pl.BlockSpec((nsub, past, KV_LORA), lambda b: (b, 0, 0)),
                  pl.BlockSpec((nsub, past, QK_ROPE), lambda b: (b, 0, 0)),
                  _const(wk.shape), _const(wv.shape)],
        out_specs=row(MLA_WIDTH),
        out_shape=jax.ShapeDtypeStruct((n, MLA_WIDTH), F32),
        compiler_params=_params("parallel"),
        name="attn_cached",
    )(q, k, v, ckv_cache, kpe_cache, wk, wv)


def _row_blocks(ref):
    rb = ref.shape[0] // MIX_SPLIT
    return [slice(i * rb, (i + 1) * rb) for i in range(MIX_SPLIT)]


def _mix_head(blocks, oss, om_ref, x_ref, gm_ref, wo_ref, lng_ref, lnb_ref, wxq_ref):
    om = [_rms(om_ref[r, :], gm_ref[...]).astype(BF16) for r in blocks]
    a = [_dot(s, wo_ref[:SSM_WIDTH, :]) + _dot(o, wo_ref[SSM_WIDTH:, :]) for s, o in zip(oss, om)]
    h1 = [_ln(ALPHA * x_ref[r, :] + aa, lng_ref[0:1, :], lnb_ref[0:1, :]) for r, aa in zip(blocks, a)]
    return h1, [_dot(h.astype(BF16), wxq_ref[...]).astype(BF16) for h in h1]


def _mix_body(os_ref, om_ref, x_ref, gm_ref, wo_ref, lng_ref, lnb_ref, wxq_ref, h1_ref, qx_ref):
    blocks = _row_blocks(x_ref)
    oss = [os_ref[r, :] for r in blocks]
    h1, qx = _mix_head(blocks, oss, om_ref, x_ref, gm_ref, wo_ref, lng_ref, lnb_ref, wxq_ref)
    for r, h, q in zip(blocks, h1, qx):
        h1_ref[r, :] = h
        qx_ref[r, :] = q


def _mix(o_ssm, o_mla, x2, g_mla, w_o, ln_g, ln_b, w_xq, tn):
    n = x2.shape[0]
    row = lambda w: pl.BlockSpec((tn, w), lambda i: (i, 0))
    return pl.pallas_call(
        _mix_body,
        grid=(n // tn,),
        in_specs=[row(SSM_WIDTH), row(MLA_WIDTH), row(D_MODEL), _const(g_mla.shape), _const(w_o.shape),
                  _const(ln_g.shape), _const(ln_b.shape), _const(w_xq.shape)],
        out_specs=[row(D_MODEL), row(D_MODEL)],
        out_shape=[jax.ShapeDtypeStruct((n, D_MODEL), F32), jax.ShapeDtypeStruct((n, D_MODEL), BF16)],
        compiler_params=_params("parallel"),
        name="mix",
    )(o_ssm, o_mla, x2, g_mla, w_o, ln_g, ln_b, w_xq)


def _mem_attn_heads(qs, mks, mvs):
    ss = [_dot_nt(q, mk.astype(BF16)) * (X_HEAD_DIM ** -0.5) for q, mk in zip(qs, mks)]
    es = [jnp.exp(s - jnp.max(s, -1, keepdims=True)) for s in ss]
    ps = [(e / jnp.sum(e, -1, keepdims=True)).astype(BF16) for e in es]
    return [_dot(p, mv.astype(BF16)).astype(BF16) for p, mv in zip(ps, mvs)]


def _mix_mem_body(y_ref, u_ref, d_ref, wglu_ref, gs_ref, om_ref, x_ref, gm_ref, wo_ref, lng_ref, lnb_ref, wxq_ref,
                  mk_ref, mv_ref, wxo_ref, h2_ref):
    blocks = _row_blocks(x_ref)
    oss = _glu_rows([y_ref[r, :] for r in blocks], [u_ref[r, :] for r in blocks], d_ref, wglu_ref, gs_ref)
    h1, qx = _mix_head(blocks, oss, om_ref, x_ref, gm_ref, wo_ref, lng_ref, lnb_ref, wxq_ref)
    cols = [slice(h * X_HEAD_DIM, (h + 1) * X_HEAD_DIM) for h in range(X_HEADS)]
    heads = _mem_attn_heads([q[:, c] for q in qx for c in cols], [mk_ref[:, c] for _ in qx for c in cols],
                            [mv_ref[:, c] for _ in qx for c in cols])
    ox = [jnp.concatenate(heads[i * X_HEADS:(i + 1) * X_HEADS], axis=1) for i in range(len(blocks))]
    att = [_dot(o, wxo_ref[...]) for o in ox]
    for r, h, a in zip(blocks, h1, att):
        h2_ref[r, :] = _ln(ALPHA * h + a, lng_ref[1:2, :], lnb_ref[1:2, :])


def _mix_mem(y_ssm, u, d_skip, w_glu, g_ssm, o_mla, x2, g_mla, w_o, ln_g, ln_b, w_xq, mem_k, mem_v, w_xo, tn):
    n = x2.shape[0]
    row = lambda w: pl.BlockSpec((tn, w), lambda i: (i, 0))
    glu_consts = (d_skip, w_glu, g_ssm)
    consts = (g_mla, w_o, ln_g, ln_b, w_xq, mem_k, mem_v, w_xo)
    return pl.pallas_call(
        _mix_mem_body,
        grid=(n // tn,),
        in_specs=[row(SSM_WIDTH), row(SSM_WIDTH)] + [_const(c.shape) for c in glu_consts]
        + [row(MLA_WIDTH), row(D_MODEL)] + [_const(c.shape) for c in consts],
        out_specs=row(D_MODEL),
        out_shape=jax.ShapeDtypeStruct((n, D_MODEL), F32),
        compiler_params=_params("parallel"),
        name="mix_mem",
    )(y_ssm, u, *glu_consts, o_mla, x2, *consts)


def _mem_attn_cache_body(qx_ref, mk_hbm, mv_hbm, o_ref, kbuf, vbuf, sems, *, seq, nsub):
    b = pl.program_id(0)
    nb = pl.num_programs(0)

    def copies(step, slot):
        return [pltpu.make_async_copy(src.at[step * nsub + j, :, h, :], buf.at[slot, j, h], sems.at[slot, j, t, h])
                for t, (src, buf) in enumerate(((mk_hbm, kbuf), (mv_hbm, vbuf)))
                for j in range(nsub) for h in range(X_HEADS)]

    @pl.when(b == 0)
    def _():
        for c in copies(0, 0):
            c.start()

    @pl.when(b + 1 < nb)
    def _():
        for c in copies(b + 1, (b + 1) % 2):
            c.start()

    slot = b % 2
    for c in copies(b, slot):
        c.wait()
    jh = [(j, h) for j in range(nsub) for h in range(X_HEADS)]
    where = [(slice(j * seq, (j + 1) * seq), slice(h * X_HEAD_DIM, (h + 1) * X_HEAD_DIM)) for j, h in jh]
    outs = _mem_attn_heads([qx_ref[r, c] for r, c in where], [kbuf[slot, j, h] for j, h in jh],
                           [vbuf[slot, j, h] for j, h in jh])
    for (r, c), o in zip(where, outs):
        o_ref[r, c] = o


def _mem_attn_cache(qx, mem_k, mem_v, seq, nsub):
    n = qx.shape[0]
    nb = mem_k.shape[0]
    assert nb % nsub == 0
    row = pl.BlockSpec((nsub * seq, D_MODEL), lambda b: (b, 0))
    hbm = pl.BlockSpec(memory_space=pl.ANY)
    buf = pltpu.VMEM((2, nsub, X_HEADS, N_MEM, X_HEAD_DIM), F32)
    return pl.pallas_call(
        functools.partial(_mem_attn_cache_body, seq=seq, nsub=nsub),
        grid=(nb // nsub,),
        in_specs=[row, hbm, hbm],
        out_specs=row,
        out_shape=jax.ShapeDtypeStruct((n, D_MODEL), BF16),
        scratch_shapes=[buf, buf, pltpu.SemaphoreType.DMA((2, nsub, 2, X_HEADS))],
        compiler_params=_params("arbitrary"),
        name="mem_attn_cache",
    )(qx, mem_k, mem_v)


def _mlp_body(*refs, ff_blk, second_norm):
    if second_norm:
        h1_ref, ox_ref, wxo_ref, lng_ref, lnb_ref, w1_ref, w2_ref, y_ref = refs
        h2 = _ln(ALPHA * h1_ref[...] + _dot(ox_ref[...], wxo_ref[...]), lng_ref[1:2, :], lnb_ref[1:2, :])
    else:
        h2_ref, lng_ref, lnb_ref, w1_ref, w2_ref, y_ref = refs
        h2 = h2_ref[...]
    hb = h2.astype(BF16)
    acc = jnp.zeros(h2.shape, F32)
    for c in range(0, D_FF, ff_blk):
        z = jnp.maximum(_dot(hb, w1_ref[:, c:c + ff_blk]), 0.0)
        acc += _dot((z * z).astype(BF16), w2_ref[c:c + ff_blk, :])
    y_ref[...] = _ln(ALPHA * h2 + acc, lng_ref[2:3, :], lnb_ref[2:3, :])


def _mlp(h, ox, w_xo, ln_g, ln_b, w1, w2, tn):
    n = h.shape[0]
    row = pl.BlockSpec((tn, D_MODEL), lambda i: (i, 0))
    second_norm = ox is not None
    rows = (h, ox) if second_norm else (h,)
    consts = ((w_xo,) if second_norm else ()) + (ln_g, ln_b, w1, w2)
    return pl.pallas_call(
        functools.partial(_mlp_body, ff_blk=MLP_FF_BLOCK, second_norm=second_norm),
        grid=(n // tn,),
        in_specs=[row] * len(rows) + [_const(c.shape) for c in consts],
        out_specs=row,
        out_shape=jax.ShapeDtypeStruct((n, D_MODEL), F32),
        compiler_params=_params("parallel"),
        name="mlp",
    )(*rows, *consts)


def _mem_kv_body(mem_ref, wk_ref, wv_ref, k_ref, v_ref):
    m = mem_ref[...].astype(BF16)
    k_ref[...] = _dot(m, wk_ref[...])
    v_ref[...] = _dot(m, wv_ref[...])


def _mem_kv(mem2, wk, wv):
    n = mem2.shape[0]
    out = jax.ShapeDtypeStruct((n, D_MODEL), F32)
    return pl.pallas_call(
        _mem_kv_body,
        grid=(1,),
        in_specs=[_const(mem2.shape), _const(wk.shape), _const(wv.shape)],
        out_specs=[_const((n, D_MODEL))] * 2,
        out_shape=[out, out],
        compiler_params=_params("arbitrary"),
        name="mem_kv",
    )(mem2, wk, wv)


def _state_to_cols(s):
    return jnp.transpose(s.reshape(s.shape[0], S5_COLS, -1), (1, 0, 2))


def _state_from_cols(s):
    return jnp.transpose(s, (1, 0, 2)).reshape(s.shape[1], SSM_GROUPS, SSM_STATE)


def _layer(x2, pos, nseq, h0r, h0i, mem_k, mem_v, caches, wts, s5_ops, tn, scan_ct):
    n = x2.shape[0]
    seq = n // nseq
    prompt = caches is None
    u, q, k, v, ckv, kpe = _project(x2, pos, wts["inv"], wts["w_in"], wts["g_q"], wts["w_q"], wts["g_kv"],
                                    wts["w_k"], wts["w_vt"] if prompt else wts["w_v"], min(n, PROJ_TILE),
                                    v_transposed=prompt, consecutive=prompt, kpe_cache_shape=prompt)
    y_ssm, fr, fi = _s5_scan(u, s5_ops, _state_to_cols(h0r), _state_to_cols(h0i), nseq, scan_ct)
    glu_args = (y_ssm, u, wts["d_skip"], wts["w_glu"], wts["g_out_ssm"])
    if prompt:
        o_mla = _attention(q, k, v, ATTN_HEADS)
    else:
        o_mla = _attention_cached(q, k, v, caches[0], caches[1], wts["w_k"], wts["w_v"], seq, nsub=CACHE_SEQS)
    mix_args = (o_mla, x2, wts["g_out_mla"], wts["w_o"], wts["ln_g"], wts["ln_b"], wts["w_xq"])
    if prompt:
        h, ox = _mix_mem(*glu_args, *mix_args, mem_k, mem_v, wts["w_xo"], min(n, MIX_TILE)), None
    else:
        h, qx = _mix(_glu(*glu_args, tn), *mix_args, tn)
        ox = _mem_attn_cache(qx, mem_k, mem_v, seq, nsub=CACHE_SEQS)
    y = _mlp(h, ox, wts["w_xo"], wts["ln_g"], wts["ln_b"], wts["w_ff1"], wts["w_ff2"], tn)
    return y, ckv, kpe, _state_from_cols(fr), _state_from_cols(fi)


def kernel(x_prompt, x_sample, mem_prompt, cache_mla_ckv, cache_mla_kpe, state_ssm_re, state_ssm_im, cache_mem_k, cache_mem_v, w_in, g_q, w_q_up, g_kv, w_kv_up, a_re, a_im, b_re, b_im, c_re, c_im, d_skip, log_dt, w_glu, g_out_ssm, g_out_mla, w_o, w_xq, w_xk, w_xv, w_xo, w_ff1, w_ff2, ln_g, ln_b):
    assert w_in.shape[0] == DEPTH == 1
    nbp, sp, _ = x_prompt.shape
    nbs, sd, _ = x_sample.shape
    past = cache_mla_ckv.shape[2]
    assert nbp == 1

    wq = jnp.pad(w_q_up[0], ((0, 0), (0, 0), (0, QK_PAD - QK_NOPE - QK_ROPE)))
    wk = w_kv_up[0][:, :, :QK_NOPE].reshape(KV_LORA, -1).astype(BF16)
    wv = w_kv_up[0][:, :, QK_NOPE:].reshape(KV_LORA, -1).astype(BF16)
    inv = ROPE_THETA ** (-jnp.arange(ROPE_HALF, dtype=F32) / ROPE_HALF)
    wts = {
        "inv": jnp.tile(inv, LANES // ROPE_HALF).reshape(1, LANES),
        "w_in": jnp.pad(w_in[0], ((0, 0), (0, LANES - QK_ROPE))).astype(BF16),
        "g_q": g_q[0].reshape(1, -1),
        "w_q": wq.reshape(Q_LORA, MLA_HEADS * QK_PAD).astype(BF16),
        "g_kv": g_kv[0].reshape(1, -1),
        "w_k": wk,
        "w_v": wv,
        "w_vt": wv.T,
        "d_skip": d_skip[0].reshape(1, -1),
        "w_glu": w_glu[0].astype(BF16),
        "g_out_ssm": g_out_ssm[0].reshape(1, -1),
        "g_out_mla": g_out_mla[0].reshape(1, -1),
        "w_o": w_o[0].astype(BF16),
        "w_xq": w_xq[0].reshape(D_MODEL, D_MODEL).astype(BF16),
        "w_xo": w_xo[0].reshape(D_MODEL, D_MODEL).astype(BF16),
        "w_ff1": w_ff1[0].astype(BF16),
        "w_ff2": w_ff2[0].astype(BF16),
        "ln_g": ln_g[0],
        "ln_b": ln_b[0],
    }
    s5_ops = _s5_prep(a_re[0], a_im[0], b_re[0], b_im[0], c_re[0], c_im[0], log_dt[0])

    mk, mv = _mem_kv(mem_prompt.reshape(nbp * N_MEM, D_MODEL),
                     w_xk[0].reshape(D_MODEL, D_MODEL).astype(BF16),
                     w_xv[0].reshape(D_MODEL, D_MODEL).astype(BF16))
    zero = jnp.zeros((nbp, SSM_GROUPS, SSM_STATE), F32)
    pos_p = jnp.arange(sp, dtype=F32).reshape(sp, 1)
    yp, ckv_p, kpe_p, sre_p, sim_p = _layer(
        x_prompt.reshape(sp, D_MODEL), pos_p, nbp, zero, zero, mk, mv, None,
        wts, s5_ops, tn=TOKEN_TILE, scan_ct=min(sp // S5_T, SCAN_CHUNKS))

    pos_s = jnp.tile(past + jnp.arange(sd, dtype=F32), nbs).reshape(nbs * sd, 1)
    caches = (cache_mla_ckv[0], cache_mla_kpe[0])
    ys, ckv_s, kpe_s, sre_s, sim_s = _layer(
        x_sample.reshape(nbs * sd, D_MODEL), pos_s, nbs,
        state_ssm_re[0], state_ssm_im[0], cache_mem_k[0], cache_mem_v[0], caches,
        wts, s5_ops, tn=TOKEN_TILE, scan_ct=sd // S5_T)

    return (yp.reshape(nbp, sp, D_MODEL), ys.reshape(nbs, sd, D_MODEL),
            ckv_p.reshape(1, nbp, sp, KV_LORA), kpe_p.reshape(1, nbp, sp, QK_ROPE),
            sre_p.reshape(1, nbp, SSM_GROUPS, SSM_STATE), sim_p.reshape(1, nbp, SSM_GROUPS, SSM_STATE),
            mk.reshape(1, nbp, N_MEM, X_HEADS, X_HEAD_DIM), mv.reshape(1, nbp, N_MEM, X_HEADS, X_HEAD_DIM),
            ckv_s.reshape(1, nbs, sd, KV_LORA), kpe_s.reshape(1, nbs, sd, QK_ROPE),
            sre_s.reshape(1, nbs, SSM_GROUPS, SSM_STATE), sim_s.reshape(1, nbs, SSM_GROUPS, SSM_STATE))
```

```python
import functools
import math

import jax
import jax.numpy as jnp
from jax import lax
from jax.experimental import pallas as pl
from jax.experimental.pallas import tpu as pltpu

F32 = jnp.float32
BF16 = jnp.bfloat16

D_MODEL = 1024
DEPTH = 1
CHUNK = 64
SSM_WIDTH = 512
SSM_GROUP = 16
SSM_GROUPS = 32
SSM_STATE = 64
MLA_HEADS = 4
QK_NOPE = 128
QK_ROPE = 64
V_HEAD = 128
MLA_WIDTH = MLA_HEADS * V_HEAD
Q_LORA = 384
KV_LORA = 256
ROPE_THETA = 10000.0
MLA_SCALE = (QK_NOPE + QK_ROPE) ** -0.5
N_MEM = 256
X_HEADS = 4
X_HEAD_DIM = D_MODEL // X_HEADS
D_FF = 4 * D_MODEL
ALPHA = (2 * DEPTH) ** 0.25
EPS = 1e-5
NEG_INF = -1e30

LANES = 128
MXU_DEPTH = 256
VMEM_BYTES = 64 * 1024 * 1024
VMEM_LIMIT = VMEM_BYTES - 8 * 1024 * 1024

QK_PAD = MXU_DEPTH
S5_T = MXU_DEPTH // SSM_GROUP
S5_COL_GROUPS = LANES // SSM_GROUP
S5_COLS = SSM_GROUPS // S5_COL_GROUPS
S5_SW = S5_COL_GROUPS * SSM_STATE
ROPE_HALF = QK_ROPE // 2
Q_SCALE = MLA_SCALE * math.log2(math.e)
ATTN_T = 512
ATTN_HEADS = 2
ATTN_TQ = 1024
ATTN_QSPLIT = 4
MLP_FF_BLOCK = 1024
CACHE_SEQS = 4
MIX_TILE = 1024
MIX_SPLIT = 2
PROJ_TILE = 1024
PROJ_ROWS = 512
TOKEN_TILE = 512
SCAN_CHUNKS = 512

_NT = (((1,), (1,)), ((), ()))


def _rms(x, g):
    return x * lax.rsqrt(jnp.mean(x * x, -1, keepdims=True) + EPS) * g


def _ln(x, g, b):
    mu = jnp.mean(x, -1, keepdims=True)
    xc = x - mu
    var = jnp.mean(xc * xc, -1, keepdims=True)
    return xc * lax.rsqrt(var + EPS) * g + b


def _dot(a, b):
    return jnp.dot(a, b, preferred_element_type=F32)


def _dot_nt(a, b):
    return lax.dot_general(a, b, _NT, preferred_element_type=F32)


def _split(a):
    hi = a.astype(BF16)
    return hi, (a - hi.astype(F32)).astype(BF16)


def _dot_split(a, b):
    return _dot(a[0], b[0]) + _dot(a[0], b[1]) + _dot(a[1], b[0])


def _params(*sem):
    return pltpu.CompilerParams(dimension_semantics=sem, vmem_limit_bytes=VMEM_LIMIT)


def _const(shape):
    n = len(shape)
    return pl.BlockSpec(shape, lambda *_: (0,) * n)


def _proj_body(x_ref, pos_ref, inv_ref, w_in_ref, gq_ref, wq_ref, gkv_ref, wk_ref, wv_ref,
               u_ref, q_ref, k_ref, v_ref, ckv_ref, kpe_ref, tc_ref, ts_ref, *, v_transposed, consecutive):
    rb = PROJ_ROWS
    assert x_ref.shape[0] % rb == 0 and ATTN_T % rb == 0
    blocks = [slice(r0, r0 + rb) for r0 in range(0, x_ref.shape[0], rb)]
    c0 = SSM_WIDTH + Q_LORA
    projs = [_dot(x_ref[r, :].astype(BF16), w_in_ref[...]) for r in blocks]
    cqs = [_rms(p[:, SSM_WIDTH:c0], gq_ref[...]).astype(BF16) for p in projs]
    ckvs = [_rms(p[:, c0:c0 + KV_LORA], gkv_ref[...]) for p in projs]
    qs = [_dot(c, wq_ref[...]) * Q_SCALE for c in cqs]
    kns = [_dot(c.astype(BF16), wk_ref[...]) for c in ckvs]
    for r, p, ckv in zip(blocks, projs, ckvs):
        u_ref[r, :] = p[:, :SSM_WIDTH]
        ckv_ref[r, :] = ckv
        if v_transposed:
            lanes = slice(r.start % ATTN_T, r.start % ATTN_T + rb)
            v_ref[r.start // ATTN_T, :, lanes] = _dot_nt(wv_ref[...], ckv.astype(BF16)).astype(BF16)
        else:
            v_ref[r, :] = _dot(ckv.astype(BF16), wv_ref[...]).astype(BF16)

    lane = lax.broadcasted_iota(jnp.int32, (1, LANES), 1)
    live = lane < QK_ROPE
    sign = jnp.where(lane < ROPE_HALF, -1.0, 1.0)
    inv = inv_ref[...]
    if consecutive:
        @pl.when(pl.program_id(0) == 0)
        def _():
            r = lax.broadcasted_iota(jnp.int32, (x_ref.shape[0], 1), 0).astype(F32)
            tc_ref[...] = jnp.where(live, jnp.cos(r * inv), 0.0)
            ts_ref[...] = jnp.where(live, jnp.sin(r * inv), 0.0)

        base = pos_ref[0:1, :] * inv
        ca, sa = jnp.cos(base), jnp.sin(base)
        cos_ts = [ca * tc_ref[r, :] - sa * ts_ref[r, :] for r in blocks]
        sin_ts = [(sa * sign) * tc_ref[r, :] + (ca * sign) * ts_ref[r, :] for r in blocks]
    else:
        angs = [pos_ref[r, :] * inv for r in blocks]
        cos_ts = [jnp.where(live, jnp.cos(a), 0.0) for a in angs]
        sin_ts = [jnp.where(live, jnp.sin(a) * sign, 0.0) for a in angs]

    def rope(c2, cos_t, sin_t):
        swapped = jnp.where(lane < ROPE_HALF, pltpu.roll(c2, LANES - ROPE_HALF, 1), pltpu.roll(c2, ROPE_HALF, 1))
        return c2 * cos_t + swapped * sin_t

    for r, p, q, kn, cos_t, sin_t in zip(blocks, projs, qs, kns, cos_ts, sin_ts):
        kpe = rope(p[:, c0 + KV_LORA:], cos_t, sin_t)
        kpe_ref[r, :] = kpe[:, :QK_ROPE]
        kpe_b = kpe.astype(BF16)
        for h in range(MLA_HEADS):
            a = h * QK_PAD
            q_ref[r, a:a + QK_NOPE] = q[:, a:a + QK_NOPE].astype(BF16)
            q_ref[r, a + QK_NOPE:a + QK_PAD] = rope(q[:, a + QK_NOPE:a + QK_PAD], cos_t, sin_t).astype(BF16)
            k_ref[r, a:a + QK_NOPE] = kn[:, h * QK_NOPE:(h + 1) * QK_NOPE].astype(BF16)
            k_ref[r, a + QK_NOPE:a + QK_PAD] = kpe_b


def _project(x2, pos, inv, w_in, gq, wq, gkv, wk, wv, tn, v_transposed, consecutive, kpe_cache_shape):
    n = x2.shape[0]
    row = lambda w: pl.BlockSpec((tn, w), lambda i: (i, 0))
    if kpe_cache_shape:
        kpe_spec = pl.BlockSpec((None, None, tn, QK_ROPE), lambda i: (0, 0, i, 0))
        kpe_shape = jax.ShapeDtypeStruct((1, 1, n, QK_ROPE), F32)
    else:
        kpe_spec, kpe_shape = row(QK_ROPE), jax.ShapeDtypeStruct((n, QK_ROPE), F32)
    if v_transposed:
        assert tn % ATTN_T == 0
        v_spec = pl.BlockSpec((tn // ATTN_T, MLA_WIDTH, ATTN_T), lambda i: (i, 0, 0))
        v_shape = jax.ShapeDtypeStruct((n // ATTN_T, MLA_WIDTH, ATTN_T), BF16)
    else:
        v_spec, v_shape = row(MLA_WIDTH), jax.ShapeDtypeStruct((n, MLA_WIDTH), BF16)
    return pl.pallas_call(
        functools.partial(_proj_body, v_transposed=v_transposed, consecutive=consecutive),
        grid=(n // tn,),
        scratch_shapes=[pltpu.VMEM((tn, LANES), F32)] * 2,
        in_specs=[row(D_MODEL), row(1), _const(inv.shape), _const(w_in.shape), _const(gq.shape),
                  _const(wq.shape), _const(gkv.shape), _const(wk.shape), _const(wv.shape)],
        out_specs=[row(SSM_WIDTH), row(MLA_HEADS * QK_PAD), row(MLA_HEADS * QK_PAD),
                   v_spec, row(KV_LORA), kpe_spec],
        out_shape=[jax.ShapeDtypeStruct((n, SSM_WIDTH), F32),
                   jax.ShapeDtypeStruct((n, MLA_HEADS * QK_PAD), BF16),
                   jax.ShapeDtypeStruct((n, MLA_HEADS * QK_PAD), BF16),
                   v_shape,
                   jax.ShapeDtypeStruct((n, KV_LORA), F32),
                   kpe_shape],
        compiler_params=_params("arbitrary"),
        name="proj",
    )(x2, pos, inv, w_in, gq, wq, gkv, wk, wv)


def _s5_prep_body(ar_row, ai_row, ldt_row, bt_re, bt_im, ct_re, ct_im,
                  wt_ref, v_ref, z_ref, are_ref, aim_ref):
    sw = S5_SW
    arr, air, dtr = ar_row[0], ai_row[0], jnp.exp(ldt_row[0])

    mag = jnp.exp(arr * dtr)
    lr, li = mag * jnp.cos(air * dtr), mag * jnp.sin(air * dtr)
    nr, ni = lr - 1.0, li
    den = arr * arr + air * air
    f_re, f_im = (nr * arr + ni * air) / den, (ni * arr - nr * air) / den

    same_b = (lax.broadcasted_iota(jnp.int32, (LANES, sw), 0) // SSM_GROUP
              == lax.broadcasted_iota(jnp.int32, (LANES, sw), 1) // SSM_STATE)
    br = jnp.where(same_b, bt_re[0], 0.0)
    bi = jnp.where(same_b, bt_im[0], 0.0)
    bb_re = f_re * br - f_im * bi
    bb_im = f_re * bi + f_im * br
    same_c = (lax.broadcasted_iota(jnp.int32, (sw, LANES), 0) // SSM_STATE
              == lax.broadcasted_iota(jnp.int32, (sw, LANES), 1) // SSM_GROUP)
    cr = jnp.where(same_c, ct_re[0], 0.0)
    ci = jnp.where(same_c, ct_im[0], 0.0)
    cr_s, ci_s = _split(cr), _split(ci)

    e = lax.broadcasted_iota(jnp.int32, (2 * S5_T, 1), 0).astype(F32)
    pm = jnp.exp(arr * dtr * e)
    pw_re, pw_im = pm * jnp.cos(air * dtr * e), pm * jnp.sin(air * dtr * e)
    pt_re, pt_im = pw_re.T, pw_im.T

    lag_ops = []
    for lag in range(S5_T):
        p_re, p_im = pw_re[lag:lag + 1, :], pw_im[lag:lag + 1, :]
        k_re = p_re * bb_re - p_im * bb_im
        k_im = p_re * bb_im + p_im * bb_re
        i = S5_T - 1 - lag
        v_ref[0, i * LANES:(i + 1) * LANES, :sw] = k_re.astype(BF16)
        v_ref[0, i * LANES:(i + 1) * LANES, sw:] = k_im.astype(BF16)
        lag_ops.append(_dot_split(_split(k_re), cr_s) - _dot_split(_split(k_im), ci_s))

        q_re, q_im = pt_re[:, lag + 1:lag + 2], pt_im[:, lag + 1:lag + 2]
        z_ref[0, :sw, lag * LANES:(lag + 1) * LANES] = (cr * q_re - ci * q_im).astype(BF16)
        z_ref[0, sw:, lag * LANES:(lag + 1) * LANES] = (-(cr * q_im + ci * q_re)).astype(BF16)

    zero = jnp.zeros((LANES, LANES), BF16)
    for d in range(S5_T // 2):
        wt_ref[0, d, :LANES, :LANES] = lag_ops[2 * d].astype(BF16)
        wt_ref[0, d, :LANES, LANES:] = lag_ops[2 * d + 1].astype(BF16)
        wt_ref[0, d, LANES:, :LANES] = lag_ops[2 * d - 1].astype(BF16) if d else zero
        wt_ref[0, d, LANES:, LANES:] = lag_ops[2 * d].astype(BF16)

    are_ref[0] = pw_re[S5_T:S5_T + 1, :]
    aim_ref[0] = pw_im[S5_T:S5_T + 1, :]


def _s5_prep(a_re, a_im, b_re, b_im, c_re, c_im, log_dt):
    nc, r, t, h, p, sw = S5_COLS, S5_COL_GROUPS, S5_T, SSM_GROUP, SSM_STATE, S5_SW
    ldt = jnp.repeat(log_dt, p)
    bt = lambda b: jnp.tile(jnp.transpose(b.reshape(nc, r, p, h), (0, 1, 3, 2)).reshape(nc, r * h, p), (1, 1, r))
    ct = lambda c: jnp.tile(jnp.transpose(c.reshape(nc, r, h, p), (0, 1, 3, 2)).reshape(nc, r * p, h), (1, 1, r))
    args = (a_re.reshape(nc, 1, sw), a_im.reshape(nc, 1, sw), ldt.reshape(nc, 1, sw),
            bt(b_re), bt(b_im), ct(c_re), ct(c_im))
    blk = lambda s: pl.BlockSpec((1,) + s[1:], lambda i: (i,) + (0,) * (len(s) - 1))
    outs = [((nc, t // 2, 2 * LANES, 2 * LANES), BF16), ((nc, t * LANES, 2 * sw), BF16),
            ((nc, 2 * sw, t * LANES), BF16), ((nc, 1, sw), F32), ((nc, 1, sw), F32)]
    return pl.pallas_call(
        _s5_prep_body,
        grid=(nc,),
        in_specs=[blk(a.shape) for a in args],
        out_specs=[blk(s) for s, _ in outs],
        out_shape=[jax.ShapeDtypeStruct(s, d) for s, d in outs],
        compiler_params=_params("parallel"),
        name="s5_prep",
    )(*args)


def _s5_scan_body(u_ref, wt_ref, v_ref, z_ref, are_ref, aim_ref, h0r_ref, h0i_ref,
                  y_ref, fr_ref, fi_ref, lhs, sre, sim, xre, xim, cr, ci, *, nseq, ct):
    t = pl.program_id(1)
    sw = S5_SW
    seq = ct * S5_T

    @pl.when(t == 0)
    def _():
        cr[...] = h0r_ref[0]
        ci[...] = h0i_ref[0]

    def token_rows(cc, i):
        if nseq == 1:
            return pl.ds(i, ct, stride=S5_T), slice(None)
        return pl.ds(cc * S5_T + i, nseq, stride=seq), slice(cc * nseq, (cc + 1) * nseq)

    for cc in range(1 if nseq == 1 else ct):
        for i in range(S5_T):
            tok, crow = token_rows(cc, i)
            lhs[crow, i * LANES:(i + 1) * LANES] = u_ref[tok, :].astype(BF16)

    s = _dot(lhs[...], v_ref[0])
    sre[...] = s[:, :sw]
    sim[...] = s[:, sw:]

    a_re, a_im = are_ref[0], aim_ref[0]

    def step(c, carry):
        x_re, x_im = carry
        r = pl.ds(c * nseq, nseq)
        xre[r, :] = x_re
        xim[r, :] = x_im
        n_re = a_re * x_re - a_im * x_im + sre[r, :]
        n_im = a_re * x_im + a_im * x_re + sim[r, :]
        return n_re, n_im

    x_re, x_im = lax.fori_loop(0, ct, step, (cr[...], ci[...]))
    cr[...] = x_re
    ci[...] = x_im
    fr_ref[0] = x_re
    fi_ref[0] = x_im

    xb_re, xb_im = xre[...].astype(BF16), xim[...].astype(BF16)
    for jp in range(S5_T // 2):
        cols = slice(jp * 2 * LANES, (jp + 1) * 2 * LANES)
        acc = _dot(xb_re, z_ref[0, :sw, cols]) + _dot(xb_im, z_ref[0, sw:, cols])
        for ip in range(jp + 1):
            acc += _dot(lhs[:, ip * 2 * LANES:(ip + 1) * 2 * LANES], wt_ref[0, jp - ip])
        for cc in range(1 if nseq == 1 else ct):
            for jj in range(2):
                tok, crow = token_rows(cc, 2 * jp + jj)
                y_ref[tok, :] = acc[crow, jj * LANES:(jj + 1) * LANES]


def _s5_scan(u, col_ops, h0r, h0i, nseq, ct):
    wt, v, z, are, aim = col_ops
    n = u.shape[0]
    cps = n // nseq // S5_T
    assert nseq == 1 or ct == cps
    nc = nseq * ct
    sw = S5_SW
    tile = pl.BlockSpec((nc * S5_T, LANES), lambda c, t: (t, c))
    op = lambda a: pl.BlockSpec((1,) + a.shape[1:], lambda c, t: (c,) + (0,) * (a.ndim - 1))
    st = pl.BlockSpec((1, nseq, sw), lambda c, t: (c, 0, 0))
    st_shape = jax.ShapeDtypeStruct((S5_COLS, nseq, sw), F32)
    return pl.pallas_call(
        functools.partial(_s5_scan_body, nseq=nseq, ct=ct),
        grid=(S5_COLS, cps // ct),
        in_specs=[tile, op(wt), op(v), op(z), op(are), op(aim), st, st],
        out_specs=[tile, st, st],
        out_shape=[jax.ShapeDtypeStruct(u.shape, F32), st_shape, st_shape],
        scratch_shapes=[pltpu.VMEM((nc, S5_T * LANES), BF16)] + [pltpu.VMEM((nc, sw), F32)] * 4
        + [pltpu.VMEM((nseq, sw), F32)] * 2,
        compiler_params=_params("arbitrary", "arbitrary"),
        name="s5_scan",
    )(u, wt, v, z, are, aim, h0r, h0i)


def _glu_rows(ys, us, d_ref, w_ref, g_ref):
    acts = [jax.nn.gelu(y + d_ref[...] * u).astype(BF16) for y, u in zip(ys, us)]
    gls = [_dot(a, w_ref[...]) for a in acts]
    os = [gl[:, :SSM_WIDTH] * jax.nn.sigmoid(gl[:, SSM_WIDTH:]) for gl in gls]
    return [_rms(o, g_ref[...]).astype(BF16) for o in os]


def _glu_body(y_ref, u_ref, d_ref, w_ref, g_ref, o_ref):
    o_ref[...] = _glu_rows([y_ref[...]], [u_ref[...]], d_ref, w_ref, g_ref)[0]


def _glu(y, u, d, w, g, tn):
    n = y.shape[0]
    row = pl.BlockSpec((tn, SSM_WIDTH), lambda i: (i, 0))
    return pl.pallas_call(
        _glu_body,
        grid=(n // tn,),
        in_specs=[row, row, _const(d.shape), _const(w.shape), _const(g.shape)],
        out_specs=row,
        out_shape=jax.ShapeDtypeStruct((n, SSM_WIDTH), BF16),
        compiler_params=_params("parallel"),
        name="glu",
    )(y, u, d, w, g)


def _attn_body(q_ref, k_ref, vt_ref, o_ref, s0, s1, mt0, mt1, m_sc, l_sc, acc_sc, *, heads):
    i = pl.program_id(1)
    t = ATTN_T
    m_sc[...] = jnp.full(m_sc.shape, -jnp.inf, F32)
    l_sc[...] = jnp.zeros(l_sc.shape, F32)
    acc_sc[...] = jnp.zeros(acc_sc.shape, F32)

    qb = q_ref.shape[0] // ATTN_QSPLIT
    units = [(h, c * qb) for c in range(ATTN_QSPLIT) for h in range(heads)]

    def scores(tile, s_buf, mt_buf, unit, diag=None):
        h, c0 = unit
        r = pl.ds(pl.multiple_of(tile * t, t), t)
        s = _dot_nt(k_ref[r, h * QK_PAD:(h + 1) * QK_PAD],
                    q_ref[c0:c0 + qb, h * QK_PAD:(h + 1) * QK_PAD])
        if diag is not None:
            kc = (diag * t + lax.broadcasted_iota(jnp.int32, s.shape, 0)) // CHUNK
            qc = (c0 + lax.broadcasted_iota(jnp.int32, s.shape, 1)) // CHUNK
            s = jnp.where(kc <= qc, s, NEG_INF)
        s_buf[h, :, c0:c0 + qb] = s
        mt_buf[h, :, c0:c0 + qb] = jnp.max(s, 0, keepdims=True)

    def absorb(tile, s_buf, mt_buf, unit):
        h, c0 = unit
        cols = slice(c0, c0 + qb)
        m_prev = m_sc[h, :, cols]
        m_new = jnp.maximum(m_prev, mt_buf[h, :, cols])
        p = jnp.exp2(s_buf[h, :, cols] - m_new)
        alpha = jnp.exp2(m_prev - m_new)
        l_sc[h, :, cols] = alpha * l_sc[h, :, cols] + jnp.sum(p, 0, keepdims=True)
        m_sc[h, :, cols] = m_new
        pv = _dot(vt_ref[tile, h * V_HEAD:(h + 1) * V_HEAD, :], p.astype(BF16))
        acc_sc[h, :, cols] = alpha * acc_sc[h, :, cols] + pv

    nd = q_ref.shape[0] // t
    bufs = ((s0, mt0), (s1, mt1))
    partner = units[heads:] + units[:heads]
    for u in units:
        scores(nd * i, *bufs[0], u, diag=0)
    for d in range(1, nd):
        for u, v in zip(units, partner):
            scores(nd * i + d, *bufs[d % 2], u, diag=d)
            absorb(nd * i + d - 1, *bufs[(d - 1) % 2], v)

    def pair(jj, c):
        prev = jnp.where(jj == 0, nd * i + nd - 1, 2 * jj - 1)
        for u, v in zip(units, partner):
            scores(2 * jj, s0, mt0, u)
            absorb(prev, s1, mt1, v)
        for u, v in zip(units, partner):
            scores(2 * jj + 1, s1, mt1, u)
            absorb(2 * jj, s0, mt0, v)
        return c

    lax.fori_loop(0, (nd // 2) * i, pair, 0)
    last = jnp.where(i == 0, nd - 1, nd * i - 1)
    for u in units:
        absorb(last, s1, mt1, u)

    for h in range(heads):
        o_ref[:, h * V_HEAD:(h + 1) * V_HEAD] = (acc_sc[h] / l_sc[h]).T


def _attention(q, k, vt, heads):
    n = q.shape[0]
    t = ATTN_T
    tq = ATTN_TQ
    assert t % CHUNK == 0 and n % tq == 0 and MLA_HEADS % heads == 0 and (tq // t) % 2 == 0
    once = pl.Buffered(1)
    return pl.pallas_call(
        functools.partial(_attn_body, heads=heads),
        grid=(MLA_HEADS // heads, n // tq),
        in_specs=[pl.BlockSpec((tq, heads * QK_PAD), lambda g, i: (i, g)),
                  pl.BlockSpec((n, heads * QK_PAD), lambda g, i: (0, g), pipeline_mode=once),
                  pl.BlockSpec((n // t, heads * V_HEAD, t), lambda g, i: (0, g, 0), pipeline_mode=once)],
        out_specs=pl.BlockSpec((tq, heads * V_HEAD), lambda g, i: (i, g)),
        out_shape=jax.ShapeDtypeStruct((n, MLA_WIDTH), F32),
        scratch_shapes=[pltpu.VMEM((heads, t, tq), F32)] * 2 + [pltpu.VMEM((heads, 1, tq), F32)] * 4
        + [pltpu.VMEM((heads, V_HEAD, tq), F32)],
        compiler_params=_params("arbitrary", "arbitrary"),
        name="attn",
    )(q, k, vt)


def _attn_cached_body(q_ref, kn_ref, vn_ref, ckv_ref, kpe_ref, wk_ref, wv_ref, o_ref, *, past, seq, nsub):
    heads = range(MLA_HEADS)
    pos = past + lax.broadcasted_iota(jnp.int32, (seq, 1), 0)
    qc = jnp.concatenate([pos] * MLA_HEADS, axis=0) // CHUNK
    kc_past = lax.broadcasted_iota(jnp.int32, (1, past), 1) // CHUNK
    kc_new = (past + lax.broadcasted_iota(jnp.int32, (1, seq), 1)) // CHUNK
    seqs = range(nsub)
    rows = [slice(b * seq, (b + 1) * seq) for b in seqs]
    ckv = [ckv_ref[b].astype(BF16) for b in seqs]
    kpe = [kpe_ref[b].astype(BF16) for b in seqs]
    qa = [jnp.concatenate(
        [_dot_nt(q_ref[r, h * QK_PAD:h * QK_PAD + QK_NOPE], wk_ref[:, h * QK_NOPE:(h + 1) * QK_NOPE])
         for h in heads], axis=0).astype(BF16) for r in rows]
    qpe = [jnp.concatenate(
        [q_ref[r, h * QK_PAD + QK_NOPE:h * QK_PAD + QK_NOPE + QK_ROPE] for h in heads], axis=0) for r in rows]
    s_p = [_dot_nt(qa[b], ckv[b]) + _dot_nt(qpe[b], kpe[b]) for b in seqs]
    s_n = [jnp.concatenate(
        [_dot_nt(q_ref[r, h * QK_PAD:(h + 1) * QK_PAD], kn_ref[r, h * QK_PAD:(h + 1) * QK_PAD])
         for h in heads], axis=0) for r in rows]
    s_p = [jnp.where(kc_past <= qc, s, NEG_INF) for s in s_p]
    s_n = [jnp.where(kc_new <= qc, s, NEG_INF) for s in s_n]
    m = [jnp.maximum(jnp.max(a, -1, keepdims=True), jnp.max(c, -1, keepdims=True)) for a, c in zip(s_p, s_n)]
    e_p = [jnp.exp2(s - mm) for s, mm in zip(s_p, m)]
    e_n = [jnp.exp2(s - mm) for s, mm in zip(s_n, m)]
    l = [jnp.sum(a, -1, keepdims=True) + jnp.sum(c, -1, keepdims=True) for a, c in zip(e_p, e_n)]
    p_n = [(e / ll).astype(BF16) for e, ll in zip(e_n, l)]
    o_lat = [_dot((e / ll).astype(BF16), c).astype(BF16) for e, ll, c in zip(e_p, l, ckv)]
    for b in seqs:
        for h in heads:
            hr = slice(h * seq, (h + 1) * seq)
            cols = slice(h * V_HEAD, (h + 1) * V_HEAD)
            o_ref[rows[b], cols] = _dot(o_lat[b][hr], wv_ref[:, cols]) + _dot(p_n[b][hr], vn_ref[rows[b], cols])


def _attention_cached(q, k, v, ckv_cache, kpe_cache, wk, wv, seq, nsub):
    n = q.shape[0]
    nb, past, _ = ckv_cache.shape
    assert nb % nsub == 0
    row = lambda w: pl.BlockSpec((nsub * seq, w), lambda b: (b, 0))
    kern = functools.partial(_attn_cached_body, past=past, seq=seq, nsub=nsub)
    return pl.pallas_call(
        kern,
        grid=(nb // nsub,),
        in_specs=[row(MLA_HEADS * QK_PAD), row(MLA_HEADS * QK_PAD), row(MLA_WIDTH),
                  pl.BlockSpec((nsub, past, KV_LORA), lambda b: (b, 0, 0)),
                  pl.BlockSpec((nsub, past, QK_ROPE), lambda b: (b, 0, 0)),
                  _const(wk.shape), _const(wv.shape)],
        out_specs=row(MLA_WIDTH),
        out_shape=jax.ShapeDtypeStruct((n, MLA_WIDTH), F32),
        compiler_params=_params("parallel"),
        name="attn_cached",
    )(q, k, v, ckv_cache, kpe_cache, wk, wv)


def _row_blocks(ref):
    rb = ref.shape[0] // MIX_SPLIT
    return [slice(i * rb, (i + 1) * rb) for i in range(MIX_SPLIT)]


def _mix_head(blocks, oss, om_ref, x_ref, gm_ref, wo_ref, lng_ref, lnb_ref, wxq_ref):
    om = [_rms(om_ref[r, :], gm_ref[...]).astype(BF16) for r in blocks]
    a = [_dot(s, wo_ref[:SSM_WIDTH, :]) + _dot(o, wo_ref[SSM_WIDTH:, :]) for s, o in zip(oss, om)]
    h1 = [_ln(ALPHA * x_ref[r, :] + aa, lng_ref[0:1, :], lnb_ref[0:1, :]) for r, aa in zip(blocks, a)]
    return h1, [_dot(h.astype(BF16), wxq_ref[...]).astype(BF16) for h in h1]


def _mix_body(os_ref, om_ref, x_ref, gm_ref, wo_ref, lng_ref, lnb_ref, wxq_ref, h1_ref, qx_ref):
    blocks = _row_blocks(x_ref)
    oss = [os_ref[r, :] for r in blocks]
    h1, qx = _mix_head(blocks, oss, om_ref, x_ref, gm_ref, wo_ref, lng_ref, lnb_ref, wxq_ref)
    for r, h, q in zip(blocks, h1, qx):
        h1_ref[r, :] = h
        qx_ref[r, :] = q


def _mix(o_ssm, o_mla, x2, g_mla, w_o, ln_g, ln_b, w_xq, tn):
    n = x2.shape[0]
    row = lambda w: pl.BlockSpec((tn, w), lambda i: (i, 0))
    return pl.pallas_call(
        _mix_body,
        grid=(n // tn,),
        in_specs=[row(SSM_WIDTH), row(MLA_WIDTH), row(D_MODEL), _const(g_mla.shape), _const(w_o.shape),
                  _const(ln_g.shape), _const(ln_b.shape), _const(w_xq.shape)],
        out_specs=[row(D_MODEL), row(D_MODEL)],
        out_shape=[jax.ShapeDtypeStruct((n, D_MODEL), F32), jax.ShapeDtypeStruct((n, D_MODEL), BF16)],
        compiler_params=_params("parallel"),
        name="mix",
    )(o_ssm, o_mla, x2, g_mla, w_o, ln_g, ln_b, w_xq)


def _mem_attn_heads(qs, mks, mvs):
    ss = [_dot_nt(q, mk.astype(BF16)) * (X_HEAD_DIM ** -0.5) for q, mk in zip(qs, mks)]
    es = [jnp.exp(s - jnp.max(s, -1, keepdims=True)) for s in ss]
    ps = [(e / jnp.sum(e, -1, keepdims=True)).astype(BF16) for e in es]
    return [_dot(p, mv.astype(BF16)).astype(BF16) for p, mv in zip(ps, mvs)]


def _mix_mem_body(y_ref, u_ref, d_ref, wglu_ref, gs_ref, om_ref, x_ref, gm_ref, wo_ref, lng_ref, lnb_ref, wxq_ref,
                  mk_ref, mv_ref, wxo_ref, h2_ref):
    blocks = _row_blocks(x_ref)
    oss = _glu_rows([y_ref[r, :] for r in blocks], [u_ref[r, :] for r in blocks], d_ref, wglu_ref, gs_ref)
    h1, qx = _mix_head(blocks, oss, om_ref, x_ref, gm_ref, wo_ref, lng_ref, lnb_ref, wxq_ref)
    cols = [slice(h * X_HEAD_DIM, (h + 1) * X_HEAD_DIM) for h in range(X_HEADS)]
    heads = _mem_attn_heads([q[:, c] for q in qx for c in cols], [mk_ref[:, c] for _ in qx for c in cols],
                            [mv_ref[:, c] for _ in qx for c in cols])
    ox = [jnp.concatenate(heads[i * X_HEADS:(i + 1) * X_HEADS], axis=1) for i in range(len(blocks))]
    att = [_dot(o, wxo_ref[...]) for o in ox]
    for r, h, a in zip(blocks, h1, att):
        h2_ref[r, :] = _ln(ALPHA * h + a, lng_ref[1:2, :], lnb_ref[1:2, :])


def _mix_mem(y_ssm, u, d_skip, w_glu, g_ssm, o_mla, x2, g_mla, w_o, ln_g, ln_b, w_xq, mem_k, mem_v, w_xo, tn):
    n = x2.shape[0]
    row = lambda w: pl.BlockSpec((tn, w), lambda i: (i, 0))
    glu_consts = (d_skip, w_glu, g_ssm)
    consts = (g_mla, w_o, ln_g, ln_b, w_xq, mem_k, mem_v, w_xo)
    return pl.pallas_call(
        _mix_mem_body,
        grid=(n // tn,),
        in_specs=[row(SSM_WIDTH), row(SSM_WIDTH)] + [_const(c.shape) for c in glu_consts]
        + [row(MLA_WIDTH), row(D_MODEL)] + [_const(c.shape) for c in consts],
        out_specs=row(D_MODEL),
        out_shape=jax.ShapeDtypeStruct((n, D_MODEL), F32),
        compiler_params=_params("parallel"),
        name="mix_mem",
    )(y_ssm, u, *glu_consts, o_mla, x2, *consts)


def _mem_attn_cache_body(qx_ref, mk_hbm, mv_hbm, o_ref, kbuf, vbuf, sems, *, seq, nsub):
    b = pl.program_id(0)
    nb = pl.num_programs(0)

    def copies(step, slot):
        return [pltpu.make_async_copy(src.at[step * nsub + j, :, h, :], buf.at[slot, j, h], sems.at[slot, j, t, h])
                for t, (src, buf) in enumerate(((mk_hbm, kbuf), (mv_hbm, vbuf)))
                for j in range(nsub) for h in range(X_HEADS)]

    @pl.when(b == 0)
    def _():
        for c in copies(0, 0):
            c.start()

    @pl.when(b + 1 < nb)
    def _():
        for c in copies(b + 1, (b + 1) % 2):
            c.start()

    slot = b % 2
    for c in copies(b, slot):
        c.wait()
    jh = [(j, h) for j in range(nsub) for h in range(X_HEADS)]
    where = [(slice(j * seq, (j + 1) * seq), slice(h * X_HEAD_DIM, (h + 1) * X_HEAD_DIM)) for j, h in jh]
    outs = _mem_attn_heads([qx_ref[r, c] for r, c in where], [kbuf[slot, j, h] for j, h in jh],
                           [vbuf[slot, j, h] for j, h in jh])
    for (r, c), o in zip(where, outs):
        o_ref[r, c] = o


def _mem_attn_cache(qx, mem_k, mem_v, seq, nsub):
    n = qx.shape[0]
    nb = mem_k.shape[0]
    assert nb % nsub == 0
    row = pl.BlockSpec((nsub * seq, D_MODEL), lambda b: (b, 0))
    hbm = pl.BlockSpec(memory_space=pl.ANY)
    buf = pltpu.VMEM((2, nsub, X_HEADS, N_MEM, X_HEAD_DIM), F32)
    return pl.pallas_call(
        functools.partial(_mem_attn_cache_body, seq=seq, nsub=nsub),
        grid=(nb // nsub,),
        in_specs=[row, hbm, hbm],
        out_specs=row,
        out_shape=jax.ShapeDtypeStruct((n, D_MODEL), BF16),
        scratch_shapes=[buf, buf, pltpu.SemaphoreType.DMA((2, nsub, 2, X_HEADS))],
        compiler_params=_params("arbitrary"),
        name="mem_attn_cache",
    )(qx, mem_k, mem_v)


def _mlp_body(*refs, ff_blk, second_norm):
    if second_norm:
        h1_ref, ox_ref, wxo_ref, lng_ref, lnb_ref, w1_ref, w2_ref, y_ref = refs
        h2 = _ln(ALPHA * h1_ref[...] + _dot(ox_ref[...], wxo_ref[...]), lng_ref[1:2, :], lnb_ref[1:2, :])
    else:
        h2_ref, lng_ref, lnb_ref, w1_ref, w2_ref, y_ref = refs
        h2 = h2_ref[...]
    hb = h2.astype(BF16)
    acc = jnp.zeros(h2.shape, F32)
    for c in range(0, D_FF, ff_blk):
        z = jnp.maximum(_dot(hb, w1_ref[:, c:c + ff_blk]), 0.0)
        acc += _dot((z * z).astype(BF16), w2_ref[c:c + ff_blk, :])
    y_ref[...] = _ln(ALPHA * h2 + acc, lng_ref[2:3, :], lnb_ref[2:3, :])


def _mlp(h, ox, w_xo, ln_g, ln_b, w1, w2, tn):
    n = h.shape[0]
    row = pl.BlockSpec((tn, D_MODEL), lambda i: (i, 0))
    second_norm = ox is not None
    rows = (h, ox) if second_norm else (h,)
    consts = ((w_xo,) if second_norm else ()) + (ln_g, ln_b, w1, w2)
    return pl.pallas_call(
        functools.partial(_mlp_body, ff_blk=MLP_FF_BLOCK, second_norm=second_norm),
        grid=(n // tn,),
        in_specs=[row] * len(rows) + [_const(c.shape) for c in consts],
        out_specs=row,
        out_shape=jax.ShapeDtypeStruct((n, D_MODEL), F32),
        compiler_params=_params("parallel"),
        name="mlp",
    )(*rows, *consts)


def _mem_kv_body(mem_ref, wk_ref, wv_ref, k_ref, v_ref):
    m = mem_ref[...].astype(BF16)
    k_ref[...] = _dot(m, wk_ref[...])
    v_ref[...] = _dot(m, wv_ref[...])


def _mem_kv(mem2, wk, wv):
    n = mem2.shape[0]
    out = jax.ShapeDtypeStruct((n, D_MODEL), F32)
    return pl.pallas_call(
        _mem_kv_body,
        grid=(1,),
        in_specs=[_const(mem2.shape), _const(wk.shape), _const(wv.shape)],
        out_specs=[_const((n, D_MODEL))] * 2,
        out_shape=[out, out],
        compiler_params=_params("arbitrary"),
        name="mem_kv",
    )(mem2, wk, wv)


def _state_to_cols(s):
    return jnp.transpose(s.reshape(s.shape[0], S5_COLS, -1), (1, 0, 2))


def _state_from_cols(s):
    return jnp.transpose(s, (1, 0, 2)).reshape(s.shape[1], SSM_GROUPS, SSM_STATE)


def _layer(x2, pos, nseq, h0r, h0i, mem_k, mem_v, caches, wts, s5_ops, tn, scan_ct):
    n = x2.shape[0]
    seq = n // nseq
    prompt = caches is None
    u, q, k, v, ckv, kpe = _project(x2, pos, wts["inv"], wts["w_in"], wts["g_q"], wts["w_q"], wts["g_kv"],
                                    wts["w_k"], wts["w_vt"] if prompt else wts["w_v"], min(n, PROJ_TILE),
                                    v_transposed=prompt, consecutive=prompt, kpe_cache_shape=prompt)
    y_ssm, fr, fi = _s5_scan(u, s5_ops, _state_to_cols(h0r), _state_to_cols(h0i), nseq, scan_ct)
    glu_args = (y_ssm, u, wts["d_skip"], wts["w_glu"], wts["g_out_ssm"])
    if prompt:
        o_mla = _attention(q, k, v, ATTN_HEADS)
    else:
        o_mla = _attention_cached(q, k, v, caches[0], caches[1], wts["w_k"], wts["w_v"], seq, nsub=CACHE_SEQS)
    mix_args = (o_mla, x2, wts["g_out_mla"], wts["w_o"], wts["ln_g"], wts["ln_b"], wts["w_xq"])
    if prompt:
        h, ox = _mix_mem(*glu_args, *mix_args, mem_k, mem_v, wts["w_xo"], min(n, MIX_TILE)), None
    else:
        h, qx = _mix(_glu(*glu_args, tn), *mix_args, tn)
        ox = _mem_attn_cache(qx, mem_k, mem_v, seq, nsub=CACHE_SEQS)
    y = _mlp(h, ox, wts["w_xo"], wts["ln_g"], wts["ln_b"], wts["w_ff1"], wts["w_ff2"], tn)
    return y, ckv, kpe, _state_from_cols(fr), _state_from_cols(fi)


def kernel(x_prompt, x_sample, mem_prompt, cache_mla_ckv, cache_mla_kpe, state_ssm_re, state_ssm_im, cache_mem_k, cache_mem_v, w_in, g_q, w_q_up, g_kv, w_kv_up, a_re, a_im, b_re, b_im, c_re, c_im, d_skip, log_dt, w_glu, g_out_ssm, g_out_mla, w_o, w_xq, w_xk, w_xv, w_xo, w_ff1, w_ff2, ln_g, ln_b):
    assert w_in.shape[0] == DEPTH == 1
    nbp, sp, _ = x_prompt.shape
    nbs, sd, _ = x_sample.shape
    past = cache_mla_ckv.shape[2]
    assert nbp == 1

    wq = jnp.pad(w_q_up[0], ((0, 0), (0, 0), (0, QK_PAD - QK_NOPE - QK_ROPE)))
    wk = w_kv_up[0][:, :, :QK_NOPE].reshape(KV_LORA, -1).astype(BF16)
    wv = w_kv_up[0][:, :, QK_NOPE:].reshape(KV_LORA, -1).astype(BF16)
    inv = ROPE_THETA ** (-jnp.arange(ROPE_HALF, dtype=F32) / ROPE_HALF)
    wts = {
        "inv": jnp.tile(inv, LANES // ROPE_HALF).reshape(1, LANES),
        "w_in": jnp.pad(w_in[0], ((0, 0), (0, LANES - QK_ROPE))).astype(BF16),
        "g_q": g_q[0].reshape(1, -1),
        "w_q": wq.reshape(Q_LORA, MLA_HEADS * QK_PAD).astype(BF16),
        "g_kv": g_kv[0].reshape(1, -1),
        "w_k": wk,
        "w_v": wv,
        "w_vt": wv.T,
        "d_skip": d_skip[0].reshape(1, -1),
        "w_glu": w_glu[0].astype(BF16),
        "g_out_ssm": g_out_ssm[0].reshape(1, -1),
        "g_out_mla": g_out_mla[0].reshape(1, -1),
        "w_o": w_o[0].astype(BF16),
        "w_xq": w_xq[0].reshape(D_MODEL, D_MODEL).astype(BF16),
        "w_xo": w_xo[0].reshape(D_MODEL, D_MODEL).astype(BF16),
        "w_ff1": w_ff1[0].astype(BF16),
        "w_ff2": w_ff2[0].astype(BF16),
        "ln_g": ln_g[0],
        "ln_b": ln_b[0],
    }
    s5_ops = _s5_prep(a_re[0], a_im[0], b_re[0], b_im[0], c_re[0], c_im[0], log_dt[0])

    mk, mv = _mem_kv(mem_prompt.reshape(nbp * N_MEM, D_MODEL),
                     w_xk[0].reshape(D_MODEL, D_MODEL).astype(BF16),
                     w_xv[0].reshape(D_MODEL, D_MODEL).astype(BF16))
    zero = jnp.zeros((nbp, SSM_GROUPS, SSM_STATE), F32)
    pos_p = jnp.arange(sp, dtype=F32).reshape(sp, 1)
    yp, ckv_p, kpe_p, sre_p, sim_p = _layer(
        x_prompt.reshape(sp, D_MODEL), pos_p, nbp, zero, zero, mk, mv, None,
        wts, s5_ops, tn=TOKEN_TILE, scan_ct=min(sp // S5_T, SCAN_CHUNKS))

    pos_s = jnp.tile(past + jnp.arange(sd, dtype=F32), nbs).reshape(nbs * sd, 1)
    caches = (cache_mla_ckv[0], cache_mla_kpe[0])
    ys, ckv_s, kpe_s, sre_s, sim_s = _layer(
        x_sample.reshape(nbs * sd, D_MODEL), pos_s, nbs,
        state_ssm_re[0], state_ssm_im[0], cache_mem_k[0], cache_mem_v[0], caches,
        wts, s5_ops, tn=TOKEN_TILE, scan_ct=sd // S5_T)

    return (yp.reshape(nbp, sp, D_MODEL), ys.reshape(nbs, sd, D_MODEL),
            ckv_p.reshape(1, nbp, sp, KV_LORA), kpe_p.reshape(1, nbp, sp, QK_ROPE),
            sre_p.reshape(1, nbp, SSM_GROUPS, SSM_STATE), sim_p.reshape(1, nbp, SSM_GROUPS, SSM_STATE),
            mk.reshape(1, nbp, N_MEM, X_HEADS, X_HEAD_DIM), mv.reshape(1, nbp, N_MEM, X_HEADS, X_HEAD_DIM),
            ckv_s.reshape(1, nbs, sd, KV_LORA), kpe_s.reshape(1, nbs, sd, QK_ROPE),
            sre_s.reshape(1, nbs, SSM_GROUPS, SSM_STATE), sim_s.reshape(1, nbs, SSM_GROUPS, SSM_STATE))
```

```python
import functools
import math

import jax
import jax.numpy as jnp
from jax import lax
from jax.experimental import pallas as pl
from jax.experimental.pallas import tpu as pltpu

F32 = jnp.float32
BF16 = jnp.bfloat16

D_MODEL = 1024
DEPTH = 1
CHUNK = 64
SSM_WIDTH = 512
SSM_GROUP = 16
SSM_GROUPS = 32
SSM_STATE = 64
MLA_HEADS = 4
QK_NOPE = 128
QK_ROPE = 64
V_HEAD = 128
MLA_WIDTH = MLA_HEADS * V_HEAD
Q_LORA = 384
KV_LORA = 256
ROPE_THETA = 10000.0
MLA_SCALE = (QK_NOPE + QK_ROPE) ** -0.5
N_MEM = 256
X_HEADS = 4
X_HEAD_DIM = D_MODEL // X_HEADS
D_FF = 4 * D_MODEL
ALPHA = (2 * DEPTH) ** 0.25
EPS = 1e-5
NEG_INF = -1e30

LANES = 128
MXU_DEPTH = 256
VMEM_BYTES = 64 * 1024 * 1024
VMEM_LIMIT = VMEM_BYTES - 8 * 1024 * 1024

QK_PAD = MXU_DEPTH
S5_T = MXU_DEPTH // SSM_GROUP
S5_COL_GROUPS = LANES // SSM_GROUP
S5_COLS = SSM_GROUPS // S5_COL_GROUPS
S5_SW = S5_COL_GROUPS * SSM_STATE
ROPE_HALF = QK_ROPE // 2
Q_SCALE = MLA_SCALE * math.log2(math.e)
ATTN_T = 512
ATTN_HEADS = 2
ATTN_TQ = 1024
ATTN_QSPLIT = 4
MLP_FF_BLOCK = 1024
CACHE_SEQS = 4
MIX_TILE = 1024
MIX_ROWS = 256
PROJ_TILE = 1024
PROJ_ROWS = 512
TOKEN_TILE = 512
SCAN_CHUNKS = 512
SCAN_UNROLL = 8

_NT = (((1,), (1,)), ((), ()))


def _rms(x, g):
    return x * lax.rsqrt(jnp.mean(x * x, -1, keepdims=True) + EPS) * g


def _ln(x, g, b):
    mu = jnp.mean(x, -1, keepdims=True)
    xc = x - mu
    var = jnp.mean(xc * xc, -1, keepdims=True)
    return xc * lax.rsqrt(var + EPS) * g + b


def _dot(a, b):
    return jnp.dot(a, b, preferred_element_type=F32)


def _dot_nt(a, b):
    return lax.dot_general(a, b, _NT, preferred_element_type=F32)


def _split(a):
    hi = a.astype(BF16)
    return hi, (a - hi.astype(F32)).astype(BF16)


def _dot_split(a, b):
    return _dot(a[0], b[0]) + _dot(a[0], b[1]) + _dot(a[1], b[0])


def _params(*sem):
    return pltpu.CompilerParams(dimension_semantics=sem, vmem_limit_bytes=VMEM_LIMIT)


def _const(shape):
    n = len(shape)
    return pl.BlockSpec(shape, lambda *_: (0,) * n)


def _proj_body(x_ref, pos_ref, inv_ref, w_in_ref, gq_ref, wq_ref, gkv_ref, wk_ref, wv_ref,
               u_ref, q_ref, k_ref, v_ref, ckv_ref, kpe_ref, tc_ref, ts_ref, *, v_transposed, consecutive):
    rb = PROJ_ROWS
    assert x_ref.shape[0] % rb == 0 and ATTN_T % rb == 0
    blocks = [slice(r0, r0 + rb) for r0 in range(0, x_ref.shape[0], rb)]
    c0 = SSM_WIDTH + Q_LORA
    projs = [_dot(x_ref[r, :].astype(BF16), w_in_ref[...]) for r in blocks]
    cqs = [_rms(p[:, SSM_WIDTH:c0], gq_ref[...]).astype(BF16) for p in projs]
    ckvs = [_rms(p[:, c0:c0 + KV_LORA], gkv_ref[...]) for p in projs]
    qs = [_dot(c, wq_ref[...]) * Q_SCALE for c in cqs]
    kns = [_dot(c.astype(BF16), wk_ref[...]) for c in ckvs]
    for r, p, ckv in zip(blocks, projs, ckvs):
        u_ref[r, :] = p[:, :SSM_WIDTH]
        ckv_ref[r, :] = ckv
        if v_transposed:
            lanes = slice(r.start % ATTN_T, r.start % ATTN_T + rb)
            v_ref[r.start // ATTN_T, :, lanes] = _dot_nt(wv_ref[...], ckv.astype(BF16)).astype(BF16)
        else:
            v_ref[r, :] = _dot(ckv.astype(BF16), wv_ref[...]).astype(BF16)

    lane = lax.broadcasted_iota(jnp.int32, (1, LANES), 1)
    live = lane < QK_ROPE
    sign = jnp.where(lane < ROPE_HALF, -1.0, 1.0)
    inv = inv_ref[...]
    if consecutive:
        @pl.when(pl.program_id(0) == 0)
        def _():
            r = lax.broadcasted_iota(jnp.int32, (x_ref.shape[0], 1), 0).astype(F32)
            tc_ref[...] = jnp.where(live, jnp.cos(r * inv), 0.0)
            ts_ref[...] = jnp.where(live, jnp.sin(r * inv), 0.0)

        base = pos_ref[0:1, :] * inv
        ca, sa = jnp.cos(base), jnp.sin(base)
        cos_ts = [ca * tc_ref[r, :] - sa * ts_ref[r, :] for r in blocks]
        sin_ts = [(sa * sign) * tc_ref[r, :] + (ca * sign) * ts_ref[r, :] for r in blocks]
    else:
        angs = [pos_ref[r, :] * inv for r in blocks]
        cos_ts = [jnp.where(live, jnp.cos(a), 0.0) for a in angs]
        sin_ts = [jnp.where(live, jnp.sin(a) * sign, 0.0) for a in angs]

    def rope(c2, cos_t, sin_t):
        swapped = jnp.where(lane < ROPE_HALF, pltpu.roll(c2, LANES - ROPE_HALF, 1), pltpu.roll(c2, ROPE_HALF, 1))
        return c2 * cos_t + swapped * sin_t

    for r, p, q, kn, cos_t, sin_t in zip(blocks, projs, qs, kns, cos_ts, sin_ts):
        kpe = rope(p[:, c0 + KV_LORA:], cos_t, sin_t)
        kpe_ref[r, :] = kpe[:, :QK_ROPE]
        kpe_b = kpe.astype(BF16)
        for h in range(MLA_HEADS):
            a = h * QK_PAD
            q_ref[r, a:a + QK_NOPE] = q[:, a:a + QK_NOPE].astype(BF16)
            q_ref[r, a + QK_NOPE:a + QK_PAD] = rope(q[:, a + QK_NOPE:a + QK_PAD], cos_t, sin_t).astype(BF16)
            k_ref[r, a:a + QK_NOPE] = kn[:, h * QK_NOPE:(h + 1) * QK_NOPE].astype(BF16)
            k_ref[r, a + QK_NOPE:a + QK_PAD] = kpe_b


def _project(x2, pos, inv, w_in, gq, wq, gkv, wk, wv, tn, v_transposed, consecutive, kpe_cache_shape):
    n = x2.shape[0]
    row = lambda w: pl.BlockSpec((tn, w), lambda i: (i, 0))
    if kpe_cache_shape:
        kpe_spec = pl.BlockSpec((None, None, tn, QK_ROPE), lambda i: (0, 0, i, 0))
        kpe_shape = jax.ShapeDtypeStruct((1, 1, n, QK_ROPE), F32)
    else:
        kpe_spec, kpe_shape = row(QK_ROPE), jax.ShapeDtypeStruct((n, QK_ROPE), F32)
    if v_transposed:
        assert tn % ATTN_T == 0
        v_spec = pl.BlockSpec((tn // ATTN_T, MLA_WIDTH, ATTN_T), lambda i: (i, 0, 0))
        v_shape = jax.ShapeDtypeStruct((n // ATTN_T, MLA_WIDTH, ATTN_T), BF16)
    else:
        v_spec, v_shape = row(MLA_WIDTH), jax.ShapeDtypeStruct((n, MLA_WIDTH), BF16)
    return pl.pallas_call(
        functools.partial(_proj_body, v_transposed=v_transposed, consecutive=consecutive),
        grid=(n // tn,),
        scratch_shapes=[pltpu.VMEM((tn, LANES), F32)] * 2,
        in_specs=[row(D_MODEL), row(1), _const(inv.shape), _const(w_in.shape), _const(gq.shape),
                  _const(wq.shape), _const(gkv.shape), _const(wk.shape), _const(wv.shape)],
        out_specs=[row(SSM_WIDTH), row(MLA_HEADS * QK_PAD), row(MLA_HEADS * QK_PAD),
                   v_spec, row(KV_LORA), kpe_spec],
        out_shape=[jax.ShapeDtypeStruct((n, SSM_WIDTH), F32),
                   jax.ShapeDtypeStruct((n, MLA_HEADS * QK_PAD), BF16),
                   jax.ShapeDtypeStruct((n, MLA_HEADS * QK_PAD), BF16),
                   v_shape,
                   jax.ShapeDtypeStruct((n, KV_LORA), F32),
                   kpe_shape],
        compiler_params=_params("arbitrary"),
        name="proj",
    )(x2, pos, inv, w_in, gq, wq, gkv, wk, wv)


def _s5_prep_body(ar_row, ai_row, ldt_row, bt_re, bt_im, ct_re, ct_im,
                  wt_ref, v_ref, z_ref, are_ref, aim_ref):
    sw = S5_SW
    arr, air, dtr = ar_row[0], ai_row[0], jnp.exp(ldt_row[0])

    mag = jnp.exp(arr * dtr)
    lr, li = mag * jnp.cos(air * dtr), mag * jnp.sin(air * dtr)
    nr, ni = lr - 1.0, li
    den = arr * arr + air * air
    f_re, f_im = (nr * arr + ni * air) / den, (ni * arr - nr * air) / den

    same_b = (lax.broadcasted_iota(jnp.int32, (LANES, sw), 0) // SSM_GROUP
              == lax.broadcasted_iota(jnp.int32, (LANES, sw), 1) // SSM_STATE)
    br = jnp.where(same_b, bt_re[0], 0.0)
    bi = jnp.where(same_b, bt_im[0], 0.0)
    bb_re = f_re * br - f_im * bi
    bb_im = f_re * bi + f_im * br
    same_c = (lax.broadcasted_iota(jnp.int32, (sw, LANES), 0) // SSM_STATE
              == lax.broadcasted_iota(jnp.int32, (sw, LANES), 1) // SSM_GROUP)
    cr = jnp.where(same_c, ct_re[0], 0.0)
    ci = jnp.where(same_c, ct_im[0], 0.0)
    cr_s, ci_s = _split(cr), _split(ci)

    e = lax.broadcasted_iota(jnp.int32, (2 * S5_T, 1), 0).astype(F32)
    pm = jnp.exp(arr * dtr * e)
    pw_re, pw_im = pm * jnp.cos(air * dtr * e), pm * jnp.sin(air * dtr * e)
    pt_re, pt_im = pw_re.T, pw_im.T

    lag_ops = []
    for lag in range(S5_T):
        p_re, p_im = pw_re[lag:lag + 1, :], pw_im[lag:lag + 1, :]
        k_re = p_re * bb_re - p_im * bb_im
        k_im = p_re * bb_im + p_im * bb_re
        i = S5_T - 1 - lag
        v_ref[0, i * LANES:(i + 1) * LANES, :sw] = k_re.astype(BF16)
        v_ref[0, i * LANES:(i + 1) * LANES, sw:] = k_im.astype(BF16)
        lag_ops.append(_dot_split(_split(k_re), cr_s) - _dot_split(_split(k_im), ci_s))

        q_re, q_im = pt_re[:, lag + 1:lag + 2], pt_im[:, lag + 1:lag + 2]
        z_ref[0, :sw, lag * LANES:(lag + 1) * LANES] = (cr * q_re - ci * q_im).astype(BF16)
        z_ref[0, sw:, lag * LANES:(lag + 1) * LANES] = (-(cr * q_im + ci * q_re)).astype(BF16)

    zero = jnp.zeros((LANES, LANES), BF16)
    for d in range(S5_T // 2):
        wt_ref[0, d, :LANES, :LANES] = lag_ops[2 * d].astype(BF16)
        wt_ref[0, d, :LANES, LANES:] = lag_ops[2 * d + 1].astype(BF16)
        wt_ref[0, d, LANES:, :LANES] = lag_ops[2 * d - 1].astype(BF16) if d else zero
        wt_ref[0, d, LANES:, LANES:] = lag_ops[2 * d].astype(BF16)

    are_ref[0] = pw_re[S5_T:S5_T + 1, :]
    aim_ref[0] = pw_im[S5_T:S5_T + 1, :]


def _s5_prep(a_re, a_im, b_re, b_im, c_re, c_im, log_dt):
    nc, r, t, h, p, sw = S5_COLS, S5_COL_GROUPS, S5_T, SSM_GROUP, SSM_STATE, S5_SW
    ldt = jnp.repeat(log_dt, p)
    bt = lambda b: jnp.tile(jnp.transpose(b.reshape(nc, r, p, h), (0, 1, 3, 2)).reshape(nc, r * h, p), (1, 1, r))
    ct = lambda c: jnp.tile(jnp.transpose(c.reshape(nc, r, h, p), (0, 1, 3, 2)).reshape(nc, r * p, h), (1, 1, r))
    args = (a_re.reshape(nc, 1, sw), a_im.reshape(nc, 1, sw), ldt.reshape(nc, 1, sw),
            bt(b_re), bt(b_im), ct(c_re), ct(c_im))
    blk = lambda s: pl.BlockSpec((1,) + s[1:], lambda i: (i,) + (0,) * (len(s) - 1))
    outs = [((nc, t // 2, 2 * LANES, 2 * LANES), BF16), ((nc, t * LANES, 2 * sw), BF16),
            ((nc, 2 * sw, t * LANES), BF16), ((nc, 1, sw), F32), ((nc, 1, sw), F32)]
    return pl.pallas_call(
        _s5_prep_body,
        grid=(nc,),
        in_specs=[blk(a.shape) for a in args],
        out_specs=[blk(s) for s, _ in outs],
        out_shape=[jax.ShapeDtypeStruct(s, d) for s, d in outs],
        compiler_params=_params("parallel"),
        name="s5_prep",
    )(*args)


def _s5_scan_body(u_ref, wt_ref, v_ref, z_ref, are_ref, aim_ref, h0r_ref, h0i_ref,
                  y_ref, fr_ref, fi_ref, lhs, sre, sim, xre, xim, cr, ci, *, nseq, ct):
    t = pl.program_id(1)
    sw = S5_SW
    seq = ct * S5_T

    @pl.when(t == 0)
    def _():
        cr[...] = h0r_ref[0]
        ci[...] = h0i_ref[0]

    def token_rows(cc, i):
        if nseq == 1:
            return pl.ds(i, ct, stride=S5_T), slice(None)
        return pl.ds(cc * S5_T + i, nseq, stride=seq), slice(cc * nseq, (cc + 1) * nseq)

    for cc in range(1 if nseq == 1 else ct):
        for i in range(S5_T):
            tok, crow = token_rows(cc, i)
            lhs[crow, i * LANES:(i + 1) * LANES] = u_ref[tok, :].astype(BF16)

    s = _dot(lhs[...], v_ref[0])
    sre[...] = s[:, :sw]
    sim[...] = s[:, sw:]

    a_re, a_im = are_ref[0], aim_ref[0]

    def step(c, carry):
        x_re, x_im = carry
        r = pl.ds(c * nseq, nseq)
        xre[r, :] = x_re
        xim[r, :] = x_im
        n_re = a_re * x_re - a_im * x_im + sre[r, :]
        n_im = a_re * x_im + a_im * x_re + sim[r, :]
        return n_re, n_im

    x_re, x_im = lax.fori_loop(0, ct, step, (cr[...], ci[...]), unroll=min(ct, SCAN_UNROLL))
    cr[...] = x_re
    ci[...] = x_im
    fr_ref[0] = x_re
    fi_ref[0] = x_im

    xb_re, xb_im = xre[...].astype(BF16), xim[...].astype(BF16)
    for jp in range(S5_T // 2):
        cols = slice(jp * 2 * LANES, (jp + 1) * 2 * LANES)
        acc = _dot(xb_re, z_ref[0, :sw, cols]) + _dot(xb_im, z_ref[0, sw:, cols])
        for ip in range(jp + 1):
            acc += _dot(lhs[:, ip * 2 * LANES:(ip + 1) * 2 * LANES], wt_ref[0, jp - ip])
        for cc in range(1 if nseq == 1 else ct):
            for jj in range(2):
                tok, crow = token_rows(cc, 2 * jp + jj)
                y_ref[tok, :] = acc[crow, jj * LANES:(jj + 1) * LANES]


def _s5_scan(u, col_ops, h0r, h0i, nseq, ct):
    wt, v, z, are, aim = col_ops
    n = u.shape[0]
    cps = n // nseq // S5_T
    assert nseq == 1 or ct == cps
    nc = nseq * ct
    sw = S5_SW
    tile = pl.BlockSpec((nc * S5_T, LANES), lambda c, t: (t, c))
    op = lambda a: pl.BlockSpec((1,) + a.shape[1:], lambda c, t: (c,) + (0,) * (a.ndim - 1))
    st = pl.BlockSpec((1, nseq, sw), lambda c, t: (c, 0, 0))
    st_shape = jax.ShapeDtypeStruct((S5_COLS, nseq, sw), F32)
    return pl.pallas_call(
        functools.partial(_s5_scan_body, nseq=nseq, ct=ct),
        grid=(S5_COLS, cps // ct),
        in_specs=[tile, op(wt), op(v), op(z), op(are), op(aim), st, st],
        out_specs=[tile, st, st],
        out_shape=[jax.ShapeDtypeStruct(u.shape, F32), st_shape, st_shape],
        scratch_shapes=[pltpu.VMEM((nc, S5_T * LANES), BF16)] + [pltpu.VMEM((nc, sw), F32)] * 4
        + [pltpu.VMEM((nseq, sw), F32)] * 2,
        compiler_params=_params("arbitrary", "arbitrary"),
        name="s5_scan",
    )(u, wt, v, z, are, aim, h0r, h0i)


def _glu_rows(ys, us, d_ref, w_ref, g_ref):
    acts = [jax.nn.gelu(y + d_ref[...] * u).astype(BF16) for y, u in zip(ys, us)]
    gls = [_dot(a, w_ref[...]) for a in acts]
    os = [gl[:, :SSM_WIDTH] * jax.nn.sigmoid(gl[:, SSM_WIDTH:]) for gl in gls]
    return [_rms(o, g_ref[...]).astype(BF16) for o in os]


def _glu_body(y_ref, u_ref, d_ref, w_ref, g_ref, o_ref):
    o_ref[...] = _glu_rows([y_ref[...]], [u_ref[...]], d_ref, w_ref, g_ref)[0]


def _glu(y, u, d, w, g, tn):
    n = y.shape[0]
    row = pl.BlockSpec((tn, SSM_WIDTH), lambda i: (i, 0))
    return pl.pallas_call(
        _glu_body,
        grid=(n // tn,),
        in_specs=[row, row, _const(d.shape), _const(w.shape), _const(g.shape)],
        out_specs=row,
        out_shape=jax.ShapeDtypeStruct((n, SSM_WIDTH), BF16),
        compiler_params=_params("parallel"),
        name="glu",
    )(y, u, d, w, g)


def _attn_body(q_ref, k_ref, vt_ref, o_ref, s0, s1, mt0, mt1, m_sc, l_sc, acc_sc, *, heads):
    i = pl.program_id(1)
    t = ATTN_T
    m_sc[...] = jnp.full(m_sc.shape, -jnp.inf, F32)
    l_sc[...] = jnp.zeros(l_sc.shape, F32)
    acc_sc[...] = jnp.zeros(acc_sc.shape, F32)

    qb = q_ref.shape[0] // ATTN_QSPLIT
    units = [(h, c * qb) for c in range(ATTN_QSPLIT) for h in range(heads)]

    def scores(tile, s_buf, mt_buf, unit, diag=None):
        h, c0 = unit
        r = pl.ds(pl.multiple_of(tile * t, t), t)
        s = _dot_nt(k_ref[r, h * QK_PAD:(h + 1) * QK_PAD],
                    q_ref[c0:c0 + qb, h * QK_PAD:(h + 1) * QK_PAD])
        if diag is not None:
            kc = (diag * t + lax.broadcasted_iota(jnp.int32, s.shape, 0)) // CHUNK
            qc = (c0 + lax.broadcasted_iota(jnp.int32, s.shape, 1)) // CHUNK
            s = jnp.where(kc <= qc, s, NEG_INF)
        s_buf[h, :, c0:c0 + qb] = s
        mt_buf[h, :, c0:c0 + qb] = jnp.max(s, 0, keepdims=True)

    def absorb(tile, s_buf, mt_buf, unit):
        h, c0 = unit
        cols = slice(c0, c0 + qb)
        m_prev = m_sc[h, :, cols]
        m_new = jnp.maximum(m_prev, mt_buf[h, :, cols])
        p = jnp.exp2(s_buf[h, :, cols] - m_new)
        alpha = jnp.exp2(m_prev - m_new)
        l_sc[h, :, cols] = alpha * l_sc[h, :, cols] + jnp.sum(p, 0, keepdims=True)
        m_sc[h, :, cols] = m_new
        pv = _dot(vt_ref[tile, h * V_HEAD:(h + 1) * V_HEAD, :], p.astype(BF16))
        acc_sc[h, :, cols] = alpha * acc_sc[h, :, cols] + pv

    nd = q_ref.shape[0] // t
    bufs = ((s0, mt0), (s1, mt1))
    for u in units:
        scores(nd * i, *bufs[0], u, diag=0)
    for d in range(1, nd):
        for u in units:
            scores(nd * i + d, *bufs[d % 2], u, diag=d)
            absorb(nd * i + d - 1, *bufs[(d - 1) % 2], u)

    def pair(jj, c):
        prev = jnp.where(jj == 0, nd * i + nd - 1, 2 * jj - 1)
        for u in units:
            scores(2 * jj, s0, mt0, u)
            absorb(prev, s1, mt1, u)
        for u in units:
            scores(2 * jj + 1, s1, mt1, u)
            absorb(2 * jj, s0, mt0, u)
        return c

    lax.fori_loop(0, (nd // 2) * i, pair, 0)
    last = jnp.where(i == 0, nd - 1, nd * i - 1)
    for u in units:
        absorb(last, s1, mt1, u)

    for h in range(heads):
        o_ref[:, h * V_HEAD:(h + 1) * V_HEAD] = (acc_sc[h] / l_sc[h]).T


def _attention(q, k, vt, heads):
    n = q.shape[0]
    t = ATTN_T
    tq = ATTN_TQ
    assert t % CHUNK == 0 and n % tq == 0 and MLA_HEADS % heads == 0 and (tq // t) % 2 == 0
    once = pl.Buffered(1)
    return pl.pallas_call(
        functools.partial(_attn_body, heads=heads),
        grid=(MLA_HEADS // heads, n // tq),
        in_specs=[pl.BlockSpec((tq, heads * QK_PAD), lambda g, i: (i, g)),
                  pl.BlockSpec((n, heads * QK_PAD), lambda g, i: (0, g), pipeline_mode=once),
                  pl.BlockSpec((n // t, heads * V_HEAD, t), lambda g, i: (0, g, 0), pipeline_mode=once)],
        out_specs=pl.BlockSpec((tq, heads * V_HEAD), lambda g, i: (i, g)),
        out_shape=jax.ShapeDtypeStruct((n, MLA_WIDTH), F32),
        scratch_shapes=[pltpu.VMEM((heads, t, tq), F32)] * 2 + [pltpu.VMEM((heads, 1, tq), F32)] * 4
        + [pltpu.VMEM((heads, V_HEAD, tq), F32)],
        compiler_params=_params("arbitrary", "arbitrary"),
        name="attn",
    )(q, k, vt)


def _attn_cached_body(q_ref, kn_ref, vn_ref, ckv_ref, kpe_ref, wk_ref, wv_ref, o_ref, *, past, seq, nsub):
    heads = range(MLA_HEADS)
    pos = past + lax.broadcasted_iota(jnp.int32, (seq, 1), 0)
    qc = jnp.concatenate([pos] * MLA_HEADS, axis=0) // CHUNK
    kc_past = lax.broadcasted_iota(jnp.int32, (1, past), 1) // CHUNK
    kc_new = (past + lax.broadcasted_iota(jnp.int32, (1, seq), 1)) // CHUNK
    seqs = range(nsub)
    rows = [slice(b * seq, (b + 1) * seq) for b in seqs]
    ckv = [ckv_ref[b].astype(BF16) for b in seqs]
    kpe = [kpe_ref[b].astype(BF16) for b in seqs]
    qa = [jnp.concatenate(
        [_dot_nt(q_ref[r, h * QK_PAD:h * QK_PAD + QK_NOPE], wk_ref[:, h * QK_NOPE:(h + 1) * QK_NOPE])
         for h in heads], axis=0).astype(BF16) for r in rows]
    qpe = [jnp.concatenate(
        [q_ref[r, h * QK_PAD + QK_NOPE:h * QK_PAD + QK_NOPE + QK_ROPE] for h in heads], axis=0) for r in rows]
    s_p = [_dot_nt(qa[b], ckv[b]) + _dot_nt(qpe[b], kpe[b]) for b in seqs]
    s_n = [jnp.concatenate(
        [_dot_nt(q_ref[r, h * QK_PAD:(h + 1) * QK_PAD], kn_ref[r, h * QK_PAD:(h + 1) * QK_PAD])
         for h in heads], axis=0) for r in rows]
    s_p = [jnp.where(kc_past <= qc, s, NEG_INF) for s in s_p]
    s_n = [jnp.where(kc_new <= qc, s, NEG_INF) for s in s_n]
    m = [jnp.maximum(jnp.max(a, -1, keepdims=True), jnp.max(c, -1, keepdims=True)) for a, c in zip(s_p, s_n)]
    e_p = [jnp.exp2(s - mm) for s, mm in zip(s_p, m)]
    e_n = [jnp.exp2(s - mm) for s, mm in zip(s_n, m)]
    l = [jnp.sum(a, -1, keepdims=True) + jnp.sum(c, -1, keepdims=True) for a, c in zip(e_p, e_n)]
    p_n = [(e / ll).astype(BF16) for e, ll in zip(e_n, l)]
    o_lat = [_dot((e / ll).astype(BF16), c).astype(BF16) for e, ll, c in zip(e_p, l, ckv)]
    for b in seqs:
        for h in heads:
            hr = slice(h * seq, (h + 1) * seq)
            cols = slice(h * V_HEAD, (h + 1) * V_HEAD)
            o_ref[rows[b], cols] = _dot(o_lat[b][hr], wv_ref[:, cols]) + _dot(p_n[b][hr], vn_ref[rows[b], cols])


def _attention_cached(q, k, v, ckv_cache, kpe_cache, wk, wv, seq, nsub):
    n = q.shape[0]
    nb, past, _ = ckv_cache.shape
    assert nb % nsub == 0
    row = lambda w: pl.BlockSpec((nsub * seq, w), lambda b: (b, 0))
    kern = functools.partial(_attn_cached_body, past=past, seq=seq, nsub=nsub)
    return pl.pallas_call(
        kern,
        grid=(nb // nsub,),
        in_specs=[row(MLA_HEADS * QK_PAD), row(MLA_HEADS * QK_PAD), row(MLA_WIDTH),
                  pl.BlockSpec((nsub, past, KV_LORA), lambda b: (b, 0, 0)),
                  pl.BlockSpec((nsub, past, QK_ROPE), lambda b: (b, 0, 0)),
                  _const(wk.shape), _const(wv.shape)],
        out_specs=row(MLA_WIDTH),
        out_shape=jax.ShapeDtypeStruct((n, MLA_WIDTH), F32),
        compiler_params=_params("parallel"),
        name="attn_cached",
    )(q, k, v, ckv_cache, kpe_cache, wk, wv)


def _row_blocks(ref):
    assert ref.shape[0] % MIX_ROWS == 0
    return [slice(r0, r0 + MIX_ROWS) for r0 in range(0, ref.shape[0], MIX_ROWS)]


def _mix_head(blocks, oss, om_ref, x_ref, gm_ref, wo_ref, lng_ref, lnb_ref, wxq_ref):
    om = [_rms(om_ref[r, :], gm_ref[...]).astype(BF16) for r in blocks]
    a = [_dot(s, wo_ref[:SSM_WIDTH, :]) + _dot(o, wo_ref[SSM_WIDTH:, :]) for s, o in zip(oss, om)]
    h1 = [_ln(ALPHA * x_ref[r, :] + aa, lng_ref[0:1, :], lnb_ref[0:1, :]) for r, aa in zip(blocks, a)]
    return h1, [_dot(h.astype(BF16), wxq_ref[...]).astype(BF16) for h in h1]


def _mix_body(os_ref, om_ref, x_ref, gm_ref, wo_ref, lng_ref, lnb_ref, wxq_ref, h1_ref, qx_ref):
    blocks = _row_blocks(x_ref)
    oss = [os_ref[r, :] for r in blocks]
    h1, qx = _mix_head(blocks, oss, om_ref, x_ref, gm_ref, wo_ref, lng_ref, lnb_ref, wxq_ref)
    for r, h, q in zip(blocks, h1, qx):
        h1_ref[r, :] = h
        qx_ref[r, :] = q


def _mix(o_ssm, o_mla, x2, g_mla, w_o, ln_g, ln_b, w_xq, tn):
    n = x2.shape[0]
    row = lambda w: pl.BlockSpec((tn, w), lambda i: (i, 0))
    return pl.pallas_call(
        _mix_body,
        grid=(n // tn,),
        in_specs=[row(SSM_WIDTH), row(MLA_WIDTH), row(D_MODEL), _const(g_mla.shape), _const(w_o.shape),
                  _const(ln_g.shape), _const(ln_b.shape), _const(w_xq.shape)],
        out_specs=[row(D_MODEL), row(D_MODEL)],
        out_shape=[jax.ShapeDtypeStruct((n, D_MODEL), F32), jax.ShapeDtypeStruct((n, D_MODEL), BF16)],
        compiler_params=_params("parallel"),
        name="mix",
    )(o_ssm, o_mla, x2, g_mla, w_o, ln_g, ln_b, w_xq)


def _mem_attn_heads(qs, mks, mvs):
    ss = [_dot_nt(q, mk.astype(BF16)) * (X_HEAD_DIM ** -0.5) for q, mk in zip(qs, mks)]
    es = [jnp.exp(s - jnp.max(s, -1, keepdims=True)) for s in ss]
    ps = [(e / jnp.sum(e, -1, keepdims=True)).astype(BF16) for e in es]
    return [_dot(p, mv.astype(BF16)).astype(BF16) for p, mv in zip(ps, mvs)]


def _mix_mem_body(y_ref, u_ref, d_ref, wglu_ref, gs_ref, om_ref, x_ref, gm_ref, wo_ref, lng_ref, lnb_ref, wxq_ref,
                  mk_ref, mv_ref, wxo_ref, h2_ref):
    blocks = _row_blocks(x_ref)
    oss = _glu_rows([y_ref[r, :] for r in blocks], [u_ref[r, :] for r in blocks], d_ref, wglu_ref, gs_ref)
    h1, qx = _mix_head(blocks, oss, om_ref, x_ref, gm_ref, wo_ref, lng_ref, lnb_ref, wxq_ref)
    cols = [slice(h * X_HEAD_DIM, (h + 1) * X_HEAD_DIM) for h in range(X_HEADS)]
    heads = _mem_attn_heads([q[:, c] for q in qx for c in cols], [mk_ref[:, c] for _ in qx for c in cols],
                            [mv_ref[:, c] for _ in qx for c in cols])
    ox = [jnp.concatenate(heads[i * X_HEADS:(i + 1) * X_HEADS], axis=1) for i in range(len(blocks))]
    att = [_dot(o, wxo_ref[...]) for o in ox]
    for r, h, a in zip(blocks, h1, att):
        h2_ref[r, :] = _ln(ALPHA * h + a, lng_ref[1:2, :], lnb_ref[1:2, :])


def _mix_mem(y_ssm, u, d_skip, w_glu, g_ssm, o_mla, x2, g_mla, w_o, ln_g, ln_b, w_xq, mem_k, mem_v, w_xo, tn):
    n = x2.shape[0]
    row = lambda w: pl.BlockSpec((tn, w), lambda i: (i, 0))
    glu_consts = (d_skip, w_glu, g_ssm)
    consts = (g_mla, w_o, ln_g, ln_b, w_xq, mem_k, mem_v, w_xo)
    return pl.pallas_call(
        _mix_mem_body,
        grid=(n // tn,),
        in_specs=[row(SSM_WIDTH), row(SSM_WIDTH)] + [_const(c.shape) for c in glu_consts]
        + [row(MLA_WIDTH), row(D_MODEL)] + [_const(c.shape) for c in consts],
        out_specs=row(D_MODEL),
        out_shape=jax.ShapeDtypeStruct((n, D_MODEL), F32),
        compiler_params=_params("parallel"),
        name="mix_mem",
    )(y_ssm, u, *glu_consts, o_mla, x2, *consts)


def _mem_attn_cache_body(qx_ref, mk_hbm, mv_hbm, o_ref, kbuf, vbuf, sems, *, seq, nsub):
    b = pl.program_id(0)
    nb = pl.num_programs(0)

    def copies(step, slot):
        return [pltpu.make_async_copy(src.at[step * nsub + j, :, h, :], buf.at[slot, j, h], sems.at[slot, j, t, h])
                for t, (src, buf) in enumerate(((mk_hbm, kbuf), (mv_hbm, vbuf)))
                for j in range(nsub) for h in range(X_HEADS)]

    @pl.when(b == 0)
    def _():
        for c in copies(0, 0):
            c.start()

    @pl.when(b + 1 < nb)
    def _():
        for c in copies(b + 1, (b + 1) % 2):
            c.start()

    slot = b % 2
    for c in copies(b, slot):
        c.wait()
    jh = [(j, h) for j in range(nsub) for h in range(X_HEADS)]
    where = [(slice(j * seq, (j + 1) * seq), slice(h * X_HEAD_DIM, (h + 1) * X_HEAD_DIM)) for j, h in jh]
    outs = _mem_attn_heads([qx_ref[r, c] for r, c in where], [kbuf[slot, j, h] for j, h in jh],
                           [vbuf[slot, j, h] for j, h in jh])
    for (r, c), o in zip(where, outs):
        o_ref[r, c] = o


def _mem_attn_cache(qx, mem_k, mem_v, seq, nsub):
    n = qx.shape[0]
    nb = mem_k.shape[0]
    assert nb % nsub == 0
    row = pl.BlockSpec((nsub * seq, D_MODEL), lambda b: (b, 0))
    hbm = pl.BlockSpec(memory_space=pl.ANY)
    buf = pltpu.VMEM((2, nsub, X_HEADS, N_MEM, X_HEAD_DIM), F32)
    return pl.pallas_call(
        functools.partial(_mem_attn_cache_body, seq=seq, nsub=nsub),
        grid=(nb // nsub,),
        in_specs=[row, hbm, hbm],
        out_specs=row,
        out_shape=jax.ShapeDtypeStruct((n, D_MODEL), BF16),
        scratch_shapes=[buf, buf, pltpu.SemaphoreType.DMA((2, nsub, 2, X_HEADS))],
        compiler_params=_params("arbitrary"),
        name="mem_attn_cache",
    )(qx, mem_k, mem_v)


def _mlp_body(*refs, ff_blk, second_norm):
    if second_norm:
        h1_ref, ox_ref, wxo_ref, lng_ref, lnb_ref, w1_ref, w2_ref, y_ref = refs
        h2 = _ln(ALPHA * h1_ref[...] + _dot(ox_ref[...], wxo_ref[...]), lng_ref[1:2, :], lnb_ref[1:2, :])
    else:
        h2_ref, lng_ref, lnb_ref, w1_ref, w2_ref, y_ref = refs
        h2 = h2_ref[...]
    hb = h2.astype(BF16)
    acc = jnp.zeros(h2.shape, F32)
    for c in range(0, D_FF, ff_blk):
        z = jnp.maximum(_dot(hb, w1_ref[:, c:c + ff_blk]), 0.0)
        acc += _dot((z * z).astype(BF16), w2_ref[c:c + ff_blk, :])
    y_ref[...] = _ln(ALPHA * h2 + acc, lng_ref[2:3, :], lnb_ref[2:3, :])


def _mlp(h, ox, w_xo, ln_g, ln_b, w1, w2, tn):
    n = h.shape[0]
    row = pl.BlockSpec((tn, D_MODEL), lambda i: (i, 0))
    second_norm = ox is not None
    rows = (h, ox) if second_norm else (h,)
    consts = ((w_xo,) if second_norm else ()) + (ln_g, ln_b, w1, w2)
    return pl.pallas_call(
        functools.partial(_mlp_body, ff_blk=MLP_FF_BLOCK, second_norm=second_norm),
        grid=(n // tn,),
        in_specs=[row] * len(rows) + [_const(c.shape) for c in consts],
        out_specs=row,
        out_shape=jax.ShapeDtypeStruct((n, D_MODEL), F32),
        compiler_params=_params("parallel"),
        name="mlp",
    )(*rows, *consts)


def _mem_kv_body(mem_ref, wk_ref, wv_ref, k_ref, v_ref):
    m = mem_ref[...].astype(BF16)
    k_ref[...] = _dot(m, wk_ref[...])
    v_ref[...] = _dot(m, wv_ref[...])


def _mem_kv(mem2, wk, wv):
    n = mem2.shape[0]
    out = jax.ShapeDtypeStruct((n, D_MODEL), F32)
    return pl.pallas_call(
        _mem_kv_body,
        grid=(1,),
        in_specs=[_const(mem2.shape), _const(wk.shape), _const(wv.shape)],
        out_specs=[_const((n, D_MODEL))] * 2,
        out_shape=[out, out],
        compiler_params=_params("arbitrary"),
        name="mem_kv",
    )(mem2, wk, wv)


def _state_to_cols(s):
    return jnp.transpose(s.reshape(s.shape[0], S5_COLS, -1), (1, 0, 2))


def _state_from_cols(s):
    return jnp.transpose(s, (1, 0, 2)).reshape(s.shape[1], SSM_GROUPS, SSM_STATE)


def _layer(x2, pos, nseq, h0r, h0i, mem_k, mem_v, caches, wts, s5_ops, tn, scan_ct):
    n = x2.shape[0]
    seq = n // nseq
    prompt = caches is None
    u, q, k, v, ckv, kpe = _project(x2, pos, wts["inv"], wts["w_in"], wts["g_q"], wts["w_q"], wts["g_kv"],
                                    wts["w_k"], wts["w_vt"] if prompt else wts["w_v"], min(n, PROJ_TILE),
                                    v_transposed=prompt, consecutive=prompt, kpe_cache_shape=prompt)
    y_ssm, fr, fi = _s5_scan(u, s5_ops, _state_to_cols(h0r), _state_to_cols(h0i), nseq, scan_ct)
    glu_args = (y_ssm, u, wts["d_skip"], wts["w_glu"], wts["g_out_ssm"])
    if prompt:
        o_mla = _attention(q, k, v, ATTN_HEADS)
    else:
        o_mla = _attention_cached(q, k, v, caches[0], caches[1], wts["w_k"], wts["w_v"], seq, nsub=CACHE_SEQS)
    mix_args = (o_mla, x2, wts["g_out_mla"], wts["w_o"], wts["ln_g"], wts["ln_b"], wts["w_xq"])
    if prompt:
        h, ox = _mix_mem(*glu_args, *mix_args, mem_k, mem_v, wts["w_xo"], min(n, MIX_TILE)), None
    else:
        h, qx = _mix(_glu(*glu_args, tn), *mix_args, tn)
        ox = _mem_attn_cache(qx, mem_k, mem_v, seq, nsub=CACHE_SEQS)
    y = _mlp(h, ox, wts["w_xo"], wts["ln_g"], wts["ln_b"], wts["w_ff1"], wts["w_ff2"], tn)
    return y, ckv, kpe, _state_from_cols(fr), _state_from_cols(fi)


def kernel(x_prompt, x_sample, mem_prompt, cache_mla_ckv, cache_mla_kpe, state_ssm_re, state_ssm_im, cache_mem_k, cache_mem_v, w_in, g_q, w_q_up, g_kv, w_kv_up, a_re, a_im, b_re, b_im, c_re, c_im, d_skip, log_dt, w_glu, g_out_ssm, g_out_mla, w_o, w_xq, w_xk, w_xv, w_xo, w_ff1, w_ff2, ln_g, ln_b):
    assert w_in.shape[0] == DEPTH == 1
    nbp, sp, _ = x_prompt.shape
    nbs, sd, _ = x_sample.shape
    past = cache_mla_ckv.shape[2]
    assert nbp == 1

    wq = jnp.pad(w_q_up[0], ((0, 0), (0, 0), (0, QK_PAD - QK_NOPE - QK_ROPE)))
    wk = w_kv_up[0][:, :, :QK_NOPE].reshape(KV_LORA, -1).astype(BF16)
    wv = w_kv_up[0][:, :, QK_NOPE:].reshape(KV_LORA, -1).astype(BF16)
    inv = ROPE_THETA ** (-jnp.arange(ROPE_HALF, dtype=F32) / ROPE_HALF)
    wts = {
        "inv": jnp.tile(inv, LANES // ROPE_HALF).reshape(1, LANES),
        "w_in": jnp.pad(w_in[0], ((0, 0), (0, LANES - QK_ROPE))).astype(BF16),
        "g_q": g_q[0].reshape(1, -1),
        "w_q": wq.reshape(Q_LORA, MLA_HEADS * QK_PAD).astype(BF16),
        "g_kv": g_kv[0].reshape(1, -1),
        "w_k": wk,
        "w_v": wv,
        "w_vt": wv.T,
        "d_skip": d_skip[0].reshape(1, -1),
        "w_glu": w_glu[0].astype(BF16),
        "g_out_ssm": g_out_ssm[0].reshape(1, -1),
        "g_out_mla": g_out_mla[0].reshape(1, -1),
        "w_o": w_o[0].astype(BF16),
        "w_xq": w_xq[0].reshape(D_MODEL, D_MODEL).astype(BF16),
        "w_xo": w_xo[0].reshape(D_MODEL, D_MODEL).astype(BF16),
        "w_ff1": w_ff1[0].astype(BF16),
        "w_ff2": w_ff2[0].astype(BF16),
        "ln_g": ln_g[0],
        "ln_b": ln_b[0],
    }
    s5_ops = _s5_prep(a_re[0], a_im[0], b_re[0], b_im[0], c_re[0], c_im[0], log_dt[0])

    mk, mv = _mem_kv(mem_prompt.reshape(nbp * N_MEM, D_MODEL),
                     w_xk[0].reshape(D_MODEL, D_MODEL).astype(BF16),
                     w_xv[0].reshape(D_MODEL, D_MODEL).astype(BF16))
    zero = jnp.zeros((nbp, SSM_GROUPS, SSM_STATE), F32)
    pos_p = jnp.arange(sp, dtype=F32).reshape(sp, 1)
    yp, ckv_p, kpe_p, sre_p, sim_p = _layer(
        x_prompt.reshape(sp, D_MODEL), pos_p, nbp, zero, zero, mk, mv, None,
        wts, s5_ops, tn=TOKEN_TILE, scan_ct=min(sp // S5_T, SCAN_CHUNKS))

    pos_s = jnp.tile(past + jnp.arange(sd, dtype=F32), nbs).reshape(nbs * sd, 1)
    caches = (cache_mla_ckv[0], cache_mla_kpe[0])
    ys, ckv_s, kpe_s, sre_s, sim_s = _layer(
        x_sample.reshape(nbs * sd, D_MODEL), pos_s, nbs,
        state_ssm_re[0], state_ssm_im[0], cache_mem_k[0], cache_mem_v[0], caches,
        wts, s5_ops, tn=TOKEN_TILE, scan_ct=sd // S5_T)

    return (yp.reshape(nbp, sp, D_MODEL), ys.reshape(nbs, sd, D_MODEL),
            ckv_p.reshape(1, nbp, sp, KV_LORA), kpe_p.reshape(1, nbp, sp, QK_ROPE),
            sre_p.reshape(1, nbp, SSM_GROUPS, SSM_STATE), sim_p.reshape(1, nbp, SSM_GROUPS, SSM_STATE),
            mk.reshape(1, nbp, N_MEM, X_HEADS, X_HEAD_DIM), mv.reshape(1, nbp, N_MEM, X_HEADS, X_HEAD_DIM),
            ckv_s.reshape(1, nbs, sd, KV_LORA), kpe_s.reshape(1, nbs, sd, QK_ROPE),
            sre_s.reshape(1, nbs, SSM_GROUPS, SSM_STATE), sim_s.reshape(1, nbs, SSM_GROUPS, SSM_STATE))
```

```python
import functools
import math

import jax
import jax.numpy as jnp
from jax import lax
from jax.experimental import pallas as pl
from jax.experimental.pallas import tpu as pltpu

F32 = jnp.float32
BF16 = jnp.bfloat16

D_MODEL = 1024
DEPTH = 1
CHUNK = 64
SSM_WIDTH = 512
SSM_GROUP = 16
SSM_GROUPS = 32
SSM_STATE = 64
MLA_HEADS = 4
QK_NOPE = 128
QK_ROPE = 64
V_HEAD = 128
MLA_WIDTH = MLA_HEADS * V_HEAD
Q_LORA = 384
KV_LORA = 256
ROPE_THETA = 10000.0
MLA_SCALE = (QK_NOPE + QK_ROPE) ** -0.5
N_MEM = 256
X_HEADS = 4
X_HEAD_DIM = D_MODEL // X_HEADS
D_FF = 4 * D_MODEL
ALPHA = (2 * DEPTH) ** 0.25
EPS = 1e-5
NEG_INF = -1e30

LANES = 128
MXU_DEPTH = 256
VMEM_BYTES = 64 * 1024 * 1024
VMEM_LIMIT = VMEM_BYTES - 8 * 1024 * 1024

QK_PAD = MXU_DEPTH
S5_T = MXU_DEPTH // SSM_GROUP
S5_COL_GROUPS = LANES // SSM_GROUP
S5_COLS = SSM_GROUPS // S5_COL_GROUPS
S5_SW = S5_COL_GROUPS * SSM_STATE
ROPE_HALF = QK_ROPE // 2
Q_SCALE = MLA_SCALE * math.log2(math.e)
ATTN_T = 512
ATTN_HEADS = 2
ATTN_TQ = 1024
ATTN_QSPLIT = 4
MLP_TILE = 1024
MLP_ROWS = 512
MLP_FF_BLOCK = 1024
CACHE_SEQS = 4
MIX_TILE = 1024
MIX_ROWS = 256
PROJ_TILE = 1024
PROJ_ROWS = 256
TOKEN_TILE = 512
SCAN_CHUNKS = 512
SCAN_UNROLL = 8

_NT = (((1,), (1,)), ((), ()))


def _rms(x, g):
    return x * lax.rsqrt(jnp.mean(x * x, -1, keepdims=True) + EPS) * g


def _ln(x, g, b):
    mu = jnp.mean(x, -1, keepdims=True)
    xc = x - mu
    var = jnp.mean(xc * xc, -1, keepdims=True)
    return xc * lax.rsqrt(var + EPS) * g + b


def _dot(a, b):
    return jnp.dot(a, b, preferred_element_type=F32)


def _dot_nt(a, b):
    return lax.dot_general(a, b, _NT, preferred_element_type=F32)


def _split(a):
    hi = a.astype(BF16)
    return hi, (a - hi.astype(F32)).astype(BF16)


def _dot_split(a, b):
    return _dot(a[0], b[0]) + _dot(a[0], b[1]) + _dot(a[1], b[0])


def _params(*sem):
    return pltpu.CompilerParams(dimension_semantics=sem, vmem_limit_bytes=VMEM_LIMIT)


def _const(shape):
    n = len(shape)
    return pl.BlockSpec(shape, lambda *_: (0,) * n)


def _proj_body(x_ref, pos_ref, inv_ref, w_in_ref, gq_ref, wq_ref, gkv_ref, wk_ref, wv_ref,
               u_ref, q_ref, k_ref, v_ref, ckv_ref, kpe_ref, tc_ref, ts_ref, *, v_transposed, consecutive):
    rb = PROJ_ROWS
    assert x_ref.shape[0] % rb == 0 and ATTN_T % rb == 0
    blocks = [slice(r0, r0 + rb) for r0 in range(0, x_ref.shape[0], rb)]
    c0 = SSM_WIDTH + Q_LORA

    lane = lax.broadcasted_iota(jnp.int32, (1, LANES), 1)
    live = lane < QK_ROPE
    sign = jnp.where(lane < ROPE_HALF, -1.0, 1.0)
    inv = inv_ref[...]
    if consecutive:
        @pl.when(pl.program_id(0) == 0)
        def _():
            r = lax.broadcasted_iota(jnp.int32, (x_ref.shape[0], 1), 0).astype(F32)
            tc_ref[...] = jnp.where(live, jnp.cos(r * inv), 0.0)
            ts_ref[...] = jnp.where(live, jnp.sin(r * inv), 0.0)

    projs = [_dot(x_ref[r, :].astype(BF16), w_in_ref[...]) for r in blocks]

    if consecutive:
        base = pos_ref[0:1, :] * inv
        ca, sa = jnp.cos(base), jnp.sin(base)
        cos_ts = [ca * tc_ref[r, :] - sa * ts_ref[r, :] for r in blocks]
        sin_ts = [(sa * sign) * tc_ref[r, :] + (ca * sign) * ts_ref[r, :] for r in blocks]
    else:
        angs = [pos_ref[r, :] * inv for r in blocks]
        cos_ts = [jnp.where(live, jnp.cos(a), 0.0) for a in angs]
        sin_ts = [jnp.where(live, jnp.sin(a) * sign, 0.0) for a in angs]

    def rope(c2, cos_t, sin_t):
        swapped = jnp.where(lane < ROPE_HALF, pltpu.roll(c2, LANES - ROPE_HALF, 1), pltpu.roll(c2, ROPE_HALF, 1))
        return c2 * cos_t + swapped * sin_t

    for r, p, cos_t, sin_t in zip(blocks, projs, cos_ts, sin_ts):
        cq = _rms(p[:, SSM_WIDTH:c0], gq_ref[...]).astype(BF16)
        ckv = _rms(p[:, c0:c0 + KV_LORA], gkv_ref[...])
        ckv_b = ckv.astype(BF16)
        q = _dot(cq, wq_ref[...]) * Q_SCALE
        kn = _dot(ckv_b, wk_ref[...])
        if v_transposed:
            lanes = slice(r.start % ATTN_T, r.start % ATTN_T + rb)
            v_ref[r.start // ATTN_T, :, lanes] = _dot_nt(wv_ref[...], ckv_b).astype(BF16)
        else:
            v_ref[r, :] = _dot(ckv_b, wv_ref[...]).astype(BF16)
        u_ref[r, :] = p[:, :SSM_WIDTH]
        ckv_ref[r, :] = ckv
        kpe = rope(p[:, c0 + KV_LORA:], cos_t, sin_t)
        kpe_ref[r, :] = kpe[:, :QK_ROPE]
        kpe_b = kpe.astype(BF16)
        for h in range(MLA_HEADS):
            a = h * QK_PAD
            q_ref[r, a:a + QK_NOPE] = q[:, a:a + QK_NOPE].astype(BF16)
            q_ref[r, a + QK_NOPE:a + QK_PAD] = rope(q[:, a + QK_NOPE:a + QK_PAD], cos_t, sin_t).astype(BF16)
            k_ref[r, a:a + QK_NOPE] = kn[:, h * QK_NOPE:(h + 1) * QK_NOPE].astype(BF16)
            k_ref[r, a + QK_NOPE:a + QK_PAD] = kpe_b


def _project(x2, pos, inv, w_in, gq, wq, gkv, wk, wv, tn, v_transposed, consecutive, kpe_cache_shape):
    n = x2.shape[0]
    row = lambda w: pl.BlockSpec((tn, w), lambda i: (i, 0))
    if kpe_cache_shape:
        kpe_spec = pl.BlockSpec((None, None, tn, QK_ROPE), lambda i: (0, 0, i, 0))
        kpe_shape = jax.ShapeDtypeStruct((1, 1, n, QK_ROPE), F32)
    else:
        kpe_spec, kpe_shape = row(QK_ROPE), jax.ShapeDtypeStruct((n, QK_ROPE), F32)
    if v_transposed:
        assert tn % ATTN_T == 0
        v_spec = pl.BlockSpec((tn // ATTN_T, MLA_WIDTH, ATTN_T), lambda i: (i, 0, 0))
        v_shape = jax.ShapeDtypeStruct((n // ATTN_T, MLA_WIDTH, ATTN_T), BF16)
    else:
        v_spec, v_shape = row(MLA_WIDTH), jax.ShapeDtypeStruct((n, MLA_WIDTH), BF16)
    return pl.pallas_call(
        functools.partial(_proj_body, v_transposed=v_transposed, consecutive=consecutive),
        grid=(n // tn,),
        scratch_shapes=[pltpu.VMEM((tn, LANES), F32)] * 2,
        in_specs=[row(D_MODEL), row(1), _const(inv.shape), _const(w_in.shape), _const(gq.shape),
                  _const(wq.shape), _const(gkv.shape), _const(wk.shape), _const(wv.shape)],
        out_specs=[row(SSM_WIDTH), row(MLA_HEADS * QK_PAD), row(MLA_HEADS * QK_PAD),
                   v_spec, row(KV_LORA), kpe_spec],
        out_shape=[jax.ShapeDtypeStruct((n, SSM_WIDTH), F32),
                   jax.ShapeDtypeStruct((n, MLA_HEADS * QK_PAD), BF16),
                   jax.ShapeDtypeStruct((n, MLA_HEADS * QK_PAD), BF16),
                   v_shape,
                   jax.ShapeDtypeStruct((n, KV_LORA), F32),
                   kpe_shape],
        compiler_params=_params("arbitrary"),
        name="proj",
    )(x2, pos, inv, w_in, gq, wq, gkv, wk, wv)


def _s5_prep_body(ar_row, ai_row, ldt_row, bt_re, bt_im, ct_re, ct_im,
                  wt_ref, v_ref, z_ref, are_ref, aim_ref):
    sw = S5_SW
    arr, air, dtr = ar_row[0], ai_row[0], jnp.exp(ldt_row[0])

    mag = jnp.exp(arr * dtr)
    lr, li = mag * jnp.cos(air * dtr), mag * jnp.sin(air * dtr)
    nr, ni = lr - 1.0, li
    den = arr * arr + air * air
    f_re, f_im = (nr * arr + ni * air) / den, (ni * arr - nr * air) / den

    same_b = (lax.broadcasted_iota(jnp.int32, (LANES, sw), 0) // SSM_GROUP
              == lax.broadcasted_iota(jnp.int32, (LANES, sw), 1) // SSM_STATE)
    br = jnp.where(same_b, bt_re[0], 0.0)
    bi = jnp.where(same_b, bt_im[0], 0.0)
    bb_re = f_re * br - f_im * bi
    bb_im = f_re * bi + f_im * br
    same_c = (lax.broadcasted_iota(jnp.int32, (sw, LANES), 0) // SSM_STATE
              == lax.broadcasted_iota(jnp.int32, (sw, LANES), 1) // SSM_GROUP)
    cr = jnp.where(same_c, ct_re[0], 0.0)
    ci = jnp.where(same_c, ct_im[0], 0.0)
    cr_s, ci_s = _split(cr), _split(ci)

    e = lax.broadcasted_iota(jnp.int32, (2 * S5_T, 1), 0).astype(F32)
    pm = jnp.exp(arr * dtr * e)
    pw_re, pw_im = pm * jnp.cos(air * dtr * e), pm * jnp.sin(air * dtr * e)
    pt_re, pt_im = pw_re.T, pw_im.T

    lag_ops = []
    for lag in range(S5_T):
        p_re, p_im = pw_re[lag:lag + 1, :], pw_im[lag:lag + 1, :]
        k_re = p_re * bb_re - p_im * bb_im
        k_im = p_re * bb_im + p_im * bb_re
        i = S5_T - 1 - lag
        v_ref[0, i * LANES:(i + 1) * LANES, :sw] = k_re.astype(BF16)
        v_ref[0, i * LANES:(i + 1) * LANES, sw:] = k_im.astype(BF16)
        lag_ops.append(_dot_split(_split(k_re), cr_s) - _dot_split(_split(k_im), ci_s))

        q_re, q_im = pt_re[:, lag + 1:lag + 2], pt_im[:, lag + 1:lag + 2]
        z_ref[0, :sw, lag * LANES:(lag + 1) * LANES] = (cr * q_re - ci * q_im).astype(BF16)
        z_ref[0, sw:, lag * LANES:(lag + 1) * LANES] = (-(cr * q_im + ci * q_re)).astype(BF16)

    zero = jnp.zeros((LANES, LANES), BF16)
    for d in range(S5_T // 2):
        wt_ref[0, d, :LANES, :LANES] = lag_ops[2 * d].astype(BF16)
        wt_ref[0, d, :LANES, LANES:] = lag_ops[2 * d + 1].astype(BF16)
        wt_ref[0, d, LANES:, :LANES] = lag_ops[2 * d - 1].astype(BF16) if d else zero
        wt_ref[0, d, LANES:, LANES:] = lag_ops[2 * d].astype(BF16)

    are_ref[0] = pw_re[S5_T:S5_T + 1, :]
    aim_ref[0] = pw_im[S5_T:S5_T + 1, :]


def _s5_prep(a_re, a_im, b_re, b_im, c_re, c_im, log_dt):
    nc, r, t, h, p, sw = S5_COLS, S5_COL_GROUPS, S5_T, SSM_GROUP, SSM_STATE, S5_SW
    ldt = jnp.repeat(log_dt, p)
    bt = lambda b: jnp.tile(jnp.transpose(b.reshape(nc, r, p, h), (0, 1, 3, 2)).reshape(nc, r * h, p), (1, 1, r))
    ct = lambda c: jnp.tile(jnp.transpose(c.reshape(nc, r, h, p), (0, 1, 3, 2)).reshape(nc, r * p, h), (1, 1, r))
    args = (a_re.reshape(nc, 1, sw), a_im.reshape(nc, 1, sw), ldt.reshape(nc, 1, sw),
            bt(b_re), bt(b_im), ct(c_re), ct(c_im))
    blk = lambda s: pl.BlockSpec((1,) + s[1:], lambda i: (i,) + (0,) * (len(s) - 1))
    outs = [((nc, t // 2, 2 * LANES, 2 * LANES), BF16), ((nc, t * LANES, 2 * sw), BF16),
            ((nc, 2 * sw, t * LANES), BF16), ((nc, 1, sw), F32), ((nc, 1, sw), F32)]
    return pl.pallas_call(
        _s5_prep_body,
        grid=(nc,),
        in_specs=[blk(a.shape) for a in args],
        out_specs=[blk(s) for s, _ in outs],
        out_shape=[jax.ShapeDtypeStruct(s, d) for s, d in outs],
        compiler_params=_params("parallel"),
        name="s5_prep",
    )(*args)


def _s5_scan_body(u_ref, wt_ref, v_ref, z_ref, are_ref, aim_ref, h0r_ref, h0i_ref,
                  y_ref, fr_ref, fi_ref, lhs, sre, sim, xre, xim, cr, ci, *, nseq, ct):
    t = pl.program_id(1)
    sw = S5_SW
    seq = ct * S5_T

    @pl.when(t == 0)
    def _():
        cr[...] = h0r_ref[0]
        ci[...] = h0i_ref[0]

    def token_rows(cc, i):
        if nseq == 1:
            return pl.ds(i, ct, stride=S5_T), slice(None)
        return pl.ds(cc * S5_T + i, nseq, stride=seq), slice(cc * nseq, (cc + 1) * nseq)

    for cc in range(1 if nseq == 1 else ct):
        for i in range(S5_T):
            tok, crow = token_rows(cc, i)
            lhs[crow, i * LANES:(i + 1) * LANES] = u_ref[tok, :].astype(BF16)

    s = _dot(lhs[...], v_ref[0])
    sre[...] = s[:, :sw]
    sim[...] = s[:, sw:]

    a_re, a_im = are_ref[0], aim_ref[0]

    def step(c, carry):
        x_re, x_im = carry
        r = pl.ds(c * nseq, nseq)
        xre[r, :] = x_re
        xim[r, :] = x_im
        n_re = a_re * x_re - a_im * x_im + sre[r, :]
        n_im = a_re * x_im + a_im * x_re + sim[r, :]
        return n_re, n_im

    x_re, x_im = lax.fori_loop(0, ct, step, (cr[...], ci[...]), unroll=min(ct, SCAN_UNROLL))
    cr[...] = x_re
    ci[...] = x_im
    fr_ref[0] = x_re
    fi_ref[0] = x_im

    xb_re, xb_im = xre[...].astype(BF16), xim[...].astype(BF16)
    for jp in range(S5_T // 2):
        cols = slice(jp * 2 * LANES, (jp + 1) * 2 * LANES)
        acc = _dot(xb_re, z_ref[0, :sw, cols]) + _dot(xb_im, z_ref[0, sw:, cols])
        for ip in range(jp + 1):
            acc += _dot(lhs[:, ip * 2 * LANES:(ip + 1) * 2 * LANES], wt_ref[0, jp - ip])
        for cc in range(1 if nseq == 1 else ct):
            for jj in range(2):
                tok, crow = token_rows(cc, 2 * jp + jj)
                y_ref[tok, :] = acc[crow, jj * LANES:(jj + 1) * LANES]


def _s5_scan(u, col_ops, h0r, h0i, nseq, ct):
    wt, v, z, are, aim = col_ops
    n = u.shape[0]
    cps = n // nseq // S5_T
    assert nseq == 1 or ct == cps
    nc = nseq * ct
    sw = S5_SW
    tile = pl.BlockSpec((nc * S5_T, LANES), lambda c, t: (t, c))
    op = lambda a: pl.BlockSpec((1,) + a.shape[1:], lambda c, t: (c,) + (0,) * (a.ndim - 1))
    st = pl.BlockSpec((1, nseq, sw), lambda c, t: (c, 0, 0))
    st_shape = jax.ShapeDtypeStruct((S5_COLS, nseq, sw), F32)
    return pl.pallas_call(
        functools.partial(_s5_scan_body, nseq=nseq, ct=ct),
        grid=(S5_COLS, cps // ct),
        in_specs=[tile, op(wt), op(v), op(z), op(are), op(aim), st, st],
        out_specs=[tile, st, st],
        out_shape=[jax.ShapeDtypeStruct(u.shape, F32), st_shape, st_shape],
        scratch_shapes=[pltpu.VMEM((nc, S5_T * LANES), BF16)] + [pltpu.VMEM((nc, sw), F32)] * 4
        + [pltpu.VMEM((nseq, sw), F32)] * 2,
        compiler_params=_params("arbitrary", "arbitrary"),
        name="s5_scan",
    )(u, wt, v, z, are, aim, h0r, h0i)


def _run_stages(stages, states):
    for t in range(len(stages) + len(states) - 1):
        for b, st in enumerate(states):
            if 0 <= t - b < len(stages):
                stages[t - b](st)


def _row_states(ref, rows):
    assert ref.shape[0] % rows == 0
    return [{"r": slice(r0, r0 + rows)} for r0 in range(0, ref.shape[0], rows)]


def _glu_stages(y_ref, u_ref, d_ref, w_ref, g_ref):
    def act(st):
        st["act"] = jax.nn.gelu(y_ref[st["r"], :] + d_ref[...] * u_ref[st["r"], :]).astype(BF16)

    def glu(st):
        st["gl"] = _dot(st.pop("act"), w_ref[...])

    def gate(st):
        gl = st.pop("gl")
        st["os"] = _rms(gl[:, :SSM_WIDTH] * jax.nn.sigmoid(gl[:, SSM_WIDTH:]), g_ref[...]).astype(BF16)

    return [act, glu, gate]


def _glu_body(y_ref, u_ref, d_ref, w_ref, g_ref, o_ref):
    def store(st):
        o_ref[st["r"], :] = st.pop("os")

    _run_stages(_glu_stages(y_ref, u_ref, d_ref, w_ref, g_ref) + [store], _row_states(y_ref, MIX_ROWS))


def _glu(y, u, d, w, g, tn):
    n = y.shape[0]
    row = pl.BlockSpec((tn, SSM_WIDTH), lambda i: (i, 0))
    return pl.pallas_call(
        _glu_body,
        grid=(n // tn,),
        in_specs=[row, row, _const(d.shape), _const(w.shape), _const(g.shape)],
        out_specs=row,
        out_shape=jax.ShapeDtypeStruct((n, SSM_WIDTH), BF16),
        compiler_params=_params("parallel"),
        name="glu",
    )(y, u, d, w, g)


def _attn_body(q_ref, k_ref, vt_ref, o_ref, s0, s1, mt0, mt1, m_sc, l_sc, acc_sc, *, heads):
    i = pl.program_id(1)
    t = ATTN_T
    m_sc[...] = jnp.full(m_sc.shape, -jnp.inf, F32)
    l_sc[...] = jnp.zeros(l_sc.shape, F32)
    acc_sc[...] = jnp.zeros(acc_sc.shape, F32)

    qb = q_ref.shape[0] // ATTN_QSPLIT
    units = [(h, c * qb) for c in range(ATTN_QSPLIT) for h in range(heads)]

    def scores(tile, s_buf, mt_buf, unit, diag=None):
        h, c0 = unit
        r = pl.ds(pl.multiple_of(tile * t, t), t)
        s = _dot_nt(k_ref[r, h * QK_PAD:(h + 1) * QK_PAD],
                    q_ref[c0:c0 + qb, h * QK_PAD:(h + 1) * QK_PAD])
        if diag is not None:
            kc = (diag * t + lax.broadcasted_iota(jnp.int32, s.shape, 0)) // CHUNK
            qc = (c0 + lax.broadcasted_iota(jnp.int32, s.shape, 1)) // CHUNK
            s = jnp.where(kc <= qc, s, NEG_INF)
        s_buf[h, :, c0:c0 + qb] = s
        mt_buf[h, :, c0:c0 + qb] = jnp.max(s, 0, keepdims=True)

    def absorb(tile, s_buf, mt_buf, unit):
        h, c0 = unit
        cols = slice(c0, c0 + qb)
        m_prev = m_sc[h, :, cols]
        m_new = jnp.maximum(m_prev, mt_buf[h, :, cols])
        p = jnp.exp2(s_buf[h, :, cols] - m_new)
        alpha = jnp.exp2(m_prev - m_new)
        l_sc[h, :, cols] = alpha * l_sc[h, :, cols] + jnp.sum(p, 0, keepdims=True)
        m_sc[h, :, cols] = m_new
        pv = _dot(vt_ref[tile, h * V_HEAD:(h + 1) * V_HEAD, :], p.astype(BF16))
        acc_sc[h, :, cols] = alpha * acc_sc[h, :, cols] + pv

    nd = q_ref.shape[0] // t
    bufs = ((s0, mt0), (s1, mt1))
    for u in units:
        scores(nd * i, *bufs[0], u, diag=0)
    for d in range(1, nd):
        for u in units:
            scores(nd * i + d, *bufs[d % 2], u, diag=d)
            absorb(nd * i + d - 1, *bufs[(d - 1) % 2], u)

    def pair(jj, c):
        prev = jnp.where(jj == 0, nd * i + nd - 1, 2 * jj - 1)
        for u in units:
            scores(2 * jj, s0, mt0, u)
            absorb(prev, s1, mt1, u)
        for u in units:
            scores(2 * jj + 1, s1, mt1, u)
            absorb(2 * jj, s0, mt0, u)
        return c

    lax.fori_loop(0, (nd // 2) * i, pair, 0)
    last = jnp.where(i == 0, nd - 1, nd * i - 1)
    for u in units:
        absorb(last, s1, mt1, u)

    for h in range(heads):
        o_ref[:, h * V_HEAD:(h + 1) * V_HEAD] = (acc_sc[h] / l_sc[h]).T


def _attention(q, k, vt, heads):
    n = q.shape[0]
    t = ATTN_T
    tq = ATTN_TQ
    assert t % CHUNK == 0 and n % tq == 0 and MLA_HEADS % heads == 0 and (tq // t) % 2 == 0
    once = pl.Buffered(1)
    return pl.pallas_call(
        functools.partial(_attn_body, heads=heads),
        grid=(MLA_HEADS // heads, n // tq),
        in_specs=[pl.BlockSpec((tq, heads * QK_PAD), lambda g, i: (i, g)),
                  pl.BlockSpec((n, heads * QK_PAD), lambda g, i: (0, g), pipeline_mode=once),
                  pl.BlockSpec((n // t, heads * V_HEAD, t), lambda g, i: (0, g, 0), pipeline_mode=once)],
        out_specs=pl.BlockSpec((tq, heads * V_HEAD), lambda g, i: (i, g)),
        out_shape=jax.ShapeDtypeStruct((n, MLA_WIDTH), F32),
        scratch_shapes=[pltpu.VMEM((heads, t, tq), F32)] * 2 + [pltpu.VMEM((heads, 1, tq), F32)] * 4
        + [pltpu.VMEM((heads, V_HEAD, tq), F32)],
        compiler_params=_params("arbitrary", "arbitrary"),
        name="attn",
    )(q, k, vt)


def _attn_cached_body(q_ref, kn_ref, vn_ref, ckv_ref, kpe_ref, wk_ref, wv_ref, o_ref, *, past, seq, nsub):
    heads = range(MLA_HEADS)
    pos = past + lax.broadcasted_iota(jnp.int32, (seq, 1), 0)
    qc = jnp.concatenate([pos] * MLA_HEADS, axis=0) // CHUNK
    kc_past = lax.broadcasted_iota(jnp.int32, (1, past), 1) // CHUNK
    kc_new = (past + lax.broadcasted_iota(jnp.int32, (1, seq), 1)) // CHUNK
    seqs = range(nsub)
    rows = [slice(b * seq, (b + 1) * seq) for b in seqs]
    ckv = [ckv_ref[b].astype(BF16) for b in seqs]
    kpe = [kpe_ref[b].astype(BF16) for b in seqs]
    qa = [jnp.concatenate(
        [_dot_nt(q_ref[r, h * QK_PAD:h * QK_PAD + QK_NOPE], wk_ref[:, h * QK_NOPE:(h + 1) * QK_NOPE])
         for h in heads], axis=0).astype(BF16) for r in rows]
    qpe = [jnp.concatenate(
        [q_ref[r, h * QK_PAD + QK_NOPE:h * QK_PAD + QK_NOPE + QK_ROPE] for h in heads], axis=0) for r in rows]
    s_p = [_dot_nt(qa[b], ckv[b]) + _dot_nt(qpe[b], kpe[b]) for b in seqs]
    s_n = [jnp.concatenate(
        [_dot_nt(q_ref[r, h * QK_PAD:(h + 1) * QK_PAD], kn_ref[r, h * QK_PAD:(h + 1) * QK_PAD])
         for h in heads], axis=0) for r in rows]
    s_p = [jnp.where(kc_past <= qc, s, NEG_INF) for s in s_p]
    s_n = [jnp.where(kc_new <= qc, s, NEG_INF) for s in s_n]
    m = [jnp.maximum(jnp.max(a, -1, keepdims=True), jnp.max(c, -1, keepdims=True)) for a, c in zip(s_p, s_n)]
    e_p = [jnp.exp2(s - mm) for s, mm in zip(s_p, m)]
    e_n = [jnp.exp2(s - mm) for s, mm in zip(s_n, m)]
    l = [jnp.sum(a, -1, keepdims=True) + jnp.sum(c, -1, keepdims=True) for a, c in zip(e_p, e_n)]
    p_n = [(e / ll).astype(BF16) for e, ll in zip(e_n, l)]
    o_lat = [_dot((e / ll).astype(BF16), c).astype(BF16) for e, ll, c in zip(e_p, l, ckv)]
    for b in seqs:
        for h in heads:
            hr = slice(h * seq, (h + 1) * seq)
            cols = slice(h * V_HEAD, (h + 1) * V_HEAD)
            o_ref[rows[b], cols] = _dot(o_lat[b][hr], wv_ref[:, cols]) + _dot(p_n[b][hr], vn_ref[rows[b], cols])


def _attention_cached(q, k, v, ckv_cache, kpe_cache, wk, wv, seq, nsub):
    n = q.shape[0]
    nb, past, _ = ckv_cache.shape
    assert nb % nsub == 0
    row = lambda w: pl.BlockSpec((nsub * seq, w), lambda b: (b, 0))
    kern = functools.partial(_attn_cached_body, past=past, seq=seq, nsub=nsub)
    return pl.pallas_call(
        kern,
        grid=(nb // nsub,),
        in_specs=[row(MLA_HEADS * QK_PAD), row(MLA_HEADS * QK_PAD), row(MLA_WIDTH),
                  pl.BlockSpec((nsub, past, KV_LORA), lambda b: (b, 0, 0)),
                  pl.BlockSpec((nsub, past, QK_ROPE), lambda b: (b, 0, 0)),
                  _const(wk.shape), _const(wv.shape)],
        out_specs=row(MLA_WIDTH),
        out_shape=jax.ShapeDtypeStruct((n, MLA_WIDTH), F32),
        compiler_params=_params("parallel"),
        name="attn_cached",
    )(q, k, v, ckv_cache, kpe_cache, wk, wv)


def _mix_stages(om_ref, x_ref, gm_ref, wo_ref, lng_ref, lnb_ref, wxq_ref):
    def norm_mla(st):
        st["om"] = _rms(om_ref[st["r"], :], gm_ref[...]).astype(BF16)

    def project(st):
        st["a"] = _dot(st.pop("os"), wo_ref[:SSM_WIDTH, :]) + _dot(st.pop("om"), wo_ref[SSM_WIDTH:, :])

    def norm1(st):
        st["h1"] = _ln(ALPHA * x_ref[st["r"], :] + st.pop("a"), lng_ref[0:1, :], lnb_ref[0:1, :])

    def query(st):
        st["qx"] = _dot(st["h1"].astype(BF16), wxq_ref[...]).astype(BF16)

    return [norm_mla, project, norm1, query]


def _mix_body(os_ref, om_ref, x_ref, gm_ref, wo_ref, lng_ref, lnb_ref, wxq_ref, h1_ref, qx_ref):
    def load(st):
        st["os"] = os_ref[st["r"], :]

    def store(st):
        h1_ref[st["r"], :] = st.pop("h1")
        qx_ref[st["r"], :] = st.pop("qx")

    stages = [load] + _mix_stages(om_ref, x_ref, gm_ref, wo_ref, lng_ref, lnb_ref, wxq_ref) + [store]
    _run_stages(stages, _row_states(x_ref, MIX_ROWS))


def _mix(o_ssm, o_mla, x2, g_mla, w_o, ln_g, ln_b, w_xq, tn):
    n = x2.shape[0]
    row = lambda w: pl.BlockSpec((tn, w), lambda i: (i, 0))
    return pl.pallas_call(
        _mix_body,
        grid=(n // tn,),
        in_specs=[row(SSM_WIDTH), row(MLA_WIDTH), row(D_MODEL), _const(g_mla.shape), _const(w_o.shape),
                  _const(ln_g.shape), _const(ln_b.shape), _const(w_xq.shape)],
        out_specs=[row(D_MODEL), row(D_MODEL)],
        out_shape=[jax.ShapeDtypeStruct((n, D_MODEL), F32), jax.ShapeDtypeStruct((n, D_MODEL), BF16)],
        compiler_params=_params("parallel"),
        name="mix",
    )(o_ssm, o_mla, x2, g_mla, w_o, ln_g, ln_b, w_xq)


def _mem_attn_heads(qs, mks, mvs):
    ss = [_dot_nt(q, mk.astype(BF16)) * (X_HEAD_DIM ** -0.5) for q, mk in zip(qs, mks)]
    es = [jnp.exp(s - jnp.max(s, -1, keepdims=True)) for s in ss]
    ps = [(e / jnp.sum(e, -1, keepdims=True)).astype(BF16) for e in es]
    return [_dot(p, mv.astype(BF16)).astype(BF16) for p, mv in zip(ps, mvs)]


def _mix_mem_body(y_ref, u_ref, d_ref, wglu_ref, gs_ref, om_ref, x_ref, gm_ref, wo_ref, lng_ref, lnb_ref, wxq_ref,
                  mk_ref, mv_ref, wxo_ref, h2_ref):
    cols = [slice(h * X_HEAD_DIM, (h + 1) * X_HEAD_DIM) for h in range(X_HEADS)]
    mks = [mk_ref[:, c].astype(BF16) for c in cols]
    mvs = [mv_ref[:, c].astype(BF16) for c in cols]

    def scores(st):
        qx = st.pop("qx")
        st["s"] = [_dot_nt(qx[:, c], mk) * (X_HEAD_DIM ** -0.5) for c, mk in zip(cols, mks)]

    def softmax(st):
        es = [jnp.exp(s - jnp.max(s, -1, keepdims=True)) for s in st.pop("s")]
        st["p"] = [(e / jnp.sum(e, -1, keepdims=True)).astype(BF16) for e in es]

    def values(st):
        st["ox"] = jnp.concatenate([_dot(p, mv).astype(BF16) for p, mv in zip(st.pop("p"), mvs)], axis=1)

    def project(st):
        st["att"] = _dot(st.pop("ox"), wxo_ref[...])

    def norm2(st):
        h2_ref[st["r"], :] = _ln(ALPHA * st.pop("h1") + st.pop("att"), lng_ref[1:2, :], lnb_ref[1:2, :])

    stages = (_glu_stages(y_ref, u_ref, d_ref, wglu_ref, gs_ref)
              + _mix_stages(om_ref, x_ref, gm_ref, wo_ref, lng_ref, lnb_ref, wxq_ref)
              + [scores, softmax, values, project, norm2])
    _run_stages(stages, _row_states(x_ref, MIX_ROWS))


def _mix_mem(y_ssm, u, d_skip, w_glu, g_ssm, o_mla, x2, g_mla, w_o, ln_g, ln_b, w_xq, mem_k, mem_v, w_xo, tn):
    n = x2.shape[0]
    row = lambda w: pl.BlockSpec((tn, w), lambda i: (i, 0))
    glu_consts = (d_skip, w_glu, g_ssm)
    consts = (g_mla, w_o, ln_g, ln_b, w_xq, mem_k, mem_v, w_xo)
    return pl.pallas_call(
        _mix_mem_body,
        grid=(n // tn,),
        in_specs=[row(SSM_WIDTH), row(SSM_WIDTH)] + [_const(c.shape) for c in glu_consts]
        + [row(MLA_WIDTH), row(D_MODEL)] + [_const(c.shape) for c in consts],
        out_specs=row(D_MODEL),
        out_shape=jax.ShapeDtypeStruct((n, D_MODEL), F32),
        compiler_params=_params("parallel"),
        name="mix_mem",
    )(y_ssm, u, *glu_consts, o_mla, x2, *consts)


def _mem_attn_cache_body(qx_ref, mk_hbm, mv_hbm, o_ref, kbuf, vbuf, sems, *, seq, nsub):
    b = pl.program_id(0)
    nb = pl.num_programs(0)

    def copies(step, slot):
        return [pltpu.make_async_copy(src.at[step * nsub + j, :, h, :], buf.at[slot, j, h], sems.at[slot, j, t, h])
                for t, (src, buf) in enumerate(((mk_hbm, kbuf), (mv_hbm, vbuf)))
                for j in range(nsub) for h in range(X_HEADS)]

    @pl.when(b == 0)
    def _():
        for c in copies(0, 0):
            c.start()

    @pl.when(b + 1 < nb)
    def _():
        for c in copies(b + 1, (b + 1) % 2):
            c.start()

    slot = b % 2
    for c in copies(b, slot):
        c.wait()
    jh = [(j, h) for j in range(nsub) for h in range(X_HEADS)]
    where = [(slice(j * seq, (j + 1) * seq), slice(h * X_HEAD_DIM, (h + 1) * X_HEAD_DIM)) for j, h in jh]
    outs = _mem_attn_heads([qx_ref[r, c] for r, c in where], [kbuf[slot, j, h] for j, h in jh],
                           [vbuf[slot, j, h] for j, h in jh])
    for (r, c), o in zip(where, outs):
        o_ref[r, c] = o


def _mem_attn_cache(qx, mem_k, mem_v, seq, nsub):
    n = qx.shape[0]
    nb = mem_k.shape[0]
    assert nb % nsub == 0
    row = pl.BlockSpec((nsub * seq, D_MODEL), lambda b: (b, 0))
    hbm = pl.BlockSpec(memory_space=pl.ANY)
    buf = pltpu.VMEM((2, nsub, X_HEADS, N_MEM, X_HEAD_DIM), F32)
    return pl.pallas_call(
        functools.partial(_mem_attn_cache_body, seq=seq, nsub=nsub),
        grid=(nb // nsub,),
        in_specs=[row, hbm, hbm],
        out_specs=row,
        out_shape=jax.ShapeDtypeStruct((n, D_MODEL), BF16),
        scratch_shapes=[buf, buf, pltpu.SemaphoreType.DMA((2, nsub, 2, X_HEADS))],
        compiler_params=_params("arbitrary"),
        name="mem_attn_cache",
    )(qx, mem_k, mem_v)


def _mlp_body(*refs, ff_blk, second_norm):
    if second_norm:
        h1_ref, ox_ref, wxo_ref, lng_ref, lnb_ref, w1_ref, w2_ref, y_ref = refs

        def head(st):
            att = _dot(ox_ref[st["r"], :], wxo_ref[...])
            st["h2"] = _ln(ALPHA * h1_ref[st["r"], :] + att, lng_ref[1:2, :], lnb_ref[1:2, :])
    else:
        h2_ref, lng_ref, lnb_ref, w1_ref, w2_ref, y_ref = refs

        def head(st):
            st["h2"] = h2_ref[st["r"], :]

    def up(c):
        def stage(st):
            z = jnp.maximum(_dot(st["h2"].astype(BF16), w1_ref[:, c:c + ff_blk]), 0.0)
            st["z"] = (z * z).astype(BF16)
        return stage

    def down(c):
        def stage(st):
            d = _dot(st.pop("z"), w2_ref[c:c + ff_blk, :])
            st["acc"] = st["acc"] + d if "acc" in st else d
        return stage

    def tail(st):
        y_ref[st["r"], :] = _ln(ALPHA * st.pop("h2") + st.pop("acc"), lng_ref[2:3, :], lnb_ref[2:3, :])

    stages = [head] + [s for c in range(0, D_FF, ff_blk) for s in (up(c), down(c))] + [tail]
    _run_stages(stages, _row_states(y_ref, MLP_ROWS))


def _mlp(h, ox, w_xo, ln_g, ln_b, w1, w2, tn):
    n = h.shape[0]
    row = pl.BlockSpec((tn, D_MODEL), lambda i: (i, 0))
    once = lambda c: pl.BlockSpec(c.shape, lambda *_: (0,) * c.ndim, pipeline_mode=pl.Buffered(1))
    second_norm = ox is not None
    rows = (h, ox) if second_norm else (h,)
    consts = ((w_xo,) if second_norm else ()) + (ln_g, ln_b, w1, w2)
    return pl.pallas_call(
        functools.partial(_mlp_body, ff_blk=MLP_FF_BLOCK, second_norm=second_norm),
        grid=(n // tn,),
        in_specs=[row] * len(rows) + [once(c) for c in consts],
        out_specs=row,
        out_shape=jax.ShapeDtypeStruct((n, D_MODEL), F32),
        compiler_params=_params("parallel"),
        name="mlp",
    )(*rows, *consts)


def _mem_kv_body(mem_ref, wk_ref, wv_ref, k_ref, v_ref):
    m = mem_ref[...].astype(BF16)
    k_ref[...] = _dot(m, wk_ref[...])
    v_ref[...] = _dot(m, wv_ref[...])


def _mem_kv(mem2, wk, wv):
    n = mem2.shape[0]
    out = jax.ShapeDtypeStruct((n, D_MODEL), F32)
    return pl.pallas_call(
        _mem_kv_body,
        grid=(1,),
        in_specs=[_const(mem2.shape), _const(wk.shape), _const(wv.shape)],
        out_specs=[_const((n, D_MODEL))] * 2,
        out_shape=[out, out],
        compiler_params=_params("arbitrary"),
        name="mem_kv",
    )(mem2, wk, wv)


def _state_to_cols(s):
    return jnp.transpose(s.reshape(s.shape[0], S5_COLS, -1), (1, 0, 2))


def _state_from_cols(s):
    return jnp.transpose(s, (1, 0, 2)).reshape(s.shape[1], SSM_GROUPS, SSM_STATE)


def _layer(x2, pos, nseq, h0r, h0i, mem_k, mem_v, caches, wts, s5_ops, tn, scan_ct):
    n = x2.shape[0]
    seq = n // nseq
    prompt = caches is None
    u, q, k, v, ckv, kpe = _project(x2, pos, wts["inv"], wts["w_in"], wts["g_q"], wts["w_q"], wts["g_kv"],
                                    wts["w_k"], wts["w_vt"] if prompt else wts["w_v"], min(n, PROJ_TILE),
                                    v_transposed=prompt, consecutive=prompt, kpe_cache_shape=prompt)
    y_ssm, fr, fi = _s5_scan(u, s5_ops, _state_to_cols(h0r), _state_to_cols(h0i), nseq, scan_ct)
    glu_args = (y_ssm, u, wts["d_skip"], wts["w_glu"], wts["g_out_ssm"])
    if prompt:
        o_mla = _attention(q, k, v, ATTN_HEADS)
    else:
        o_mla = _attention_cached(q, k, v, caches[0], caches[1], wts["w_k"], wts["w_v"], seq, nsub=CACHE_SEQS)
    mix_args = (o_mla, x2, wts["g_out_mla"], wts["w_o"], wts["ln_g"], wts["ln_b"], wts["w_xq"])
    if prompt:
        h, ox = _mix_mem(*glu_args, *mix_args, mem_k, mem_v, wts["w_xo"], min(n, MIX_TILE)), None
    else:
        h, qx = _mix(_glu(*glu_args, tn), *mix_args, tn)
        ox = _mem_attn_cache(qx, mem_k, mem_v, seq, nsub=CACHE_SEQS)
    y = _mlp(h, ox, wts["w_xo"], wts["ln_g"], wts["ln_b"], wts["w_ff1"], wts["w_ff2"], min(n, MLP_TILE))
    return y, ckv, kpe, _state_from_cols(fr), _state_from_cols(fi)


def kernel(x_prompt, x_sample, mem_prompt, cache_mla_ckv, cache_mla_kpe, state_ssm_re, state_ssm_im, cache_mem_k, cache_mem_v, w_in, g_q, w_q_up, g_kv, w_kv_up, a_re, a_im, b_re, b_im, c_re, c_im, d_skip, log_dt, w_glu, g_out_ssm, g_out_mla, w_o, w_xq, w_xk, w_xv, w_xo, w_ff1, w_ff2, ln_g, ln_b):
    assert w_in.shape[0] == DEPTH == 1
    nbp, sp, _ = x_prompt.shape
    nbs, sd, _ = x_sample.shape
    past = cache_mla_ckv.shape[2]
    assert nbp == 1

    wq = jnp.pad(w_q_up[0], ((0, 0), (0, 0), (0, QK_PAD - QK_NOPE - QK_ROPE)))
    wk = w_kv_up[0][:, :, :QK_NOPE].reshape(KV_LORA, -1).astype(BF16)
    wv = w_kv_up[0][:, :, QK_NOPE:].reshape(KV_LORA, -1).astype(BF16)
    inv = ROPE_THETA ** (-jnp.arange(ROPE_HALF, dtype=F32) / ROPE_HALF)
    wts = {
        "inv": jnp.tile(inv, LANES // ROPE_HALF).reshape(1, LANES),
        "w_in": jnp.pad(w_in[0], ((0, 0), (0, LANES - QK_ROPE))).astype(BF16),
        "g_q": g_q[0].reshape(1, -1),
        "w_q": wq.reshape(Q_LORA, MLA_HEADS * QK_PAD).astype(BF16),
        "g_kv": g_kv[0].reshape(1, -1),
        "w_k": wk,
        "w_v": wv,
        "w_vt": wv.T,
        "d_skip": d_skip[0].reshape(1, -1),
        "w_glu": w_glu[0].astype(BF16),
        "g_out_ssm": g_out_ssm[0].reshape(1, -1),
        "g_out_mla": g_out_mla[0].reshape(1, -1),
        "w_o": w_o[0].astype(BF16),
        "w_xq": w_xq[0].reshape(D_MODEL, D_MODEL).astype(BF16),
        "w_xo": w_xo[0].reshape(D_MODEL, D_MODEL).astype(BF16),
        "w_ff1": w_ff1[0].astype(BF16),
        "w_ff2": w_ff2[0].astype(BF16),
        "ln_g": ln_g[0],
        "ln_b": ln_b[0],
    }
    s5_ops = _s5_prep(a_re[0], a_im[0], b_re[0], b_im[0], c_re[0], c_im[0], log_dt[0])

    mk, mv = _mem_kv(mem_prompt.reshape(nbp * N_MEM, D_MODEL),
                     w_xk[0].reshape(D_MODEL, D_MODEL).astype(BF16),
                     w_xv[0].reshape(D_MODEL, D_MODEL).astype(BF16))
    zero = jnp.zeros((nbp, SSM_GROUPS, SSM_STATE), F32)
    pos_p = jnp.arange(sp, dtype=F32).reshape(sp, 1)
    yp, ckv_p, kpe_p, sre_p, sim_p = _layer(
        x_prompt.reshape(sp, D_MODEL), pos_p, nbp, zero, zero, mk, mv, None,
        wts, s5_ops, tn=TOKEN_TILE, scan_ct=min(sp // S5_T, SCAN_CHUNKS))

    pos_s = jnp.tile(past + jnp.arange(sd, dtype=F32), nbs).reshape(nbs * sd, 1)
    caches = (cache_mla_ckv[0], cache_mla_kpe[0])
    ys, ckv_s, kpe_s, sre_s, sim_s = _layer(
        x_sample.reshape(nbs * sd, D_MODEL), pos_s, nbs,
        state_ssm_re[0], state_ssm_im[0], cache_mem_k[0], cache_mem_v[0], caches,
        wts, s5_ops, tn=TOKEN_TILE, scan_ct=sd // S5_T)

    return (yp.reshape(nbp, sp, D_MODEL), ys.reshape(nbs, sd, D_MODEL),
            ckv_p.reshape(1, nbp, sp, KV_LORA), kpe_p.reshape(1, nbp, sp, QK_ROPE),
            sre_p.reshape(1, nbp, SSM_GROUPS, SSM_STATE), sim_p.reshape(1, nbp, SSM_GROUPS, SSM_STATE),
            mk.reshape(1, nbp, N_MEM, X_HEADS, X_HEAD_DIM), mv.reshape(1, nbp, N_MEM, X_HEADS, X_HEAD_DIM),
            ckv_s.reshape(1, nbs, sd, KV_LORA), kpe_s.reshape(1, nbs, sd, QK_ROPE),
            sre_s.reshape(1, nbs, SSM_GROUPS, SSM_STATE), sim_s.reshape(1, nbs, SSM_GROUPS, SSM_STATE))
```

```python
import functools
import math

import jax
import jax.numpy as jnp
from jax import lax
from jax.experimental import pallas as pl
from jax.experimental.pallas import tpu as pltpu

F32 = jnp.float32
BF16 = jnp.bfloat16

D_MODEL = 1024
DEPTH = 1
CHUNK = 64
SSM_WIDTH = 512
SSM_GROUP = 16
SSM_GROUPS = 32
SSM_STATE = 64
MLA_HEADS = 4
QK_NOPE = 128
QK_ROPE = 64
V_HEAD = 128
MLA_WIDTH = MLA_HEADS * V_HEAD
Q_LORA = 384
KV_LORA = 256
ROPE_THETA = 10000.0
MLA_SCALE = (QK_NOPE + QK_ROPE) ** -0.5
N_MEM = 256
X_HEADS = 4
X_HEAD_DIM = D_MODEL // X_HEADS
D_FF = 4 * D_MODEL
ALPHA = (2 * DEPTH) ** 0.25
EPS = 1e-5
NEG_INF = -1e30

LANES = 128
MXU_DEPTH = 256
VMEM_BYTES = 64 * 1024 * 1024
VMEM_LIMIT = VMEM_BYTES - 8 * 1024 * 1024

QK_PAD = MXU_DEPTH
S5_T = MXU_DEPTH // SSM_GROUP
S5_COL_GROUPS = LANES // SSM_GROUP
S5_COLS = SSM_GROUPS // S5_COL_GROUPS
S5_SW = S5_COL_GROUPS * SSM_STATE
ROPE_HALF = QK_ROPE // 2
Q_SCALE = MLA_SCALE * math.log2(math.e)
ATTN_T = 512
ATTN_HEADS = 2
ATTN_TQ = 2 * ATTN_T
ATTN_QSPLIT = 4
MLP_TILE = 1024
MLP_ROWS = 512
MLP_FF_BLOCK = 1024
CACHE_SEQS = 4
MIX_TILE = 1024
MIX_ROWS = 256
PROJ_TILE = 1024
PROJ_ROWS = 256
TOKEN_TILE = 512
SCAN_CHUNKS = 512
SCAN_UNROLL = 8

_NT = (((1,), (1,)), ((), ()))


def _rms(x, g):
    return x * lax.rsqrt(jnp.mean(x * x, -1, keepdims=True) + EPS) * g


def _ln(x, g, b):
    mu = jnp.mean(x, -1, keepdims=True)
    xc = x - mu
    var = jnp.mean(xc * xc, -1, keepdims=True)
    return xc * lax.rsqrt(var + EPS) * g + b


def _dot(a, b):
    return jnp.dot(a, b, preferred_element_type=F32)


def _dot_nt(a, b):
    return lax.dot_general(a, b, _NT, preferred_element_type=F32)


def _split(a):
    hi = a.astype(BF16)
    return hi, (a - hi.astype(F32)).astype(BF16)


def _dot_split(a, b):
    return _dot(a[0], b[0]) + _dot(a[0], b[1]) + _dot(a[1], b[0])


def _params(*sem):
    return pltpu.CompilerParams(dimension_semantics=sem, vmem_limit_bytes=VMEM_LIMIT)


def _const(shape):
    n = len(shape)
    return pl.BlockSpec(shape, lambda *_: (0,) * n)


def _proj_body(x_ref, pos_ref, inv_ref, w_in_ref, gq_ref, wq_ref, gkv_ref, wk_ref, wv_ref,
               u_ref, q_ref, k_ref, v_ref, ckv_ref, kpe_ref, tc_ref, ts_ref, *, v_transposed, consecutive):
    rb = PROJ_ROWS
    assert x_ref.shape[0] % rb == 0 and ATTN_T % rb == 0
    blocks = [slice(r0, r0 + rb) for r0 in range(0, x_ref.shape[0], rb)]
    c0 = SSM_WIDTH + Q_LORA

    lane = lax.broadcasted_iota(jnp.int32, (1, LANES), 1)
    live = lane < QK_ROPE
    sign = jnp.where(lane < ROPE_HALF, -1.0, 1.0)
    inv = inv_ref[...]
    if consecutive:
        @pl.when(pl.program_id(0) == 0)
        def _():
            r = lax.broadcasted_iota(jnp.int32, (x_ref.shape[0], 1), 0).astype(F32)
            tc_ref[...] = jnp.where(live, jnp.cos(r * inv), 0.0)
            ts_ref[...] = jnp.where(live, jnp.sin(r * inv), 0.0)

    projs = [_dot(x_ref[r, :].astype(BF16), w_in_ref[...]) for r in blocks]

    if consecutive:
        base = pos_ref[0:1, :] * inv
        ca, sa = jnp.cos(base), jnp.sin(base)
        cos_ts = [ca * tc_ref[r, :] - sa * ts_ref[r, :] for r in blocks]
        sin_ts = [(sa * sign) * tc_ref[r, :] + (ca * sign) * ts_ref[r, :] for r in blocks]
    else:
        angs = [pos_ref[r, :] * inv for r in blocks]
        cos_ts = [jnp.where(live, jnp.cos(a), 0.0) for a in angs]
        sin_ts = [jnp.where(live, jnp.sin(a) * sign, 0.0) for a in angs]

    def rope(c2, cos_t, sin_t):
        swapped = jnp.where(lane < ROPE_HALF, pltpu.roll(c2, LANES - ROPE_HALF, 1), pltpu.roll(c2, ROPE_HALF, 1))
        return c2 * cos_t + swapped * sin_t

    for r, p, cos_t, sin_t in zip(blocks, projs, cos_ts, sin_ts):
        cq = _rms(p[:, SSM_WIDTH:c0], gq_ref[...]).astype(BF16)
        ckv = _rms(p[:, c0:c0 + KV_LORA], gkv_ref[...])
        ckv_b = ckv.astype(BF16)
        q = _dot(cq, wq_ref[...]) * Q_SCALE
        kn = _dot(ckv_b, wk_ref[...])
        if v_transposed:
            lanes = slice(r.start % ATTN_T, r.start % ATTN_T + rb)
            v_ref[r.start // ATTN_T, :, lanes] = _dot_nt(wv_ref[...], ckv_b).astype(BF16)
        else:
            v_ref[r, :] = _dot(ckv_b, wv_ref[...]).astype(BF16)
        u_ref[r, :] = p[:, :SSM_WIDTH]
        ckv_ref[r, :] = ckv
        kpe = rope(p[:, c0 + KV_LORA:], cos_t, sin_t)
        kpe_ref[r, :] = kpe[:, :QK_ROPE]
        kpe_b = kpe.astype(BF16)
        for h in range(MLA_HEADS):
            a = h * QK_PAD
            q_ref[r, a:a + QK_NOPE] = q[:, a:a + QK_NOPE].astype(BF16)
            q_ref[r, a + QK_NOPE:a + QK_PAD] = rope(q[:, a + QK_NOPE:a + QK_PAD], cos_t, sin_t).astype(BF16)
            k_ref[r, a:a + QK_NOPE] = kn[:, h * QK_NOPE:(h + 1) * QK_NOPE].astype(BF16)
            k_ref[r, a + QK_NOPE:a + QK_PAD] = kpe_b


def _project(x2, pos, inv, w_in, gq, wq, gkv, wk, wv, tn, v_transposed, consecutive, kpe_cache_shape):
    n = x2.shape[0]
    row = lambda w: pl.BlockSpec((tn, w), lambda i: (i, 0))
    if kpe_cache_shape:
        kpe_spec = pl.BlockSpec((None, None, tn, QK_ROPE), lambda i: (0, 0, i, 0))
        kpe_shape = jax.ShapeDtypeStruct((1, 1, n, QK_ROPE), F32)
    else:
        kpe_spec, kpe_shape = row(QK_ROPE), jax.ShapeDtypeStruct((n, QK_ROPE), F32)
    if v_transposed:
        assert tn % ATTN_T == 0
        v_spec = pl.BlockSpec((tn // ATTN_T, MLA_WIDTH, ATTN_T), lambda i: (i, 0, 0))
        v_shape = jax.ShapeDtypeStruct((n // ATTN_T, MLA_WIDTH, ATTN_T), BF16)
    else:
        v_spec, v_shape = row(MLA_WIDTH), jax.ShapeDtypeStruct((n, MLA_WIDTH), BF16)
    return pl.pallas_call(
        functools.partial(_proj_body, v_transposed=v_transposed, consecutive=consecutive),
        grid=(n // tn,),
        scratch_shapes=[pltpu.VMEM((tn, LANES), F32)] * 2,
        in_specs=[row(D_MODEL), row(1), _const(inv.shape), _const(w_in.shape), _const(gq.shape),
                  _const(wq.shape), _const(gkv.shape), _const(wk.shape), _const(wv.shape)],
        out_specs=[row(SSM_WIDTH), row(MLA_HEADS * QK_PAD), row(MLA_HEADS * QK_PAD),
                   v_spec, row(KV_LORA), kpe_spec],
        out_shape=[jax.ShapeDtypeStruct((n, SSM_WIDTH), F32),
                   jax.ShapeDtypeStruct((n, MLA_HEADS * QK_PAD), BF16),
                   jax.ShapeDtypeStruct((n, MLA_HEADS * QK_PAD), BF16),
                   v_shape,
                   jax.ShapeDtypeStruct((n, KV_LORA), F32),
                   kpe_shape],
        compiler_params=_params("arbitrary"),
        name="proj",
    )(x2, pos, inv, w_in, gq, wq, gkv, wk, wv)


def _s5_prep_body(ar_row, ai_row, ldt_row, bt_re, bt_im, ct_re, ct_im,
                  wt_ref, v_ref, z_ref, are_ref, aim_ref):
    sw = S5_SW
    arr, air, dtr = ar_row[0], ai_row[0], jnp.exp(ldt_row[0])

    mag = jnp.exp(arr * dtr)
    lr, li = mag * jnp.cos(air * dtr), mag * jnp.sin(air * dtr)
    nr, ni = lr - 1.0, li
    den = arr * arr + air * air
    f_re, f_im = (nr * arr + ni * air) / den, (ni * arr - nr * air) / den

    same_b = (lax.broadcasted_iota(jnp.int32, (LANES, sw), 0) // SSM_GROUP
              == lax.broadcasted_iota(jnp.int32, (LANES, sw), 1) // SSM_STATE)
    br = jnp.where(same_b, bt_re[0], 0.0)
    bi = jnp.where(same_b, bt_im[0], 0.0)
    bb_re = f_re * br - f_im * bi
    bb_im = f_re * bi + f_im * br
    same_c = (lax.broadcasted_iota(jnp.int32, (sw, LANES), 0) // SSM_STATE
              == lax.broadcasted_iota(jnp.int32, (sw, LANES), 1) // SSM_GROUP)
    cr = jnp.where(same_c, ct_re[0], 0.0)
    ci = jnp.where(same_c, ct_im[0], 0.0)
    cr_s, ci_s = _split(cr), _split(ci)

    e = lax.broadcasted_iota(jnp.int32, (2 * S5_T, 1), 0).astype(F32)
    pm = jnp.exp(arr * dtr * e)
    pw_re, pw_im = pm * jnp.cos(air * dtr * e), pm * jnp.sin(air * dtr * e)
    pt_re, pt_im = pw_re.T, pw_im.T

    lag_ops = []
    for lag in range(S5_T):
        p_re, p_im = pw_re[lag:lag + 1, :], pw_im[lag:lag + 1, :]
        k_re = p_re * bb_re - p_im * bb_im
        k_im = p_re * bb_im + p_im * bb_re
        i = S5_T - 1 - lag
        v_ref[0, i * LANES:(i + 1) * LANES, :sw] = k_re.astype(BF16)
        v_ref[0, i * LANES:(i + 1) * LANES, sw:] = k_im.astype(BF16)
        lag_ops.append(_dot_split(_split(k_re), cr_s) - _dot_split(_split(k_im), ci_s))

        q_re, q_im = pt_re[:, lag + 1:lag + 2], pt_im[:, lag + 1:lag + 2]
        z_ref[0, :sw, lag * LANES:(lag + 1) * LANES] = (cr * q_re - ci * q_im).astype(BF16)
        z_ref[0, sw:, lag * LANES:(lag + 1) * LANES] = (-(cr * q_im + ci * q_re)).astype(BF16)

    zero = jnp.zeros((LANES, LANES), BF16)
    for d in range(S5_T // 2):
        wt_ref[0, d, :LANES, :LANES] = lag_ops[2 * d].astype(BF16)
        wt_ref[0, d, :LANES, LANES:] = lag_ops[2 * d + 1].astype(BF16)
        wt_ref[0, d, LANES:, :LANES] = lag_ops[2 * d - 1].astype(BF16) if d else zero
        wt_ref[0, d, LANES:, LANES:] = lag_ops[2 * d].astype(BF16)

    are_ref[0] = pw_re[S5_T:S5_T + 1, :]
    aim_ref[0] = pw_im[S5_T:S5_T + 1, :]


def _s5_prep(a_re, a_im, b_re, b_im, c_re, c_im, log_dt):
    nc, r, t, h, p, sw = S5_COLS, S5_COL_GROUPS, S5_T, SSM_GROUP, SSM_STATE, S5_SW
    ldt = jnp.repeat(log_dt, p)
    bt = lambda b: jnp.tile(jnp.transpose(b.reshape(nc, r, p, h), (0, 1, 3, 2)).reshape(nc, r * h, p), (1, 1, r))
    ct = lambda c: jnp.tile(jnp.transpose(c.reshape(nc, r, h, p), (0, 1, 3, 2)).reshape(nc, r * p, h), (1, 1, r))
    args = (a_re.reshape(nc, 1, sw), a_im.reshape(nc, 1, sw), ldt.reshape(nc, 1, sw),
            bt(b_re), bt(b_im), ct(c_re), ct(c_im))
    blk = lambda s: pl.BlockSpec((1,) + s[1:], lambda i: (i,) + (0,) * (len(s) - 1))
    outs = [((nc, t // 2, 2 * LANES, 2 * LANES), BF16), ((nc, t * LANES, 2 * sw), BF16),
            ((nc, 2 * sw, t * LANES), BF16), ((nc, 1, sw), F32), ((nc, 1, sw), F32)]
    return pl.pallas_call(
        _s5_prep_body,
        grid=(nc,),
        in_specs=[blk(a.shape) for a in args],
        out_specs=[blk(s) for s, _ in outs],
        out_shape=[jax.ShapeDtypeStruct(s, d) for s, d in outs],
        compiler_params=_params("parallel"),
        name="s5_prep",
    )(*args)


def _s5_scan_body(u_ref, wt_ref, v_ref, z_ref, are_ref, aim_ref, h0r_ref, h0i_ref,
                  y_ref, fr_ref, fi_ref, lhs, sre, sim, xre, xim, cr, ci, *, nseq, ct):
    t = pl.program_id(1)
    sw = S5_SW
    seq = ct * S5_T

    @pl.when(t == 0)
    def _():
        cr[...] = h0r_ref[0]
        ci[...] = h0i_ref[0]

    def token_rows(cc, i):
        if nseq == 1:
            return pl.ds(i, ct, stride=S5_T), slice(None)
        return pl.ds(cc * S5_T + i, nseq, stride=seq), slice(cc * nseq, (cc + 1) * nseq)

    for cc in range(1 if nseq == 1 else ct):
        for i in range(S5_T):
            tok, crow = token_rows(cc, i)
            lhs[crow, i * LANES:(i + 1) * LANES] = u_ref[tok, :].astype(BF16)

    s = _dot(lhs[...], v_ref[0])
    sre[...] = s[:, :sw]
    sim[...] = s[:, sw:]

    a_re, a_im = are_ref[0], aim_ref[0]

    def step(c, carry):
        x_re, x_im = carry
        r = pl.ds(c * nseq, nseq)
        xre[r, :] = x_re
        xim[r, :] = x_im
        n_re = a_re * x_re - a_im * x_im + sre[r, :]
        n_im = a_re * x_im + a_im * x_re + sim[r, :]
        return n_re, n_im

    x_re, x_im = lax.fori_loop(0, ct, step, (cr[...], ci[...]), unroll=min(ct, SCAN_UNROLL))
    cr[...] = x_re
    ci[...] = x_im
    fr_ref[0] = x_re
    fi_ref[0] = x_im

    xb_re, xb_im = xre[...].astype(BF16), xim[...].astype(BF16)
    for jp in range(S5_T // 2):
        cols = slice(jp * 2 * LANES, (jp + 1) * 2 * LANES)
        acc = _dot(xb_re, z_ref[0, :sw, cols]) + _dot(xb_im, z_ref[0, sw:, cols])
        for ip in range(jp + 1):
            acc += _dot(lhs[:, ip * 2 * LANES:(ip + 1) * 2 * LANES], wt_ref[0, jp - ip])
        for cc in range(1 if nseq == 1 else ct):
            for jj in range(2):
                tok, crow = token_rows(cc, 2 * jp + jj)
                y_ref[tok, :] = acc[crow, jj * LANES:(jj + 1) * LANES]


def _s5_scan(u, col_ops, h0r, h0i, nseq, ct):
    wt, v, z, are, aim = col_ops
    n = u.shape[0]
    cps = n // nseq // S5_T
    assert nseq == 1 or ct == cps
    nc = nseq * ct
    sw = S5_SW
    tile = pl.BlockSpec((nc * S5_T, LANES), lambda c, t: (t, c))
    op = lambda a: pl.BlockSpec((1,) + a.shape[1:], lambda c, t: (c,) + (0,) * (a.ndim - 1))
    st = pl.BlockSpec((1, nseq, sw), lambda c, t: (c, 0, 0))
    st_shape = jax.ShapeDtypeStruct((S5_COLS, nseq, sw), F32)
    return pl.pallas_call(
        functools.partial(_s5_scan_body, nseq=nseq, ct=ct),
        grid=(S5_COLS, cps // ct),
        in_specs=[tile, op(wt), op(v), op(z), op(are), op(aim), st, st],
        out_specs=[tile, st, st],
        out_shape=[jax.ShapeDtypeStruct(u.shape, F32), st_shape, st_shape],
        scratch_shapes=[pltpu.VMEM((nc, S5_T * LANES), BF16)] + [pltpu.VMEM((nc, sw), F32)] * 4
        + [pltpu.VMEM((nseq, sw), F32)] * 2,
        compiler_params=_params("arbitrary", "arbitrary"),
        name="s5_scan",
    )(u, wt, v, z, are, aim, h0r, h0i)


def _run_stages(stages, states):
    for t in range(len(stages) + len(states) - 1):
        for b, st in enumerate(states):
            if 0 <= t - b < len(stages):
                stages[t - b](st)


def _row_states(ref, rows):
    assert ref.shape[0] % rows == 0
    return [{"r": slice(r0, r0 + rows)} for r0 in range(0, ref.shape[0], rows)]


def _glu_stages(y_ref, u_ref, d_ref, w_ref, g_ref):
    def act(st):
        st["act"] = jax.nn.gelu(y_ref[st["r"], :] + d_ref[...] * u_ref[st["r"], :]).astype(BF16)

    def glu(st):
        st["gl"] = _dot(st.pop("act"), w_ref[...])

    def gate(st):
        gl = st.pop("gl")
        st["os"] = _rms(gl[:, :SSM_WIDTH] * jax.nn.sigmoid(gl[:, SSM_WIDTH:]), g_ref[...]).astype(BF16)

    return [act, glu, gate]


def _glu_body(y_ref, u_ref, d_ref, w_ref, g_ref, o_ref):
    def store(st):
        o_ref[st["r"], :] = st.pop("os")

    _run_stages(_glu_stages(y_ref, u_ref, d_ref, w_ref, g_ref) + [store], _row_states(y_ref, MIX_ROWS))


def _glu(y, u, d, w, g, tn):
    n = y.shape[0]
    row = pl.BlockSpec((tn, SSM_WIDTH), lambda i: (i, 0))
    return pl.pallas_call(
        _glu_body,
        grid=(n // tn,),
        in_specs=[row, row, _const(d.shape), _const(w.shape), _const(g.shape)],
        out_specs=row,
        out_shape=jax.ShapeDtypeStruct((n, SSM_WIDTH), BF16),
        compiler_params=_params("parallel"),
        name="glu",
    )(y, u, d, w, g)


def _attn_body(q_ref, k_ref, vt_ref, o_ref, s0, s1, mt0, mt1, m_sc, l_sc, acc_sc, *, heads):
    i = pl.program_id(1)
    t = ATTN_T
    m_sc[...] = jnp.full(m_sc.shape, -jnp.inf, F32)
    l_sc[...] = jnp.zeros(l_sc.shape, F32)
    acc_sc[...] = jnp.zeros(acc_sc.shape, F32)

    qb = q_ref.shape[0] // ATTN_QSPLIT
    units = [(h, c * qb) for c in range(ATTN_QSPLIT) for h in range(heads)]

    def scores(tile, s_buf, mt_buf, unit, diag=None):
        h, c0 = unit
        r = pl.ds(pl.multiple_of(tile * t, t), t)
        s = _dot_nt(k_ref[r, h * QK_PAD:(h + 1) * QK_PAD],
                    q_ref[c0:c0 + qb, h * QK_PAD:(h + 1) * QK_PAD])
        if diag is not None:
            kc = (diag * t + lax.broadcasted_iota(jnp.int32, s.shape, 0)) // CHUNK
            qc = (c0 + lax.broadcasted_iota(jnp.int32, s.shape, 1)) // CHUNK
            s = jnp.where(kc <= qc, s, NEG_INF)
        s_buf[h, :, c0:c0 + qb] = s
        mt_buf[h, :, c0:c0 + qb] = jnp.max(s, 0, keepdims=True)

    def absorb(tile, s_buf, mt_buf, unit):
        h, c0 = unit
        cols = slice(c0, c0 + qb)
        m_prev = m_sc[h, :, cols]
        m_new = jnp.maximum(m_prev, mt_buf[h, :, cols])
        p = jnp.exp2(s_buf[h, :, cols] - m_new)
        alpha = jnp.exp2(m_prev - m_new)
        l_sc[h, :, cols] = alpha * l_sc[h, :, cols] + jnp.sum(p, 0, keepdims=True)
        m_sc[h, :, cols] = m_new
        pv = _dot(vt_ref[tile, h * V_HEAD:(h + 1) * V_HEAD, :], p.astype(BF16))
        acc_sc[h, :, cols] = alpha * acc_sc[h, :, cols] + pv

    assert q_ref.shape[0] == 2 * t
    for u in units:
        scores(2 * i, s0, mt0, u, diag=0)

    def pair(jj, c):
        prev = jnp.where(jj == 0, 2 * i, 2 * jj - 1)
        for u in units:
            scores(2 * jj, s1, mt1, u)
            absorb(prev, s0, mt0, u)
        for u in units:
            scores(2 * jj + 1, s0, mt0, u)
            absorb(2 * jj, s1, mt1, u)
        return c

    lax.fori_loop(0, i, pair, 0)
    last = jnp.where(i == 0, 0, 2 * i - 1)
    live = [u for u in units if u[1] + qb > t]
    for k, u in enumerate(units):
        if k < len(live):
            scores(2 * i + 1, s1, mt1, live[k], diag=1)
        absorb(last, s0, mt0, u)
    for u in live:
        absorb(2 * i + 1, s1, mt1, u)

    for h in range(heads):
        o_ref[:, h * V_HEAD:(h + 1) * V_HEAD] = (acc_sc[h] / l_sc[h]).T


def _attention(q, k, vt, heads):
    n = q.shape[0]
    t = ATTN_T
    tq = ATTN_TQ
    assert t % CHUNK == 0 and n % tq == 0 and MLA_HEADS % heads == 0 and tq == 2 * t
    once = pl.Buffered(1)
    return pl.pallas_call(
        functools.partial(_attn_body, heads=heads),
        grid=(MLA_HEADS // heads, n // tq),
        in_specs=[pl.BlockSpec((tq, heads * QK_PAD), lambda g, i: (i, g)),
                  pl.BlockSpec((n, heads * QK_PAD), lambda g, i: (0, g), pipeline_mode=once),
                  pl.BlockSpec((n // t, heads * V_HEAD, t), lambda g, i: (0, g, 0), pipeline_mode=once)],
        out_specs=pl.BlockSpec((tq, heads * V_HEAD), lambda g, i: (i, g)),
        out_shape=jax.ShapeDtypeStruct((n, MLA_WIDTH), F32),
        scratch_shapes=[pltpu.VMEM((heads, t, tq), F32)] * 2 + [pltpu.VMEM((heads, 1, tq), F32)] * 4
        + [pltpu.VMEM((heads, V_HEAD, tq), F32)],
        compiler_params=_params("arbitrary", "arbitrary"),
        name="attn",
    )(q, k, vt)


def _attn_cached_body(q_ref, kn_ref, vn_ref, ckv_ref, kpe_ref, wk_ref, wv_ref, o_ref, *, past, seq, nsub):
    heads = range(MLA_HEADS)
    pos = past + lax.broadcasted_iota(jnp.int32, (seq, 1), 0)
    qc = jnp.concatenate([pos] * MLA_HEADS, axis=0) // CHUNK
    kc_past = lax.broadcasted_iota(jnp.int32, (1, past), 1) // CHUNK
    kc_new = (past + lax.broadcasted_iota(jnp.int32, (1, seq), 1)) // CHUNK
    seqs = range(nsub)
    rows = [slice(b * seq, (b + 1) * seq) for b in seqs]
    ckv = [ckv_ref[b].astype(BF16) for b in seqs]
    kpe = [kpe_ref[b].astype(BF16) for b in seqs]
    qa = [jnp.concatenate(
        [_dot_nt(q_ref[r, h * QK_PAD:h * QK_PAD + QK_NOPE], wk_ref[:, h * QK_NOPE:(h + 1) * QK_NOPE])
         for h in heads], axis=0).astype(BF16) for r in rows]
    qpe = [jnp.concatenate(
        [q_ref[r, h * QK_PAD + QK_NOPE:h * QK_PAD + QK_NOPE + QK_ROPE] for h in heads], axis=0) for r in rows]
    s_p = [_dot_nt(qa[b], ckv[b]) + _dot_nt(qpe[b], kpe[b]) for b in seqs]
    s_n = [jnp.concatenate(
        [_dot_nt(q_ref[r, h * QK_PAD:(h + 1) * QK_PAD], kn_ref[r, h * QK_PAD:(h + 1) * QK_PAD])
         for h in heads], axis=0) for r in rows]
    s_p = [jnp.where(kc_past <= qc, s, NEG_INF) for s in s_p]
    s_n = [jnp.where(kc_new <= qc, s, NEG_INF) for s in s_n]
    m = [jnp.maximum(jnp.max(a, -1, keepdims=True), jnp.max(c, -1, keepdims=True)) for a, c in zip(s_p, s_n)]
    e_p = [jnp.exp2(s - mm) for s, mm in zip(s_p, m)]
    e_n = [jnp.exp2(s - mm) for s, mm in zip(s_n, m)]
    l = [jnp.sum(a, -1, keepdims=True) + jnp.sum(c, -1, keepdims=True) for a, c in zip(e_p, e_n)]
    p_n = [(e / ll).astype(BF16) for e, ll in zip(e_n, l)]
    o_lat = [_dot((e / ll).astype(BF16), c).astype(BF16) for e, ll, c in zip(e_p, l, ckv)]
    for b in seqs:
        for h in heads:
            hr = slice(h * seq, (h + 1) * seq)
            cols = slice(h * V_HEAD, (h + 1) * V_HEAD)
            o_ref[rows[b], cols] = _dot(o_lat[b][hr], wv_ref[:, cols]) + _dot(p_n[b][hr], vn_ref[rows[b], cols])


def _attention_cached(q, k, v, ckv_cache, kpe_cache, wk, wv, seq, nsub):
    n = q.shape[0]
    nb, past, _ = ckv_cache.shape
    assert nb % nsub == 0
    row = lambda w: pl.BlockSpec((nsub * seq, w), lambda b: (b, 0))
    kern = functools.partial(_attn_cached_body, past=past, seq=seq, nsub=nsub)
    return pl.pallas_call(
        kern,
        grid=(nb // nsub,),
        in_specs=[row(MLA_HEADS * QK_PAD), row(MLA_HEADS * QK_PAD), row(MLA_WIDTH),
                  pl.BlockSpec((nsub, past, KV_LORA), lambda b: (b, 0, 0)),
                  pl.BlockSpec((nsub, past, QK_ROPE), lambda b: (b, 0, 0)),
                  _const(wk.shape), _const(wv.shape)],
        out_specs=row(MLA_WIDTH),
        out_shape=jax.ShapeDtypeStruct((n, MLA_WIDTH), F32),
        compiler_params=_params("parallel"),
        name="attn_cached",
    )(q, k, v, ckv_cache, kpe_cache, wk, wv)


def _mix_stages(om_ref, x_ref, gm_ref, wo_ref, lng_ref, lnb_ref, wxq_ref):
    def norm_mla(st):
        st["om"] = _rms(om_ref[st["r"], :], gm_ref[...]).astype(BF16)

    def project(st):
        st["a"] = _dot(st.pop("os"), wo_ref[:SSM_WIDTH, :]) + _dot(st.pop("om"), wo_ref[SSM_WIDTH:, :])

    def norm1(st):
        st["h1"] = _ln(ALPHA * x_ref[st["r"], :] + st.pop("a"), lng_ref[0:1, :], lnb_ref[0:1, :])

    def query(st):
        st["qx"] = _dot(st["h1"].astype(BF16), wxq_ref[...]).astype(BF16)

    return [norm_mla, project, norm1, query]


def _mix_body(os_ref, om_ref, x_ref, gm_ref, wo_ref, lng_ref, lnb_ref, wxq_ref, h1_ref, qx_ref):
    def load(st):
        st["os"] = os_ref[st["r"], :]

    def store(st):
        h1_ref[st["r"], :] = st.pop("h1")
        qx_ref[st["r"], :] = st.pop("qx")

    stages = [load] + _mix_stages(om_ref, x_ref, gm_ref, wo_ref, lng_ref, lnb_ref, wxq_ref) + [store]
    _run_stages(stages, _row_states(x_ref, MIX_ROWS))


def _mix(o_ssm, o_mla, x2, g_mla, w_o, ln_g, ln_b, w_xq, tn):
    n = x2.shape[0]
    row = lambda w: pl.BlockSpec((tn, w), lambda i: (i, 0))
    return pl.pallas_call(
        _mix_body,
        grid=(n // tn,),
        in_specs=[row(SSM_WIDTH), row(MLA_WIDTH), row(D_MODEL), _const(g_mla.shape), _const(w_o.shape),
                  _const(ln_g.shape), _const(ln_b.shape), _const(w_xq.shape)],
        out_specs=[row(D_MODEL), row(D_MODEL)],
        out_shape=[jax.ShapeDtypeStruct((n, D_MODEL), F32), jax.ShapeDtypeStruct((n, D_MODEL), BF16)],
        compiler_params=_params("parallel"),
        name="mix",
    )(o_ssm, o_mla, x2, g_mla, w_o, ln_g, ln_b, w_xq)


def _mem_attn_heads(qs, mks, mvs):
    ss = [_dot_nt(q, mk.astype(BF16)) * (X_HEAD_DIM ** -0.5) for q, mk in zip(qs, mks)]
    es = [jnp.exp(s - jnp.max(s, -1, keepdims=True)) for s in ss]
    ps = [(e / jnp.sum(e, -1, keepdims=True)).astype(BF16) for e in es]
    return [_dot(p, mv.astype(BF16)).astype(BF16) for p, mv in zip(ps, mvs)]


def _mix_mem_body(y_ref, u_ref, d_ref, wglu_ref, gs_ref, om_ref, x_ref, gm_ref, wo_ref, lng_ref, lnb_ref, wxq_ref,
                  mk_ref, mv_ref, wxo_ref, h2_ref):
    cols = [slice(h * X_HEAD_DIM, (h + 1) * X_HEAD_DIM) for h in range(X_HEADS)]
    mks = [mk_ref[:, c].astype(BF16) for c in cols]
    mvs = [mv_ref[:, c].astype(BF16) for c in cols]

    def scores(st):
        qx = st.pop("qx")
        st["s"] = [_dot_nt(qx[:, c], mk) * (X_HEAD_DIM ** -0.5) for c, mk in zip(cols, mks)]

    def softmax(st):
        es = [jnp.exp(s - jnp.max(s, -1, keepdims=True)) for s in st.pop("s")]
        st["p"] = [(e / jnp.sum(e, -1, keepdims=True)).astype(BF16) for e in es]

    def values(st):
        st["ox"] = jnp.concatenate([_dot(p, mv).astype(BF16) for p, mv in zip(st.pop("p"), mvs)], axis=1)

    def project(st):
        st["att"] = _dot(st.pop("ox"), wxo_ref[...])

    def norm2(st):
        h2_ref[st["r"], :] = _ln(ALPHA * st.pop("h1") + st.pop("att"), lng_ref[1:2, :], lnb_ref[1:2, :])

    stages = (_glu_stages(y_ref, u_ref, d_ref, wglu_ref, gs_ref)
              + _mix_stages(om_ref, x_ref, gm_ref, wo_ref, lng_ref, lnb_ref, wxq_ref)
              + [scores, softmax, values, project, norm2])
    _run_stages(stages, _row_states(x_ref, MIX_ROWS))


def _mix_mem(y_ssm, u, d_skip, w_glu, g_ssm, o_mla, x2, g_mla, w_o, ln_g, ln_b, w_xq, mem_k, mem_v, w_xo, tn):
    n = x2.shape[0]
    row = lambda w: pl.BlockSpec((tn, w), lambda i: (i, 0))
    glu_consts = (d_skip, w_glu, g_ssm)
    consts = (g_mla, w_o, ln_g, ln_b, w_xq, mem_k, mem_v, w_xo)
    return pl.pallas_call(
        _mix_mem_body,
        grid=(n // tn,),
        in_specs=[row(SSM_WIDTH), row(SSM_WIDTH)] + [_const(c.shape) for c in glu_consts]
        + [row(MLA_WIDTH), row(D_MODEL)] + [_const(c.shape) for c in consts],
        out_specs=row(D_MODEL),
        out_shape=jax.ShapeDtypeStruct((n, D_MODEL), F32),
        compiler_params=_params("parallel"),
        name="mix_mem",
    )(y_ssm, u, *glu_consts, o_mla, x2, *consts)


def _mem_attn_cache_body(qx_ref, mk_hbm, mv_hbm, o_ref, kbuf, vbuf, sems, *, seq, nsub):
    b = pl.program_id(0)
    nb = pl.num_programs(0)

    def copies(step, slot):
        return [pltpu.make_async_copy(src.at[step * nsub + j, :, h, :], buf.at[slot, j, h], sems.at[slot, j, t, h])
                for t, (src, buf) in enumerate(((mk_hbm, kbuf), (mv_hbm, vbuf)))
                for j in range(nsub) for h in range(X_HEADS)]

    @pl.when(b == 0)
    def _():
        for c in copies(0, 0):
            c.start()

    @pl.when(b + 1 < nb)
    def _():
        for c in copies(b + 1, (b + 1) % 2):
            c.start()

    slot = b % 2
    for c in copies(b, slot):
        c.wait()
    jh = [(j, h) for j in range(nsub) for h in range(X_HEADS)]
    where = [(slice(j * seq, (j + 1) * seq), slice(h * X_HEAD_DIM, (h + 1) * X_HEAD_DIM)) for j, h in jh]
    outs = _mem_attn_heads([qx_ref[r, c] for r, c in where], [kbuf[slot, j, h] for j, h in jh],
                           [vbuf[slot, j, h] for j, h in jh])
    for (r, c), o in zip(where, outs):
        o_ref[r, c] = o


def _mem_attn_cache(qx, mem_k, mem_v, seq, nsub):
    n = qx.shape[0]
    nb = mem_k.shape[0]
    assert nb % nsub == 0
    row = pl.BlockSpec((nsub * seq, D_MODEL), lambda b: (b, 0))
    hbm = pl.BlockSpec(memory_space=pl.ANY)
    buf = pltpu.VMEM((2, nsub, X_HEADS, N_MEM, X_HEAD_DIM), F32)
    return pl.pallas_call(
        functools.partial(_mem_attn_cache_body, seq=seq, nsub=nsub),
        grid=(nb // nsub,),
        in_specs=[row, hbm, hbm],
        out_specs=row,
        out_shape=jax.ShapeDtypeStruct((n, D_MODEL), BF16),
        scratch_shapes=[buf, buf, pltpu.SemaphoreType.DMA((2, nsub, 2, X_HEADS))],
        compiler_params=_params("arbitrary"),
        name="mem_attn_cache",
    )(qx, mem_k, mem_v)


def _mlp_body(*refs, ff_blk, second_norm):
    if second_norm:
        h1_ref, ox_ref, wxo_ref, lng_ref, lnb_ref, w1_ref, w2_ref, y_ref = refs

        def head(st):
            att = _dot(ox_ref[st["r"], :], wxo_ref[...])
            st["h2"] = _ln(ALPHA * h1_ref[st["r"], :] + att, lng_ref[1:2, :], lnb_ref[1:2, :])
    else:
        h2_ref, lng_ref, lnb_ref, w1_ref, w2_ref, y_ref = refs

        def head(st):
            st["h2"] = h2_ref[st["r"], :]

    def up(c):
        def stage(st):
            z = jnp.maximum(_dot(st["h2"].astype(BF16), w1_ref[:, c:c + ff_blk]), 0.0)
            st["z"] = (z * z).astype(BF16)
        return stage

    def down(c):
        def stage(st):
            d = _dot(st.pop("z"), w2_ref[c:c + ff_blk, :])
            st["acc"] = st["acc"] + d if "acc" in st else d
        return stage

    def tail(st):
        y_ref[st["r"], :] = _ln(ALPHA * st.pop("h2") + st.pop("acc"), lng_ref[2:3, :], lnb_ref[2:3, :])

    stages = [head] + [s for c in range(0, D_FF, ff_blk) for s in (up(c), down(c))] + [tail]
    _run_stages(stages, _row_states(y_ref, MLP_ROWS))


def _mlp(h, ox, w_xo, ln_g, ln_b, w1, w2, tn):
    n = h.shape[0]
    row = pl.BlockSpec((tn, D_MODEL), lambda i: (i, 0))
    once = lambda c: pl.BlockSpec(c.shape, lambda *_: (0,) * c.ndim, pipeline_mode=pl.Buffered(1))
    second_norm = ox is not None
    rows = (h, ox) if second_norm else (h,)
    consts = ((w_xo,) if second_norm else ()) + (ln_g, ln_b, w1, w2)
    return pl.pallas_call(
        functools.partial(_mlp_body, ff_blk=MLP_FF_BLOCK, second_norm=second_norm),
        grid=(n // tn,),
        in_specs=[row] * len(rows) + [once(c) for c in consts],
        out_specs=row,
        out_shape=jax.ShapeDtypeStruct((n, D_MODEL), F32),
        compiler_params=_params("parallel"),
        name="mlp",
    )(*rows, *consts)


def _mem_kv_body(mem_ref, wk_ref, wv_ref, k_ref, v_ref):
    m = mem_ref[...].astype(BF16)
    k_ref[...] = _dot(m, wk_ref[...])
    v_ref[...] = _dot(m, wv_ref[...])


def _mem_kv(mem2, wk, wv):
    n = mem2.shape[0]
    out = jax.ShapeDtypeStruct((n, D_MODEL), F32)
    return pl.pallas_call(
        _mem_kv_body,
        grid=(1,),
        in_specs=[_const(mem2.shape), _const(wk.shape), _const(wv.shape)],
        out_specs=[_const((n, D_MODEL))] * 2,
        out_shape=[out, out],
        compiler_params=_params("arbitrary"),
        name="mem_kv",
    )(mem2, wk, wv)


def _state_to_cols(s):
    return jnp.transpose(s.reshape(s.shape[0], S5_COLS, -1), (1, 0, 2))


def _state_from_cols(s):
    return jnp.transpose(s, (1, 0, 2)).reshape(s.shape[1], SSM_GROUPS, SSM_STATE)


def _layer(x2, pos, nseq, h0r, h0i, mem_k, mem_v, caches, wts, s5_ops, tn, scan_ct):
    n = x2.shape[0]
    seq = n // nseq
    prompt = caches is None
    u, q, k, v, ckv, kpe = _project(x2, pos, wts["inv"], wts["w_in"], wts["g_q"], wts["w_q"], wts["g_kv"],
                                    wts["w_k"], wts["w_vt"] if prompt else wts["w_v"], min(n, PROJ_TILE),
                                    v_transposed=prompt, consecutive=prompt, kpe_cache_shape=prompt)
    y_ssm, fr, fi = _s5_scan(u, s5_ops, _state_to_cols(h0r), _state_to_cols(h0i), nseq, scan_ct)
    glu_args = (y_ssm, u, wts["d_skip"], wts["w_glu"], wts["g_out_ssm"])
    if prompt:
        o_mla = _attention(q, k, v, ATTN_HEADS)
    else:
        o_mla = _attention_cached(q, k, v, caches[0], caches[1], wts["w_k"], wts["w_v"], seq, nsub=CACHE_SEQS)
    mix_args = (o_mla, x2, wts["g_out_mla"], wts["w_o"], wts["ln_g"], wts["ln_b"], wts["w_xq"])
    if prompt:
        h, ox = _mix_mem(*glu_args, *mix_args, mem_k, mem_v, wts["w_xo"], min(n, MIX_TILE)), None
    else:
        h, qx = _mix(_glu(*glu_args, tn), *mix_args, tn)
        ox = _mem_attn_cache(qx, mem_k, mem_v, seq, nsub=CACHE_SEQS)
    y = _mlp(h, ox, wts["w_xo"], wts["ln_g"], wts["ln_b"], wts["w_ff1"], wts["w_ff2"], min(n, MLP_TILE))
    return y, ckv, kpe, _state_from_cols(fr), _state_from_cols(fi)


def kernel(x_prompt, x_sample, mem_prompt, cache_mla_ckv, cache_mla_kpe, state_ssm_re, state_ssm_im, cache_mem_k, cache_mem_v, w_in, g_q, w_q_up, g_kv, w_kv_up, a_re, a_im, b_re, b_im, c_re, c_im, d_skip, log_dt, w_glu, g_out_ssm, g_out_mla, w_o, w_xq, w_xk, w_xv, w_xo, w_ff1, w_ff2, ln_g, ln_b):
    assert w_in.shape[0] == DEPTH == 1
    nbp, sp, _ = x_prompt.shape
    nbs, sd, _ = x_sample.shape
    past = cache_mla_ckv.shape[2]
    assert nbp == 1

    wq = jnp.pad(w_q_up[0], ((0, 0), (0, 0), (0, QK_PAD - QK_NOPE - QK_ROPE)))
    wk = w_kv_up[0][:, :, :QK_NOPE].reshape(KV_LORA, -1).astype(BF16)
    wv = w_kv_up[0][:, :, QK_NOPE:].reshape(KV_LORA, -1).astype(BF16)
    inv = ROPE_THETA ** (-jnp.arange(ROPE_HALF, dtype=F32) / ROPE_HALF)
    wts = {
        "inv": jnp.tile(inv, LANES // ROPE_HALF).reshape(1, LANES),
        "w_in": jnp.pad(w_in[0], ((0, 0), (0, LANES - QK_ROPE))).astype(BF16),
        "g_q": g_q[0].reshape(1, -1),
        "w_q": wq.reshape(Q_LORA, MLA_HEADS * QK_PAD).astype(BF16),
        "g_kv": g_kv[0].reshape(1, -1),
        "w_k": wk,
        "w_v": wv,
        "w_vt": wv.T,
        "d_skip": d_skip[0].reshape(1, -1),
        "w_glu": w_glu[0].astype(BF16),
        "g_out_ssm": g_out_ssm[0].reshape(1, -1),
        "g_out_mla": g_out_mla[0].reshape(1, -1),
        "w_o": w_o[0].astype(BF16),
        "w_xq": w_xq[0].reshape(D_MODEL, D_MODEL).astype(BF16),
        "w_xo": w_xo[0].reshape(D_MODEL, D_MODEL).astype(BF16),
        "w_ff1": w_ff1[0].astype(BF16),
        "w_ff2": w_ff2[0].astype(BF16),
        "ln_g": ln_g[0],
        "ln_b": ln_b[0],
    }
    s5_ops = _s5_prep(a_re[0], a_im[0], b_re[0], b_im[0], c_re[0], c_im[0], log_dt[0])

    mk, mv = _mem_kv(mem_prompt.reshape(nbp * N_MEM, D_MODEL),
                     w_xk[0].reshape(D_MODEL, D_MODEL).astype(BF16),
                     w_xv[0].reshape(D_MODEL, D_MODEL).astype(BF16))
    zero = jnp.zeros((nbp, SSM_GROUPS, SSM_STATE), F32)
    pos_p = jnp.arange(sp, dtype=F32).reshape(sp, 1)
    yp, ckv_p, kpe_p, sre_p, sim_p = _layer(
        x_prompt.reshape(sp, D_MODEL), pos_p, nbp, zero, zero, mk, mv, None,
        wts, s5_ops, tn=TOKEN_TILE, scan_ct=min(sp // S5_T, SCAN_CHUNKS))

    pos_s = jnp.tile(past + jnp.arange(sd, dtype=F32), nbs).reshape(nbs * sd, 1)
    caches = (cache_mla_ckv[0], cache_mla_kpe[0])
    ys, ckv_s, kpe_s, sre_s, sim_s = _layer(
        x_sample.reshape(nbs * sd, D_MODEL), pos_s, nbs,
        state_ssm_re[0], state_ssm_im[0], cache_mem_k[0], cache_mem_v[0], caches,
        wts, s5_ops, tn=TOKEN_TILE, scan_ct=sd // S5_T)

    return (yp.reshape(nbp, sp, D_MODEL), ys.reshape(nbs, sd, D_MODEL),
            ckv_p.reshape(1, nbp, sp, KV_LORA), kpe_p.reshape(1, nbp, sp, QK_ROPE),
            sre_p.reshape(1, nbp, SSM_GROUPS, SSM_STATE), sim_p.reshape(1, nbp, SSM_GROUPS, SSM_STATE),
            mk.reshape(1, nbp, N_MEM, X_HEADS, X_HEAD_DIM), mv.reshape(1, nbp, N_MEM, X_HEADS, X_HEAD_DIM),
            ckv_s.reshape(1, nbs, sd, KV_LORA), kpe_s.reshape(1, nbs, sd, QK_ROPE),
            sre_s.reshape(1, nbs, SSM_GROUPS, SSM_STATE), sim_s.reshape(1, nbs, SSM_GROUPS, SSM_STATE))
```

```python
import functools
import math

import jax
import jax.numpy as jnp
from jax import lax
from jax.experimental import pallas as pl
from jax.experimental.pallas import tpu as pltpu

F32 = jnp.float32
BF16 = jnp.bfloat16

D_MODEL = 1024
DEPTH = 1
CHUNK = 64
SSM_WIDTH = 512
SSM_GROUP = 16
SSM_GROUPS = 32
SSM_STATE = 64
MLA_HEADS = 4
QK_NOPE = 128
QK_ROPE = 64
V_HEAD = 128
MLA_WIDTH = MLA_HEADS * V_HEAD
Q_LORA = 384
KV_LORA = 256
ROPE_THETA = 10000.0
MLA_SCALE = (QK_NOPE + QK_ROPE) ** -0.5
N_MEM = 256
X_HEADS = 4
X_HEAD_DIM = D_MODEL // X_HEADS
D_FF = 4 * D_MODEL
ALPHA = (2 * DEPTH) ** 0.25
EPS = 1e-5
NEG_INF = -1e30

LANES = 128
MXU_DEPTH = 256
VMEM_BYTES = 64 * 1024 * 1024
VMEM_LIMIT = VMEM_BYTES - 8 * 1024 * 1024

QK_PAD = MXU_DEPTH
S5_T = MXU_DEPTH // SSM_GROUP
S5_COL_GROUPS = LANES // SSM_GROUP
S5_COLS = SSM_GROUPS // S5_COL_GROUPS
S5_SW = S5_COL_GROUPS * SSM_STATE
ROPE_HALF = QK_ROPE // 2
Q_SCALE = MLA_SCALE * math.log2(math.e)
ATTN_T = 512
ATTN_HEADS = 2
ATTN_TQ = 2 * ATTN_T
ATTN_QSPLIT = 4
MLP_TILE = 1024
MLP_ROWS = 512
MLP_FF_BLOCK = 1024
CACHE_SEQS = 4
MIX_TILE = 1024
MIX_ROWS = 256
PROJ_TILE = 1024
PROJ_ROWS = 256
TOKEN_TILE = 512
SCAN_CHUNKS = 512
SCAN_UNROLL = 8

_NT = (((1,), (1,)), ((), ()))


def _rms(x, g):
    return x * lax.rsqrt(jnp.mean(x * x, -1, keepdims=True) + EPS) * g


def _ln(x, g, b):
    mu = jnp.mean(x, -1, keepdims=True)
    xc = x - mu
    var = jnp.mean(xc * xc, -1, keepdims=True)
    return xc * lax.rsqrt(var + EPS) * g + b


def _dot(a, b):
    return jnp.dot(a, b, preferred_element_type=F32)


def _dot_nt(a, b):
    return lax.dot_general(a, b, _NT, preferred_element_type=F32)


def _split(a):
    hi = a.astype(BF16)
    return hi, (a - hi.astype(F32)).astype(BF16)


def _dot_split(a, b):
    return _dot(a[0], b[0]) + _dot(a[0], b[1]) + _dot(a[1], b[0])


def _params(*sem):
    return pltpu.CompilerParams(dimension_semantics=sem, vmem_limit_bytes=VMEM_LIMIT)


def _const(shape):
    n = len(shape)
    return pl.BlockSpec(shape, lambda *_: (0,) * n)


def _proj_body(x_ref, pos_ref, inv_ref, w_in_ref, gq_ref, wq_ref, gkv_ref, wk_ref, wv_ref,
               u_ref, q_ref, k_ref, v_ref, ckv_ref, kpe_ref, tc_ref, ts_ref, *, v_transposed, consecutive):
    rb = PROJ_ROWS
    assert x_ref.shape[0] % rb == 0 and ATTN_T % rb == 0
    blocks = [slice(r0, r0 + rb) for r0 in range(0, x_ref.shape[0], rb)]
    c0 = SSM_WIDTH + Q_LORA

    lane = lax.broadcasted_iota(jnp.int32, (1, LANES), 1)
    live = lane < QK_ROPE
    sign = jnp.where(lane < ROPE_HALF, -1.0, 1.0)
    inv = inv_ref[...]
    if consecutive:
        @pl.when(pl.program_id(0) == 0)
        def _():
            r = lax.broadcasted_iota(jnp.int32, (x_ref.shape[0], 1), 0).astype(F32)
            tc_ref[...] = jnp.where(live, jnp.cos(r * inv), 0.0)
            ts_ref[...] = jnp.where(live, jnp.sin(r * inv), 0.0)

    projs = [_dot(x_ref[r, :].astype(BF16), w_in_ref[...]) for r in blocks]

    if consecutive:
        base = pos_ref[0:1, :] * inv
        ca, sa = jnp.cos(base), jnp.sin(base)
        cos_ts = [ca * tc_ref[r, :] - sa * ts_ref[r, :] for r in blocks]
        sin_ts = [(sa * sign) * tc_ref[r, :] + (ca * sign) * ts_ref[r, :] for r in blocks]
    else:
        angs = [pos_ref[r, :] * inv for r in blocks]
        cos_ts = [jnp.where(live, jnp.cos(a), 0.0) for a in angs]
        sin_ts = [jnp.where(live, jnp.sin(a) * sign, 0.0) for a in angs]

    def rope(c2, cos_t, sin_t):
        swapped = jnp.where(lane < ROPE_HALF, pltpu.roll(c2, LANES - ROPE_HALF, 1), pltpu.roll(c2, ROPE_HALF, 1))
        return c2 * cos_t + swapped * sin_t

    for r, p, cos_t, sin_t in zip(blocks, projs, cos_ts, sin_ts):
        cq = _rms(p[:, SSM_WIDTH:c0], gq_ref[...]).astype(BF16)
        ckv = _rms(p[:, c0:c0 + KV_LORA], gkv_ref[...])
        ckv_b = ckv.astype(BF16)
        q = _dot(cq, wq_ref[...]) * Q_SCALE
        kn = _dot(ckv_b, wk_ref[...])
        if v_transposed:
            lanes = slice(r.start % ATTN_T, r.start % ATTN_T + rb)
            v_ref[r.start // ATTN_T, :, lanes] = _dot_nt(wv_ref[...], ckv_b).astype(BF16)
        else:
            v_ref[r, :] = _dot(ckv_b, wv_ref[...]).astype(BF16)
        u_ref[r, :] = p[:, :SSM_WIDTH]
        ckv_ref[r, :] = ckv
        kpe = rope(p[:, c0 + KV_LORA:], cos_t, sin_t)
        kpe_ref[r, :] = kpe[:, :QK_ROPE]
        kpe_b = kpe.astype(BF16)
        for h in range(MLA_HEADS):
            a = h * QK_PAD
            q_ref[r, a:a + QK_NOPE] = q[:, a:a + QK_NOPE].astype(BF16)
            q_ref[r, a + QK_NOPE:a + QK_PAD] = rope(q[:, a + QK_NOPE:a + QK_PAD], cos_t, sin_t).astype(BF16)
            k_ref[r, a:a + QK_NOPE] = kn[:, h * QK_NOPE:(h + 1) * QK_NOPE].astype(BF16)
            k_ref[r, a + QK_NOPE:a + QK_PAD] = kpe_b


def _project(x2, pos, inv, w_in, gq, wq, gkv, wk, wv, tn, v_transposed, consecutive, kpe_cache_shape):
    n = x2.shape[0]
    row = lambda w: pl.BlockSpec((tn, w), lambda i: (i, 0))
    if kpe_cache_shape:
        kpe_spec = pl.BlockSpec((None, None, tn, QK_ROPE), lambda i: (0, 0, i, 0))
        kpe_shape = jax.ShapeDtypeStruct((1, 1, n, QK_ROPE), F32)
    else:
        kpe_spec, kpe_shape = row(QK_ROPE), jax.ShapeDtypeStruct((n, QK_ROPE), F32)
    if v_transposed:
        assert tn % ATTN_T == 0
        v_spec = pl.BlockSpec((tn // ATTN_T, MLA_WIDTH, ATTN_T), lambda i: (i, 0, 0))
        v_shape = jax.ShapeDtypeStruct((n // ATTN_T, MLA_WIDTH, ATTN_T), BF16)
    else:
        v_spec, v_shape = row(MLA_WIDTH), jax.ShapeDtypeStruct((n, MLA_WIDTH), BF16)
    return pl.pallas_call(
        functools.partial(_proj_body, v_transposed=v_transposed, consecutive=consecutive),
        grid=(n // tn,),
        scratch_shapes=[pltpu.VMEM((tn, LANES), F32)] * 2,
        in_specs=[row(D_MODEL), row(1), _const(inv.shape), _const(w_in.shape), _const(gq.shape),
                  _const(wq.shape), _const(gkv.shape), _const(wk.shape), _const(wv.shape)],
        out_specs=[row(SSM_WIDTH), row(MLA_HEADS * QK_PAD), row(MLA_HEADS * QK_PAD),
                   v_spec, row(KV_LORA), kpe_spec],
        out_shape=[jax.ShapeDtypeStruct((n, SSM_WIDTH), F32),
                   jax.ShapeDtypeStruct((n, MLA_HEADS * QK_PAD), BF16),
                   jax.ShapeDtypeStruct((n, MLA_HEADS * QK_PAD), BF16),
                   v_shape,
                   jax.ShapeDtypeStruct((n, KV_LORA), F32),
                   kpe_shape],
        compiler_params=_params("arbitrary"),
        name="proj",
    )(x2, pos, inv, w_in, gq, wq, gkv, wk, wv)


def _s5_prep_body(ar_row, ai_row, ldt_row, bt_re, bt_im, ct_re, ct_im,
                  wt_ref, v_ref, z_ref, are_ref, aim_ref):
    sw = S5_SW
    arr, air, dtr = ar_row[0], ai_row[0], jnp.exp(ldt_row[0])

    mag = jnp.exp(arr * dtr)
    lr, li = mag * jnp.cos(air * dtr), mag * jnp.sin(air * dtr)
    nr, ni = lr - 1.0, li
    den = arr * arr + air * air
    f_re, f_im = (nr * arr + ni * air) / den, (ni * arr - nr * air) / den

    same_b = (lax.broadcasted_iota(jnp.int32, (LANES, sw), 0) // SSM_GROUP
              == lax.broadcasted_iota(jnp.int32, (LANES, sw), 1) // SSM_STATE)
    br = jnp.where(same_b, bt_re[0], 0.0)
    bi = jnp.where(same_b, bt_im[0], 0.0)
    bb_re = f_re * br - f_im * bi
    bb_im = f_re * bi + f_im * br
    same_c = (lax.broadcasted_iota(jnp.int32, (sw, LANES), 0) // SSM_STATE
              == lax.broadcasted_iota(jnp.int32, (sw, LANES), 1) // SSM_GROUP)
    cr = jnp.where(same_c, ct_re[0], 0.0)
    ci = jnp.where(same_c, ct_im[0], 0.0)
    cr_s, ci_s = _split(cr), _split(ci)

    e = lax.broadcasted_iota(jnp.int32, (2 * S5_T, 1), 0).astype(F32)
    pm = jnp.exp(arr * dtr * e)
    pw_re, pw_im = pm * jnp.cos(air * dtr * e), pm * jnp.sin(air * dtr * e)
    pt_re, pt_im = pw_re.T, pw_im.T

    lag_ops = []
    for lag in range(S5_T):
        p_re, p_im = pw_re[lag:lag + 1, :], pw_im[lag:lag + 1, :]
        k_re = p_re * bb_re - p_im * bb_im
        k_im = p_re * bb_im + p_im * bb_re
        i = S5_T - 1 - lag
        v_ref[0, i * LANES:(i + 1) * LANES, :sw] = k_re.astype(BF16)
        v_ref[0, i * LANES:(i + 1) * LANES, sw:] = k_im.astype(BF16)
        lag_ops.append(_dot_split(_split(k_re), cr_s) - _dot_split(_split(k_im), ci_s))

        q_re, q_im = pt_re[:, lag + 1:lag + 2], pt_im[:, lag + 1:lag + 2]
        z_ref[0, :sw, lag * LANES:(lag + 1) * LANES] = (cr * q_re - ci * q_im).astype(BF16)
        z_ref[0, sw:, lag * LANES:(lag + 1) * LANES] = (-(cr * q_im + ci * q_re)).astype(BF16)

    zero = jnp.zeros((LANES, LANES), BF16)
    for d in range(S5_T // 2):
        wt_ref[0, d, :LANES, :LANES] = lag_ops[2 * d].astype(BF16)
        wt_ref[0, d, :LANES, LANES:] = lag_ops[2 * d + 1].astype(BF16)
        wt_ref[0, d, LANES:, :LANES] = lag_ops[2 * d - 1].astype(BF16) if d else zero
        wt_ref[0, d, LANES:, LANES:] = lag_ops[2 * d].astype(BF16)

    are_ref[0] = pw_re[S5_T:S5_T + 1, :]
    aim_ref[0] = pw_im[S5_T:S5_T + 1, :]


def _s5_prep(a_re, a_im, b_re, b_im, c_re, c_im, log_dt):
    nc, r, t, h, p, sw = S5_COLS, S5_COL_GROUPS, S5_T, SSM_GROUP, SSM_STATE, S5_SW
    ldt = jnp.repeat(log_dt, p)
    bt = lambda b: jnp.tile(jnp.transpose(b.reshape(nc, r, p, h), (0, 1, 3, 2)).reshape(nc, r * h, p), (1, 1, r))
    ct = lambda c: jnp.tile(jnp.transpose(c.reshape(nc, r, h, p), (0, 1, 3, 2)).reshape(nc, r * p, h), (1, 1, r))
    args = (a_re.reshape(nc, 1, sw), a_im.reshape(nc, 1, sw), ldt.reshape(nc, 1, sw),
            bt(b_re), bt(b_im), ct(c_re), ct(c_im))
    blk = lambda s: pl.BlockSpec((1,) + s[1:], lambda i: (i,) + (0,) * (len(s) - 1))
    outs = [((nc, t // 2, 2 * LANES, 2 * LANES), BF16), ((nc, t * LANES, 2 * sw), BF16),
            ((nc, 2 * sw, t * LANES), BF16), ((nc, 1, sw), F32), ((nc, 1, sw), F32)]
    return pl.pallas_call(
        _s5_prep_body,
        grid=(nc,),
        in_specs=[blk(a.shape) for a in args],
        out_specs=[blk(s) for s, _ in outs],
        out_shape=[jax.ShapeDtypeStruct(s, d) for s, d in outs],
        compiler_params=_params("parallel"),
        name="s5_prep",
    )(*args)


def _s5_scan_body(u_ref, wt_ref, v_ref, z_ref, are_ref, aim_ref, h0r_ref, h0i_ref,
                  y_ref, fr_ref, fi_ref, lhs, sre, sim, xre, xim, cr, ci, *, nseq, ct):
    t = pl.program_id(1)
    sw = S5_SW
    seq = ct * S5_T

    @pl.when(t == 0)
    def _():
        cr[...] = h0r_ref[0]
        ci[...] = h0i_ref[0]

    def token_rows(cc, i):
        if nseq == 1:
            return pl.ds(i, ct, stride=S5_T), slice(None)
        return pl.ds(cc * S5_T + i, nseq, stride=seq), slice(cc * nseq, (cc + 1) * nseq)

    for cc in range(1 if nseq == 1 else ct):
        for i in range(S5_T):
            tok, crow = token_rows(cc, i)
            lhs[crow, i * LANES:(i + 1) * LANES] = u_ref[tok, :].astype(BF16)

    s = _dot(lhs[...], v_ref[0])
    sre[...] = s[:, :sw]
    sim[...] = s[:, sw:]

    a_re, a_im = are_ref[0], aim_ref[0]

    def step(c, carry):
        x_re, x_im = carry
        r = pl.ds(c * nseq, nseq)
        xre[r, :] = x_re
        xim[r, :] = x_im
        n_re = a_re * x_re - a_im * x_im + sre[r, :]
        n_im = a_re * x_im + a_im * x_re + sim[r, :]
        return n_re, n_im

    x_re, x_im = lax.fori_loop(0, ct, step, (cr[...], ci[...]), unroll=min(ct, SCAN_UNROLL))
    cr[...] = x_re
    ci[...] = x_im
    fr_ref[0] = x_re
    fi_ref[0] = x_im

    xb_re, xb_im = xre[...].astype(BF16), xim[...].astype(BF16)
    for jp in range(S5_T // 2):
        cols = slice(jp * 2 * LANES, (jp + 1) * 2 * LANES)
        acc = _dot(xb_re, z_ref[0, :sw, cols]) + _dot(xb_im, z_ref[0, sw:, cols])
        for ip in range(jp + 1):
            acc += _dot(lhs[:, ip * 2 * LANES:(ip + 1) * 2 * LANES], wt_ref[0, jp - ip])
        for cc in range(1 if nseq == 1 else ct):
            for jj in range(2):
                tok, crow = token_rows(cc, 2 * jp + jj)
                y_ref[tok, :] = acc[crow, jj * LANES:(jj + 1) * LANES]


def _s5_scan(u, col_ops, h0r, h0i, nseq, ct):
    wt, v, z, are, aim = col_ops
    n = u.shape[0]
    cps = n // nseq // S5_T
    assert nseq == 1 or ct == cps
    nc = nseq * ct
    sw = S5_SW
    tile = pl.BlockSpec((nc * S5_T, LANES), lambda c, t: (t, c))
    op = lambda a: pl.BlockSpec((1,) + a.shape[1:], lambda c, t: (c,) + (0,) * (a.ndim - 1))
    st = pl.BlockSpec((1, nseq, sw), lambda c, t: (c, 0, 0))
    st_shape = jax.ShapeDtypeStruct((S5_COLS, nseq, sw), F32)
    return pl.pallas_call(
        functools.partial(_s5_scan_body, nseq=nseq, ct=ct),
        grid=(S5_COLS, cps // ct),
        in_specs=[tile, op(wt), op(v), op(z), op(are), op(aim), st, st],
        out_specs=[tile, st, st],
        out_shape=[jax.ShapeDtypeStruct(u.shape, F32), st_shape, st_shape],
        scratch_shapes=[pltpu.VMEM((nc, S5_T * LANES), BF16)] + [pltpu.VMEM((nc, sw), F32)] * 4
        + [pltpu.VMEM((nseq, sw), F32)] * 2,
        compiler_params=_params("arbitrary", "arbitrary"),
        name="s5_scan",
    )(u, wt, v, z, are, aim, h0r, h0i)


def _run_stages(stages, states):
    for t in range(len(stages) + len(states) - 1):
        for b, st in enumerate(states):
            if 0 <= t - b < len(stages):
                stages[t - b](st)


def _row_states(ref, rows):
    assert ref.shape[0] % rows == 0
    return [{"r": slice(r0, r0 + rows)} for r0 in range(0, ref.shape[0], rows)]


def _glu_stages(y_ref, u_ref, d_ref, w_ref, g_ref):
    def act(st):
        st["act"] = jax.nn.gelu(y_ref[st["r"], :] + d_ref[...] * u_ref[st["r"], :]).astype(BF16)

    def glu(st):
        st["gl"] = _dot(st.pop("act"), w_ref[...])

    def gate(st):
        gl = st.pop("gl")
        st["os"] = _rms(gl[:, :SSM_WIDTH] * jax.nn.sigmoid(gl[:, SSM_WIDTH:]), g_ref[...]).astype(BF16)

    return [act, glu, gate]


def _glu_body(y_ref, u_ref, d_ref, w_ref, g_ref, o_ref):
    def store(st):
        o_ref[st["r"], :] = st.pop("os")

    _run_stages(_glu_stages(y_ref, u_ref, d_ref, w_ref, g_ref) + [store], _row_states(y_ref, MIX_ROWS))


def _glu(y, u, d, w, g, tn):
    n = y.shape[0]
    row = pl.BlockSpec((tn, SSM_WIDTH), lambda i: (i, 0))
    return pl.pallas_call(
        _glu_body,
        grid=(n // tn,),
        in_specs=[row, row, _const(d.shape), _const(w.shape), _const(g.shape)],
        out_specs=row,
        out_shape=jax.ShapeDtypeStruct((n, SSM_WIDTH), BF16),
        compiler_params=_params("parallel"),
        name="glu",
    )(y, u, d, w, g)


def _attn_body(q_ref, k_ref, vt_ref, o_ref, s0, s1, mt0, mt1, m_sc, l_sc, acc_sc, *, heads):
    i = pl.program_id(1)
    t = ATTN_T
    m_sc[...] = jnp.full(m_sc.shape, -jnp.inf, F32)
    l_sc[...] = jnp.zeros(l_sc.shape, F32)
    acc_sc[...] = jnp.zeros(acc_sc.shape, F32)

    qb = q_ref.shape[0] // ATTN_QSPLIT
    units = [(h, c * qb) for c in range(ATTN_QSPLIT) for h in range(heads)]

    def scores(tile, s_buf, mt_buf, unit, diag=None):
        h, c0 = unit
        r = pl.ds(pl.multiple_of(tile * t, t), t)
        s = _dot_nt(k_ref[r, h * QK_PAD:(h + 1) * QK_PAD],
                    q_ref[c0:c0 + qb, h * QK_PAD:(h + 1) * QK_PAD])
        if diag is not None:
            kc = (diag * t + lax.broadcasted_iota(jnp.int32, s.shape, 0)) // CHUNK
            qc = (c0 + lax.broadcasted_iota(jnp.int32, s.shape, 1)) // CHUNK
            s = jnp.where(kc <= qc, s, NEG_INF)
        s_buf[h, :, c0:c0 + qb] = s
        mt_buf[h, :, c0:c0 + qb] = jnp.max(s, 0, keepdims=True)

    def absorb(tile, s_buf, mt_buf, unit):
        h, c0 = unit
        cols = slice(c0, c0 + qb)
        m_prev = m_sc[h, :, cols]
        m_new = jnp.maximum(m_prev, mt_buf[h, :, cols])
        p = jnp.exp2(s_buf[h, :, cols] - m_new)
        alpha = jnp.exp2(m_prev - m_new)
        l_sc[h, :, cols] = alpha * l_sc[h, :, cols] + jnp.sum(p, 0, keepdims=True)
        m_sc[h, :, cols] = m_new
        pv = _dot(vt_ref[tile, h * V_HEAD:(h + 1) * V_HEAD, :], p.astype(BF16))
        acc_sc[h, :, cols] = alpha * acc_sc[h, :, cols] + pv

    assert q_ref.shape[0] == 2 * t
    for u in units:
        scores(2 * i, s0, mt0, u, diag=0)

    def pair(jj, c):
        prev = jnp.where(jj == 0, 2 * i, 2 * jj - 1)
        for u in units:
            scores(2 * jj, s1, mt1, u)
            absorb(prev, s0, mt0, u)
        for u in units:
            scores(2 * jj + 1, s0, mt0, u)
            absorb(2 * jj, s1, mt1, u)
        return c

    lax.fori_loop(0, i, pair, 0)
    def finish(unit):
        h, c0 = unit
        cols = slice(c0, c0 + qb)
        o_ref[cols, h * V_HEAD:(h + 1) * V_HEAD] = (acc_sc[h, :, cols] / l_sc[h, :, cols]).T

    last = jnp.where(i == 0, 0, 2 * i - 1)
    live = [u for u in units if u[1] + qb > t]
    for k, u in enumerate(units):
        if k < len(live):
            scores(2 * i + 1, s1, mt1, live[k], diag=1)
        absorb(last, s0, mt0, u)
        if u not in live:
            finish(u)
    for u in live:
        absorb(2 * i + 1, s1, mt1, u)
        finish(u)


def _attention(q, k, vt, heads):
    n = q.shape[0]
    t = ATTN_T
    tq = ATTN_TQ
    assert t % CHUNK == 0 and n % tq == 0 and MLA_HEADS % heads == 0 and tq == 2 * t
    once = pl.Buffered(1)
    return pl.pallas_call(
        functools.partial(_attn_body, heads=heads),
        grid=(MLA_HEADS // heads, n // tq),
        in_specs=[pl.BlockSpec((tq, heads * QK_PAD), lambda g, i: (i, g)),
                  pl.BlockSpec((n, heads * QK_PAD), lambda g, i: (0, g), pipeline_mode=once),
                  pl.BlockSpec((n // t, heads * V_HEAD, t), lambda g, i: (0, g, 0), pipeline_mode=once)],
        out_specs=pl.BlockSpec((tq, heads * V_HEAD), lambda g, i: (i, g)),
        out_shape=jax.ShapeDtypeStruct((n, MLA_WIDTH), F32),
        scratch_shapes=[pltpu.VMEM((heads, t, tq), F32)] * 2 + [pltpu.VMEM((heads, 1, tq), F32)] * 4
        + [pltpu.VMEM((heads, V_HEAD, tq), F32)],
        compiler_params=_params("arbitrary", "arbitrary"),
        name="attn",
    )(q, k, vt)


def _attn_cached_body(q_ref, kn_ref, vn_ref, ckv_ref, kpe_ref, wk_ref, wv_ref, o_ref, *, past, seq, nsub):
    heads = range(MLA_HEADS)
    pos = past + lax.broadcasted_iota(jnp.int32, (seq, 1), 0)
    qc = jnp.concatenate([pos] * MLA_HEADS, axis=0) // CHUNK
    kc_past = lax.broadcasted_iota(jnp.int32, (1, past), 1) // CHUNK
    kc_new = (past + lax.broadcasted_iota(jnp.int32, (1, seq), 1)) // CHUNK
    seqs = range(nsub)
    rows = [slice(b * seq, (b + 1) * seq) for b in seqs]
    ckv = [ckv_ref[b].astype(BF16) for b in seqs]
    kpe = [kpe_ref[b].astype(BF16) for b in seqs]
    qa = [jnp.concatenate(
        [_dot_nt(q_ref[r, h * QK_PAD:h * QK_PAD + QK_NOPE], wk_ref[:, h * QK_NOPE:(h + 1) * QK_NOPE])
         for h in heads], axis=0).astype(BF16) for r in rows]
    qpe = [jnp.concatenate(
        [q_ref[r, h * QK_PAD + QK_NOPE:h * QK_PAD + QK_NOPE + QK_ROPE] for h in heads], axis=0) for r in rows]
    s_p = [_dot_nt(qa[b], ckv[b]) + _dot_nt(qpe[b], kpe[b]) for b in seqs]
    s_n = [jnp.concatenate(
        [_dot_nt(q_ref[r, h * QK_PAD:(h + 1) * QK_PAD], kn_ref[r, h * QK_PAD:(h + 1) * QK_PAD])
         for h in heads], axis=0) for r in rows]
    s_p = [jnp.where(kc_past <= qc, s, NEG_INF) for s in s_p]
    s_n = [jnp.where(kc_new <= qc, s, NEG_INF) for s in s_n]
    m = [jnp.maximum(jnp.max(a, -1, keepdims=True), jnp.max(c, -1, keepdims=True)) for a, c in zip(s_p, s_n)]
    e_p = [jnp.exp2(s - mm) for s, mm in zip(s_p, m)]
    e_n = [jnp.exp2(s - mm) for s, mm in zip(s_n, m)]
    l = [jnp.sum(a, -1, keepdims=True) + jnp.sum(c, -1, keepdims=True) for a, c in zip(e_p, e_n)]
    p_n = [(e / ll).astype(BF16) for e, ll in zip(e_n, l)]
    o_lat = [_dot((e / ll).astype(BF16), c).astype(BF16) for e, ll, c in zip(e_p, l, ckv)]
    for b in seqs:
        for h in heads:
            hr = slice(h * seq, (h + 1) * seq)
            cols = slice(h * V_HEAD, (h + 1) * V_HEAD)
            o_ref[rows[b], cols] = _dot(o_lat[b][hr], wv_ref[:, cols]) + _dot(p_n[b][hr], vn_ref[rows[b], cols])


def _attention_cached(q, k, v, ckv_cache, kpe_cache, wk, wv, seq, nsub):
    n = q.shape[0]
    nb, past, _ = ckv_cache.shape
    assert nb % nsub == 0
    row = lambda w: pl.BlockSpec((nsub * seq, w), lambda b: (b, 0))
    kern = functools.partial(_attn_cached_body, past=past, seq=seq, nsub=nsub)
    return pl.pallas_call(
        kern,
        grid=(nb // nsub,),
        in_specs=[row(MLA_HEADS * QK_PAD), row(MLA_HEADS * QK_PAD), row(MLA_WIDTH),
                  pl.BlockSpec((nsub, past, KV_LORA), lambda b: (b, 0, 0)),
                  pl.BlockSpec((nsub, past, QK_ROPE), lambda b: (b, 0, 0)),
                  _const(wk.shape), _const(wv.shape)],
        out_specs=row(MLA_WIDTH),
        out_shape=jax.ShapeDtypeStruct((n, MLA_WIDTH), F32),
        compiler_params=_params("parallel"),
        name="attn_cached",
    )(q, k, v, ckv_cache, kpe_cache, wk, wv)


def _mix_stages(om_ref, x_ref, gm_ref, wo_ref, lng_ref, lnb_ref, wxq_ref):
    def norm_mla(st):
        st["om"] = _rms(om_ref[st["r"], :], gm_ref[...]).astype(BF16)

    def project(st):
        st["a"] = _dot(st.pop("os"), wo_ref[:SSM_WIDTH, :]) + _dot(st.pop("om"), wo_ref[SSM_WIDTH:, :])

    def norm1(st):
        st["h1"] = _ln(ALPHA * x_ref[st["r"], :] + st.pop("a"), lng_ref[0:1, :], lnb_ref[0:1, :])

    def query(st):
        st["qx"] = _dot(st["h1"].astype(BF16), wxq_ref[...]).astype(BF16)

    return [norm_mla, project, norm1, query]


def _mix_body(os_ref, om_ref, x_ref, gm_ref, wo_ref, lng_ref, lnb_ref, wxq_ref, h1_ref, qx_ref):
    def load(st):
        st["os"] = os_ref[st["r"], :]

    def store(st):
        h1_ref[st["r"], :] = st.pop("h1")
        qx_ref[st["r"], :] = st.pop("qx")

    stages = [load] + _mix_stages(om_ref, x_ref, gm_ref, wo_ref, lng_ref, lnb_ref, wxq_ref) + [store]
    _run_stages(stages, _row_states(x_ref, MIX_ROWS))


def _mix(o_ssm, o_mla, x2, g_mla, w_o, ln_g, ln_b, w_xq, tn):
    n = x2.shape[0]
    row = lambda w: pl.BlockSpec((tn, w), lambda i: (i, 0))
    return pl.pallas_call(
        _mix_body,
        grid=(n // tn,),
        in_specs=[row(SSM_WIDTH), row(MLA_WIDTH), row(D_MODEL), _const(g_mla.shape), _const(w_o.shape),
                  _const(ln_g.shape), _const(ln_b.shape), _const(w_xq.shape)],
        out_specs=[row(D_MODEL), row(D_MODEL)],
        out_shape=[jax.ShapeDtypeStruct((n, D_MODEL), F32), jax.ShapeDtypeStruct((n, D_MODEL), BF16)],
        compiler_params=_params("parallel"),
        name="mix",
    )(o_ssm, o_mla, x2, g_mla, w_o, ln_g, ln_b, w_xq)


def _mem_attn_heads(qs, mks, mvs):
    ss = [_dot_nt(q, mk.astype(BF16)) * (X_HEAD_DIM ** -0.5) for q, mk in zip(qs, mks)]
    es = [jnp.exp(s - jnp.max(s, -1, keepdims=True)) for s in ss]
    ps = [(e / jnp.sum(e, -1, keepdims=True)).astype(BF16) for e in es]
    return [_dot(p, mv.astype(BF16)).astype(BF16) for p, mv in zip(ps, mvs)]


def _mix_mem_body(y_ref, u_ref, d_ref, wglu_ref, gs_ref, om_ref, x_ref, gm_ref, wo_ref, lng_ref, lnb_ref, wxq_ref,
                  mk_ref, mv_ref, wxo_ref, h2_ref):
    cols = [slice(h * X_HEAD_DIM, (h + 1) * X_HEAD_DIM) for h in range(X_HEADS)]
    mks = [mk_ref[:, c].astype(BF16) for c in cols]
    mvs = [mv_ref[:, c].astype(BF16) for c in cols]

    def scores(st):
        qx = st.pop("qx")
        st["s"] = [_dot_nt(qx[:, c], mk) * (X_HEAD_DIM ** -0.5) for c, mk in zip(cols, mks)]

    def softmax(st):
        es = [jnp.exp(s - jnp.max(s, -1, keepdims=True)) for s in st.pop("s")]
        st["p"] = [(e / jnp.sum(e, -1, keepdims=True)).astype(BF16) for e in es]

    def values(st):
        st["ox"] = jnp.concatenate([_dot(p, mv).astype(BF16) for p, mv in zip(st.pop("p"), mvs)], axis=1)

    def project(st):
        st["att"] = _dot(st.pop("ox"), wxo_ref[...])

    def norm2(st):
        h2_ref[st["r"], :] = _ln(ALPHA * st.pop("h1") + st.pop("att"), lng_ref[1:2, :], lnb_ref[1:2, :])

    stages = (_glu_stages(y_ref, u_ref, d_ref, wglu_ref, gs_ref)
              + _mix_stages(om_ref, x_ref, gm_ref, wo_ref, lng_ref, lnb_ref, wxq_ref)
              + [scores, softmax, values, project, norm2])
    _run_stages(stages, _row_states(x_ref, MIX_ROWS))


def _mix_mem(y_ssm, u, d_skip, w_glu, g_ssm, o_mla, x2, g_mla, w_o, ln_g, ln_b, w_xq, mem_k, mem_v, w_xo, tn):
    n = x2.shape[0]
    row = lambda w: pl.BlockSpec((tn, w), lambda i: (i, 0))
    glu_consts = (d_skip, w_glu, g_ssm)
    consts = (g_mla, w_o, ln_g, ln_b, w_xq, mem_k, mem_v, w_xo)
    return pl.pallas_call(
        _mix_mem_body,
        grid=(n // tn,),
        in_specs=[row(SSM_WIDTH), row(SSM_WIDTH)] + [_const(c.shape) for c in glu_consts]
        + [row(MLA_WIDTH), row(D_MODEL)] + [_const(c.shape) for c in consts],
        out_specs=row(D_MODEL),
        out_shape=jax.ShapeDtypeStruct((n, D_MODEL), F32),
        compiler_params=_params("parallel"),
        name="mix_mem",
    )(y_ssm, u, *glu_consts, o_mla, x2, *consts)


def _mem_attn_cache_body(qx_ref, mk_hbm, mv_hbm, o_ref, kbuf, vbuf, sems, *, seq, nsub):
    b = pl.program_id(0)
    nb = pl.num_programs(0)

    def copies(step, slot):
        return [pltpu.make_async_copy(src.at[step * nsub + j, :, h, :], buf.at[slot, j, h], sems.at[slot, j, t, h])
                for t, (src, buf) in enumerate(((mk_hbm, kbuf), (mv_hbm, vbuf)))
                for j in range(nsub) for h in range(X_HEADS)]

    @pl.when(b == 0)
    def _():
        for c in copies(0, 0):
            c.start()

    @pl.when(b + 1 < nb)
    def _():
        for c in copies(b + 1, (b + 1) % 2):
            c.start()

    slot = b % 2
    for c in copies(b, slot):
        c.wait()
    jh = [(j, h) for j in range(nsub) for h in range(X_HEADS)]
    where = [(slice(j * seq, (j + 1) * seq), slice(h * X_HEAD_DIM, (h + 1) * X_HEAD_DIM)) for j, h in jh]
    outs = _mem_attn_heads([qx_ref[r, c] for r, c in where], [kbuf[slot, j, h] for j, h in jh],
                           [vbuf[slot, j, h] for j, h in jh])
    for (r, c), o in zip(where, outs):
        o_ref[r, c] = o


def _mem_attn_cache(qx, mem_k, mem_v, seq, nsub):
    n = qx.shape[0]
    nb = mem_k.shape[0]
    assert nb % nsub == 0
    row = pl.BlockSpec((nsub * seq, D_MODEL), lambda b: (b, 0))
    hbm = pl.BlockSpec(memory_space=pl.ANY)
    buf = pltpu.VMEM((2, nsub, X_HEADS, N_MEM, X_HEAD_DIM), F32)
    return pl.pallas_call(
        functools.partial(_mem_attn_cache_body, seq=seq, nsub=nsub),
        grid=(nb // nsub,),
        in_specs=[row, hbm, hbm],
        out_specs=row,
        out_shape=jax.ShapeDtypeStruct((n, D_MODEL), BF16),
        scratch_shapes=[buf, buf, pltpu.SemaphoreType.DMA((2, nsub, 2, X_HEADS))],
        compiler_params=_params("arbitrary"),
        name="mem_attn_cache",
    )(qx, mem_k, mem_v)


def _mlp_body(*refs, ff_blk, second_norm):
    if second_norm:
        h1_ref, ox_ref, wxo_ref, lng_ref, lnb_ref, w1_ref, w2_ref, y_ref = refs

        def head(st):
            att = _dot(ox_ref[st["r"], :], wxo_ref[...])
            st["h2"] = _ln(ALPHA * h1_ref[st["r"], :] + att, lng_ref[1:2, :], lnb_ref[1:2, :])
    else:
        h2_ref, lng_ref, lnb_ref, w1_ref, w2_ref, y_ref = refs

        def head(st):
            st["h2"] = h2_ref[st["r"], :]

    def up(c):
        def stage(st):
            z = jnp.maximum(_dot(st["h2"].astype(BF16), w1_ref[:, c:c + ff_blk]), 0.0)
            st["z"] = (z * z).astype(BF16)
        return stage

    def down(c):
        def stage(st):
            d = _dot(st.pop("z"), w2_ref[c:c + ff_blk, :])
            st["acc"] = st["acc"] + d if "acc" in st else d
        return stage

    def tail(st):
        y_ref[st["r"], :] = _ln(ALPHA * st.pop("h2") + st.pop("acc"), lng_ref[2:3, :], lnb_ref[2:3, :])

    stages = [head] + [s for c in range(0, D_FF, ff_blk) for s in (up(c), down(c))] + [tail]
    _run_stages(stages, _row_states(y_ref, MLP_ROWS))


def _mlp(h, ox, w_xo, ln_g, ln_b, w1, w2, tn):
    n = h.shape[0]
    row = pl.BlockSpec((tn, D_MODEL), lambda i: (i, 0))
    once = lambda c: pl.BlockSpec(c.shape, lambda *_: (0,) * c.ndim, pipeline_mode=pl.Buffered(1))
    second_norm = ox is not None
    rows = (h, ox) if second_norm else (h,)
    consts = ((w_xo,) if second_norm else ()) + (ln_g, ln_b, w1, w2)
    return pl.pallas_call(
        functools.partial(_mlp_body, ff_blk=MLP_FF_BLOCK, second_norm=second_norm),
        grid=(n // tn,),
        in_specs=[row] * len(rows) + [once(c) for c in consts],
        out_specs=row,
        out_shape=jax.ShapeDtypeStruct((n, D_MODEL), F32),
        compiler_params=_params("parallel"),
        name="mlp",
    )(*rows, *consts)


def _mem_kv_body(mem_ref, wk_ref, wv_ref, k_ref, v_ref):
    m = mem_ref[...].astype(BF16)
    k_ref[...] = _dot(m, wk_ref[...])
    v_ref[...] = _dot(m, wv_ref[...])


def _mem_kv(mem2, wk, wv):
    n = mem2.shape[0]
    out = jax.ShapeDtypeStruct((n, D_MODEL), F32)
    return pl.pallas_call(
        _mem_kv_body,
        grid=(1,),
        in_specs=[_const(mem2.shape), _const(wk.shape), _const(wv.shape)],
        out_specs=[_const((n, D_MODEL))] * 2,
        out_shape=[out, out],
        compiler_params=_params("arbitrary"),
        name="mem_kv",
    )(mem2, wk, wv)


def _state_to_cols(s):
    return jnp.transpose(s.reshape(s.shape[0], S5_COLS, -1), (1, 0, 2))


def _state_from_cols(s):
    return jnp.transpose(s, (1, 0, 2)).reshape(s.shape[1], SSM_GROUPS, SSM_STATE)


def _layer(x2, pos, nseq, h0r, h0i, mem_k, mem_v, caches, wts, s5_ops, tn, scan_ct):
    n = x2.shape[0]
    seq = n // nseq
    prompt = caches is None
    u, q, k, v, ckv, kpe = _project(x2, pos, wts["inv"], wts["w_in"], wts["g_q"], wts["w_q"], wts["g_kv"],
                                    wts["w_k"], wts["w_vt"] if prompt else wts["w_v"], min(n, PROJ_TILE),
                                    v_transposed=prompt, consecutive=prompt, kpe_cache_shape=prompt)
    y_ssm, fr, fi = _s5_scan(u, s5_ops, _state_to_cols(h0r), _state_to_cols(h0i), nseq, scan_ct)
    glu_args = (y_ssm, u, wts["d_skip"], wts["w_glu"], wts["g_out_ssm"])
    if prompt:
        o_mla = _attention(q, k, v, ATTN_HEADS)
    else:
        o_mla = _attention_cached(q, k, v, caches[0], caches[1], wts["w_k"], wts["w_v"], seq, nsub=CACHE_SEQS)
    mix_args = (o_mla, x2, wts["g_out_mla"], wts["w_o"], wts["ln_g"], wts["ln_b"], wts["w_xq"])
    if prompt:
        h, ox = _mix_mem(*glu_args, *mix_args, mem_k, mem_v, wts["w_xo"], min(n, MIX_TILE)), None
    else:
        h, qx = _mix(_glu(*glu_args, tn), *mix_args, tn)
        ox = _mem_attn_cache(qx, mem_k, mem_v, seq, nsub=CACHE_SEQS)
    y = _mlp(h, ox, wts["w_xo"], wts["ln_g"], wts["ln_b"], wts["w_ff1"], wts["w_ff2"], min(n, MLP_TILE))
    return y, ckv, kpe, _state_from_cols(fr), _state_from_cols(fi)


def kernel(x_prompt, x_sample, mem_prompt, cache_mla_ckv, cache_mla_kpe, state_ssm_re, state_ssm_im, cache_mem_k, cache_mem_v, w_in, g_q, w_q_up, g_kv, w_kv_up, a_re, a_im, b_re, b_im, c_re, c_im, d_skip, log_dt, w_glu, g_out_ssm, g_out_mla, w_o, w_xq, w_xk, w_xv, w_xo, w_ff1, w_ff2, ln_g, ln_b):
    assert w_in.shape[0] == DEPTH == 1
    nbp, sp, _ = x_prompt.shape
    nbs, sd, _ = x_sample.shape
    past = cache_mla_ckv.shape[2]
    assert nbp == 1

    wq = jnp.pad(w_q_up[0], ((0, 0), (0, 0), (0, QK_PAD - QK_NOPE - QK_ROPE)))
    wk = w_kv_up[0][:, :, :QK_NOPE].reshape(KV_LORA, -1).astype(BF16)
    wv = w_kv_up[0][:, :, QK_NOPE:].reshape(KV_LORA, -1).astype(BF16)
    inv = ROPE_THETA ** (-jnp.arange(ROPE_HALF, dtype=F32) / ROPE_HALF)
    wts = {
        "inv": jnp.tile(inv, LANES // ROPE_HALF).reshape(1, LANES),
        "w_in": jnp.pad(w_in[0], ((0, 0), (0, LANES - QK_ROPE))).astype(BF16),
        "g_q": g_q[0].reshape(1, -1),
        "w_q": wq.reshape(Q_LORA, MLA_HEADS * QK_PAD).astype(BF16),
        "g_kv": g_kv[0].reshape(1, -1),
        "w_k": wk,
        "w_v": wv,
        "w_vt": wv.T,
        "d_skip": d_skip[0].reshape(1, -1),
        "w_glu": w_glu[0].astype(BF16),
        "g_out_ssm": g_out_ssm[0].reshape(1, -1),
        "g_out_mla": g_out_mla[0].reshape(1, -1),
        "w_o": w_o[0].astype(BF16),
        "w_xq": w_xq[0].reshape(D_MODEL, D_MODEL).astype(BF16),
        "w_xo": w_xo[0].reshape(D_MODEL, D_MODEL).astype(BF16),
        "w_ff1": w_ff1[0].astype(BF16),
        "w_ff2": w_ff2[0].astype(BF16),
        "ln_g": ln_g[0],
        "ln_b": ln_b[0],
    }
    s5_ops = _s5_prep(a_re[0], a_im[0], b_re[0], b_im[0], c_re[0], c_im[0], log_dt[0])

    mk, mv = _mem_kv(mem_prompt.reshape(nbp * N_MEM, D_MODEL),
                     w_xk[0].reshape(D_MODEL, D_MODEL).astype(BF16),
                     w_xv[0].reshape(D_MODEL, D_MODEL).astype(BF16))
    zero = jnp.zeros((nbp, SSM_GROUPS, SSM_STATE), F32)
    pos_p = jnp.arange(sp, dtype=F32).reshape(sp, 1)
    yp, ckv_p, kpe_p, sre_p, sim_p = _layer(
        x_prompt.reshape(sp, D_MODEL), pos_p, nbp, zero, zero, mk, mv, None,
        wts, s5_ops, tn=TOKEN_TILE, scan_ct=min(sp // S5_T, SCAN_CHUNKS))

    pos_s = jnp.tile(past + jnp.arange(sd, dtype=F32), nbs).reshape(nbs * sd, 1)
    caches = (cache_mla_ckv[0], cache_mla_kpe[0])
    ys, ckv_s, kpe_s, sre_s, sim_s = _layer(
        x_sample.reshape(nbs * sd, D_MODEL), pos_s, nbs,
        state_ssm_re[0], state_ssm_im[0], cache_mem_k[0], cache_mem_v[0], caches,
        wts, s5_ops, tn=TOKEN_TILE, scan_ct=sd // S5_T)

    return (yp.reshape(nbp, sp, D_MODEL), ys.reshape(nbs, sd, D_MODEL),
            ckv_p.reshape(1, nbp, sp, KV_LORA), kpe_p.reshape(1, nbp, sp, QK_ROPE),
            sre_p.reshape(1, nbp, SSM_GROUPS, SSM_STATE), sim_p.reshape(1, nbp, SSM_GROUPS, SSM_STATE),
            mk.reshape(1, nbp, N_MEM, X_HEADS, X_HEAD_DIM), mv.reshape(1, nbp, N_MEM, X_HEADS, X_HEAD_DIM),
            ckv_s.reshape(1, nbs, sd, KV_LORA), kpe_s.reshape(1, nbs, sd, QK_ROPE),
            sre_s.reshape(1, nbs, SSM_GROUPS, SSM_STATE), sim_s.reshape(1, nbs, SSM_GROUPS, SSM_STATE))
```

```python
import functools
import math

import jax
import jax.numpy as jnp
from jax import lax
from jax.experimental import pallas as pl
from jax.experimental.pallas import tpu as pltpu

F32 = jnp.float32
BF16 = jnp.bfloat16

D_MODEL = 1024
DEPTH = 1
CHUNK = 64
SSM_WIDTH = 512
SSM_GROUP = 16
SSM_GROUPS = 32
SSM_STATE = 64
MLA_HEADS = 4
QK_NOPE = 128
QK_ROPE = 64
V_HEAD = 128
MLA_WIDTH = MLA_HEADS * V_HEAD
Q_LORA = 384
KV_LORA = 256
ROPE_THETA = 10000.0
MLA_SCALE = (QK_NOPE + QK_ROPE) ** -0.5
N_MEM = 256
X_HEADS = 4
X_HEAD_DIM = D_MODEL // X_HEADS
D_FF = 4 * D_MODEL
ALPHA = (2 * DEPTH) ** 0.25
EPS = 1e-5
NEG_INF = -1e30

LANES = 128
MXU_DEPTH = 256
VMEM_BYTES = 64 * 1024 * 1024
VMEM_LIMIT = VMEM_BYTES - 8 * 1024 * 1024

QK_PAD = MXU_DEPTH
S5_T = MXU_DEPTH // SSM_GROUP
S5_COL_GROUPS = LANES // SSM_GROUP
S5_COLS = SSM_GROUPS // S5_COL_GROUPS
S5_SW = S5_COL_GROUPS * SSM_STATE
ROPE_HALF = QK_ROPE // 2
Q_SCALE = MLA_SCALE * math.log2(math.e)
ATTN_T = 512
ATTN_HEADS = 2
ATTN_TQ = 2 * ATTN_T
ATTN_QSPLIT = 4
MLP_TILE = 1024
MLP_ROWS = 512
MLP_FF_BLOCK = 1024
CACHE_SEQS = 4
MIX_TILE = 1024
MIX_ROWS = 256
PROJ_TILE = 1024
PROJ_ROWS = 256
TOKEN_TILE = 512
SCAN_CHUNKS = 512
SCAN_UNROLL = 8

_NT = (((1,), (1,)), ((), ()))


def _rms(x, g):
    return x * lax.rsqrt(jnp.mean(x * x, -1, keepdims=True) + EPS) * g


def _ln(x, g, b):
    mu = jnp.mean(x, -1, keepdims=True)
    xc = x - mu
    var = jnp.mean(xc * xc, -1, keepdims=True)
    return xc * lax.rsqrt(var + EPS) * g + b


def _dot(a, b):
    return jnp.dot(a, b, preferred_element_type=F32)


def _dot_nt(a, b):
    return lax.dot_general(a, b, _NT, preferred_element_type=F32)


def _split(a):
    hi = a.astype(BF16)
    return hi, (a - hi.astype(F32)).astype(BF16)


def _dot_split(a, b):
    return _dot(a[0], b[0]) + _dot(a[0], b[1]) + _dot(a[1], b[0])


def _params(*sem):
    return pltpu.CompilerParams(dimension_semantics=sem, vmem_limit_bytes=VMEM_LIMIT)


def _const(shape):
    n = len(shape)
    return pl.BlockSpec(shape, lambda *_: (0,) * n)


def _proj_body(x_ref, pos_ref, inv_ref, w_in_ref, gq_ref, wq_ref, gkv_ref, wk_ref, wv_ref,
               u_ref, q_ref, k_ref, v_ref, ckv_ref, kpe_ref, tc_ref, ts_ref, *, v_transposed, consecutive):
    rb = PROJ_ROWS
    assert x_ref.shape[0] % rb == 0 and ATTN_T % rb == 0
    blocks = [slice(r0, r0 + rb) for r0 in range(0, x_ref.shape[0], rb)]
    c0 = SSM_WIDTH + Q_LORA

    lane = lax.broadcasted_iota(jnp.int32, (1, LANES), 1)
    live = lane < QK_ROPE
    sign = jnp.where(lane < ROPE_HALF, -1.0, 1.0)
    inv = inv_ref[...]
    if consecutive:
        @pl.when(pl.program_id(0) == 0)
        def _():
            r = lax.broadcasted_iota(jnp.int32, (x_ref.shape[0], 1), 0).astype(F32)
            tc_ref[...] = jnp.where(live, jnp.cos(r * inv), 0.0)
            ts_ref[...] = jnp.where(live, jnp.sin(r * inv), 0.0)

    projs = [_dot(x_ref[r, :].astype(BF16), w_in_ref[...]) for r in blocks]

    if consecutive:
        base = pos_ref[0:1, :] * inv
        ca, sa = jnp.cos(base), jnp.sin(base)
        cos_ts = [ca * tc_ref[r, :] - sa * ts_ref[r, :] for r in blocks]
        sin_ts = [(sa * sign) * tc_ref[r, :] + (ca * sign) * ts_ref[r, :] for r in blocks]
    else:
        angs = [pos_ref[r, :] * inv for r in blocks]
        cos_ts = [jnp.where(live, jnp.cos(a), 0.0) for a in angs]
        sin_ts = [jnp.where(live, jnp.sin(a) * sign, 0.0) for a in angs]

    def rope(c2, cos_t, sin_t):
        swapped = jnp.where(lane < ROPE_HALF, pltpu.roll(c2, LANES - ROPE_HALF, 1), pltpu.roll(c2, ROPE_HALF, 1))
        return c2 * cos_t + swapped * sin_t

    for r, p, cos_t, sin_t in zip(blocks, projs, cos_ts, sin_ts):
        cq = _rms(p[:, SSM_WIDTH:c0], gq_ref[...]).astype(BF16)
        ckv = _rms(p[:, c0:c0 + KV_LORA], gkv_ref[...])
        ckv_b = ckv.astype(BF16)
        q = _dot(cq, wq_ref[...]) * Q_SCALE
        kn = _dot(ckv_b, wk_ref[...])
        if v_transposed:
            lanes = slice(r.start % ATTN_T, r.start % ATTN_T + rb)
            v_ref[r.start // ATTN_T, :, lanes] = _dot_nt(wv_ref[...], ckv_b).astype(BF16)
        else:
            v_ref[r, :] = _dot(ckv_b, wv_ref[...]).astype(BF16)
        u_ref[r, :] = p[:, :SSM_WIDTH]
        ckv_ref[r, :] = ckv
        kpe = rope(p[:, c0 + KV_LORA:], cos_t, sin_t)
        kpe_ref[r, :] = kpe[:, :QK_ROPE]
        kpe_b = kpe.astype(BF16)
        for h in range(MLA_HEADS):
            a = h * QK_PAD
            q_ref[r, a:a + QK_NOPE] = q[:, a:a + QK_NOPE].astype(BF16)
            q_ref[r, a + QK_NOPE:a + QK_PAD] = rope(q[:, a + QK_NOPE:a + QK_PAD], cos_t, sin_t).astype(BF16)
            k_ref[r, a:a + QK_NOPE] = kn[:, h * QK_NOPE:(h + 1) * QK_NOPE].astype(BF16)
            k_ref[r, a + QK_NOPE:a + QK_PAD] = kpe_b


def _project(x2, pos, inv, w_in, gq, wq, gkv, wk, wv, tn, v_transposed, consecutive, kpe_cache_shape):
    n = x2.shape[0]
    row = lambda w: pl.BlockSpec((tn, w), lambda i: (i, 0))
    if kpe_cache_shape:
        kpe_spec = pl.BlockSpec((None, None, tn, QK_ROPE), lambda i: (0, 0, i, 0))
        kpe_shape = jax.ShapeDtypeStruct((1, 1, n, QK_ROPE), F32)
    else:
        kpe_spec, kpe_shape = row(QK_ROPE), jax.ShapeDtypeStruct((n, QK_ROPE), F32)
    if v_transposed:
        assert tn % ATTN_T == 0
        v_spec = pl.BlockSpec((tn // ATTN_T, MLA_WIDTH, ATTN_T), lambda i: (i, 0, 0))
        v_shape = jax.ShapeDtypeStruct((n // ATTN_T, MLA_WIDTH, ATTN_T), BF16)
    else:
        v_spec, v_shape = row(MLA_WIDTH), jax.ShapeDtypeStruct((n, MLA_WIDTH), BF16)
    return pl.pallas_call(
        functools.partial(_proj_body, v_transposed=v_transposed, consecutive=consecutive),
        grid=(n // tn,),
        scratch_shapes=[pltpu.VMEM((tn, LANES), F32)] * 2,
        in_specs=[row(D_MODEL), row(1), _const(inv.shape), _const(w_in.shape), _const(gq.shape),
                  _const(wq.shape), _const(gkv.shape), _const(wk.shape), _const(wv.shape)],
        out_specs=[row(SSM_WIDTH), row(MLA_HEADS * QK_PAD), row(MLA_HEADS * QK_PAD),
                   v_spec, row(KV_LORA), kpe_spec],
        out_shape=[jax.ShapeDtypeStruct((n, SSM_WIDTH), F32),
                   jax.ShapeDtypeStruct((n, MLA_HEADS * QK_PAD), BF16),
                   jax.ShapeDtypeStruct((n, MLA_HEADS * QK_PAD), BF16),
                   v_shape,
                   jax.ShapeDtypeStruct((n, KV_LORA), F32),
                   kpe_shape],
        compiler_params=_params("arbitrary"),
        name="proj",
    )(x2, pos, inv, w_in, gq, wq, gkv, wk, wv)


def _s5_prep_body(ar_row, ai_row, ldt_row, bt_re, bt_im, ct_re, ct_im,
                  wt_ref, v_ref, z_ref, are_ref, aim_ref):
    sw = S5_SW
    arr, air, dtr = ar_row[0], ai_row[0], jnp.exp(ldt_row[0])

    mag = jnp.exp(arr * dtr)
    lr, li = mag * jnp.cos(air * dtr), mag * jnp.sin(air * dtr)
    nr, ni = lr - 1.0, li
    den = arr * arr + air * air
    f_re, f_im = (nr * arr + ni * air) / den, (ni * arr - nr * air) / den

    same_b = (lax.broadcasted_iota(jnp.int32, (LANES, sw), 0) // SSM_GROUP
              == lax.broadcasted_iota(jnp.int32, (LANES, sw), 1) // SSM_STATE)
    br = jnp.where(same_b, bt_re[0], 0.0)
    bi = jnp.where(same_b, bt_im[0], 0.0)
    bb_re = f_re * br - f_im * bi
    bb_im = f_re * bi + f_im * br
    same_c = (lax.broadcasted_iota(jnp.int32, (sw, LANES), 0) // SSM_STATE
              == lax.broadcasted_iota(jnp.int32, (sw, LANES), 1) // SSM_GROUP)
    cr = jnp.where(same_c, ct_re[0], 0.0)
    ci = jnp.where(same_c, ct_im[0], 0.0)
    cr_s, ci_s = _split(cr), _split(ci)

    e = lax.broadcasted_iota(jnp.int32, (2 * S5_T, 1), 0).astype(F32)
    pm = jnp.exp(arr * dtr * e)
    pw_re, pw_im = pm * jnp.cos(air * dtr * e), pm * jnp.sin(air * dtr * e)
    pt_re, pt_im = pw_re.T, pw_im.T

    lag_ops = []
    for lag in range(S5_T):
        p_re, p_im = pw_re[lag:lag + 1, :], pw_im[lag:lag + 1, :]
        k_re = p_re * bb_re - p_im * bb_im
        k_im = p_re * bb_im + p_im * bb_re
        i = S5_T - 1 - lag
        v_ref[0, i * LANES:(i + 1) * LANES, :sw] = k_re.astype(BF16)
        v_ref[0, i * LANES:(i + 1) * LANES, sw:] = k_im.astype(BF16)
        lag_ops.append(_dot_split(_split(k_re), cr_s) - _dot_split(_split(k_im), ci_s))

        q_re, q_im = pt_re[:, lag + 1:lag + 2], pt_im[:, lag + 1:lag + 2]
        z_ref[0, :sw, lag * LANES:(lag + 1) * LANES] = (cr * q_re - ci * q_im).astype(BF16)
        z_ref[0, sw:, lag * LANES:(lag + 1) * LANES] = (-(cr * q_im + ci * q_re)).astype(BF16)

    zero = jnp.zeros((LANES, LANES), BF16)
    for d in range(S5_T // 2):
        wt_ref[0, d, :LANES, :LANES] = lag_ops[2 * d].astype(BF16)
        wt_ref[0, d, :LANES, LANES:] = lag_ops[2 * d + 1].astype(BF16)
        wt_ref[0, d, LANES:, :LANES] = lag_ops[2 * d - 1].astype(BF16) if d else zero
        wt_ref[0, d, LANES:, LANES:] = lag_ops[2 * d].astype(BF16)

    are_ref[0] = pw_re[S5_T:S5_T + 1, :]
    aim_ref[0] = pw_im[S5_T:S5_T + 1, :]


def _s5_prep(a_re, a_im, b_re, b_im, c_re, c_im, log_dt):
    nc, r, t, h, p, sw = S5_COLS, S5_COL_GROUPS, S5_T, SSM_GROUP, SSM_STATE, S5_SW
    ldt = jnp.repeat(log_dt, p)
    bt = lambda b: jnp.tile(jnp.transpose(b.reshape(nc, r, p, h), (0, 1, 3, 2)).reshape(nc, r * h, p), (1, 1, r))
    ct = lambda c: jnp.tile(jnp.transpose(c.reshape(nc, r, h, p), (0, 1, 3, 2)).reshape(nc, r * p, h), (1, 1, r))
    args = (a_re.reshape(nc, 1, sw), a_im.reshape(nc, 1, sw), ldt.reshape(nc, 1, sw),
            bt(b_re), bt(b_im), ct(c_re), ct(c_im))
    blk = lambda s: pl.BlockSpec((1,) + s[1:], lambda i: (i,) + (0,) * (len(s) - 1))
    outs = [((nc, t // 2, 2 * LANES, 2 * LANES), BF16), ((nc, t * LANES, 2 * sw), BF16),
            ((nc, 2 * sw, t * LANES), BF16), ((nc, 1, sw), F32), ((nc, 1, sw), F32)]
    return pl.pallas_call(
        _s5_prep_body,
        grid=(nc,),
        in_specs=[blk(a.shape) for a in args],
        out_specs=[blk(s) for s, _ in outs],
        out_shape=[jax.ShapeDtypeStruct(s, d) for s, d in outs],
        compiler_params=_params("parallel"),
        name="s5_prep",
    )(*args)


def _s5_scan_body(u_ref, wt_ref, v_ref, z_ref, are_ref, aim_ref, h0r_ref, h0i_ref,
                  y_ref, fr_ref, fi_ref, lhs, sre, sim, xre, xim, cr, ci, *, nseq, ct):
    t = pl.program_id(1)
    sw = S5_SW
    seq = ct * S5_T

    @pl.when(t == 0)
    def _():
        cr[...] = h0r_ref[0]
        ci[...] = h0i_ref[0]

    def token_rows(cc, i):
        if nseq == 1:
            return pl.ds(i, ct, stride=S5_T), slice(None)
        return pl.ds(cc * S5_T + i, nseq, stride=seq), slice(cc * nseq, (cc + 1) * nseq)

    for cc in range(1 if nseq == 1 else ct):
        for i in range(S5_T):
            tok, crow = token_rows(cc, i)
            lhs[crow, i * LANES:(i + 1) * LANES] = u_ref[tok, :].astype(BF16)

    s = _dot(lhs[...], v_ref[0])
    sre[...] = s[:, :sw]
    sim[...] = s[:, sw:]

    a_re, a_im = are_ref[0], aim_ref[0]

    def step(c, carry):
        x_re, x_im = carry
        r = pl.ds(c * nseq, nseq)
        xre[r, :] = x_re
        xim[r, :] = x_im
        n_re = a_re * x_re - a_im * x_im + sre[r, :]
        n_im = a_re * x_im + a_im * x_re + sim[r, :]
        return n_re, n_im

    x_re, x_im = lax.fori_loop(0, ct, step, (cr[...], ci[...]), unroll=min(ct, SCAN_UNROLL))
    cr[...] = x_re
    ci[...] = x_im
    fr_ref[0] = x_re
    fi_ref[0] = x_im

    xb_re, xb_im = xre[...].astype(BF16), xim[...].astype(BF16)
    for jp in range(S5_T // 2):
        cols = slice(jp * 2 * LANES, (jp + 1) * 2 * LANES)
        acc = _dot(xb_re, z_ref[0, :sw, cols]) + _dot(xb_im, z_ref[0, sw:, cols])
        for ip in range(jp + 1):
            acc += _dot(lhs[:, ip * 2 * LANES:(ip + 1) * 2 * LANES], wt_ref[0, jp - ip])
        for cc in range(1 if nseq == 1 else ct):
            for jj in range(2):
                tok, crow = token_rows(cc, 2 * jp + jj)
                y_ref[tok, :] = acc[crow, jj * LANES:(jj + 1) * LANES]


def _s5_scan(u, col_ops, h0r, h0i, nseq, ct):
    wt, v, z, are, aim = col_ops
    n = u.shape[0]
    cps = n // nseq // S5_T
    assert nseq == 1 or ct == cps
    nc = nseq * ct
    sw = S5_SW
    tile = pl.BlockSpec((nc * S5_T, LANES), lambda c, t: (t, c))
    op = lambda a: pl.BlockSpec((1,) + a.shape[1:], lambda c, t: (c,) + (0,) * (a.ndim - 1))
    st = pl.BlockSpec((1, nseq, sw), lambda c, t: (c, 0, 0))
    st_shape = jax.ShapeDtypeStruct((S5_COLS, nseq, sw), F32)
    return pl.pallas_call(
        functools.partial(_s5_scan_body, nseq=nseq, ct=ct),
        grid=(S5_COLS, cps // ct),
        in_specs=[tile, op(wt), op(v), op(z), op(are), op(aim), st, st],
        out_specs=[tile, st, st],
        out_shape=[jax.ShapeDtypeStruct(u.shape, F32), st_shape, st_shape],
        scratch_shapes=[pltpu.VMEM((nc, S5_T * LANES), BF16)] + [pltpu.VMEM((nc, sw), F32)] * 4
        + [pltpu.VMEM((nseq, sw), F32)] * 2,
        compiler_params=_params("arbitrary", "arbitrary"),
        name="s5_scan",
    )(u, wt, v, z, are, aim, h0r, h0i)


def _run_stages(stages, states):
    for t in range(len(stages) + len(states) - 1):
        for b, st in enumerate(states):
            if 0 <= t - b < len(stages):
                stages[t - b](st)


def _row_states(ref, rows):
    assert ref.shape[0] % rows == 0
    return [{"r": slice(r0, r0 + rows)} for r0 in range(0, ref.shape[0], rows)]


def _glu_stages(y_ref, u_ref, d_ref, w_ref, g_ref):
    def act(st):
        st["act"] = jax.nn.gelu(y_ref[st["r"], :] + d_ref[...] * u_ref[st["r"], :]).astype(BF16)

    def glu(st):
        st["gl"] = _dot(st.pop("act"), w_ref[...])

    def gate(st):
        gl = st.pop("gl")
        st["os"] = _rms(gl[:, :SSM_WIDTH] * jax.nn.sigmoid(gl[:, SSM_WIDTH:]), g_ref[...]).astype(BF16)

    return [act, glu, gate]


def _glu_body(y_ref, u_ref, d_ref, w_ref, g_ref, o_ref):
    def store(st):
        o_ref[st["r"], :] = st.pop("os")

    _run_stages(_glu_stages(y_ref, u_ref, d_ref, w_ref, g_ref) + [store], _row_states(y_ref, MIX_ROWS))


def _glu(y, u, d, w, g, tn):
    n = y.shape[0]
    row = pl.BlockSpec((tn, SSM_WIDTH), lambda i: (i, 0))
    return pl.pallas_call(
        _glu_body,
        grid=(n // tn,),
        in_specs=[row, row, _const(d.shape), _const(w.shape), _const(g.shape)],
        out_specs=row,
        out_shape=jax.ShapeDtypeStruct((n, SSM_WIDTH), BF16),
        compiler_params=_params("parallel"),
        name="glu",
    )(y, u, d, w, g)


def _attn_body(q_ref, k_ref, vt_ref, o_ref, s0, s1, mt0, mt1, m_sc, l_sc, acc_sc, *, heads):
    i = pl.program_id(1)
    t = ATTN_T
    m_sc[...] = jnp.full(m_sc.shape, -jnp.inf, F32)
    l_sc[...] = jnp.zeros(l_sc.shape, F32)
    acc_sc[...] = jnp.zeros(acc_sc.shape, F32)

    qb = q_ref.shape[0] // ATTN_QSPLIT
    units = [(h, c * qb) for c in range(ATTN_QSPLIT) for h in range(heads)]

    def scores(tile, s_buf, mt_buf, unit, diag=None):
        h, c0 = unit
        r = pl.ds(pl.multiple_of(tile * t, t), t)
        s = _dot_nt(k_ref[r, h * QK_PAD:(h + 1) * QK_PAD],
                    q_ref[c0:c0 + qb, h * QK_PAD:(h + 1) * QK_PAD])
        if diag is not None:
            kc = (diag * t + lax.broadcasted_iota(jnp.int32, s.shape, 0)) // CHUNK
            qc = (c0 + lax.broadcasted_iota(jnp.int32, s.shape, 1)) // CHUNK
            s = jnp.where(kc <= qc, s, NEG_INF)
        s_buf[h, :, c0:c0 + qb] = s
        mt_buf[h, :, c0:c0 + qb] = jnp.max(s, 0, keepdims=True)

    def absorb(tile, s_buf, mt_buf, unit):
        h, c0 = unit
        cols = slice(c0, c0 + qb)
        m_prev = m_sc[h, :, cols]
        m_new = jnp.maximum(m_prev, mt_buf[h, :, cols])
        p = jnp.exp2(s_buf[h, :, cols] - m_new)
        alpha = jnp.exp2(m_prev - m_new)
        l_sc[h, :, cols] = alpha * l_sc[h, :, cols] + jnp.sum(p, 0, keepdims=True)
        m_sc[h, :, cols] = m_new
        pv = _dot(vt_ref[tile, h * V_HEAD:(h + 1) * V_HEAD, :], p.astype(BF16))
        acc_sc[h, :, cols] = alpha * acc_sc[h, :, cols] + pv

    assert q_ref.shape[0] == 2 * t
    for u in units:
        scores(2 * i, s0, mt0, u, diag=0)

    def pair(jj, c):
        prev = jnp.where(jj == 0, 2 * i, 2 * jj - 1)
        for u in units:
            scores(2 * jj, s1, mt1, u)
            absorb(prev, s0, mt0, u)
        for u in units:
            scores(2 * jj + 1, s0, mt0, u)
            absorb(2 * jj, s1, mt1, u)
        return c

    lax.fori_loop(0, i, pair, 0)
    def finish(unit):
        h, c0 = unit
        cols = slice(c0, c0 + qb)
        o_ref[cols, h * V_HEAD:(h + 1) * V_HEAD] = (acc_sc[h, :, cols] / l_sc[h, :, cols]).T

    last = jnp.where(i == 0, 0, 2 * i - 1)
    live = [u for u in units if u[1] + qb > t]
    for k, u in enumerate(units):
        if k < len(live):
            scores(2 * i + 1, s1, mt1, live[k], diag=1)
        absorb(last, s0, mt0, u)
        if u not in live:
            finish(u)
    for u in live:
        absorb(2 * i + 1, s1, mt1, u)
        finish(u)


def _attention(q, k, vt, heads):
    n = q.shape[0]
    t = ATTN_T
    tq = ATTN_TQ
    assert t % CHUNK == 0 and n % tq == 0 and MLA_HEADS % heads == 0 and tq == 2 * t
    once = pl.Buffered(1)
    return pl.pallas_call(
        functools.partial(_attn_body, heads=heads),
        grid=(MLA_HEADS // heads, n // tq),
        in_specs=[pl.BlockSpec((tq, heads * QK_PAD), lambda g, i: (i, g)),
                  pl.BlockSpec((n, heads * QK_PAD), lambda g, i: (0, g), pipeline_mode=once),
                  pl.BlockSpec((n // t, heads * V_HEAD, t), lambda g, i: (0, g, 0), pipeline_mode=once)],
        out_specs=pl.BlockSpec((tq, heads * V_HEAD), lambda g, i: (i, g)),
        out_shape=jax.ShapeDtypeStruct((n, MLA_WIDTH), F32),
        scratch_shapes=[pltpu.VMEM((heads, t, tq), F32)] * 2 + [pltpu.VMEM((heads, 1, tq), F32)] * 4
        + [pltpu.VMEM((heads, V_HEAD, tq), F32)],
        compiler_params=_params("arbitrary", "arbitrary"),
        name="attn",
    )(q, k, vt)


def _attn_cached_body(q_ref, kn_ref, vn_ref, ckv_ref, kpe_ref, wk_ref, wv_ref, o_ref, *, past, seq, nsub):
    heads = range(MLA_HEADS)
    pos = past + lax.broadcasted_iota(jnp.int32, (seq, 1), 0)
    qc = jnp.concatenate([pos] * MLA_HEADS, axis=0) // CHUNK
    kc_past = lax.broadcasted_iota(jnp.int32, (1, past), 1) // CHUNK
    kc_new = (past + lax.broadcasted_iota(jnp.int32, (1, seq), 1)) // CHUNK
    seqs = range(nsub)
    rows = [slice(b * seq, (b + 1) * seq) for b in seqs]
    ckv = [ckv_ref[b].astype(BF16) for b in seqs]
    kpe = [kpe_ref[b].astype(BF16) for b in seqs]
    qa = [jnp.concatenate(
        [_dot_nt(q_ref[r, h * QK_PAD:h * QK_PAD + QK_NOPE], wk_ref[:, h * QK_NOPE:(h + 1) * QK_NOPE])
         for h in heads], axis=0).astype(BF16) for r in rows]
    qpe = [jnp.concatenate(
        [q_ref[r, h * QK_PAD + QK_NOPE:h * QK_PAD + QK_NOPE + QK_ROPE] for h in heads], axis=0) for r in rows]
    s_p = [_dot_nt(qa[b], ckv[b]) + _dot_nt(qpe[b], kpe[b]) for b in seqs]
    s_n = [jnp.concatenate(
        [_dot_nt(q_ref[r, h * QK_PAD:(h + 1) * QK_PAD], kn_ref[r, h * QK_PAD:(h + 1) * QK_PAD])
         for h in heads], axis=0) for r in rows]
    s_p = [jnp.where(kc_past <= qc, s, NEG_INF) for s in s_p]
    s_n = [jnp.where(kc_new <= qc, s, NEG_INF) for s in s_n]
    m = [jnp.maximum(jnp.max(a, -1, keepdims=True), jnp.max(c, -1, keepdims=True)) for a, c in zip(s_p, s_n)]
    e_p = [jnp.exp2(s - mm) for s, mm in zip(s_p, m)]
    e_n = [jnp.exp2(s - mm) for s, mm in zip(s_n, m)]
    l = [jnp.sum(a, -1, keepdims=True) + jnp.sum(c, -1, keepdims=True) for a, c in zip(e_p, e_n)]
    p_n = [(e / ll).astype(BF16) for e, ll in zip(e_n, l)]
    o_lat = [_dot((e / ll).astype(BF16), c).astype(BF16) for e, ll, c in zip(e_p, l, ckv)]
    for b in seqs:
        for h in heads:
            hr = slice(h * seq, (h + 1) * seq)
            cols = slice(h * V_HEAD, (h + 1) * V_HEAD)
            o_ref[rows[b], cols] = _dot(o_lat[b][hr], wv_ref[:, cols]) + _dot(p_n[b][hr], vn_ref[rows[b], cols])


def _attention_cached(q, k, v, ckv_cache, kpe_cache, wk, wv, seq, nsub):
    n = q.shape[0]
    nb, past, _ = ckv_cache.shape
    assert nb % nsub == 0
    row = lambda w: pl.BlockSpec((nsub * seq, w), lambda b: (b, 0))
    kern = functools.partial(_attn_cached_body, past=past, seq=seq, nsub=nsub)
    return pl.pallas_call(
        kern,
        grid=(nb // nsub,),
        in_specs=[row(MLA_HEADS * QK_PAD), row(MLA_HEADS * QK_PAD), row(MLA_WIDTH),
                  pl.BlockSpec((nsub, past, KV_LORA), lambda b: (b, 0, 0)),
                  pl.BlockSpec((nsub, past, QK_ROPE), lambda b: (b, 0, 0)),
                  _const(wk.shape), _const(wv.shape)],
        out_specs=row(MLA_WIDTH),
        out_shape=jax.ShapeDtypeStruct((n, MLA_WIDTH), F32),
        compiler_params=_params("parallel"),
        name="attn_cached",
    )(q, k, v, ckv_cache, kpe_cache, wk, wv)


def _mix_stages(om_ref, x_ref, gm_ref, wo_ref, lng_ref, lnb_ref, wxq_ref):
    def norm_mla(st):
        st["om"] = _rms(om_ref[st["r"], :], gm_ref[...]).astype(BF16)

    def project(st):
        st["a"] = _dot(st.pop("os"), wo_ref[:SSM_WIDTH, :]) + _dot(st.pop("om"), wo_ref[SSM_WIDTH:, :])

    def norm1(st):
        st["h1"] = _ln(ALPHA * x_ref[st["r"], :] + st.pop("a"), lng_ref[0:1, :], lnb_ref[0:1, :])

    def query(st):
        st["qx"] = _dot(st["h1"].astype(BF16), wxq_ref[...]).astype(BF16)

    return [norm_mla, project, norm1, query]


def _mix_body(os_ref, om_ref, x_ref, gm_ref, wo_ref, lng_ref, lnb_ref, wxq_ref, h1_ref, qx_ref):
    def load(st):
        st["os"] = os_ref[st["r"], :]

    def store(st):
        h1_ref[st["r"], :] = st.pop("h1")
        qx_ref[st["r"], :] = st.pop("qx")

    stages = [load] + _mix_stages(om_ref, x_ref, gm_ref, wo_ref, lng_ref, lnb_ref, wxq_ref) + [store]
    _run_stages(stages, _row_states(x_ref, MIX_ROWS))


def _mix(o_ssm, o_mla, x2, g_mla, w_o, ln_g, ln_b, w_xq, tn):
    n = x2.shape[0]
    row = lambda w: pl.BlockSpec((tn, w), lambda i: (i, 0))
    return pl.pallas_call(
        _mix_body,
        grid=(n // tn,),
        in_specs=[row(SSM_WIDTH), row(MLA_WIDTH), row(D_MODEL), _const(g_mla.shape), _const(w_o.shape),
                  _const(ln_g.shape), _const(ln_b.shape), _const(w_xq.shape)],
        out_specs=[row(D_MODEL), row(D_MODEL)],
        out_shape=[jax.ShapeDtypeStruct((n, D_MODEL), F32), jax.ShapeDtypeStruct((n, D_MODEL), BF16)],
        compiler_params=_params("parallel"),
        name="mix",
    )(o_ssm, o_mla, x2, g_mla, w_o, ln_g, ln_b, w_xq)


def _mem_attn_heads(qs, mks, mvs):
    ss = [_dot_nt(q, mk.astype(BF16)) * (X_HEAD_DIM ** -0.5) for q, mk in zip(qs, mks)]
    es = [jnp.exp(s - jnp.max(s, -1, keepdims=True)) for s in ss]
    ps = [(e / jnp.sum(e, -1, keepdims=True)).astype(BF16) for e in es]
    return [_dot(p, mv.astype(BF16)).astype(BF16) for p, mv in zip(ps, mvs)]


def _mix_mem_body(y_ref, u_ref, d_ref, wglu_ref, gs_ref, om_ref, x_ref, gm_ref, wo_ref, lng_ref, lnb_ref, wxq_ref,
                  mk_ref, mv_ref, wxo_ref, h2_ref):
    cols = [slice(h * X_HEAD_DIM, (h + 1) * X_HEAD_DIM) for h in range(X_HEADS)]
    mks = [mk_ref[:, c].astype(BF16) for c in cols]
    mvs = [mv_ref[:, c].astype(BF16) for c in cols]

    def scores(st):
        qx = st.pop("qx")
        st["s"] = [_dot_nt(qx[:, c], mk) * (X_HEAD_DIM ** -0.5) for c, mk in zip(cols, mks)]

    def softmax(st):
        es = [jnp.exp(s - jnp.max(s, -1, keepdims=True)) for s in st.pop("s")]
        st["p"] = [(e / jnp.sum(e, -1, keepdims=True)).astype(BF16) for e in es]

    def values(st):
        st["ox"] = jnp.concatenate([_dot(p, mv).astype(BF16) for p, mv in zip(st.pop("p"), mvs)], axis=1)

    def project(st):
        st["att"] = _dot(st.pop("ox"), wxo_ref[...])

    def norm2(st):
        h2_ref[st["r"], :] = _ln(ALPHA * st.pop("h1") + st.pop("att"), lng_ref[1:2, :], lnb_ref[1:2, :])

    stages = (_glu_stages(y_ref, u_ref, d_ref, wglu_ref, gs_ref)
              + _mix_stages(om_ref, x_ref, gm_ref, wo_ref, lng_ref, lnb_ref, wxq_ref)
              + [scores, softmax, values, project, norm2])
    _run_stages(stages, _row_states(x_ref, MIX_ROWS))


def _mix_mem(y_ssm, u, d_skip, w_glu, g_ssm, o_mla, x2, g_mla, w_o, ln_g, ln_b, w_xq, mem_k, mem_v, w_xo, tn):
    n = x2.shape[0]
    row = lambda w: pl.BlockSpec((tn, w), lambda i: (i, 0))
    glu_consts = (d_skip, w_glu, g_ssm)
    consts = (g_mla, w_o, ln_g, ln_b, w_xq, mem_k, mem_v, w_xo)
    return pl.pallas_call(
        _mix_mem_body,
        grid=(n // tn,),
        in_specs=[row(SSM_WIDTH), row(SSM_WIDTH)] + [_const(c.shape) for c in glu_consts]
        + [row(MLA_WIDTH), row(D_MODEL)] + [_const(c.shape) for c in consts],
        out_specs=row(D_MODEL),
        out_shape=jax.ShapeDtypeStruct((n, D_MODEL), F32),
        compiler_params=_params("parallel"),
        name="mix_mem",
    )(y_ssm, u, *glu_consts, o_mla, x2, *consts)


def _mem_attn_cache_body(qx_ref, mk_hbm, mv_hbm, o_ref, kbuf, vbuf, sems, *, seq, nsub):
    b = pl.program_id(0)
    nb = pl.num_programs(0)

    def copies(step, slot):
        return [pltpu.make_async_copy(src.at[step * nsub + j, :, h, :], buf.at[slot, j, h], sems.at[slot, j, t, h])
                for t, (src, buf) in enumerate(((mk_hbm, kbuf), (mv_hbm, vbuf)))
                for j in range(nsub) for h in range(X_HEADS)]

    @pl.when(b == 0)
    def _():
        for n, c in enumerate(copies(0, 0)):
            c.start(priority=n % 2)

    @pl.when(b + 1 < nb)
    def _():
        for n, c in enumerate(copies(b + 1, (b + 1) % 2)):
            c.start(priority=n % 2)

    slot = b % 2
    for c in copies(b, slot):
        c.wait()
    jh = [(j, h) for j in range(nsub) for h in range(X_HEADS)]
    where = [(slice(j * seq, (j + 1) * seq), slice(h * X_HEAD_DIM, (h + 1) * X_HEAD_DIM)) for j, h in jh]
    outs = _mem_attn_heads([qx_ref[r, c] for r, c in where], [kbuf[slot, j, h] for j, h in jh],
                           [vbuf[slot, j, h] for j, h in jh])
    for (r, c), o in zip(where, outs):
        o_ref[r, c] = o


def _mem_attn_cache(qx, mem_k, mem_v, seq, nsub):
    n = qx.shape[0]
    nb = mem_k.shape[0]
    assert nb % nsub == 0
    row = pl.BlockSpec((nsub * seq, D_MODEL), lambda b: (b, 0))
    hbm = pl.BlockSpec(memory_space=pl.ANY)
    buf = pltpu.VMEM((2, nsub, X_HEADS, N_MEM, X_HEAD_DIM), F32)
    return pl.pallas_call(
        functools.partial(_mem_attn_cache_body, seq=seq, nsub=nsub),
        grid=(nb // nsub,),
        in_specs=[row, hbm, hbm],
        out_specs=row,
        out_shape=jax.ShapeDtypeStruct((n, D_MODEL), BF16),
        scratch_shapes=[buf, buf, pltpu.SemaphoreType.DMA((2, nsub, 2, X_HEADS))],
        compiler_params=_params("arbitrary"),
        name="mem_attn_cache",
    )(qx, mem_k, mem_v)


def _mlp_body(*refs, ff_blk, second_norm):
    if second_norm:
        h1_ref, ox_ref, wxo_ref, lng_ref, lnb_ref, w1_ref, w2_ref, y_ref = refs

        def head(st):
            att = _dot(ox_ref[st["r"], :], wxo_ref[...])
            st["h2"] = _ln(ALPHA * h1_ref[st["r"], :] + att, lng_ref[1:2, :], lnb_ref[1:2, :])
    else:
        h2_ref, lng_ref, lnb_ref, w1_ref, w2_ref, y_ref = refs

        def head(st):
            st["h2"] = h2_ref[st["r"], :]

    def up(c):
        def stage(st):
            z = jnp.maximum(_dot(st["h2"].astype(BF16), w1_ref[:, c:c + ff_blk]), 0.0)
            st["z"] = (z * z).astype(BF16)
        return stage

    def down(c):
        def stage(st):
            d = _dot(st.pop("z"), w2_ref[c:c + ff_blk, :])
            st["acc"] = st["acc"] + d if "acc" in st else d
        return stage

    def tail(st):
        y_ref[st["r"], :] = _ln(ALPHA * st.pop("h2") + st.pop("acc"), lng_ref[2:3, :], lnb_ref[2:3, :])

    stages = [head] + [s for c in range(0, D_FF, ff_blk) for s in (up(c), down(c))] + [tail]
    _run_stages(stages, _row_states(y_ref, MLP_ROWS))


def _mlp(h, ox, w_xo, ln_g, ln_b, w1, w2, tn):
    n = h.shape[0]
    row = pl.BlockSpec((tn, D_MODEL), lambda i: (i, 0))
    once = lambda c: pl.BlockSpec(c.shape, lambda *_: (0,) * c.ndim, pipeline_mode=pl.Buffered(1))
    second_norm = ox is not None
    rows = (h, ox) if second_norm else (h,)
    consts = ((w_xo,) if second_norm else ()) + (ln_g, ln_b, w1, w2)
    return pl.pallas_call(
        functools.partial(_mlp_body, ff_blk=MLP_FF_BLOCK, second_norm=second_norm),
        grid=(n // tn,),
        in_specs=[row] * len(rows) + [once(c) for c in consts],
        out_specs=row,
        out_shape=jax.ShapeDtypeStruct((n, D_MODEL), F32),
        compiler_params=_params("parallel"),
        name="mlp",
    )(*rows, *consts)


def _mem_kv_body(mem_ref, wk_ref, wv_ref, k_ref, v_ref):
    m = mem_ref[...].astype(BF16)
    k_ref[...] = _dot(m, wk_ref[...])
    v_ref[...] = _dot(m, wv_ref[...])


def _mem_kv(mem2, wk, wv):
    n = mem2.shape[0]
    out = jax.ShapeDtypeStruct((n, D_MODEL), F32)
    return pl.pallas_call(
        _mem_kv_body,
        grid=(1,),
        in_specs=[_const(mem2.shape), _const(wk.shape), _const(wv.shape)],
        out_specs=[_const((n, D_MODEL))] * 2,
        out_shape=[out, out],
        compiler_params=_params("arbitrary"),
        name="mem_kv",
    )(mem2, wk, wv)


def _state_to_cols(s):
    return jnp.transpose(s.reshape(s.shape[0], S5_COLS, -1), (1, 0, 2))


def _state_from_cols(s):
    return jnp.transpose(s, (1, 0, 2)).reshape(s.shape[1], SSM_GROUPS, SSM_STATE)


def _layer(x2, pos, nseq, h0r, h0i, mem_k, mem_v, caches, wts, s5_ops, tn, scan_ct):
    n = x2.shape[0]
    seq = n // nseq
    prompt = caches is None
    u, q, k, v, ckv, kpe = _project(x2, pos, wts["inv"], wts["w_in"], wts["g_q"], wts["w_q"], wts["g_kv"],
                                    wts["w_k"], wts["w_vt"] if prompt else wts["w_v"], min(n, PROJ_TILE),
                                    v_transposed=prompt, consecutive=prompt, kpe_cache_shape=prompt)
    y_ssm, fr, fi = _s5_scan(u, s5_ops, _state_to_cols(h0r), _state_to_cols(h0i), nseq, scan_ct)
    glu_args = (y_ssm, u, wts["d_skip"], wts["w_glu"], wts["g_out_ssm"])
    if prompt:
        o_mla = _attention(q, k, v, ATTN_HEADS)
    else:
        o_mla = _attention_cached(q, k, v, caches[0], caches[1], wts["w_k"], wts["w_v"], seq, nsub=CACHE_SEQS)
    mix_args = (o_mla, x2, wts["g_out_mla"], wts["w_o"], wts["ln_g"], wts["ln_b"], wts["w_xq"])
    if prompt:
        h, ox = _mix_mem(*glu_args, *mix_args, mem_k, mem_v, wts["w_xo"], min(n, MIX_TILE)), None
    else:
        h, qx = _mix(_glu(*glu_args, tn), *mix_args, tn)
        ox = _mem_attn_cache(qx, mem_k, mem_v, seq, nsub=CACHE_SEQS)
    y = _mlp(h, ox, wts["w_xo"], wts["ln_g"], wts["ln_b"], wts["w_ff1"], wts["w_ff2"], min(n, MLP_TILE))
    return y, ckv, kpe, _state_from_cols(fr), _state_from_cols(fi)


def kernel(x_prompt, x_sample, mem_prompt, cache_mla_ckv, cache_mla_kpe, state_ssm_re, state_ssm_im, cache_mem_k, cache_mem_v, w_in, g_q, w_q_up, g_kv, w_kv_up, a_re, a_im, b_re, b_im, c_re, c_im, d_skip, log_dt, w_glu, g_out_ssm, g_out_mla, w_o, w_xq, w_xk, w_xv, w_xo, w_ff1, w_ff2, ln_g, ln_b):
    assert w_in.shape[0] == DEPTH == 1
    nbp, sp, _ = x_prompt.shape
    nbs, sd, _ = x_sample.shape
    past = cache_mla_ckv.shape[2]
    assert nbp == 1

    wq = jnp.pad(w_q_up[0], ((0, 0), (0, 0), (0, QK_PAD - QK_NOPE - QK_ROPE)))
    wk = w_kv_up[0][:, :, :QK_NOPE].reshape(KV_LORA, -1).astype(BF16)
    wv = w_kv_up[0][:, :, QK_NOPE:].reshape(KV_LORA, -1).astype(BF16)
    inv = ROPE_THETA ** (-jnp.arange(ROPE_HALF, dtype=F32) / ROPE_HALF)
    wts = {
        "inv": jnp.tile(inv, LANES // ROPE_HALF).reshape(1, LANES),
        "w_in": jnp.pad(w_in[0], ((0, 0), (0, LANES - QK_ROPE))).astype(BF16),
        "g_q": g_q[0].reshape(1, -1),
        "w_q": wq.reshape(Q_LORA, MLA_HEADS * QK_PAD).astype(BF16),
        "g_kv": g_kv[0].reshape(1, -1),
        "w_k": wk,
        "w_v": wv,
        "w_vt": wv.T,
        "d_skip": d_skip[0].reshape(1, -1),
        "w_glu": w_glu[0].astype(BF16),
        "g_out_ssm": g_out_ssm[0].reshape(1, -1),
        "g_out_mla": g_out_mla[0].reshape(1, -1),
        "w_o": w_o[0].astype(BF16),
        "w_xq": w_xq[0].reshape(D_MODEL, D_MODEL).astype(BF16),
        "w_xo": w_xo[0].reshape(D_MODEL, D_MODEL).astype(BF16),
        "w_ff1": w_ff1[0].astype(BF16),
        "w_ff2": w_ff2[0].astype(BF16),
        "ln_g": ln_g[0],
        "ln_b": ln_b[0],
    }
    s5_ops = _s5_prep(a_re[0], a_im[0], b_re[0], b_im[0], c_re[0], c_im[0], log_dt[0])

    mk, mv = _mem_kv(mem_prompt.reshape(nbp * N_MEM, D_MODEL),
                     w_xk[0].reshape(D_MODEL, D_MODEL).astype(BF16),
                     w_xv[0].reshape(D_MODEL, D_MODEL).astype(BF16))
    zero = jnp.zeros((nbp, SSM_GROUPS, SSM_STATE), F32)
    pos_p = jnp.arange(sp, dtype=F32).reshape(sp, 1)
    yp, ckv_p, kpe_p, sre_p, sim_p = _layer(
        x_prompt.reshape(sp, D_MODEL), pos_p, nbp, zero, zero, mk, mv, None,
        wts, s5_ops, tn=TOKEN_TILE, scan_ct=min(sp // S5_T, SCAN_CHUNKS))

    pos_s = jnp.tile(past + jnp.arange(sd, dtype=F32), nbs).reshape(nbs * sd, 1)
    caches = (cache_mla_ckv[0], cache_mla_kpe[0])
    ys, ckv_s, kpe_s, sre_s, sim_s = _layer(
        x_sample.reshape(nbs * sd, D_MODEL), pos_s, nbs,
        state_ssm_re[0], state_ssm_im[0], cache_mem_k[0], cache_mem_v[0], caches,
        wts, s5_ops, tn=TOKEN_TILE, scan_ct=sd // S5_T)

    return (yp.reshape(nbp, sp, D_MODEL), ys.reshape(nbs, sd, D_MODEL),
            ckv_p.reshape(1, nbp, sp, KV_LORA), kpe_p.reshape(1, nbp, sp, QK_ROPE),
            sre_p.reshape(1, nbp, SSM_GROUPS, SSM_STATE), sim_p.reshape(1, nbp, SSM_GROUPS, SSM_STATE),
            mk.reshape(1, nbp, N_MEM, X_HEADS, X_HEAD_DIM), mv.reshape(1, nbp, N_MEM, X_HEADS, X_HEAD_DIM),
            ckv_s.reshape(1, nbs, sd, KV_LORA), kpe_s.reshape(1, nbs, sd, QK_ROPE),
            sre_s.reshape(1, nbs, SSM_GROUPS, SSM_STATE), sim_s.reshape(1, nbs, SSM_GROUPS, SSM_STATE))
```
